```python
import math
import jax, jax.numpy as jnp
from jax import lax
import numpy as np

D_MODEL = 1024
BATCH = 16
SEQ = 2048
DEPTH = 4

D_FF = 2816
N_BRANCH = 3
BRANCH_WIDTH = D_MODEL // 2
POOL_WINDOWS = (2, 4, 8, 16)
POOL_GROUPS = len(POOL_WINDOWS)
POOL_GROUP_DIM = BRANCH_WIDTH // POOL_GROUPS
DN_HEAD_DIM = 128
DN_HEADS = BRANCH_WIDTH // DN_HEAD_DIM
DN_CONV = 4
DN_CHUNK = 64
SB_HEAD_DIM = 128
SB_HEADS = BRANCH_WIDTH // SB_HEAD_DIM
SB_BLOCK = 128
EPS = 1e-6

IN_SPLITS = (
    BRANCH_WIDTH,
    3 * BRANCH_WIDTH,
    BRANCH_WIDTH,
    DN_HEADS,
    DN_HEADS,
    3 * BRANCH_WIDTH,
    N_BRANCH * D_MODEL,
)
P_IN = sum(IN_SPLITS)

kernel_name = "hybrid_pool_deltanet_stickbreak_macaron"


def rms_norm(x, g):
    xf = x.astype(jnp.float32)
    y = xf * lax.rsqrt(jnp.mean(xf * xf, axis=-1, keepdims=True) + EPS)
    return (y * g.astype(jnp.float32)).astype(x.dtype)


def swiglu(h, w_gate, w_up, w_down):
    return (jax.nn.silu(h @ w_gate) * (h @ w_up)) @ w_down


def pool_mixer(u, w_group, scale):
    b, s, _ = u.shape
    uf = u.astype(jnp.float32)
    csum = jnp.cumsum(uf, axis=1)
    count = jnp.arange(1, s + 1, dtype=jnp.float32)[None, :, None]
    outs = []
    for gi, win in enumerate(POOL_WINDOWS):
        sl = slice(gi * POOL_GROUP_DIM, (gi + 1) * POOL_GROUP_DIM)
        c = csum[..., sl]
        c_lag = jnp.pad(c, ((0, 0), (win, 0), (0, 0)))[:, :s]
        mean = (c - c_lag) / jnp.minimum(count, float(win))
        outs.append(mean - uf[..., sl])
    pooled = jnp.stack(outs, axis=2).astype(u.dtype)
    mixed = jnp.einsum('bsgc,gcd->bsgd', pooled, w_group).reshape(b, s, BRANCH_WIDTH)
    return mixed * scale


def causal_depthwise_conv(x, w):
    k, c = w.shape
    return lax.conv_general_dilated(
        x, w[:, None, :].astype(x.dtype), window_strides=(1,), padding=((k - 1, 0),),
        dimension_numbers=('NWC', 'WIO', 'NWC'), feature_group_count=c)


def gated_deltanet(qkv_in, z, a, b_logit, conv_w, A_log, dt_bias, out_gain):
    f32 = jnp.float32
    bsz, s, _ = qkv_in.shape
    h, d, c = DN_HEADS, DN_HEAD_DIM, DN_CHUNK
    n = s // c
    qkv = jax.nn.silu(causal_depthwise_conv(qkv_in, conv_w)).astype(f32)
    q, k, v = jnp.split(qkv, 3, axis=-1)

    def to_chunks(t):
        return t.reshape(bsz, n, c, h, d).transpose(0, 3, 1, 2, 4)

    q, k, v = to_chunks(q), to_chunks(k), to_chunks(v)
    q = q * lax.rsqrt(jnp.sum(q * q, -1, keepdims=True) + EPS) * (d ** -0.5)
    k = k * lax.rsqrt(jnp.sum(k * k, -1, keepdims=True) + EPS)
    beta = jax.nn.sigmoid(b_logit.astype(f32)).reshape(bsz, n, c, h).transpose(0, 3, 1, 2)
    g = -jnp.exp(A_log.astype(f32)) * jax.nn.softplus(a.astype(f32) + dt_bias.astype(f32))
    g = g.reshape(bsz, n, c, h).transpose(0, 3, 1, 2)
    gc = jnp.cumsum(g, axis=-1)

    idx = jnp.arange(c)
    lower_incl = idx[:, None] >= idx[None, :]
    strict = idx[:, None] > idx[None, :]
    diff = gc[..., :, None] - gc[..., None, :]
    decay = jnp.where(lower_incl, jnp.exp(jnp.where(lower_incl, diff, 0.0)), 0.0)

    kb = k * beta[..., None]
    lmat = jnp.einsum('bhnid,bhnjd->bhnij', kb, k) * jnp.where(strict, decay, 0.0)
    eye = jnp.eye(c, dtype=f32)
    rhs = jnp.concatenate([v * beta[..., None], kb * jnp.exp(gc)[..., None]], axis=-1)
    sol = lax.linalg.triangular_solve(lmat + eye, rhs, left_side=True, lower=True, unit_diagonal=True)
    u, w = sol[..., :d], sol[..., d:]

    attn_qk = jnp.einsum('bhnid,bhnjd->bhnij', q, k) * decay
    q_dec = q * jnp.exp(gc)[..., None]
    k_dec = k * jnp.exp(gc[..., -1:] - gc)[..., None]
    chunk_decay = jnp.exp(gc[..., -1])

    def step(state, xs):
        u_n, w_n, qd_n, kd_n, a_n, cd_n = xs
        v_new = u_n - jnp.einsum('bhcd,bhde->bhce', w_n, state)
        o_n = (jnp.einsum('bhcd,bhde->bhce', qd_n, state)
               + jnp.einsum('bhij,bhje->bhie', a_n, v_new))
        state = state * cd_n[..., None, None] + jnp.einsum('bhcd,bhce->bhde', kd_n, v_new)
        return state, o_n

    xs = tuple(jnp.moveaxis(t, 2, 0) for t in (u, w, q_dec, k_dec, attn_qk, chunk_decay))
    state0 = jnp.zeros((bsz, h, d, d), f32)
    _, o = lax.scan(step, state0, xs)
    o = o.transpose(1, 0, 3, 2, 4).reshape(bsz, s, h, d)
    o = o * lax.rsqrt(jnp.mean(o * o, -1, keepdims=True) + EPS) * out_gain.astype(f32)
    o = o * jax.nn.silu(z.astype(f32)).reshape(bsz, s, h, d)
    return o.reshape(bsz, s, BRANCH_WIDTH).astype(qkv_in.dtype)


def stick_breaking_attention(qkv):
    f32 = jnp.float32
    bsz, s, _ = qkv.shape
    h, d, blk = SB_HEADS, SB_HEAD_DIM, SB_BLOCK
    q, k, v = [t.reshape(bsz, s, h, d).transpose(0, 2, 1, 3) for t in jnp.split(qkv, 3, axis=-1)]
    scale = d ** -0.5
    outs = []
    for i in range(s // blk):
        q0, kl = i * blk, (i + 1) * blk
        qb = q[:, :, q0:kl]
        kb, vb = k[:, :, :kl], v[:, :, :kl]
        logits = jnp.einsum('bhqd,bhkd->bhqk', qb, kb).astype(f32) * scale
        causal = jnp.arange(kl)[None, :] < jnp.arange(q0, kl)[:, None]
        log_not = jnp.where(causal, jax.nn.log_sigmoid(-logits), 0.0)
        tail = lax.cumsum(log_not, axis=3, reverse=True) - log_not
        weights = jnp.where(causal, jnp.exp(jax.nn.log_sigmoid(logits) + tail), 0.0)
        outs.append(jnp.einsum('bhqk,bhkd->bhqd', weights.astype(vb.dtype), vb))
    o = jnp.concatenate(outs, axis=2)
    return o.transpose(0, 2, 1, 3).reshape(bsz, s, BRANCH_WIDTH)


def _fwd_setup_inputs(seed: int = 0) -> dict:
    key = jax.random.key(seed)
    ks = jax.random.split(key, 17)
    L, D, F = DEPTH, D_MODEL, D_FF
    f32 = jnp.float32

    def dense(k, shape, fan_in):
        return jax.random.normal(k, shape, f32) * (fan_in ** -0.5)

    def gain(k, shape):
        return 1.0 + 0.02 * jax.random.normal(k, shape, f32)

    dt = jnp.exp(jax.random.uniform(ks[12], (L, DN_HEADS), f32,
                                    minval=math.log(1e-3), maxval=math.log(1e-1)))
    return {
        "x": jax.random.normal(ks[0], (BATCH, SEQ, D), f32),
        "ffn_norm": gain(ks[1], (L, 2, D)),
        "ffn_w_gate": dense(ks[2], (L, 2, D, F), D),
        "ffn_w_up": dense(ks[3], (L, 2, D, F), D),
        "ffn_w_down": dense(ks[4], (L, 2, F, D), F),
        "mix_norm": gain(ks[5], (L, D)),
        "w_in": dense(ks[6], (L, D, P_IN), D),
        "b_gate": 0.01 * jax.random.normal(ks[7], (L, N_BRANCH * D), f32),
        "pool_w": dense(ks[8], (L, POOL_GROUPS, POOL_GROUP_DIM, POOL_GROUP_DIM), POOL_GROUP_DIM),
        "pool_scale": gain(ks[9], (L, BRANCH_WIDTH)),
        "dn_conv": dense(ks[10], (L, DN_CONV, 3 * BRANCH_WIDTH), DN_CONV),
        "dn_A_log": jnp.log(jax.random.uniform(ks[11], (L, DN_HEADS), f32, minval=1.0, maxval=16.0)),
        "dn_dt_bias": dt + jnp.log(-jnp.expm1(-dt)),
        "dn_out_norm": gain(ks[13], (L, DN_HEAD_DIM)),
        "w_branch": dense(ks[14], (L, N_BRANCH, BRANCH_WIDTH, D), BRANCH_WIDTH),
        "w_out": dense(ks[15], (L, D, D), D),
        "final_norm": gain(ks[16], (D,)),
    }


def _fwd_reference(x, ffn_norm, ffn_w_gate, ffn_w_up, ffn_w_down, mix_norm, w_in, b_gate,
              pool_w, pool_scale, dn_conv, dn_A_log, dn_dt_bias, dn_out_norm,
              w_branch, w_out, final_norm):
    bsz, s, d_model = x.shape
    split_points = [int(p) for p in np.cumsum(IN_SPLITS)[:-1]]
    for l in range(DEPTH):
        hf = rms_norm(x, ffn_norm[l, 0])
        x = x + 0.5 * swiglu(hf, ffn_w_gate[l, 0], ffn_w_up[l, 0], ffn_w_down[l, 0])

        h = rms_norm(x, mix_norm[l])
        proj = h @ w_in[l]
        u_pool, dn_qkv, dn_z, dn_a, dn_b, sb_qkv, gate_logits = jnp.split(proj, split_points, axis=-1)
        y_pool = pool_mixer(u_pool, pool_w[l], pool_scale[l])
        y_dn = gated_deltanet(dn_qkv, dn_z, dn_a, dn_b, dn_conv[l], dn_A_log[l], dn_dt_bias[l], dn_out_norm[l])
        y_sb = stick_breaking_attention(sb_qkv)

        branches = jnp.stack([y_pool, y_dn, y_sb], axis=2)
        branch_d = jnp.einsum('bsnw,nwd->bsnd', branches, w_branch[l])
        gates = jax.nn.sigmoid((gate_logits + b_gate[l]).astype(jnp.float32)).astype(x.dtype)
        gates = gates.reshape(bsz, s, N_BRANCH, d_model)
        merged = jnp.sum(gates * branch_d, axis=2)
        x = x + merged @ w_out[l]

        hf = rms_norm(x, ffn_norm[l, 1])
        x = x + 0.5 * swiglu(hf, ffn_w_gate[l, 1], ffn_w_up[l, 1], ffn_w_down[l, 1])
    return rms_norm(x, final_norm)


import jax as _jax
import jax.numpy as _jnp

TWIN_FORMAT = 'train_step'
FWD_PARAMS = ['x', 'ffn_norm', 'ffn_w_gate', 'ffn_w_up', 'ffn_w_down', 'mix_norm', 'w_in', 'b_gate', 'pool_w', 'pool_scale', 'dn_conv', 'dn_A_log', 'dn_dt_bias', 'dn_out_norm', 'w_branch', 'w_out', 'final_norm']
TWIN_WEIGHTS = ['ffn_norm', 'ffn_w_gate', 'ffn_w_up', 'ffn_w_down', 'mix_norm', 'w_in', 'b_gate', 'pool_w', 'pool_scale', 'dn_conv', 'dn_A_log', 'dn_dt_bias', 'dn_out_norm', 'w_branch', 'w_out', 'final_norm']
TWIN_DIFF_INPUT = 'x'
TWIN_INPUTS = ['x', 'ffn_norm', 'ffn_w_gate', 'ffn_w_up', 'ffn_w_down', 'mix_norm', 'w_in', 'b_gate', 'pool_w', 'pool_scale', 'dn_conv', 'dn_A_log', 'dn_dt_bias', 'dn_out_norm', 'w_branch', 'w_out', 'final_norm', 'loss_target', 'm_ffn_norm', 'm_ffn_w_gate', 'm_ffn_w_up', 'm_ffn_w_down', 'm_mix_norm', 'm_w_in', 'm_b_gate', 'm_pool_w', 'm_pool_scale', 'm_dn_conv', 'm_dn_A_log', 'm_dn_dt_bias', 'm_dn_out_norm', 'm_w_branch', 'm_w_out', 'm_final_norm', 'v_ffn_norm', 'v_ffn_w_gate', 'v_ffn_w_up', 'v_ffn_w_down', 'v_mix_norm', 'v_w_in', 'v_b_gate', 'v_pool_w', 'v_pool_scale', 'v_dn_conv', 'v_dn_A_log', 'v_dn_dt_bias', 'v_dn_out_norm', 'v_w_branch', 'v_w_out', 'v_final_norm']
TWIN_OUTPUTS = ['loss', 'grad_x', 'grad_ffn_norm', 'grad_ffn_w_gate', 'grad_ffn_w_up', 'grad_ffn_w_down', 'grad_mix_norm', 'grad_w_in', 'grad_b_gate', 'grad_pool_w', 'grad_pool_scale', 'grad_dn_conv', 'grad_dn_A_log', 'grad_dn_dt_bias', 'grad_dn_out_norm', 'grad_w_branch', 'grad_w_out', 'grad_final_norm', 'delta_ffn_norm', 'delta_ffn_w_gate', 'delta_ffn_w_up', 'delta_ffn_w_down', 'delta_mix_norm', 'delta_w_in', 'delta_b_gate', 'delta_pool_w', 'delta_pool_scale', 'delta_dn_conv', 'delta_dn_A_log', 'delta_dn_dt_bias', 'delta_dn_out_norm', 'delta_w_branch', 'delta_w_out', 'delta_final_norm', 'new_m_ffn_norm', 'new_m_ffn_w_gate', 'new_m_ffn_w_up', 'new_m_ffn_w_down', 'new_m_mix_norm', 'new_m_w_in', 'new_m_b_gate', 'new_m_pool_w', 'new_m_pool_scale', 'new_m_dn_conv', 'new_m_dn_A_log', 'new_m_dn_dt_bias', 'new_m_dn_out_norm', 'new_m_w_branch', 'new_m_w_out', 'new_m_final_norm', 'new_v_ffn_norm', 'new_v_ffn_w_gate', 'new_v_ffn_w_up', 'new_v_ffn_w_down', 'new_v_mix_norm', 'new_v_w_in', 'new_v_b_gate', 'new_v_pool_w', 'new_v_pool_scale', 'new_v_dn_conv', 'new_v_dn_A_log', 'new_v_dn_dt_bias', 'new_v_dn_out_norm', 'new_v_w_branch', 'new_v_w_out', 'new_v_final_norm']
TWIN_LEAF_KINDS = {'loss': 'loss', 'grad_x': 'grad_x', 'grad_ffn_norm': 'grad_w', 'grad_ffn_w_gate': 'grad_w', 'grad_ffn_w_up': 'grad_w', 'grad_ffn_w_down': 'grad_w', 'grad_mix_norm': 'grad_w', 'grad_w_in': 'grad_w', 'grad_b_gate': 'grad_w', 'grad_pool_w': 'grad_w', 'grad_pool_scale': 'grad_w', 'grad_dn_conv': 'grad_w', 'grad_dn_A_log': 'grad_w', 'grad_dn_dt_bias': 'grad_w', 'grad_dn_out_norm': 'grad_w', 'grad_w_branch': 'grad_w', 'grad_w_out': 'grad_w', 'grad_final_norm': 'grad_w', 'delta_ffn_norm': 'delta_w', 'delta_ffn_w_gate': 'delta_w', 'delta_ffn_w_up': 'delta_w', 'delta_ffn_w_down': 'delta_w', 'delta_mix_norm': 'delta_w', 'delta_w_in': 'delta_w', 'delta_b_gate': 'delta_w', 'delta_pool_w': 'delta_w', 'delta_pool_scale': 'delta_w', 'delta_dn_conv': 'delta_w', 'delta_dn_A_log': 'delta_w', 'delta_dn_dt_bias': 'delta_w', 'delta_dn_out_norm': 'delta_w', 'delta_w_branch': 'delta_w', 'delta_w_out': 'delta_w', 'delta_final_norm': 'delta_w', 'new_m_ffn_norm': 'new_m', 'new_m_ffn_w_gate': 'new_m', 'new_m_ffn_w_up': 'new_m', 'new_m_ffn_w_down': 'new_m', 'new_m_mix_norm': 'new_m', 'new_m_w_in': 'new_m', 'new_m_b_gate': 'new_m', 'new_m_pool_w': 'new_m', 'new_m_pool_scale': 'new_m', 'new_m_dn_conv': 'new_m', 'new_m_dn_A_log': 'new_m', 'new_m_dn_dt_bias': 'new_m', 'new_m_dn_out_norm': 'new_m', 'new_m_w_branch': 'new_m', 'new_m_w_out': 'new_m', 'new_m_final_norm': 'new_m', 'new_v_ffn_norm': 'new_v', 'new_v_ffn_w_gate': 'new_v', 'new_v_ffn_w_up': 'new_v', 'new_v_ffn_w_down': 'new_v', 'new_v_mix_norm': 'new_v', 'new_v_w_in': 'new_v', 'new_v_b_gate': 'new_v', 'new_v_pool_w': 'new_v', 'new_v_pool_scale': 'new_v', 'new_v_dn_conv': 'new_v', 'new_v_dn_A_log': 'new_v', 'new_v_dn_dt_bias': 'new_v', 'new_v_dn_out_norm': 'new_v', 'new_v_w_branch': 'new_v', 'new_v_w_out': 'new_v', 'new_v_final_norm': 'new_v'}


def _forward(args):
    return _fwd_reference(*[args[k] for k in FWD_PARAMS])


def _output_shape():
    out = _jax.eval_shape(lambda: _forward(_fwd_setup_inputs(0)))
    return out.shape, out.dtype

N_MICROBATCH = 1
ADAM_LR = 0.001
ADAM_B1 = 0.9
ADAM_B2 = 0.999
ADAM_EPS = 1e-08
ADAM_WD = 0.01
ADAM_STEP = 10
PER_EXAMPLE_BATCH_AXIS = {'x': 0, 'loss_target': 0}
SHARED_INPUTS = []
_WEIGHT_DTYPES = {'ffn_norm': _jnp.float32, 'ffn_w_gate': _jnp.float32, 'ffn_w_up': _jnp.float32, 'ffn_w_down': _jnp.float32, 'mix_norm': _jnp.float32, 'w_in': _jnp.float32, 'b_gate': _jnp.float32, 'pool_w': _jnp.float32, 'pool_scale': _jnp.float32, 'dn_conv': _jnp.float32, 'dn_A_log': _jnp.float32, 'dn_dt_bias': _jnp.float32, 'dn_out_norm': _jnp.float32, 'w_branch': _jnp.float32, 'w_out': _jnp.float32, 'final_norm': _jnp.float32}
MOMENT_SCALE = {'ffn_norm': 6.794959e-02, 'ffn_w_gate': 2.921335e-02, 'ffn_w_up': 2.828033e-02, 'ffn_w_down': 4.690673e-02, 'mix_norm': 1.230439e-01, 'w_in': 4.658679e-02, 'b_gate': 2.198906e-02, 'pool_w': 9.625070e-02, 'pool_scale': 9.558816e-02, 'dn_conv': 4.947642e-02, 'dn_A_log': 2.667423e-01, 'dn_dt_bias': 2.622655e-01, 'dn_out_norm': 1.465962e-01, 'w_branch': 5.551338e-02, 'w_out': 9.589919e-02, 'final_norm': 3.203257e+01}


def _to_microbatches(a, axis):
    t = _jnp.moveaxis(a, axis, 0)
    t = t.reshape((N_MICROBATCH, t.shape[0] // N_MICROBATCH) + t.shape[1:])
    return _jnp.moveaxis(t, 1, axis + 1)


def setup_inputs(seed: int = 0) -> dict:
    inp = _fwd_setup_inputs(seed)
    key = _jax.random.fold_in(_jax.random.key(seed), 7919)
    shape, _ = _output_shape()
    out = dict(inp)
    out["loss_target"] = _jax.random.normal(_jax.random.fold_in(key, 0), shape, _jnp.float32)
    for i, name in enumerate(TWIN_WEIGHTS):
        w = inp[name].astype(_jnp.float32)
        if MOMENT_SCALE is None:
            s = _jnp.sqrt(_jnp.mean(_jnp.square(w)) + 1e-30)
        else:
            s = MOMENT_SCALE[name]
        km, kv = _jax.random.split(_jax.random.fold_in(key, i + 1))
        out[name] = w
        out["m_" + name] = s * _jax.random.normal(km, w.shape, _jnp.float32)
        out["v_" + name] = (s * s) * _jax.random.uniform(kv, w.shape, _jnp.float32, 0.5, 1.5)
    if N_MICROBATCH > 1:
        for name, axis in PER_EXAMPLE_BATCH_AXIS.items():
            out[name] = _to_microbatches(out[name], axis)
    return {'x': out['x'], 'ffn_norm': out['ffn_norm'], 'ffn_w_gate': out['ffn_w_gate'], 'ffn_w_up': out['ffn_w_up'], 'ffn_w_down': out['ffn_w_down'], 'mix_norm': out['mix_norm'], 'w_in': out['w_in'], 'b_gate': out['b_gate'], 'pool_w': out['pool_w'], 'pool_scale': out['pool_scale'], 'dn_conv': out['dn_conv'], 'dn_A_log': out['dn_A_log'], 'dn_dt_bias': out['dn_dt_bias'], 'dn_out_norm': out['dn_out_norm'], 'w_branch': out['w_branch'], 'w_out': out['w_out'], 'final_norm': out['final_norm'], 'loss_target': out['loss_target'], 'm_ffn_norm': out['m_ffn_norm'], 'm_ffn_w_gate': out['m_ffn_w_gate'], 'm_ffn_w_up': out['m_ffn_w_up'], 'm_ffn_w_down': out['m_ffn_w_down'], 'm_mix_norm': out['m_mix_norm'], 'm_w_in': out['m_w_in'], 'm_b_gate': out['m_b_gate'], 'm_pool_w': out['m_pool_w'], 'm_pool_scale': out['m_pool_scale'], 'm_dn_conv': out['m_dn_conv'], 'm_dn_A_log': out['m_dn_A_log'], 'm_dn_dt_bias': out['m_dn_dt_bias'], 'm_dn_out_norm': out['m_dn_out_norm'], 'm_w_branch': out['m_w_branch'], 'm_w_out': out['m_w_out'], 'm_final_norm': out['m_final_norm'], 'v_ffn_norm': out['v_ffn_norm'], 'v_ffn_w_gate': out['v_ffn_w_gate'], 'v_ffn_w_up': out['v_ffn_w_up'], 'v_ffn_w_down': out['v_ffn_w_down'], 'v_mix_norm': out['v_mix_norm'], 'v_w_in': out['v_w_in'], 'v_b_gate': out['v_b_gate'], 'v_pool_w': out['v_pool_w'], 'v_pool_scale': out['v_pool_scale'], 'v_dn_conv': out['v_dn_conv'], 'v_dn_A_log': out['v_dn_A_log'], 'v_dn_dt_bias': out['v_dn_dt_bias'], 'v_dn_out_norm': out['v_dn_out_norm'], 'v_w_branch': out['v_w_branch'], 'v_w_out': out['v_w_out'], 'v_final_norm': out['v_final_norm']}


def _loss(weights, diff, rest, loss_target):
    with _jax.named_scope("forward"):
        args = {**rest, TWIN_DIFF_INPUT: diff, **{k: w.astype(_WEIGHT_DTYPES[k]) for k, w in weights.items()}}
        y = _forward(args)
    with _jax.named_scope("loss_head"):
        err = _jnp.square(y.astype(_jnp.float32) - loss_target)
        return 0.5 * _jnp.sum(_jnp.mean(err, axis=-1)) if err.ndim else 0.5 * err


def _adamw(w, g, m, v):
    m = ADAM_B1 * m + (1.0 - ADAM_B1) * g
    v = ADAM_B2 * v + (1.0 - ADAM_B2) * _jnp.square(g)
    m_hat = m / (1.0 - ADAM_B1 ** ADAM_STEP)
    v_hat = v / (1.0 - ADAM_B2 ** ADAM_STEP)
    delta = -ADAM_LR * (m_hat / (_jnp.sqrt(v_hat) + ADAM_EPS) + ADAM_WD * w)
    return delta, m, v


def reference(x, ffn_norm, ffn_w_gate, ffn_w_up, ffn_w_down, mix_norm, w_in, b_gate, pool_w, pool_scale, dn_conv, dn_A_log, dn_dt_bias, dn_out_norm, w_branch, w_out, final_norm, loss_target, m_ffn_norm, m_ffn_w_gate, m_ffn_w_up, m_ffn_w_down, m_mix_norm, m_w_in, m_b_gate, m_pool_w, m_pool_scale, m_dn_conv, m_dn_A_log, m_dn_dt_bias, m_dn_out_norm, m_w_branch, m_w_out, m_final_norm, v_ffn_norm, v_ffn_w_gate, v_ffn_w_up, v_ffn_w_down, v_mix_norm, v_w_in, v_b_gate, v_pool_w, v_pool_scale, v_dn_conv, v_dn_A_log, v_dn_dt_bias, v_dn_out_norm, v_w_branch, v_w_out, v_final_norm):
    given = dict(x=x, ffn_norm=ffn_norm, ffn_w_gate=ffn_w_gate, ffn_w_up=ffn_w_up, ffn_w_down=ffn_w_down, mix_norm=mix_norm, w_in=w_in, b_gate=b_gate, pool_w=pool_w, pool_scale=pool_scale, dn_conv=dn_conv, dn_A_log=dn_A_log, dn_dt_bias=dn_dt_bias, dn_out_norm=dn_out_norm, w_branch=w_branch, w_out=w_out, final_norm=final_norm, loss_target=loss_target, m_ffn_norm=m_ffn_norm, m_ffn_w_gate=m_ffn_w_gate, m_ffn_w_up=m_ffn_w_up, m_ffn_w_down=m_ffn_w_down, m_mix_norm=m_mix_norm, m_w_in=m_w_in, m_b_gate=m_b_gate, m_pool_w=m_pool_w, m_pool_scale=m_pool_scale, m_dn_conv=m_dn_conv, m_dn_A_log=m_dn_A_log, m_dn_dt_bias=m_dn_dt_bias, m_dn_out_norm=m_dn_out_norm, m_w_branch=m_w_branch, m_w_out=m_w_out, m_final_norm=m_final_norm, v_ffn_norm=v_ffn_norm, v_ffn_w_gate=v_ffn_w_gate, v_ffn_w_up=v_ffn_w_up, v_ffn_w_down=v_ffn_w_down, v_mix_norm=v_mix_norm, v_w_in=v_w_in, v_b_gate=v_b_gate, v_pool_w=v_pool_w, v_pool_scale=v_pool_scale, v_dn_conv=v_dn_conv, v_dn_A_log=v_dn_A_log, v_dn_dt_bias=v_dn_dt_bias, v_dn_out_norm=v_dn_out_norm, v_w_branch=v_w_branch, v_w_out=v_w_out, v_final_norm=v_final_norm)
    weights = {n: given[n] for n in TWIN_WEIGHTS}
    shared = {n: given[n] for n in SHARED_INPUTS}
    per_example = {n: given[n] for n in ['x']}
    grad_fn = _jax.value_and_grad(_loss, argnums=(0, 1))

    def one_microbatch(ex, loss_target):
        ex = dict(ex)
        diff = ex.pop(TWIN_DIFF_INPUT)
        return grad_fn(weights, diff, {**shared, **ex}, loss_target)

    if N_MICROBATCH == 1:
        loss, (grad_w, grad_x) = one_microbatch(per_example, given["loss_target"])
    else:
        def body(carry, xs):
            loss_sum, grad_sum = carry
            l_k, (gw_k, gx_k) = one_microbatch(xs[0], xs[1])
            with _jax.named_scope("update"):
                return (loss_sum + l_k, _jax.tree.map(_jnp.add, grad_sum, gw_k)), gx_k

        init = (_jnp.zeros((), _jnp.float32), _jax.tree.map(_jnp.zeros_like, weights))
        (loss, grad_w), grad_x = _jax.lax.scan(body, init, (per_example, given["loss_target"]))
    with _jax.named_scope("update"):
        delta_w, new_m, new_v = {}, {}, {}
        for n in TWIN_WEIGHTS:
            delta_w[n], new_m[n], new_v[n] = _adamw(weights[n], grad_w[n], given["m_" + n], given["v_" + n])
    return (loss, grad_x, *[grad_w[n] for n in TWIN_WEIGHTS], *[delta_w[n] for n in TWIN_WEIGHTS],
            *[new_m[n] for n in TWIN_WEIGHTS], *[new_v[n] for n in TWIN_WEIGHTS])
```

```python
import functools
import math

import jax
import jax.numpy as jnp
from jax import lax
from jax.experimental import pallas as pl
from jax.experimental.pallas import tpu as pltpu

f32 = jnp.float32
bf16 = jnp.bfloat16
HI = lax.Precision.HIGHEST

EPS = 1e-6
HEAD = 128
N_HEADS = 4
BRANCH = 512
CHUNK = 64
SB_BLOCK = 128
POOL_WINDOWS = (2, 4, 8, 16)
N_CHIPS = 4
LANES = 128
PACK_W = 1024
ADAM_LR, ADAM_B1, ADAM_B2, ADAM_EPS, ADAM_WD, ADAM_STEP = 0.001, 0.9, 0.999, 1e-08, 0.01, 10
VMEM_LIMIT = 56 * 1024 * 1024

NN = (((1,), (0,)), ((), ()))
NT = (((1,), (1,)), ((), ()))
TN = (((0,), (0,)), ((), ()))


def _cp(sem=None):
    return pltpu.CompilerParams(dimension_semantics=sem, vmem_limit_bytes=VMEM_LIMIT)


def _tile(n, prefs):
    for p in prefs:
        if n % p == 0:
            return p
    return n


def _dot(a, b, dn=NN):
    return lax.dot_general(a, b, dn, preferred_element_type=f32)


def _sigmoid(x):
    return 1.0 / (1.0 + jnp.exp(-x))


def _softplus(x):
    return jnp.maximum(x, 0.0) + jnp.log1p(jnp.exp(-jnp.abs(x)))


def _mm(pairs, M, N, out_dtype, name, tm=None, tn=None, res=None, scale=1.0):
    tm = tm or _tile(M, (512, 256, 128))
    tn = tn or _tile(N, (512, 256, 128))
    specs, arrs, dns = [], [], []

    def lead(sel, shape, imap):
        if sel is None:
            return pl.BlockSpec(shape, imap)
        return pl.BlockSpec((None,) + shape, lambda i, j, sel=sel, imap=imap: (sel,) + imap(i, j))

    for form, a, b, o in pairs:
        ka, kb, moff, noff = o.get("ka", 0), o.get("kb", 0), o.get("moff", 0), o.get("noff", 0)
        asel, bsel = o.get("asel"), o.get("bsel")
        if form == "nn":
            K = o.get("K") or a.shape[-1]
            sa = lead(asel, (tm, K), lambda i, j, ka=ka, moff=moff: (i + moff, ka))
            sb = lead(bsel, (K, tn), lambda i, j, kb=kb, noff=noff: (kb, j + noff))
            dn = NN
        elif form == "nt":
            K = o.get("K") or a.shape[-1]
            sa = lead(asel, (tm, K), lambda i, j, ka=ka, moff=moff: (i + moff, ka))
            sb = lead(bsel, (tn, K), lambda i, j, kb=kb, noff=noff: (j + noff, kb))
            dn = NT
        else:
            K = a.shape[-2]
            sa = lead(asel, (K, tm), lambda i, j, moff=moff: (0, i + moff))
            sb = lead(bsel, (K, tn), lambda i, j, noff=noff: (0, j + noff))
            dn = TN
        specs += [sa, sb]
        arrs += [a, b]
        dns.append(dn)
    if res is not None:
        specs.append(pl.BlockSpec((tm, tn), lambda i, j: (i, j)))
        arrs.append(res)
    n = len(pairs)

    def body(*refs):
        o_ref = refs[-1]
        acc = None
        for p in range(n):
            d = _dot(refs[2 * p][...].astype(bf16), refs[2 * p + 1][...].astype(bf16), dns[p])
            acc = d if acc is None else acc + d
        if scale != 1.0:
            acc = acc * scale
        if res is not None:
            acc = acc + refs[2 * n][...]
        o_ref[...] = acc.astype(o_ref.dtype)

    return pl.pallas_call(
        body, name=name, grid=(M // tm, N // tn), in_specs=specs,
        out_specs=pl.BlockSpec((tm, tn), lambda i, j: (i, j)),
        out_shape=jax.ShapeDtypeStruct((M, N), out_dtype),
        compiler_params=_cp(("parallel", "parallel")),
    )(*arrs)


def _rms_fwd(x, g, name):
    T, D = x.shape
    tm = _tile(T, (512, 256, 128))

    def body(x_ref, g_ref, h_ref):
        xv = x_ref[...]
        r = lax.rsqrt(jnp.mean(xv * xv, axis=-1, keepdims=True) + EPS)
        h_ref[...] = (xv * r * g_ref[...]).astype(bf16)

    return pl.pallas_call(
        body, name=name, grid=(T // tm,),
        in_specs=[pl.BlockSpec((tm, D), lambda i: (i, 0)), pl.BlockSpec((1, D), lambda i: (0, 0))],
        out_specs=pl.BlockSpec((tm, D), lambda i: (i, 0)),
        out_shape=jax.ShapeDtypeStruct((T, D), bf16), compiler_params=_cp(("parallel",)),
    )(x, g.reshape(1, D))


def _fold8(v):
    r, d = v.shape
    return jnp.sum(v.reshape(r // 8, 8, d), axis=0)


def _rms_bwd(dh, x, g, dres, name):
    T, D = x.shape
    tm = _tile(T, (512, 256, 128))

    def body(dh_ref, x_ref, g_ref, dres_ref, dx_ref, dg_ref):
        xv = x_ref[...]
        r = lax.rsqrt(jnp.mean(xv * xv, axis=-1, keepdims=True) + EPS)
        xh = xv * r
        dhv = dh_ref[...]
        dxh = dhv * g_ref[...]
        dx_ref[...] = dres_ref[...] + r * (dxh - xh * jnp.mean(dxh * xh, axis=-1, keepdims=True))

        @pl.when(pl.program_id(0) == 0)
        def _():
            dg_ref[...] = jnp.zeros_like(dg_ref)

        dg_ref[...] += _fold8(dhv * xh)

    return pl.pallas_call(
        body, name=name, grid=(T // tm,),
        in_specs=[pl.BlockSpec((tm, D), lambda i: (i, 0)), pl.BlockSpec((tm, D), lambda i: (i, 0)),
                  pl.BlockSpec((1, D), lambda i: (0, 0)), pl.BlockSpec((tm, D), lambda i: (i, 0))],
        out_specs=[pl.BlockSpec((tm, D), lambda i: (i, 0)), pl.BlockSpec((8, D), lambda i: (0, 0))],
        out_shape=[jax.ShapeDtypeStruct((T, D), f32), jax.ShapeDtypeStruct((8, D), f32)],
        compiler_params=_cp(("arbitrary",)),
    )(dh, x, g.reshape(1, D), dres)


def _final_loss(x, target, g):
    T, D = x.shape
    tm = _tile(T, (512, 256, 128))

    def body(x_ref, t_ref, g_ref, dx_ref, dg_ref, ls_ref):
        xv = x_ref[...]
        r = lax.rsqrt(jnp.mean(xv * xv, axis=-1, keepdims=True) + EPS)
        xh = xv * r
        gv = g_ref[...]
        e = xh * gv - t_ref[...]
        dy = e * (1.0 / D)
        dxh = dy * gv
        dx_ref[...] = r * (dxh - xh * jnp.mean(dxh * xh, axis=-1, keepdims=True))

        @pl.when(pl.program_id(0) == 0)
        def _():
            dg_ref[...] = jnp.zeros_like(dg_ref)
            ls_ref[...] = jnp.zeros_like(ls_ref)

        dg_ref[...] += _fold8(dy * xh)
        ls_ref[...] += _fold8(e * e)

    return pl.pallas_call(
        body, name="final_loss", grid=(T // tm,),
        in_specs=[pl.BlockSpec((tm, D), lambda i: (i, 0)), pl.BlockSpec((tm, D), lambda i: (i, 0)),
                  pl.BlockSpec((1, D), lambda i: (0, 0))],
        out_specs=[pl.BlockSpec((tm, D), lambda i: (i, 0)), pl.BlockSpec((8, D), lambda i: (0, 0)),
                   pl.BlockSpec((8, D), lambda i: (0, 0))],
        out_shape=[jax.ShapeDtypeStruct((T, D), f32), jax.ShapeDtypeStruct((8, D), f32),
                   jax.ShapeDtypeStruct((8, D), f32)],
        compiler_params=_cp(("arbitrary",)),
    )(x, target, g.reshape(1, D))


def _ffn_up(h, wg, wu, name):
    T, D = h.shape
    F = wg.shape[1]
    tm = _tile(T, (512, 256, 128))
    tn = _tile(F, (1408, 512, 256, 128))

    def body(h_ref, wg_ref, wu_ref, a_ref, b_ref, m_ref):
        hv = h_ref[...]
        a = _dot(hv, wg_ref[...])
        b = _dot(hv, wu_ref[...])
        a_ref[...] = a.astype(bf16)
        b_ref[...] = b.astype(bf16)
        m_ref[...] = (a * _sigmoid(a) * b).astype(bf16)

    o = jax.ShapeDtypeStruct((T, F), bf16)
    ospec = pl.BlockSpec((tm, tn), lambda j, i: (i, j))
    return pl.pallas_call(
        body, name=name, grid=(F // tn, T // tm),
        in_specs=[pl.BlockSpec((tm, D), lambda j, i: (i, 0)), pl.BlockSpec((D, tn), lambda j, i: (0, j)),
                  pl.BlockSpec((D, tn), lambda j, i: (0, j))],
        out_specs=[ospec, ospec, ospec], out_shape=[o, o, o], compiler_params=_cp(("parallel", "parallel")),
    )(h, wg, wu)


def _ffn_bwd_mid(dx, wd, a, b, name):
    T, D = dx.shape
    F = wd.shape[0]
    tm = _tile(T, (512, 256, 128))
    tn = _tile(F, (1408, 512, 256, 128))

    def body(dx_ref, wd_ref, a_ref, b_ref, da_ref, db_ref, m_ref):
        dm = 0.5 * _dot(dx_ref[...].astype(bf16), wd_ref[...], NT)
        av = a_ref[...].astype(f32)
        bv = b_ref[...].astype(f32)
        s = _sigmoid(av)
        silu = av * s
        da_ref[...] = (dm * bv * (s * (1.0 + av * (1.0 - s)))).astype(bf16)
        db_ref[...] = (dm * silu).astype(bf16)
        m_ref[...] = (silu * bv).astype(bf16)

    o = jax.ShapeDtypeStruct((T, F), bf16)
    ospec = pl.BlockSpec((tm, tn), lambda j, i: (i, j))
    return pl.pallas_call(
        body, name=name, grid=(F // tn, T // tm),
        in_specs=[pl.BlockSpec((tm, D), lambda j, i: (i, 0)), pl.BlockSpec((tn, D), lambda j, i: (j, 0)), ospec, ospec],
        out_specs=[ospec, ospec, ospec], out_shape=[o, o, o], compiler_params=_cp(("parallel", "parallel")),
    )(dx, wd, a, b)


def _shift_down(x, k, row):
    return jnp.where(row < k, 0.0, pltpu.roll(x, k, 0))


def _shift_up(x, k, row):
    s = x.shape[0]
    return jnp.where(row >= s - k, 0.0, pltpu.roll(x, s - k, 0))


def _window_sum(x, g, row, shift):
    s2 = x + shift(x, 1, row)
    s4 = s2 + shift(s2, 2, row)
    s8 = s4 + shift(s4, 4, row)
    s16 = s8 + shift(s8, 8, row)
    return jnp.where(g == 0, s2, jnp.where(g == 1, s4, jnp.where(g == 2, s8, s16)))


def _pool_fwd(proj, pool_w, scale, name):
    B, S = proj.shape[0], proj.shape[1]
    G = len(POOL_WINDOWS)

    def body(u_ref, w_ref, sc_ref, y_ref):
        g = pl.program_id(1)
        u = u_ref[0].astype(f32)
        row = lax.broadcasted_iota(jnp.int32, u.shape, 0)
        win = _window_sum(u, g, row, _shift_down)
        cnt = jnp.minimum(row + 1, jnp.left_shift(2, g)).astype(f32)
        pooled = win / cnt - u
        mixed = _dot(pooled.astype(bf16), w_ref[0].astype(bf16))
        y_ref[0] = (mixed * sc_ref[...]).astype(bf16)

    return pl.pallas_call(
        body, name=name, grid=(B, G),
        in_specs=[pl.BlockSpec((1, S, HEAD), lambda b, g: (b, 0, g)), pl.BlockSpec((1, HEAD, HEAD), lambda b, g: (g, 0, 0)),
                  pl.BlockSpec((1, HEAD), lambda b, g: (0, g))],
        out_specs=pl.BlockSpec((1, S, HEAD), lambda b, g: (b, 0, g)),
        out_shape=jax.ShapeDtypeStruct((B, S, BRANCH), bf16), compiler_params=_cp(("parallel", "parallel")),
    )(proj, pool_w, scale.reshape(1, BRANCH))


def _pool_bwd(proj, dy, pool_w, scale, name):
    B, S = proj.shape[0], proj.shape[1]
    G = len(POOL_WINDOWS)

    def body(u_ref, dy_ref, w_ref, sc_ref, du_ref, dw_ref, dsc_ref):
        g = pl.program_id(0)
        u = u_ref[0].astype(f32)
        row = lax.broadcasted_iota(jnp.int32, u.shape, 0)
        cnt = jnp.minimum(row + 1, jnp.left_shift(2, g)).astype(f32)
        pooled = _window_sum(u, g, row, _shift_down) / cnt - u
        wv = w_ref[0].astype(bf16)
        mixed = _dot(pooled.astype(bf16), wv)
        dyv = dy_ref[0].astype(f32)
        dmix = (dyv * sc_ref[...]).astype(bf16)
        dpool = _dot(dmix, wv, NT)
        du_ref[0] = (_window_sum(dpool / cnt, g, row, _shift_up) - dpool).astype(bf16)

        @pl.when(pl.program_id(1) == 0)
        def _():
            dw_ref[...] = jnp.zeros_like(dw_ref)
            dsc_ref[...] = jnp.zeros_like(dsc_ref)

        dw_ref[0] += _dot(pooled.astype(bf16), dmix, TN)
        dsc_ref[...] += _fold8(dyv * mixed)

    return pl.pallas_call(
        body, name=name, grid=(G, B),
        in_specs=[pl.BlockSpec((1, S, HEAD), lambda g, b: (b, 0, g)), pl.BlockSpec((1, S, HEAD), lambda g, b: (b, 0, g)),
                  pl.BlockSpec((1, HEAD, HEAD), lambda g, b: (g, 0, 0)), pl.BlockSpec((1, HEAD), lambda g, b: (0, g))],
        out_specs=[pl.BlockSpec((1, S, HEAD), lambda g, b: (b, 0, g)), pl.BlockSpec((1, HEAD, HEAD), lambda g, b: (g, 0, 0)),
                   pl.BlockSpec((8, HEAD), lambda g, b: (0, g))],
        out_shape=[jax.ShapeDtypeStruct((B, S, BRANCH), bf16), jax.ShapeDtypeStruct((G, HEAD, HEAD), f32),
                   jax.ShapeDtypeStruct((8, BRANCH), f32)],
        compiler_params=_cp(("parallel", "arbitrary")),
    )(proj, dy, pool_w, scale.reshape(1, BRANCH))


def _split_dot(x, u):
    hi = x.astype(bf16)
    lo = (x - hi.astype(f32)).astype(bf16)
    return _dot(hi, u) + _dot(lo, u)


def _sb_fwd(sbqkv, name):
    B, S, _ = sbqkv.shape
    TQ = SB_BLOCK
    nq = S // TQ
    scale = HEAD ** -0.5

    def body(q_ref, k_ref, v_ref, o_ref, tot_ref):
        r = lax.broadcasted_iota(jnp.int32, (TQ, TQ), 0)
        c = lax.broadcasted_iota(jnp.int32, (TQ, TQ), 1)
        causal = c < r
        after = (r > c).astype(bf16)

        def blk(qi, j, masked, run, acc):
            rows = pl.ds(pl.multiple_of(j * TQ, TQ), TQ)
            z = _dot(qi, k_ref[0, rows, :], NT) * scale
            sp = _softplus(z)
            ln = -sp
            if masked:
                ln = jnp.where(causal, ln, 0.0)
            w = jnp.exp(z - sp + _split_dot(ln, after) + run)
            if masked:
                w = jnp.where(causal, w, 0.0)
            acc = acc + _dot(w.astype(bf16), v_ref[0, rows, :])
            return run + jnp.sum(ln, axis=1, keepdims=True), acc

        def qloop(i, carry):
            rows = pl.ds(pl.multiple_of(i * TQ, TQ), TQ)
            qi = q_ref[0, rows, :]
            run, acc = blk(qi, i, True, jnp.zeros((TQ, 1), f32), jnp.zeros((TQ, HEAD), f32))
            run, acc = lax.fori_loop(0, i, lambda t, cr: blk(qi, i - 1 - t, False, *cr), (run, acc))
            o_ref[0, rows, :] = acc.astype(bf16)
            tot_ref[0, 0, rows, :] = jnp.broadcast_to(run, (TQ, LANES))
            return carry

        lax.fori_loop(0, nq, qloop, 0)

    def spec(off):
        return pl.BlockSpec((1, S, HEAD), lambda b, h, off=off: (b, 0, off + h))

    return pl.pallas_call(
        body, name=name, grid=(B, N_HEADS), in_specs=[spec(0), spec(N_HEADS), spec(2 * N_HEADS)],
        out_specs=[pl.BlockSpec((1, S, HEAD), lambda b, h: (b, 0, h)), pl.BlockSpec((1, 1, S, LANES), lambda b, h: (b, h, 0, 0))],
        out_shape=[jax.ShapeDtypeStruct((B, S, BRANCH), bf16), jax.ShapeDtypeStruct((B, N_HEADS, S, LANES), f32)],
        compiler_params=_cp(("parallel", "parallel")),
    )(sbqkv, sbqkv, sbqkv)


def _sb_bwd(sbqkv, do, tot, name):
    B, S, _ = sbqkv.shape
    TQ = SB_BLOCK
    nq = S // TQ
    scale = HEAD ** -0.5

    def body(q_ref, k_ref, v_ref, do_ref, tot_ref, dq_ref, dk_ref, dv_ref, dk_acc, dv_acc):
        r = lax.broadcasted_iota(jnp.int32, (TQ, TQ), 0)
        c = lax.broadcasted_iota(jnp.int32, (TQ, TQ), 1)
        causal = c < r
        after = (r > c).astype(bf16)
        before = (r < c).astype(bf16)
        dk_acc[...] = jnp.zeros_like(dk_acc)
        dv_acc[...] = jnp.zeros_like(dv_acc)

        def blk(qi, doi, total, j, masked, cl, cp, dq):
            rows = pl.ds(pl.multiple_of(j * TQ, TQ), TQ)
            kj = k_ref[0, rows, :]
            vj = v_ref[0, rows, :]
            z = _dot(qi, kj, NT) * scale
            sp = _softplus(z)
            ln = -sp
            if masked:
                ln = jnp.where(causal, ln, 0.0)
            bs = jnp.sum(ln, axis=1, keepdims=True)
            w = jnp.exp(z - sp + _split_dot(ln, after) + (total - cl - bs))
            if masked:
                w = jnp.where(causal, w, 0.0)
            p = _dot(doi, vj, NT) * w
            qsum = cp + _split_dot(p, before)
            sig = jnp.exp(z - sp)
            dz = (p * (1.0 - sig) - qsum * sig) * scale
            if masked:
                dz = jnp.where(causal, dz, 0.0)
            dzb = dz.astype(bf16)
            dq = dq + _dot(dzb, kj)
            dk_acc[rows, :] += _dot(dzb, qi, TN)
            dv_acc[rows, :] += _dot(w.astype(bf16), doi, TN)
            return cl + bs, cp + jnp.sum(p, axis=1, keepdims=True), dq

        def qloop(i, carry):
            rows = pl.ds(pl.multiple_of(i * TQ, TQ), TQ)
            qi = q_ref[0, rows, :]
            doi = do_ref[0, rows, :]
            total = tot_ref[0, 0, rows, :][:, 0:1]
            zero = jnp.zeros((TQ, 1), f32)
            st = lax.fori_loop(0, i, lambda j, s: blk(qi, doi, total, j, False, *s), (zero, zero, jnp.zeros((TQ, HEAD), f32)))
            _, _, dq = blk(qi, doi, total, i, True, *st)
            dq_ref[0, rows, :] = dq.astype(bf16)
            return carry

        lax.fori_loop(0, nq, qloop, 0)
        dk_ref[0] = dk_acc[...].astype(bf16)
        dv_ref[0] = dv_acc[...].astype(bf16)

    def spec(off):
        return pl.BlockSpec((1, S, HEAD), lambda b, h, off=off: (b, 0, off + h))

    o = jax.ShapeDtypeStruct((B, S, BRANCH), bf16)
    return pl.pallas_call(
        body, name=name, grid=(B, N_HEADS),
        in_specs=[spec(0), spec(N_HEADS), spec(2 * N_HEADS), spec(0), pl.BlockSpec((1, 1, S, LANES), lambda b, h: (b, h, 0, 0))],
        out_specs=[spec(0), spec(0), spec(0)], out_shape=[o, o, o],
        scratch_shapes=[pltpu.VMEM((S, HEAD), f32), pltpu.VMEM((S, HEAD), f32)],
        compiler_params=_cp(("parallel", "parallel")),
    )(sbqkv, sbqkv, sbqkv, do, tot)


def _dn_params(a_log, dt_bias):
    p = jnp.zeros((8, LANES), f32)
    p = p.at[0, :N_HEADS].set(a_log)
    return p.at[1, :N_HEADS].set(dt_bias)


def _dn_prep(ab, par, name):
    B, S, _ = ab.shape
    R = 2 * CHUNK
    nt = S // R

    def body(ab_ref, par_ref, gcb_ref, bb_ref, gcr_ref):
        x = ab_ref[0]
        g = -jnp.exp(par_ref[0:1, :]) * _softplus(x + par_ref[1:2, :])
        r = lax.broadcasted_iota(jnp.int32, (R, R), 0)
        c = lax.broadcasted_iota(jnp.int32, (R, R), 1)
        tri = ((r >= c) & ((r >> 6) == (c >> 6))).astype(f32)
        cs = jnp.dot(tri, g, precision=HI, preferred_element_type=f32)
        beta = _sigmoid(x)
        cst = cs.T
        for h in range(N_HEADS):
            gcb_ref[0, h] = jnp.broadcast_to(cs[:, h:h + 1], (R, LANES))
            bb_ref[0, h] = jnp.broadcast_to(beta[:, N_HEADS + h:N_HEADS + h + 1], (R, LANES))
            gcr_ref[0, h, 0] = jnp.broadcast_to(cst[h:h + 1, 0:CHUNK], (8, CHUNK))
            gcr_ref[0, h, 1] = jnp.broadcast_to(cst[h:h + 1, CHUNK:R], (8, CHUNK))

    return pl.pallas_call(
        body, name=name, grid=(B, nt),
        in_specs=[pl.BlockSpec((1, R, LANES), lambda b, i: (b, i, 0)), pl.BlockSpec((8, LANES), lambda b, i: (0, 0))],
        out_specs=[pl.BlockSpec((1, N_HEADS, R, LANES), lambda b, i: (b, 0, i, 0)),
                   pl.BlockSpec((1, N_HEADS, R, LANES), lambda b, i: (b, 0, i, 0)),
                   pl.BlockSpec((1, N_HEADS, 2, 8, CHUNK), lambda b, i: (b, 0, i, 0, 0))],
        out_shape=[jax.ShapeDtypeStruct((B, N_HEADS, S, LANES), f32), jax.ShapeDtypeStruct((B, N_HEADS, S, LANES), f32),
                   jax.ShapeDtypeStruct((B, N_HEADS, S // CHUNK, 8, CHUNK), f32)],
        compiler_params=_cp(("parallel", "parallel")),
    )(ab, par)


def _bmm(a, b, prec=None):
    return jnp.einsum("nij,njk->nik", a, b, preferred_element_type=f32, precision=prec)


def _bmm_nt(a, b, prec=None):
    return jnp.einsum("nik,njk->nij", a, b, preferred_element_type=f32, precision=prec)


def _bmm_tn(a, b, prec=None):
    return jnp.einsum("nki,nkj->nij", a, b, preferred_element_type=f32, precision=prec)


def _tri_inv(L):
    C = L.shape[-1]
    r = lax.broadcasted_iota(jnp.int32, (C, C), 0)
    c = lax.broadcasted_iota(jnp.int32, (C, C), 1)
    eye = (r == c).astype(f32)
    bd16 = (r >> 4) == (c >> 4)
    bd32 = (r >> 5) == (c >> 5)
    mm = functools.partial(_bmm, prec=HI)
    n1 = -jnp.where(bd16, L, 0.0)
    n2 = mm(n1, n1)
    n4 = mm(n2, n2)
    n8 = mm(n4, n4)
    t = mm(mm(mm(eye + n1, eye + n2), eye + n4), eye + n8)
    t = t - mm(mm(t, jnp.where(bd32 & jnp.logical_not(bd16), L, 0.0)), t)
    t = t - mm(mm(t, jnp.where(bd32, 0.0, L)), t)
    return t


def _conv_silu(x, w, row):
    c = w[3:4] * x + w[2:3] * _shift_down(x, 1, row) + w[1:2] * _shift_down(x, 2, row) + w[0:1] * _shift_down(x, 3, row)
    return c, c * _sigmoid(c)


def _dn_intra(qn, kn, v, gcb, beta, gr):
    C = CHUNK
    r = lax.broadcasted_iota(jnp.int32, (C, C), 0)
    c = lax.broadcasted_iota(jnp.int32, (C, C), 1)
    incl = r >= c
    diff = gcb[:, :, :C] - gr
    dm = jnp.where(incl, jnp.exp(jnp.where(incl, diff, 0.0)), 0.0)
    ds = jnp.where(r > c, dm, 0.0)
    kb = kn * beta
    knb = kn.astype(bf16)
    L = _bmm_nt(kb.astype(bf16), knb) * ds
    eg = jnp.exp(gcb)
    a = _bmm_nt(qn.astype(bf16), knb) * dm
    gl = gcb[:, C - 1:C, :]
    ekd = jnp.exp(gl - gcb)
    return dict(dm=dm, ds=ds, kb=kb, L=L, eg=eg, rhs_u=v * beta, rhs_w=kb * eg, a=a,
                qd=qn * eg, kd=kn * ekd, ekd=ekd, cd=jnp.exp(gl))


def _dn_specs(S):
    def col(off):
        return pl.BlockSpec((1, S, HEAD), lambda b, h, off=off: (b, 0, off + h))

    def cw(off):
        return pl.BlockSpec((4, HEAD), lambda b, h, off=off: (0, off + h))

    per_head = pl.BlockSpec((1, 1, S, LANES), lambda b, h: (b, h, 0, 0))
    rowform = pl.BlockSpec((1, 1, S // CHUNK, 8, CHUNK), lambda b, h: (b, h, 0, 0, 0))
    gain = pl.BlockSpec((1, HEAD), lambda b, h: (0, 0))
    ins = [col(4), col(8), col(12), col(16), cw(0), cw(4), cw(8), per_head, per_head, rowform, gain]
    return ins, per_head


def _dn_act(x_ref, cw_ref, row, normalise, out_scale=1.0):
    _, act = _conv_silu(x_ref[0].astype(f32), cw_ref[...], row)
    if normalise:
        act = act * (lax.rsqrt(jnp.sum(act * act, axis=-1, keepdims=True) + EPS) * out_scale)
    return act


def _dn_group(qn_s, kn_s, v_s, gcb_ref, bb_ref, gcr_ref, g, ng):
    C = CHUNK
    rows = pl.ds(pl.multiple_of(g * (ng * C), ng * C), ng * C)
    ch = pl.ds(g * ng, ng)
    sh = (ng, C, HEAD)
    qn, kn, v = qn_s[rows, :].reshape(sh), kn_s[rows, :].reshape(sh), v_s[rows, :].reshape(sh)
    gcb3, beta = gcb_ref[0, 0, rows, :].reshape(sh), bb_ref[0, 0, rows, :].reshape(sh)
    gr = gcr_ref[0, 0, ch][:, 0:1, :]
    it = _dn_intra(qn, kn, v, gcb3, beta, gr)
    it.update(qn=qn, kn=kn, v=v, gcb=gcb3, beta=beta, gr=gr)
    return rows, ch, it


def _dn_fwd(proj, conv_w, gcb, betab, gcr, gain, name):
    B, S, _ = proj.shape
    n, C = S // CHUNK, CHUNK
    ng = min(8, n)
    ins, per_head = _dn_specs(S)

    def body(q_ref, k_ref, v_ref, z_ref, cq_ref, ck_ref, cv_ref, gcb_ref, bb_ref, gcr_ref, gain_ref,
             y_ref, st_ref, vn_ref, qn_s, kn_s, v_s, u_s, w_s, qd_s, kd_s, a_s, cd_s, o_s):
        row = lax.broadcasted_iota(jnp.int32, (S, HEAD), 0)
        qn_s[...] = _dn_act(q_ref, cq_ref, row, True, HEAD ** -0.5)
        kn_s[...] = _dn_act(k_ref, ck_ref, row, True)
        v_s[...] = _dn_act(v_ref, cv_ref, row, False)

        def group(g, carry):
            _, ch, it = _dn_group(qn_s, kn_s, v_s, gcb_ref, bb_ref, gcr_ref, g, ng)
            t = _tri_inv(it["L"])
            u_s[ch] = _bmm(t, it["rhs_u"], HI)
            w_s[ch] = _bmm(t, it["rhs_w"], HI).astype(bf16)
            qd_s[ch] = it["qd"].astype(bf16)
            kd_s[ch] = it["kd"].astype(bf16)
            a_s[ch] = it["a"].astype(bf16)
            cd_s[ch] = it["cd"]
            return carry

        lax.fori_loop(0, n // ng, group, 0)

        def step(i, st):
            sb = st.astype(bf16)
            st_ref[0, 0, i] = sb
            vn = (u_s[i] - _dot(w_s[i], sb)).astype(bf16)
            vn_ref[0, 0, pl.ds(pl.multiple_of(i * C, C), C), :] = vn
            o_s[i] = _dot(qd_s[i], sb) + _dot(a_s[i], vn)
            return st * cd_s[i] + _dot(kd_s[i], vn, TN)

        lax.fori_loop(0, n, step, jnp.zeros((HEAD, HEAD), f32))
        o = o_s[...].reshape(S, HEAD)
        zz = z_ref[0].astype(f32)
        rr = lax.rsqrt(jnp.mean(o * o, axis=-1, keepdims=True) + EPS)
        y_ref[0] = (o * rr * gain_ref[...] * (zz * _sigmoid(zz))).astype(bf16)

    seq = pltpu.VMEM((S, HEAD), f32)
    return pl.pallas_call(
        body, name=name, grid=(B, N_HEADS), in_specs=ins,
        out_specs=[pl.BlockSpec((1, S, HEAD), lambda b, h: (b, 0, h)),
                   pl.BlockSpec((1, 1, n, HEAD, HEAD), lambda b, h: (b, h, 0, 0, 0)), per_head],
        out_shape=[jax.ShapeDtypeStruct((B, S, BRANCH), bf16), jax.ShapeDtypeStruct((B, N_HEADS, n, HEAD, HEAD), bf16),
                   jax.ShapeDtypeStruct((B, N_HEADS, S, HEAD), bf16)],
        scratch_shapes=[seq, seq, seq, pltpu.VMEM((n, C, HEAD), f32), pltpu.VMEM((n, C, HEAD), bf16),
                        pltpu.VMEM((n, C, HEAD), bf16), pltpu.VMEM((n, C, HEAD), bf16), pltpu.VMEM((n, C, C), bf16),
                        pltpu.VMEM((n, 1, HEAD), f32), pltpu.VMEM((n, C, HEAD), f32)],
        compiler_params=_cp(("parallel", "parallel")),
    )(proj, proj, proj, proj, conv_w, conv_w, conv_w, gcb, betab, gcr, gain.reshape(1, HEAD))


def _rowsum(x):
    return jnp.sum(x, axis=-1, keepdims=True)


def _dn_bwd(proj, dy, conv_w, gcb, betab, gcr, gain, states, vnew, name):
    B, S, _ = proj.shape
    n, C = S // CHUNK, CHUNK
    ng = min(8, n)
    ins, per_head = _dn_specs(S)
    ins = ins + [pl.BlockSpec((1, S, HEAD), lambda b, h: (b, 0, h)),
                 pl.BlockSpec((1, 1, n, HEAD, HEAD), lambda b, h: (b, h, 0, 0, 0)), per_head]

    def body(q_ref, k_ref, v_ref, z_ref, cq_ref, ck_ref, cv_ref, gcb_ref, bb_ref, gcr_ref, gain_ref, dy_ref, st_ref, vn_ref,
             dq_ref, dk_ref, dv_ref, dz_ref, dg_ref, dbeta_ref, dconv_ref, dgain_ref,
             qn_s, kn_s, v_s, t_s, u_s, at_s, kd_s, qd_s, w_s, cd_s, do_s, dsp_s, dvn_s):
        row = lax.broadcasted_iota(jnp.int32, (S, HEAD), 0)
        qn_s[...] = _dn_act(q_ref, cq_ref, row, True, HEAD ** -0.5)
        kn_s[...] = _dn_act(k_ref, ck_ref, row, True)
        v_s[...] = _dn_act(v_ref, cv_ref, row, False)
        dgain_ref[...] = jnp.zeros_like(dgain_ref)
        gv = gain_ref[...]

        def group_fwd(g, carry):
            rows, ch, it = _dn_group(qn_s, kn_s, v_s, gcb_ref, bb_ref, gcr_ref, g, ng)
            t = _tri_inv(it["L"])
            ub = _bmm(t, it["rhs_u"], HI).astype(bf16)
            wb = _bmm(t, it["rhs_w"], HI).astype(bf16)
            ab, qdb = it["a"].astype(bf16), it["qd"].astype(bf16)
            vn = vn_ref[0, 0, rows, :].reshape(ng, C, HEAD)
            o = (_bmm(qdb, st_ref[0, 0, ch]) + _bmm(ab, vn)).reshape(ng * C, HEAD)
            zz = z_ref[0, rows, :].astype(f32)
            dyv = dy_ref[0, rows, :].astype(f32)
            rr = lax.rsqrt(jnp.mean(o * o, axis=-1, keepdims=True) + EPS)
            on = o * rr
            sz = _sigmoid(zz)
            dz_ref[0, rows, :] = (dyv * on * gv * (sz * (1.0 + zz * (1.0 - sz)))).astype(bf16)
            dnrm = dyv * (zz * sz)
            dgain_ref[0, 0] += _fold8(dnrm * on)
            doh = dnrm * gv
            do = rr * (doh - on * jnp.mean(doh * on, axis=-1, keepdims=True))
            t_s[ch] = t
            u_s[ch] = ub
            w_s[ch] = wb
            at_s[ch] = ab
            qd_s[ch] = qdb
            kd_s[ch] = it["kd"].astype(bf16)
            cd_s[ch] = it["cd"]
            do_s[ch] = do.reshape(ng, C, HEAD).astype(bf16)
            return carry

        lax.fori_loop(0, n // ng, group_fwd, 0)

        def step(t, dsp):
            i = n - 1 - t
            dspb = dsp.astype(bf16)
            dsp_s[i] = dspb
            dvn = _dot(at_s[i], do_s[i], TN) + _dot(kd_s[i], dspb)
            dvn_s[i] = dvn
            return dsp * cd_s[i] + _dot(qd_s[i], do_s[i], TN) - _dot(w_s[i], dvn.astype(bf16), TN)

        lax.fori_loop(0, n, step, jnp.zeros((HEAD, HEAD), f32))

        r = lax.broadcasted_iota(jnp.int32, (C, C), 0)
        c = lax.broadcasted_iota(jnp.int32, (C, C), 1)
        upper = r <= c

        def group_bwd(g, carry):
            rows, ch, it = _dn_group(qn_s, kn_s, v_s, gcb_ref, bb_ref, gcr_ref, g, ng)
            sh = (ng, C, HEAD)
            qn, kn, v, beta, gcb3, gr = it["qn"], it["kn"], it["v"], it["beta"], it["gcb"], it["gr"]
            sn = st_ref[0, 0, ch]
            vn = vn_ref[0, 0, rows, :].reshape(sh)
            dsp, dvn, dob = dsp_s[ch], dvn_s[ch], do_s[ch]
            t, ub, wb = t_s[ch], u_s[ch], w_s[ch]
            dvnb = dvn.astype(bf16)
            da = _bmm_nt(dob, vn)
            dat = _bmm_nt(vn, dob)
            dqd = _bmm_nt(dob, sn)
            dkd = _bmm_nt(vn, dsp)
            dcd = jnp.sum(jnp.sum(dsp.astype(f32) * sn.astype(f32), axis=2, keepdims=True), axis=1, keepdims=True)
            dw = -_bmm_nt(dvnb, sn)
            ru = _bmm_tn(t, dvn, HI)
            rw = _bmm_tn(t, dw, HI)
            rub, rwb = ru.astype(bf16), rw.astype(bf16)
            dL = -(_bmm_nt(rub, ub) + _bmm_nt(rwb, wb))
            dLt = -(_bmm_nt(ub, rub) + _bmm_nt(wb, rwb))
            knb, qnb, kbb = kn.astype(bf16), qn.astype(bf16), it["kb"].astype(bf16)
            dmt = jnp.where(upper, jnp.exp(jnp.where(upper, gr - gcb3[:, :, :C], 0.0)), 0.0)
            Lt = _bmm_nt(knb, kbb) * jnp.where(r < c, dmt, 0.0)
            At = _bmm_nt(knb, qnb) * dmt
            dgc = _rowsum(dL * it["L"] + da * it["a"]) - _rowsum(dLt * Lt + dat * At)
            dkk = (dL * it["ds"]).astype(bf16)
            dqk = (da * it["dm"]).astype(bf16)
            dkb = _bmm(dkk, knb) + rw * it["eg"]
            dkn = _bmm_tn(dkk, kbb) + _bmm_tn(dqk, qnb) + dkd * it["ekd"] + dkb * beta
            dqn = _bmm(dqk, knb) + dqd * it["eg"]
            tkd = _rowsum(dkd * it["kd"])
            dgl = jnp.sum(tkd, axis=1, keepdims=True) + dcd * it["cd"][:, :, 0:1]
            dgc = dgc + _rowsum(dqd * it["qd"]) - tkd + _rowsum(rw * it["rhs_w"])
            dbeta = _rowsum(ru * v) + _rowsum(dkb * kn)
            rowc = lax.broadcasted_iota(jnp.int32, (ng, C, 1), 1)
            dgc = dgc + jnp.where(rowc == C - 1, dgl, 0.0)
            rev = jnp.broadcast_to(upper.astype(f32), (ng, C, C))
            dg_ref[0, 0, rows, :] = _bmm(rev, jnp.broadcast_to(dgc, sh), HI).reshape(ng * C, LANES).astype(bf16)
            dbeta_ref[0, 0, rows, :] = jnp.broadcast_to(dbeta, sh).reshape(ng * C, LANES).astype(bf16)
            qn_s[rows, :] = dqn.reshape(ng * C, HEAD)
            kn_s[rows, :] = dkn.reshape(ng * C, HEAD)
            v_s[rows, :] = (ru * beta).reshape(ng * C, HEAD)
            return carry

        lax.fori_loop(0, n // ng, group_bwd, 0)

        def conv_back(x_ref, cw_ref, grad_s, out_ref, slot, normalise, out_scale):
            x = x_ref[0].astype(f32)
            w = cw_ref[...]
            pre, act = _conv_silu(x, w, row)
            dact = grad_s[...]
            if normalise:
                rn = lax.rsqrt(jnp.sum(act * act, axis=-1, keepdims=True) + EPS)
                unit = act * rn
                dact = (out_scale * rn) * (dact - unit * _rowsum(dact * unit))
            s = _sigmoid(pre)
            dc = dact * (s * (1.0 + pre * (1.0 - s)))
            out_ref[0] = (w[3:4] * dc + w[2:3] * _shift_up(dc, 1, row) + w[1:2] * _shift_up(dc, 2, row)
                          + w[0:1] * _shift_up(dc, 3, row)).astype(bf16)
            for tap in range(4):
                xs = x if tap == 3 else _shift_down(x, 3 - tap, row)
                dconv_ref[0, slot, tap:tap + 1, :] = jnp.sum(dc * xs, axis=0, keepdims=True)

        conv_back(q_ref, cq_ref, qn_s, dq_ref, 0, True, HEAD ** -0.5)
        conv_back(k_ref, ck_ref, kn_s, dk_ref, 1, True, 1.0)
        conv_back(v_ref, cv_ref, v_s, dv_ref, 2, False, 1.0)

    o512 = jax.ShapeDtypeStruct((B, S, BRANCH), bf16)
    s512 = pl.BlockSpec((1, S, HEAD), lambda b, h: (b, 0, h))
    ph = jax.ShapeDtypeStruct((B, N_HEADS, S, LANES), bf16)
    seq = pltpu.VMEM((S, HEAD), f32)
    cb = pltpu.VMEM((n, C, HEAD), bf16)
    return pl.pallas_call(
        body, name=name, grid=(B, N_HEADS), in_specs=ins,
        out_specs=[s512, s512, s512, s512, per_head, per_head,
                   pl.BlockSpec((1, 3, 4, HEAD), lambda b, h: (b, 0, 0, h)), pl.BlockSpec((1, 1, 8, HEAD), lambda b, h: (b, h, 0, 0))],
        out_shape=[o512, o512, o512, o512, ph, ph, jax.ShapeDtypeStruct((B, 3, 4, BRANCH), f32),
                   jax.ShapeDtypeStruct((B, N_HEADS, 8, HEAD), f32)],
        scratch_shapes=[seq, seq, seq, pltpu.VMEM((n, C, C), f32), cb, pltpu.VMEM((n, C, C), bf16), cb, cb, cb,
                        pltpu.VMEM((n, 1, HEAD), f32), cb, pltpu.VMEM((n, HEAD, HEAD), bf16), pltpu.VMEM((n, C, HEAD), f32)],
        compiler_params=_cp(("parallel", "parallel")),
    )(proj, proj, proj, proj, conv_w, conv_w, conv_w, gcb, betab, gcr, gain.reshape(1, HEAD), dy, states, vnew)


def _dn_post(ab, par, dg, dbeta, name):
    B, S, _ = ab.shape
    ts = _tile(S, (512, 256, 128))

    def body(ab_ref, par_ref, dg_ref, db_ref, dab_ref, acc_ref):
        x = ab_ref[0]
        lane = lax.broadcasted_iota(jnp.int32, x.shape, 1)
        dgs = jnp.zeros_like(x)
        dbs = jnp.zeros_like(x)
        for h in range(N_HEADS):
            dgs = jnp.where(lane == h, dg_ref[0, h], dgs)
            dbs = jnp.where(lane == N_HEADS + h, db_ref[0, h], dbs)
        nega = -jnp.exp(par_ref[0:1, :])
        pre = x + par_ref[1:2, :]
        da = dgs * nega * _sigmoid(pre)
        beta = _sigmoid(x)
        dab_ref[0] = (da + dbs * beta * (1.0 - beta)).astype(bf16)

        @pl.when((pl.program_id(0) == 0) & (pl.program_id(1) == 0))
        def _():
            acc_ref[...] = jnp.zeros_like(acc_ref)

        acc_ref[0] += _fold8(dgs * nega * _softplus(pre))
        acc_ref[1] += _fold8(da)

    return pl.pallas_call(
        body, name=name, grid=(B, S // ts),
        in_specs=[pl.BlockSpec((1, ts, LANES), lambda b, i: (b, i, 0)), pl.BlockSpec((8, LANES), lambda b, i: (0, 0)),
                  pl.BlockSpec((1, N_HEADS, ts, LANES), lambda b, i: (b, 0, i, 0)),
                  pl.BlockSpec((1, N_HEADS, ts, LANES), lambda b, i: (b, 0, i, 0))],
        out_specs=[pl.BlockSpec((1, ts, LANES), lambda b, i: (b, i, 0)), pl.BlockSpec((2, 8, LANES), lambda b, i: (0, 0, 0))],
        out_shape=[jax.ShapeDtypeStruct((B, S, LANES), bf16), jax.ShapeDtypeStruct((2, 8, LANES), f32)],
        compiler_params=_cp(("arbitrary", "arbitrary")),
    )(ab, par, dg, dbeta)


def _merge_specs(T, D, tm, tn, order):
    nj = D // tn

    def ij(f):
        return (lambda i, j: f(i, j)) if order == "ij" else (lambda j, i: f(i, j))

    ys = [pl.BlockSpec((tm, BRANCH), ij(lambda i, j: (i, 0))) for _ in range(3)]
    wb = pl.BlockSpec((3, BRANCH, tn), ij(lambda i, j: (0, 0, j)))
    gl = [pl.BlockSpec((tm, tn), ij(lambda i, j, k=k: (i, k * nj + j))) for k in range(3)]
    bg = [pl.BlockSpec((1, tn), ij(lambda i, j, k=k: (0, k * nj + j))) for k in range(3)]
    return ys, wb, gl, bg


def _merge_fwd(ys, wb, gl, b_gate, name):
    T, D = ys[0].shape[0], wb.shape[2]
    tm, tn = _tile(T, (512, 256, 128)), _tile(D, (512, 256, 128))
    sy, swb, sgl, sbg = _merge_specs(T, D, tm, tn, "ij")

    def body(y0, y1, y2, wb_ref, g0, g1, g2, b0, b1, b2, o_ref):
        acc = None
        for k, (y, g, b) in enumerate(((y0, g0, b0), (y1, g1, b1), (y2, g2, b2))):
            term = _sigmoid(g[...].astype(f32) + b[...]) * _dot(y[...], wb_ref[k])
            acc = term if acc is None else acc + term
        o_ref[...] = acc.astype(bf16)

    bg = b_gate.reshape(1, 3 * D)
    return pl.pallas_call(
        body, name=name, grid=(T // tm, D // tn), in_specs=sy + [swb] + sgl + sbg,
        out_specs=pl.BlockSpec((tm, tn), lambda i, j: (i, j)), out_shape=jax.ShapeDtypeStruct((T, D), bf16),
        compiler_params=_cp(("parallel", "parallel")),
    )(*ys, wb, gl, gl, gl, bg, bg, bg)


def _merge_bwd(dm, ys, wb, gl, b_gate, name):
    T, D = dm.shape
    tm, tn = _tile(T, (512, 256, 128)), _tile(D, (512, 256, 128))
    sy, swb, sgl, sbg = _merge_specs(T, D, tm, tn, "ji")

    def body(dm_ref, y0, y1, y2, wb_ref, g0, g1, g2, b0, b1, b2, dgl_ref, dbd_ref, dbg_ref):
        dmv = dm_ref[...].astype(f32)

        @pl.when(pl.program_id(1) == 0)
        def _():
            dbg_ref[...] = jnp.zeros_like(dbg_ref)

        for k, (y, g, b) in enumerate(((y0, g0, b0), (y1, g1, b1), (y2, g2, b2))):
            s = _sigmoid(g[...].astype(f32) + b[...])
            dg = dmv * _dot(y[...], wb_ref[k]) * s * (1.0 - s)
            dgl_ref[k] = dg.astype(bf16)
            dbd_ref[k] = (dmv * s).astype(bf16)
            dbg_ref[k] += _fold8(dg)

    bg = b_gate.reshape(1, 3 * D)
    o3 = jax.ShapeDtypeStruct((3, T, D), bf16)
    s3 = pl.BlockSpec((3, tm, tn), lambda j, i: (0, i, j))
    return pl.pallas_call(
        body, name=name, grid=(D // tn, T // tm),
        in_specs=[pl.BlockSpec((tm, tn), lambda j, i: (i, j))] + sy + [swb] + sgl + sbg,
        out_specs=[s3, s3, pl.BlockSpec((3, 8, tn), lambda j, i: (0, 0, j))],
        out_shape=[o3, o3, jax.ShapeDtypeStruct((3, 8, D), f32)],
        compiler_params=_cp(("parallel", "arbitrary")),
    )(dm, *ys, wb, gl, gl, gl, bg, bg, bg)


def _ffn_fwd(x, g, wg, wu, wd, tag):
    T, D = x.shape
    h = _rms_fwd(x, g, f"rms_{tag}")
    a, b, hm = _ffn_up(h, wg, wu, f"ffn_up_{tag}")
    y = _mm([("nn", hm, wd, {})], T, D, f32, f"ffn_down_{tag}", res=x, scale=0.5)
    return y, (x, h, a, b)


def _ffn_bwd(dy, saved, g, wg, wu, wd, tag):
    x, h, a, b = saved
    T, D = x.shape
    F = wg.shape[1]
    da, db, hm = _ffn_bwd_mid(dy, wd, a, b, f"ffn_mid_bwd_{tag}")
    dwd = _mm([("tn", hm, dy, {})], F, D, bf16, f"ffn_dwd_{tag}", scale=0.5)
    dwg = _mm([("tn", h, da, {})], D, F, bf16, f"ffn_dwg_{tag}")
    dwu = _mm([("tn", h, db, {})], D, F, bf16, f"ffn_dwu_{tag}")
    dh = _mm([("nt", da, wg, {}), ("nt", db, wu, {})], T, D, f32, f"ffn_dh_{tag}", tm=_tile(T, (256, 128)))
    dx, dg8 = _rms_bwd(dh, x, g, dy, f"rms_bwd_{tag}")
    return dx, dict(norm=jnp.sum(dg8, axis=0), wg=dwg, wu=dwu, wd=dwd)


def _layer_fwd(x, w, B, tag):
    T, D = x.shape
    S = T // B
    x1, sv0 = _ffn_fwd(x, w["ffn_norm"][0], w["wg"][0], w["wu"][0], w["wd"][0], f"pre_{tag}")
    h = _rms_fwd(x1, w["mix_norm"], f"rms_mix_{tag}")
    pm = _mm([("nn", h, w["w_main"], {})], T, 5 * BRANCH, bf16, f"proj_main_{tag}")
    ab = _mm([("nn", h, w["w_ab"], {})], T, LANES, f32, f"proj_ab_{tag}", tn=LANES)
    sb = _mm([("nn", h, w["w_sb"], {})], T, 3 * BRANCH, bf16, f"proj_sb_{tag}")
    gl = _mm([("nn", h, w["w_gates"], {})], T, 3 * D, bf16, f"proj_gates_{tag}")
    pm3, ab3, sb3 = pm.reshape(B, S, -1), ab.reshape(B, S, LANES), sb.reshape(B, S, -1)
    y_pool = _pool_fwd(pm3, w["pool_w"], w["pool_scale"], f"pool_{tag}")
    par = _dn_params(w["dn_A_log"], w["dn_dt_bias"])
    gcb, betab, gcr = _dn_prep(ab3, par, f"dn_prep_{tag}")
    y_dn, states, vnew = _dn_fwd(pm3, w["dn_conv"], gcb, betab, gcr, w["dn_out_norm"], f"dn_fwd_{tag}")
    y_sb, tot = _sb_fwd(sb3, f"sb_fwd_{tag}")
    ys = [y_pool.reshape(T, BRANCH), y_dn.reshape(T, BRANCH), y_sb.reshape(T, BRANCH)]
    merged = _merge_fwd(ys, w["w_branch"], gl, w["b_gate"], f"merge_{tag}")
    x2 = _mm([("nn", merged, w["w_out"], {})], T, D, f32, f"mix_out_{tag}", res=x1)
    x3, sv1 = _ffn_fwd(x2, w["ffn_norm"][1], w["wg"][1], w["wu"][1], w["wd"][1], f"post_{tag}")
    saved = dict(sv0=sv0, sv1=sv1, x1=x1, h=h, pm3=pm3, ab3=ab3, sb3=sb3, gl=gl, par=par, gcb=gcb, betab=betab, gcr=gcr,
                 states=states, vnew=vnew, tot=tot, ys=ys, merged=merged)
    return x3, saved


def _layer_bwd(dx3, w, sv, B, tag):
    T, D = dx3.shape
    S = T // B
    dx2, g1 = _ffn_bwd(dx3, sv["sv1"], w["ffn_norm"][1], w["wg"][1], w["wu"][1], w["wd"][1], f"post_{tag}")
    dmerged = _mm([("nt", dx2, w["w_out"], {})], T, D, bf16, f"mix_dmerged_{tag}")
    dw_out = _mm([("tn", sv["merged"], dx2, {})], D, D, bf16, f"mix_dwout_{tag}")
    ys = sv["ys"]
    dgl, dbd, dbg8 = _merge_bwd(dmerged, ys, w["w_branch"], sv["gl"], w["b_gate"], f"merge_bwd_{tag}")
    dys, dwb = [], []
    for k in range(3):
        dys.append(_mm([("nt", dbd, w["w_branch"], {"asel": k, "bsel": k})], T, BRANCH, bf16, f"branch_dy{k}_{tag}"))
        dwb.append(_mm([("tn", ys[k], dbd, {"bsel": k})], BRANCH, D, bf16, f"branch_dw{k}_{tag}"))
    pm3, ab3, sb3 = sv["pm3"], sv["ab3"], sv["sb3"]
    du, dpool_w, dsc8 = _pool_bwd(pm3, dys[0].reshape(B, S, BRANCH), w["pool_w"], w["pool_scale"], f"pool_bwd_{tag}")
    dq, dk, dv, dz, dg, dbeta, dconv, dgain = _dn_bwd(
        pm3, dys[1].reshape(B, S, BRANCH), w["dn_conv"], sv["gcb"], sv["betab"], sv["gcr"], w["dn_out_norm"],
        sv["states"], sv["vnew"], f"dn_bwd_{tag}")
    dab, dn_acc = _dn_post(ab3, sv["par"], dg, dbeta, f"dn_post_{tag}")
    dsq, dsk, dsv = _sb_bwd(sb3, dys[2].reshape(B, S, BRANCH), sv["tot"], f"sb_bwd_{tag}")
    main_parts = [t.reshape(T, BRANCH) for t in (du, dq, dk, dv, dz)]
    sb_parts = [t.reshape(T, BRANCH) for t in (dsq, dsk, dsv)]
    dab2 = dab.reshape(T, LANES)
    pairs = [("nt", t, w["w_main"], {"K": BRANCH, "kb": k}) for k, t in enumerate(main_parts)]
    pairs.append(("nt", dab2, w["w_ab"], {}))
    pairs += [("nt", t, w["w_sb"], {"K": BRANCH, "kb": k}) for k, t in enumerate(sb_parts)]
    pairs += [("nt", dgl, w["w_gates"], {"K": D, "kb": k, "asel": k}) for k in range(3)]
    dh = _mm(pairs, T, D, f32, f"mix_dh_{tag}", tm=_tile(T, (256, 128)))
    h = sv["h"]
    dw_cols = [_mm([("tn", h, t, {})], D, BRANCH, bf16, f"dwin_main{k}_{tag}") for k, t in enumerate(main_parts)]
    dw_cols.append(_mm([("tn", h, dab2, {})], D, LANES, bf16, f"dwin_ab_{tag}", tn=LANES)[:, :2 * N_HEADS])
    dw_cols += [_mm([("tn", h, t, {})], D, BRANCH, bf16, f"dwin_sb{k}_{tag}") for k, t in enumerate(sb_parts)]
    dw_cols += [_mm([("tn", h, dgl, {"bsel": k})], D, D, bf16, f"dwin_gate{k}_{tag}") for k in range(3)]
    dx1, dmix8 = _rms_bwd(dh, sv["x1"], w["mix_norm"], dx2, f"rms_mix_bwd_{tag}")
    dx0, g0 = _ffn_bwd(dx1, sv["sv0"], w["ffn_norm"][0], w["wg"][0], w["wu"][0], w["wd"][0], f"pre_{tag}")
    grads = dict(
        ffn_norm=jnp.stack([g0["norm"], g1["norm"]]), wg=[g0["wg"], g1["wg"]], wu=[g0["wu"], g1["wu"]], wd=[g0["wd"], g1["wd"]],
        mix_norm=jnp.sum(dmix8, axis=0), w_in=jnp.concatenate(dw_cols, axis=1), b_gate=jnp.sum(dbg8, axis=1).reshape(3 * D),
        pool_w=dpool_w, pool_scale=jnp.sum(dsc8, axis=0), dn_conv=jnp.sum(dconv, axis=0).transpose(1, 0, 2).reshape(4, 3 * BRANCH),
        dn_A_log=jnp.sum(dn_acc[0], axis=0)[:N_HEADS], dn_dt_bias=jnp.sum(dn_acc[1], axis=0)[:N_HEADS],
        dn_out_norm=jnp.sum(dgain, axis=(0, 1, 2)), w_branch=jnp.stack(dwb), w_out=dw_out)
    return dx0, grads


def _local_step(x, target, layers, final_norm, B):
    T, D = x.shape
    saved = []
    for l, w in enumerate(layers):
        x, sv = _layer_fwd(x, w, B, f"l{l}")
        saved.append(sv)
    dx, dfn8, ls8 = _final_loss(x, target, final_norm)
    grads = [None] * len(layers)
    for l in reversed(range(len(layers))):
        dx, grads[l] = _layer_bwd(dx, layers[l], saved[l], B, f"l{l}")
    return jnp.sum(ls8), dx, grads, jnp.sum(dfn8, axis=0)


def _adamw(w, g, m, v, name):
    shape = w.shape
    cols = shape[-1]
    rows = math.prod(shape[:-1]) if len(shape) > 1 else 1
    w2, g2, m2, v2 = (t.reshape(rows, cols) for t in (w, g, m, v))
    block_elems = 256 * 1024
    tr = rows if rows * cols <= block_elems else _tile(rows, [t for t in (512, 256, 128, 64, 32, 16, 8) if t * cols <= block_elems])

    def body(w_ref, g_ref, m_ref, v_ref, d_ref, mo_ref, vo_ref):
        gv = g_ref[...]
        mn = ADAM_B1 * m_ref[...] + (1.0 - ADAM_B1) * gv
        vn = ADAM_B2 * v_ref[...] + (1.0 - ADAM_B2) * (gv * gv)
        m_hat = mn / (1.0 - ADAM_B1 ** ADAM_STEP)
        v_hat = vn / (1.0 - ADAM_B2 ** ADAM_STEP)
        d_ref[...] = -ADAM_LR * (m_hat / (jnp.sqrt(v_hat) + ADAM_EPS) + ADAM_WD * w_ref[...])
        mo_ref[...] = mn
        vo_ref[...] = vn

    spec = pl.BlockSpec((tr, cols), lambda i: (i, 0))
    o = jax.ShapeDtypeStruct((rows, cols), f32)
    d, mo, vo = pl.pallas_call(
        body, name=name, grid=(rows // tr,), in_specs=[spec] * 4, out_specs=[spec] * 3, out_shape=[o, o, o],
        compiler_params=_cp(("parallel",)),
    )(w2, g2, m2, v2)
    return d.reshape(shape), mo.reshape(shape), vo.reshape(shape)


MESH = pl.DeviceIdType.MESH
_ANY = pl.BlockSpec(memory_space=pl.ANY)


def _place():
    x, y, c = lax.axis_index("x"), lax.axis_index("y"), lax.axis_index("c")
    return x, y, c, [(1 - x, y), (x, 1 - y), (1 - x, 1 - y)]


def _half(c, rh):
    return pl.ds(c * rh, rh)


def _remote(src, dst, ssem, rsem, to):
    return pltpu.make_async_remote_copy(src_ref=src, dst_ref=dst, send_sem=ssem, recv_sem=rsem, device_id=to,
                                        device_id_type=MESH)


def _gather_chips(shard, name):
    R, W = shard.shape
    rh = R // 2

    def body(x_ref, out_ref, ssem, rsem, fsend, frecv, lsem):
        x, y, c, chips = _place()
        me = 2 * x + y
        mine = pltpu.make_async_copy(x_ref, out_ref.at[me], lsem)
        mine.start()
        started = []
        for k, (px, py) in enumerate(chips):
            cp = _remote(x_ref.at[_half(c, rh)], out_ref.at[me, _half(c, rh)], ssem.at[k], rsem.at[k], (px, py, c))
            cp.start()
            started.append(cp)
        for k, (px, py) in enumerate(chips):
            block = out_ref.at[2 * px + py, _half(c, rh)]
            _remote(block, block, ssem.at[k], rsem.at[k], (px, py, c)).wait_recv()
            cp = _remote(block, block, fsend.at[k], frecv.at[k], (x, y, 1 - c))
            cp.start()
            started.append(cp)
        for k, (px, py) in enumerate(chips):
            block = out_ref.at[2 * px + py, _half(1 - c, rh)]
            _remote(block, block, fsend.at[k], frecv.at[k], (x, y, 1 - c)).wait_recv()
        for cp in started:
            cp.wait_send()
        mine.wait()

    return pl.pallas_call(
        body, name=name, in_specs=[_ANY], out_specs=_ANY, out_shape=jax.ShapeDtypeStruct((N_CHIPS, R, W), shard.dtype),
        scratch_shapes=[pltpu.SemaphoreType.DMA((3,))] * 4 + [pltpu.SemaphoreType.DMA],
    )(shard)


def _pair_swap_halves(p, name):
    n, R, W = p.shape
    rh = R // 2

    def body(p_ref, out_ref, ssem, rsem):
        x, y, c, _ = _place()
        cp = _remote(p_ref.at[:, _half(1 - c, rh)], out_ref, ssem, rsem, (x, y, 1 - c))
        cp.start()
        cp.wait()

    return pl.pallas_call(
        body, name=name, in_specs=[_ANY], out_specs=_ANY, out_shape=jax.ShapeDtypeStruct((n, rh, W), p.dtype),
        scratch_shapes=[pltpu.SemaphoreType.DMA, pltpu.SemaphoreType.DMA],
    )(p)


def _pair_add(p, got, name):
    n, R, W = p.shape
    rh = R // 2
    tr = _tile(rh, (512, 256, 192, 128, 96, 64, 32, 16, 8))
    nb = rh // tr

    def body(p_ref, g_ref, o_ref):
        o_ref[...] = (p_ref[...].astype(f32) + g_ref[...].astype(f32)).astype(o_ref.dtype)

    return pl.pallas_call(
        body, name=name, grid=(n, nb),
        in_specs=[pl.BlockSpec((1, tr, W), lambda j, i: (j, lax.axis_index("c") * nb + i, 0)),
                  pl.BlockSpec((1, tr, W), lambda j, i: (j, i, 0))],
        out_specs=pl.BlockSpec((1, tr, W), lambda j, i: (j, i, 0)),
        out_shape=jax.ShapeDtypeStruct((n, rh, W), p.dtype), compiler_params=_cp(("parallel", "parallel")),
    )(p, got)


def _scatter_chips(ps, name):
    n, R, W = ps.shape

    def body(p_ref, out_ref, ssem, rsem, lsem):
        x, y, c, chips = _place()
        me = 2 * x + y
        mine = pltpu.make_async_copy(p_ref.at[me], out_ref.at[me], lsem)
        mine.start()
        started = []
        for k, (px, py) in enumerate(chips):
            cp = _remote(p_ref.at[2 * px + py], out_ref.at[me], ssem.at[k], rsem.at[k], (px, py, c))
            cp.start()
            started.append(cp)
        for k, (px, py) in enumerate(chips):
            slot = out_ref.at[2 * px + py]
            _remote(slot, slot, ssem.at[k], rsem.at[k], (px, py, c)).wait_recv()
        for cp in started:
            cp.wait_send()
        mine.wait()

    return pl.pallas_call(
        body, name=name, in_specs=[_ANY], out_specs=_ANY, out_shape=jax.ShapeDtypeStruct((n, R, W), ps.dtype),
        scratch_shapes=[pltpu.SemaphoreType.DMA((3,)), pltpu.SemaphoreType.DMA((3,)), pltpu.SemaphoreType.DMA],
    )(ps)


def _sum_slots(r4, name):
    n, R, W = r4.shape
    tr = _tile(R, (512, 256, 192, 128, 96, 64, 32, 16, 8))

    def body(r_ref, o_ref):
        acc = r_ref[0].astype(f32)
        for k in range(1, n):
            acc = acc + r_ref[k].astype(f32)
        o_ref[...] = acc

    return pl.pallas_call(
        body, name=name, grid=(R // tr,), in_specs=[pl.BlockSpec((n, tr, W), lambda i: (0, i, 0))],
        out_specs=pl.BlockSpec((tr, W), lambda i: (i, 0)), out_shape=jax.ShapeDtypeStruct((R, W), f32),
        compiler_params=_cp(("parallel",)),
    )(r4)


def _pair_share(s, name):
    R, W = s.shape

    def body(s_ref, out_ref, ssem, rsem, lsem):
        x, y, c, _ = _place()
        mine = pltpu.make_async_copy(s_ref, out_ref.at[c], lsem)
        mine.start()
        cp = _remote(s_ref, out_ref.at[c], ssem, rsem, (x, y, 1 - c))
        cp.start()
        _remote(s_ref, out_ref.at[1 - c], ssem, rsem, (x, y, 1 - c)).wait_recv()
        cp.wait_send()
        mine.wait()

    return pl.pallas_call(
        body, name=name, in_specs=[_ANY], out_specs=_ANY, out_shape=jax.ShapeDtypeStruct((2, R, W), s.dtype),
        scratch_shapes=[pltpu.SemaphoreType.DMA, pltpu.SemaphoreType.DMA, pltpu.SemaphoreType.DMA],
    )(s)


def _reduce_to_chips(p, tag):
    n, R, W = p.shape
    got = _pair_swap_halves(p, f"rs_pair_swap_{tag}")
    ps = _pair_add(p, got, f"rs_pair_add_{tag}")
    r4 = _scatter_chips(ps, f"rs_scatter_{tag}")
    s = _sum_slots(r4, f"rs_sum_{tag}")
    return _pair_share(s, f"rs_share_{tag}").reshape(R, W)


W_IN_ROWS_PAD = 16


def _rows(a):
    return a.reshape(-1, PACK_W)


def _pad_rows(a, mult):
    r = (-a.shape[-2]) % mult
    return a if r == 0 else jnp.pad(a, [(0, 0)] * (a.ndim - 2) + [(0, r), (0, 0)])


def _pack_layer_shard(wg, wu, wd, w_in, w_branch, w_out):
    parts = [wg, wu, wd, w_in, w_branch, w_out]
    lead = wg.shape[:-3]
    out = []
    for a, nd in zip(parts, (3, 3, 3, 2, 3, 2)):
        flat = a.reshape(lead + (-1, PACK_W))
        out.append(_pad_rows(flat, W_IN_ROWS_PAD))
    return jnp.concatenate(out, axis=-2)


def _small_pack(pieces):
    flat, offs, r = [], [], 0
    for a in pieces:
        v = a.reshape(-1)
        pad = (-v.shape[0]) % PACK_W
        flat.append(jnp.pad(v, (0, pad)) if pad else v)
        offs.append(r)
        r += (v.shape[0] + pad) // PACK_W
    pack = jnp.concatenate(flat).reshape(r, PACK_W)
    return _pad_rows(pack, 16), offs


def _small_unpack(pack, offs, shapes):
    out = []
    for o, s in zip(offs, shapes):
        n = math.prod(s)
        rows = -(-n // PACK_W)
        out.append(pack[o:o + rows].reshape(-1)[:n].reshape(s))
    return out


SMALL_SHARDED = ("ffn_norm", "dn_conv")
SMALL_REPLICATED = ("mix_norm", "b_gate", "pool_w", "pool_scale", "dn_A_log", "dn_dt_bias", "dn_out_norm")


def kernel(x, ffn_norm, ffn_w_gate, ffn_w_up, ffn_w_down, mix_norm, w_in, b_gate, pool_w, pool_scale, dn_conv, dn_A_log, dn_dt_bias, dn_out_norm, w_branch, w_out, final_norm, loss_target, m_ffn_norm, m_ffn_w_gate, m_ffn_w_up, m_ffn_w_down, m_mix_norm, m_w_in, m_b_gate, m_pool_w, m_pool_scale, m_dn_conv, m_dn_A_log, m_dn_dt_bias, m_dn_out_norm, m_w_branch, m_w_out, m_final_norm, v_ffn_norm, v_ffn_w_gate, v_ffn_w_up, v_ffn_w_down, v_mix_norm, v_w_in, v_b_gate, v_pool_w, v_pool_scale, v_dn_conv, v_dn_A_log, v_dn_dt_bias, v_dn_out_norm, v_w_branch, v_w_out, v_final_norm):
    names = ("ffn_norm", "ffn_w_gate", "ffn_w_up", "ffn_w_down", "mix_norm", "w_in", "b_gate", "pool_w", "pool_scale", "dn_conv",
             "dn_A_log", "dn_dt_bias", "dn_out_norm", "w_branch", "w_out", "final_norm")
    wts = dict(zip(names, (ffn_norm, ffn_w_gate, ffn_w_up, ffn_w_down, mix_norm, w_in, b_gate, pool_w, pool_scale, dn_conv,
                           dn_A_log, dn_dt_bias, dn_out_norm, w_branch, w_out, final_norm)))
    ms = dict(zip(names, (m_ffn_norm, m_ffn_w_gate, m_ffn_w_up, m_ffn_w_down, m_mix_norm, m_w_in, m_b_gate, m_pool_w, m_pool_scale,
                          m_dn_conv, m_dn_A_log, m_dn_dt_bias, m_dn_out_norm, m_w_branch, m_w_out, m_final_norm)))
    vs = dict(zip(names, (v_ffn_norm, v_ffn_w_gate, v_ffn_w_up, v_ffn_w_down, v_mix_norm, v_w_in, v_b_gate, v_pool_w, v_pool_scale,
                          v_dn_conv, v_dn_A_log, v_dn_dt_bias, v_dn_out_norm, v_w_branch, v_w_out, v_final_norm)))
    B, S, D = x.shape
    T = B * S
    L = ffn_w_gate.shape[0]
    F = ffn_w_gate.shape[3] * N_CHIPS
    fs, ds_, cs = F // N_CHIPS, D // N_CHIPS, dn_conv.shape[2]
    p_in = w_in.shape[2] * N_CHIPS
    chip = 2 * lax.axis_index("x") + lax.axis_index("y")

    big = jnp.concatenate(
        [_pack_layer_shard(ffn_w_gate[l].astype(bf16), ffn_w_up[l].astype(bf16), ffn_w_down[l].astype(bf16), w_in[l].astype(bf16),
                           w_branch[l].astype(bf16), w_out[l].astype(bf16)) for l in range(L)], axis=0)
    rows_layer = big.shape[0] // L
    gathered = _gather_chips(big, "gather_weights")
    small_shard, soffs = _small_pack([ffn_norm, dn_conv])
    small_g = _gather_chips(small_shard, "gather_small")

    def chip_major(i, a):
        rows = -(-a.size // PACK_W)
        return small_g[:, soffs[i]:soffs[i] + rows].reshape(N_CHIPS, -1)[:, :a.size].reshape((N_CHIPS,) + a.shape)

    fn_full = jnp.moveaxis(chip_major(0, ffn_norm), 0, 2).reshape(L, 2, D)
    conv_full = jnp.moveaxis(chip_major(1, dn_conv), 0, 2).reshape(L, dn_conv.shape[1], N_CHIPS * cs)

    r_ffn = 2 * D * fs // PACK_W
    r_in = D * w_in.shape[2] // PACK_W
    r_in_pad = r_in + (-r_in) % W_IN_ROWS_PAD
    r_br = 3 * BRANCH * ds_ // PACK_W
    r_out = ds_ * D // PACK_W
    layers = []
    for l in range(L):
        g = gathered[:, l * rows_layer:(l + 1) * rows_layer]
        o = 0
        wg = g[:, o:o + r_ffn].reshape(N_CHIPS, 2, D, fs).transpose(1, 2, 0, 3).reshape(2, D, F); o += r_ffn
        wu = g[:, o:o + r_ffn].reshape(N_CHIPS, 2, D, fs).transpose(1, 2, 0, 3).reshape(2, D, F); o += r_ffn
        wd = g[:, o:o + r_ffn].reshape(N_CHIPS, 2, fs, D).transpose(1, 0, 2, 3).reshape(2, F, D); o += r_ffn
        wi = g[:, o:o + r_in].reshape(N_CHIPS, D, p_in // N_CHIPS).transpose(1, 0, 2).reshape(D, p_in); o += r_in_pad
        wb = g[:, o:o + r_br].reshape(N_CHIPS, 3, BRANCH, ds_).transpose(1, 2, 0, 3).reshape(3, BRANCH, D); o += r_br
        wo = g[:, o:o + r_out].reshape(D, D)
        c0, c1, c2 = 5 * BRANCH, 5 * BRANCH + 2 * N_HEADS, 8 * BRANCH + 2 * N_HEADS
        layers.append(dict(
            ffn_norm=fn_full[l], wg=[wg[0], wg[1]], wu=[wu[0], wu[1]], wd=[wd[0], wd[1]], mix_norm=mix_norm[l],
            w_main=wi[:, :c0], w_ab=jnp.pad(wi[:, c0:c1], ((0, 0), (0, LANES - 2 * N_HEADS))), w_sb=wi[:, c1:c2], w_gates=wi[:, c2:],
            b_gate=b_gate[l], pool_w=pool_w[l], pool_scale=pool_scale[l], dn_conv=conv_full[l], dn_A_log=dn_A_log[l],
            dn_dt_bias=dn_dt_bias[l], dn_out_norm=dn_out_norm[l], w_branch=wb, w_out=wo))

    sq, dx, grads, dfn = _local_step(x.reshape(T, D), loss_target.reshape(T, D), layers, final_norm, B)
    loss = lax.psum(sq, ("x", "y", "c")) * (0.5 / D)

    def by_chip(a, axis, n_lead):
        s = a.shape
        a = a.reshape(s[:axis] + (N_CHIPS, s[axis] // N_CHIPS) + s[axis + 1:])
        return jnp.moveaxis(a, axis, 0)

    gbig = jnp.concatenate(
        [_pack_layer_shard(by_chip(jnp.stack(g["wg"]), 2, 0), by_chip(jnp.stack(g["wu"]), 2, 0), by_chip(jnp.stack(g["wd"]), 1, 0),
                           by_chip(g["w_in"], 1, 0), by_chip(g["w_branch"], 2, 0), by_chip(g["w_out"], 0, 0)) for g in grads], axis=1)
    red = _reduce_to_chips(gbig, "big")
    small_names = SMALL_SHARDED + SMALL_REPLICATED
    pieces = [g[k] for g in grads for k in small_names] + [dfn]
    spack, offs = _small_pack(pieces)
    sred = _reduce_to_chips(jnp.broadcast_to(spack[None], (N_CHIPS,) + spack.shape), "small")
    small_red = _small_unpack(sred, offs, [p.shape for p in pieces])

    gw = {k: [] for k in names if k != "final_norm"}
    for l in range(L):
        g = red[l * rows_layer:(l + 1) * rows_layer]
        o = 0
        gw["ffn_w_gate"].append(g[o:o + r_ffn].reshape(2, D, fs)); o += r_ffn
        gw["ffn_w_up"].append(g[o:o + r_ffn].reshape(2, D, fs)); o += r_ffn
        gw["ffn_w_down"].append(g[o:o + r_ffn].reshape(2, fs, D)); o += r_ffn
        gw["w_in"].append(g[o:o + r_in].reshape(D, p_in // N_CHIPS)); o += r_in_pad
        gw["w_branch"].append(g[o:o + r_br].reshape(3, BRANCH, ds_)); o += r_br
        gw["w_out"].append(g[o:o + r_out].reshape(ds_, D))
        sm = dict(zip(small_names, small_red[l * len(small_names):(l + 1) * len(small_names)]))
        gw["ffn_norm"].append(lax.dynamic_slice_in_dim(sm["ffn_norm"], chip * ds_, ds_, axis=1))
        gw["dn_conv"].append(lax.dynamic_slice_in_dim(sm["dn_conv"], chip * cs, cs, axis=1))
        for k in SMALL_REPLICATED:
            gw[k].append(sm[k])
    gw = {k: jnp.stack(v) for k, v in gw.items()}
    gw["final_norm"] = small_red[-1]

    deltas, new_m, new_v = [], [], []
    for k in names:
        d, mo, vo = _adamw(wts[k], gw[k], ms[k], vs[k], f"adamw_{k}")
        deltas.append(d)
        new_m.append(mo)
        new_v.append(vo)
    return (loss, dx.reshape(B, S, D), *[gw[k] for k in names], *deltas, *new_m, *new_v)
```

```python
import functools
import math

import jax
import jax.numpy as jnp
from jax import lax
from jax.experimental import pallas as pl
from jax.experimental.pallas import tpu as pltpu

f32 = jnp.float32
bf16 = jnp.bfloat16
HI = lax.Precision.HIGHEST

EPS = 1e-6
HEAD = 128
N_HEADS = 4
BRANCH = 512
CHUNK = 64
SB_BLOCK = 128
SB_QUERIES = 512
POOL_WINDOWS = (2, 4, 8, 16)
N_CHIPS = 4
LANES = 128
PACK_W = 1024
ADAM_LR, ADAM_B1, ADAM_B2, ADAM_EPS, ADAM_WD, ADAM_STEP = 0.001, 0.9, 0.999, 1e-08, 0.01, 10
VMEM_LIMIT = 56 * 1024 * 1024

NN = (((1,), (0,)), ((), ()))
NT = (((1,), (1,)), ((), ()))
TN = (((0,), (0,)), ((), ()))


def _cp(sem=None):
    return pltpu.CompilerParams(dimension_semantics=sem, vmem_limit_bytes=VMEM_LIMIT)


def _tile(n, prefs):
    for p in prefs:
        if n % p == 0:
            return p
    return n


def _dot(a, b, dn=NN):
    return lax.dot_general(a, b, dn, preferred_element_type=f32)


def _sigmoid(x):
    return 1.0 / (1.0 + jnp.exp(-x))


def _softplus(x):
    return jnp.maximum(x, 0.0) + jnp.log1p(jnp.exp(-jnp.abs(x)))


def _mm(pairs, M, N, out_dtype, name, tm=None, tn=None, res=None, scale=1.0):
    tm = tm or _tile(M, (512, 256, 128))
    tn = tn or _tile(N, (512, 256, 128))
    specs, arrs, dns = [], [], []

    def lead(sel, shape, imap):
        if sel is None:
            return pl.BlockSpec(shape, imap)
        return pl.BlockSpec((None,) + shape, lambda i, j, sel=sel, imap=imap: (sel,) + imap(i, j))

    for form, a, b, o in pairs:
        ka, kb, moff, noff = o.get("ka", 0), o.get("kb", 0), o.get("moff", 0), o.get("noff", 0)
        asel, bsel = o.get("asel"), o.get("bsel")
        if form == "nn":
            K = o.get("K") or a.shape[-1]
            sa = lead(asel, (tm, K), lambda i, j, ka=ka, moff=moff: (i + moff, ka))
            sb = lead(bsel, (K, tn), lambda i, j, kb=kb, noff=noff: (kb, j + noff))
            dn = NN
        elif form == "nt":
            K = o.get("K") or a.shape[-1]
            sa = lead(asel, (tm, K), lambda i, j, ka=ka, moff=moff: (i + moff, ka))
            sb = lead(bsel, (tn, K), lambda i, j, kb=kb, noff=noff: (j + noff, kb))
            dn = NT
        else:
            K = a.shape[-2]
            sa = lead(asel, (K, tm), lambda i, j, moff=moff: (0, i + moff))
            sb = lead(bsel, (K, tn), lambda i, j, noff=noff: (0, j + noff))
            dn = TN
        specs += [sa, sb]
        arrs += [a, b]
        dns.append(dn)
    if res is not None:
        specs.append(pl.BlockSpec((tm, tn), lambda i, j: (i, j)))
        arrs.append(res)
    n = len(pairs)

    def body(*refs):
        o_ref = refs[-1]
        acc = None
        for p in range(n):
            d = _dot(refs[2 * p][...].astype(bf16), refs[2 * p + 1][...].astype(bf16), dns[p])
            acc = d if acc is None else acc + d
        if scale != 1.0:
            acc = acc * scale
        if res is not None:
            acc = acc + refs[2 * n][...]
        o_ref[...] = acc.astype(o_ref.dtype)

    return pl.pallas_call(
        body, name=name, grid=(M // tm, N // tn), in_specs=specs,
        out_specs=pl.BlockSpec((tm, tn), lambda i, j: (i, j)),
        out_shape=jax.ShapeDtypeStruct((M, N), out_dtype),
        compiler_params=_cp(("parallel", "parallel")),
    )(*arrs)


def _rms_fwd(x, g, name):
    T, D = x.shape
    tm = _tile(T, (512, 256, 128))

    def body(x_ref, g_ref, h_ref):
        xv = x_ref[...]
        r = lax.rsqrt(jnp.mean(xv * xv, axis=-1, keepdims=True) + EPS)
        h_ref[...] = (xv * r * g_ref[...]).astype(bf16)

    return pl.pallas_call(
        body, name=name, grid=(T // tm,),
        in_specs=[pl.BlockSpec((tm, D), lambda i: (i, 0)), pl.BlockSpec((1, D), lambda i: (0, 0))],
        out_specs=pl.BlockSpec((tm, D), lambda i: (i, 0)),
        out_shape=jax.ShapeDtypeStruct((T, D), bf16), compiler_params=_cp(("parallel",)),
    )(x, g.reshape(1, D))


def _fold8(v):
    r, d = v.shape
    return jnp.sum(v.reshape(r // 8, 8, d), axis=0)


def _rms_bwd(dh, x, g, dres, name):
    T, D = x.shape
    tm = _tile(T, (512, 256, 128))

    def body(dh_ref, x_ref, g_ref, dres_ref, dx_ref, dg_ref):
        xv = x_ref[...]
        r = lax.rsqrt(jnp.mean(xv * xv, axis=-1, keepdims=True) + EPS)
        xh = xv * r
        dhv = dh_ref[...]
        dxh = dhv * g_ref[...]
        dx_ref[...] = dres_ref[...] + r * (dxh - xh * jnp.mean(dxh * xh, axis=-1, keepdims=True))

        @pl.when(pl.program_id(0) == 0)
        def _():
            dg_ref[...] = jnp.zeros_like(dg_ref)

        dg_ref[...] += _fold8(dhv * xh)

    return pl.pallas_call(
        body, name=name, grid=(T // tm,),
        in_specs=[pl.BlockSpec((tm, D), lambda i: (i, 0)), pl.BlockSpec((tm, D), lambda i: (i, 0)),
                  pl.BlockSpec((1, D), lambda i: (0, 0)), pl.BlockSpec((tm, D), lambda i: (i, 0))],
        out_specs=[pl.BlockSpec((tm, D), lambda i: (i, 0)), pl.BlockSpec((8, D), lambda i: (0, 0))],
        out_shape=[jax.ShapeDtypeStruct((T, D), f32), jax.ShapeDtypeStruct((8, D), f32)],
        compiler_params=_cp(("arbitrary",)),
    )(dh, x, g.reshape(1, D), dres)


def _final_loss(x, target, g):
    T, D = x.shape
    tm = _tile(T, (512, 256, 128))

    def body(x_ref, t_ref, g_ref, dx_ref, dg_ref, ls_ref):
        xv = x_ref[...]
        r = lax.rsqrt(jnp.mean(xv * xv, axis=-1, keepdims=True) + EPS)
        xh = xv * r
        gv = g_ref[...]
        e = xh * gv - t_ref[...]
        dy = e * (1.0 / D)
        dxh = dy * gv
        dx_ref[...] = r * (dxh - xh * jnp.mean(dxh * xh, axis=-1, keepdims=True))

        @pl.when(pl.program_id(0) == 0)
        def _():
            dg_ref[...] = jnp.zeros_like(dg_ref)
            ls_ref[...] = jnp.zeros_like(ls_ref)

        dg_ref[...] += _fold8(dy * xh)
        ls_ref[...] += _fold8(e * e)

    return pl.pallas_call(
        body, name="final_loss", grid=(T // tm,),
        in_specs=[pl.BlockSpec((tm, D), lambda i: (i, 0)), pl.BlockSpec((tm, D), lambda i: (i, 0)),
                  pl.BlockSpec((1, D), lambda i: (0, 0))],
        out_specs=[pl.BlockSpec((tm, D), lambda i: (i, 0)), pl.BlockSpec((8, D), lambda i: (0, 0)),
                   pl.BlockSpec((8, D), lambda i: (0, 0))],
        out_shape=[jax.ShapeDtypeStruct((T, D), f32), jax.ShapeDtypeStruct((8, D), f32),
                   jax.ShapeDtypeStruct((8, D), f32)],
        compiler_params=_cp(("arbitrary",)),
    )(x, target, g.reshape(1, D))


def _ffn_up(h, wg, wu, name):
    T, D = h.shape
    F = wg.shape[1]
    tm = _tile(T, (512, 256, 128))
    tn = _tile(F, (1408, 512, 256, 128))

    def body(h_ref, wg_ref, wu_ref, a_ref, b_ref, m_ref):
        hv = h_ref[...]
        a = _dot(hv, wg_ref[...])
        b = _dot(hv, wu_ref[...])
        a_ref[...] = a.astype(bf16)
        b_ref[...] = b.astype(bf16)
        m_ref[...] = (a * _sigmoid(a) * b).astype(bf16)

    o = jax.ShapeDtypeStruct((T, F), bf16)
    ospec = pl.BlockSpec((tm, tn), lambda j, i: (i, j))
    return pl.pallas_call(
        body, name=name, grid=(F // tn, T // tm),
        in_specs=[pl.BlockSpec((tm, D), lambda j, i: (i, 0)), pl.BlockSpec((D, tn), lambda j, i: (0, j)),
                  pl.BlockSpec((D, tn), lambda j, i: (0, j))],
        out_specs=[ospec, ospec, ospec], out_shape=[o, o, o], compiler_params=_cp(("parallel", "parallel")),
    )(h, wg, wu)


def _ffn_bwd_mid(dx, wd, a, b, name):
    T, D = dx.shape
    F = wd.shape[0]
    tm = _tile(T, (512, 256, 128))
    tn = _tile(F, (1408, 512, 256, 128))

    def body(dx_ref, wd_ref, a_ref, b_ref, da_ref, db_ref, m_ref):
        dm = 0.5 * _dot(dx_ref[...].astype(bf16), wd_ref[...], NT)
        av = a_ref[...].astype(f32)
        bv = b_ref[...].astype(f32)
        s = _sigmoid(av)
        silu = av * s
        da_ref[...] = (dm * bv * (s * (1.0 + av * (1.0 - s)))).astype(bf16)
        db_ref[...] = (dm * silu).astype(bf16)
        m_ref[...] = (silu * bv).astype(bf16)

    o = jax.ShapeDtypeStruct((T, F), bf16)
    ospec = pl.BlockSpec((tm, tn), lambda j, i: (i, j))
    return pl.pallas_call(
        body, name=name, grid=(F // tn, T // tm),
        in_specs=[pl.BlockSpec((tm, D), lambda j, i: (i, 0)), pl.BlockSpec((tn, D), lambda j, i: (j, 0)), ospec, ospec],
        out_specs=[ospec, ospec, ospec], out_shape=[o, o, o], compiler_params=_cp(("parallel", "parallel")),
    )(dx, wd, a, b)


def _shift_down(x, k, row):
    return jnp.where(row < k, 0.0, pltpu.roll(x, k, 0))


def _shift_up(x, k, row):
    s = x.shape[0]
    return jnp.where(row >= s - k, 0.0, pltpu.roll(x, s - k, 0))


def _window_sum(x, g, row, shift):
    s2 = x + shift(x, 1, row)
    s4 = s2 + shift(s2, 2, row)
    s8 = s4 + shift(s4, 4, row)
    s16 = s8 + shift(s8, 8, row)
    return jnp.where(g == 0, s2, jnp.where(g == 1, s4, jnp.where(g == 2, s8, s16)))


def _pool_fwd(proj, pool_w, scale, name):
    B, S = proj.shape[0], proj.shape[1]
    G = len(POOL_WINDOWS)

    def body(u_ref, w_ref, sc_ref, y_ref):
        g = pl.program_id(1)
        u = u_ref[0].astype(f32)
        row = lax.broadcasted_iota(jnp.int32, u.shape, 0)
        win = _window_sum(u, g, row, _shift_down)
        cnt = jnp.minimum(row + 1, jnp.left_shift(2, g)).astype(f32)
        pooled = win / cnt - u
        mixed = _dot(pooled.astype(bf16), w_ref[0].astype(bf16))
        y_ref[0] = (mixed * sc_ref[...]).astype(bf16)

    return pl.pallas_call(
        body, name=name, grid=(B, G),
        in_specs=[pl.BlockSpec((1, S, HEAD), lambda b, g: (b, 0, g)), pl.BlockSpec((1, HEAD, HEAD), lambda b, g: (g, 0, 0)),
                  pl.BlockSpec((1, HEAD), lambda b, g: (0, g))],
        out_specs=pl.BlockSpec((1, S, HEAD), lambda b, g: (b, 0, g)),
        out_shape=jax.ShapeDtypeStruct((B, S, BRANCH), bf16), compiler_params=_cp(("parallel", "parallel")),
    )(proj, pool_w, scale.reshape(1, BRANCH))


def _pool_bwd(proj, dy, pool_w, scale, name):
    B, S = proj.shape[0], proj.shape[1]
    G = len(POOL_WINDOWS)

    def body(u_ref, dy_ref, w_ref, sc_ref, du_ref, dw_ref, dsc_ref):
        g = pl.program_id(0)
        u = u_ref[0].astype(f32)
        row = lax.broadcasted_iota(jnp.int32, u.shape, 0)
        cnt = jnp.minimum(row + 1, jnp.left_shift(2, g)).astype(f32)
        pooled = _window_sum(u, g, row, _shift_down) / cnt - u
        wv = w_ref[0].astype(bf16)
        mixed = _dot(pooled.astype(bf16), wv)
        dyv = dy_ref[0].astype(f32)
        dmix = (dyv * sc_ref[...]).astype(bf16)
        dpool = _dot(dmix, wv, NT)
        du_ref[0] = (_window_sum(dpool / cnt, g, row, _shift_up) - dpool).astype(bf16)

        @pl.when(pl.program_id(1) == 0)
        def _():
            dw_ref[...] = jnp.zeros_like(dw_ref)
            dsc_ref[...] = jnp.zeros_like(dsc_ref)

        dw_ref[0] += _dot(pooled.astype(bf16), dmix, TN)
        dsc_ref[...] += _fold8(dyv * mixed)

    return pl.pallas_call(
        body, name=name, grid=(G, B),
        in_specs=[pl.BlockSpec((1, S, HEAD), lambda g, b: (b, 0, g)), pl.BlockSpec((1, S, HEAD), lambda g, b: (b, 0, g)),
                  pl.BlockSpec((1, HEAD, HEAD), lambda g, b: (g, 0, 0)), pl.BlockSpec((1, HEAD), lambda g, b: (0, g))],
        out_specs=[pl.BlockSpec((1, S, HEAD), lambda g, b: (b, 0, g)), pl.BlockSpec((1, HEAD, HEAD), lambda g, b: (g, 0, 0)),
                   pl.BlockSpec((8, HEAD), lambda g, b: (0, g))],
        out_shape=[jax.ShapeDtypeStruct((B, S, BRANCH), bf16), jax.ShapeDtypeStruct((G, HEAD, HEAD), f32),
                   jax.ShapeDtypeStruct((8, BRANCH), f32)],
        compiler_params=_cp(("parallel", "arbitrary")),
    )(proj, dy, pool_w, scale.reshape(1, BRANCH))


def _split_dot(x, u):
    hi = x.astype(bf16)
    lo = (x - hi.astype(f32)).astype(bf16)
    return _dot(hi, u) + _dot(lo, u)


def _sb_fwd(sbqkv, name):
    B, S, _ = sbqkv.shape
    KB = SB_BLOCK
    TQ = _tile(S, (SB_QUERIES, KB))
    ns = TQ // KB
    nq = S // TQ
    scale = HEAD ** -0.5

    def body(q_ref, k_ref, v_ref, o_ref, tot_ref, run_s, acc_s):
        r = lax.broadcasted_iota(jnp.int32, (KB, KB), 0)
        c = lax.broadcasted_iota(jnp.int32, (KB, KB), 1)
        causal = c < r
        after = (r > c).astype(bf16)

        def sub(qa, krows, masked, run, acc):
            z = _dot(qa, k_ref[0, krows, :], NT) * scale
            sp = jnp.maximum(z, 0.0) + jnp.log(1.0 + jnp.exp(-jnp.abs(z)))
            ln = -sp
            if masked:
                ln = jnp.where(causal, ln, 0.0)
            w = jnp.exp(z - sp + _split_dot(ln, after) + run)
            if masked:
                w = jnp.where(causal, w, 0.0)
            acc = acc + _dot(w.astype(bf16), v_ref[0, krows, :])
            return run + jnp.sum(ln, axis=1, keepdims=True), acc

        def qloop(i, carry):
            base = pl.multiple_of(i * TQ, TQ)
            qi = q_ref[0, pl.ds(base, TQ), :]
            for a in range(ns):
                qa = qi[a * KB:(a + 1) * KB]
                run, acc = jnp.zeros((KB, LANES), f32), jnp.zeros((KB, HEAD), f32)
                for s in range(a, -1, -1):
                    run, acc = sub(qa, pl.ds(base + s * KB, KB), s == a, run, acc)
                run_s[a * KB:(a + 1) * KB, :] = run
                acc_s[a * KB:(a + 1) * KB, :] = acc

            def group(t, cr):
                g0 = pl.multiple_of((i - 1 - t) * TQ, TQ)
                for s in range(ns - 1, -1, -1):
                    cr = sub(qi, pl.ds(g0 + s * KB, KB), False, *cr)
                return cr

            run, acc = lax.fori_loop(0, i, group, (run_s[...], acc_s[...]))
            o_ref[0, pl.ds(base, TQ), :] = acc.astype(bf16)
            tot_ref[0, 0, pl.ds(base, TQ), :] = run
            return carry

        lax.fori_loop(0, nq, qloop, 0)

    def spec(off):
        return pl.BlockSpec((1, S, HEAD), lambda b, h, off=off: (b, 0, off + h))

    return pl.pallas_call(
        body, name=name, grid=(B, N_HEADS), in_specs=[spec(0), spec(N_HEADS), spec(2 * N_HEADS)],
        out_specs=[pl.BlockSpec((1, S, HEAD), lambda b, h: (b, 0, h)), pl.BlockSpec((1, 1, S, LANES), lambda b, h: (b, h, 0, 0))],
        out_shape=[jax.ShapeDtypeStruct((B, S, BRANCH), bf16), jax.ShapeDtypeStruct((B, N_HEADS, S, LANES), f32)],
        scratch_shapes=[pltpu.VMEM((TQ, LANES), f32), pltpu.VMEM((TQ, HEAD), f32)],
        compiler_params=_cp(("parallel", "parallel")),
    )(sbqkv, sbqkv, sbqkv)


def _sb_bwd(sbqkv, do, tot, name):
    B, S, _ = sbqkv.shape
    KB = SB_BLOCK
    TQ = _tile(S, (SB_QUERIES, KB))
    ns = TQ // KB
    nq = S // TQ
    scale = HEAD ** -0.5

    def body(q_ref, k_ref, v_ref, do_ref, tot_ref, dq_ref, dk_ref, dv_ref, dk_acc, dv_acc):
        r = lax.broadcasted_iota(jnp.int32, (KB, KB), 0)
        c = lax.broadcasted_iota(jnp.int32, (KB, KB), 1)
        causal = c < r
        after = (r > c).astype(bf16)
        before = (r < c).astype(bf16)
        dk_acc[...] = jnp.zeros_like(dk_acc)
        dv_acc[...] = jnp.zeros_like(dv_acc)

        def sub(qi, doi, total, rows, masked, cl, cp, dq):
            kj = k_ref[0, rows, :]
            vj = v_ref[0, rows, :]
            z = _dot(qi, kj, NT) * scale
            sp = jnp.maximum(z, 0.0) + jnp.log(1.0 + jnp.exp(-jnp.abs(z)))
            ln = -sp
            if masked:
                ln = jnp.where(causal, ln, 0.0)
            bs = jnp.sum(ln, axis=1, keepdims=True)
            w = jnp.exp(z - sp + _split_dot(ln, after) + (total - cl - bs))
            if masked:
                w = jnp.where(causal, w, 0.0)
            p = _dot(doi, vj, NT) * w
            qsum = cp + _split_dot(p, before)
            sig = jnp.exp(z - sp)
            dz = (p * (1.0 - sig) - qsum * sig) * scale
            if masked:
                dz = jnp.where(causal, dz, 0.0)
            dzb = dz.astype(bf16)
            dq = dq + _dot(dzb, kj)
            dk_acc[rows, :] += _dot(dzb, qi, TN)
            dv_acc[rows, :] += _dot(w.astype(bf16), doi, TN)
            return cl + bs, cp + jnp.sum(p, axis=1, keepdims=True), dq

        def qloop(i, carry):
            base = pl.multiple_of(i * TQ, TQ)
            rows = pl.ds(base, TQ)
            qi = q_ref[0, rows, :]
            doi = do_ref[0, rows, :]
            total = tot_ref[0, 0, rows, :][:, 0:1]
            zero = jnp.zeros((TQ, 1), f32)

            def group(g, st):
                g0 = pl.multiple_of(g * TQ, TQ)
                for s in range(ns):
                    st = sub(qi, doi, total, pl.ds(g0 + s * KB, KB), False, *st)
                return st

            cl, cp, dq = lax.fori_loop(0, i, group, (zero, zero, jnp.zeros((TQ, HEAD), f32)))
            for a in range(ns):
                ra = slice(a * KB, (a + 1) * KB)
                st = (cl[ra], cp[ra], dq[ra])
                for s in range(a + 1):
                    st = sub(qi[ra], doi[ra], total[ra], pl.ds(base + s * KB, KB), s == a, *st)
                dq_ref[0, pl.ds(base + a * KB, KB), :] = st[2].astype(bf16)
            return carry

        lax.fori_loop(0, nq, qloop, 0)
        dk_ref[0] = dk_acc[...].astype(bf16)
        dv_ref[0] = dv_acc[...].astype(bf16)

    def spec(off):
        return pl.BlockSpec((1, S, HEAD), lambda b, h, off=off: (b, 0, off + h))

    o = jax.ShapeDtypeStruct((B, S, BRANCH), bf16)
    return pl.pallas_call(
        body, name=name, grid=(B, N_HEADS),
        in_specs=[spec(0), spec(N_HEADS), spec(2 * N_HEADS), spec(0), pl.BlockSpec((1, 1, S, LANES), lambda b, h: (b, h, 0, 0))],
        out_specs=[spec(0), spec(0), spec(0)], out_shape=[o, o, o],
        scratch_shapes=[pltpu.VMEM((S, HEAD), f32), pltpu.VMEM((S, HEAD), f32)],
        compiler_params=_cp(("parallel", "parallel")),
    )(sbqkv, sbqkv, sbqkv, do, tot)


def _dn_params(a_log, dt_bias):
    p = jnp.zeros((8, LANES), f32)
    p = p.at[0, :N_HEADS].set(a_log)
    return p.at[1, :N_HEADS].set(dt_bias)


def _dn_prep(ab, par, name):
    B, S, _ = ab.shape
    R = 2 * CHUNK
    nt = S // R

    def body(ab_ref, par_ref, gcb_ref, bb_ref, gcr_ref):
        x = ab_ref[0]
        g = -jnp.exp(par_ref[0:1, :]) * _softplus(x + par_ref[1:2, :])
        r = lax.broadcasted_iota(jnp.int32, (R, R), 0)
        c = lax.broadcasted_iota(jnp.int32, (R, R), 1)
        tri = ((r >= c) & ((r >> 6) == (c >> 6))).astype(f32)
        cs = jnp.dot(tri, g, precision=HI, preferred_element_type=f32)
        beta = _sigmoid(x)
        cst = cs.T
        for h in range(N_HEADS):
            gcb_ref[0, h] = jnp.broadcast_to(cs[:, h:h + 1], (R, LANES))
            bb_ref[0, h] = jnp.broadcast_to(beta[:, N_HEADS + h:N_HEADS + h + 1], (R, LANES))
            gcr_ref[0, h, 0] = jnp.broadcast_to(cst[h:h + 1, 0:CHUNK], (8, CHUNK))
            gcr_ref[0, h, 1] = jnp.broadcast_to(cst[h:h + 1, CHUNK:R], (8, CHUNK))

    return pl.pallas_call(
        body, name=name, grid=(B, nt),
        in_specs=[pl.BlockSpec((1, R, LANES), lambda b, i: (b, i, 0)), pl.BlockSpec((8, LANES), lambda b, i: (0, 0))],
        out_specs=[pl.BlockSpec((1, N_HEADS, R, LANES), lambda b, i: (b, 0, i, 0)),
                   pl.BlockSpec((1, N_HEADS, R, LANES), lambda b, i: (b, 0, i, 0)),
                   pl.BlockSpec((1, N_HEADS, 2, 8, CHUNK), lambda b, i: (b, 0, i, 0, 0))],
        out_shape=[jax.ShapeDtypeStruct((B, N_HEADS, S, LANES), f32), jax.ShapeDtypeStruct((B, N_HEADS, S, LANES), f32),
                   jax.ShapeDtypeStruct((B, N_HEADS, S // CHUNK, 8, CHUNK), f32)],
        compiler_params=_cp(("parallel", "parallel")),
    )(ab, par)


def _bmm(a, b, prec=None):
    return jnp.einsum("nij,njk->nik", a, b, preferred_element_type=f32, precision=prec)


def _bmm_nt(a, b, prec=None):
    return jnp.einsum("nik,njk->nij", a, b, preferred_element_type=f32, precision=prec)


def _bmm_tn(a, b, prec=None):
    return jnp.einsum("nki,nkj->nij", a, b, preferred_element_type=f32, precision=prec)


def _tri_inv(L):
    C = L.shape[-1]
    r = lax.broadcasted_iota(jnp.int32, (C, C), 0)
    c = lax.broadcasted_iota(jnp.int32, (C, C), 1)
    eye = (r == c).astype(f32)
    bd16 = (r >> 4) == (c >> 4)
    bd32 = (r >> 5) == (c >> 5)
    mm = functools.partial(_bmm, prec=HI)
    n1 = -jnp.where(bd16, L, 0.0)
    n2 = mm(n1, n1)
    n4 = mm(n2, n2)
    n8 = mm(n4, n4)
    t = mm(mm(mm(eye + n1, eye + n2), eye + n4), eye + n8)
    t = t - mm(mm(t, jnp.where(bd32 & jnp.logical_not(bd16), L, 0.0)), t)
    t = t - mm(mm(t, jnp.where(bd32, 0.0, L)), t)
    return t


def _conv_silu(x, w, row):
    c = w[3:4] * x + w[2:3] * _shift_down(x, 1, row) + w[1:2] * _shift_down(x, 2, row) + w[0:1] * _shift_down(x, 3, row)
    return c, c * _sigmoid(c)


def _dn_intra(qn, kn, v, gcb, beta, gr):
    C = CHUNK
    r = lax.broadcasted_iota(jnp.int32, (C, C), 0)
    c = lax.broadcasted_iota(jnp.int32, (C, C), 1)
    incl = r >= c
    diff = gcb[:, :, :C] - gr
    dm = jnp.where(incl, jnp.exp(jnp.where(incl, diff, 0.0)), 0.0)
    ds = jnp.where(r > c, dm, 0.0)
    kb = kn * beta
    knb = kn.astype(bf16)
    L = _bmm_nt(kb.astype(bf16), knb) * ds
    eg = jnp.exp(gcb)
    a = _bmm_nt(qn.astype(bf16), knb) * dm
    gl = gcb[:, C - 1:C, :]
    ekd = jnp.exp(gl - gcb)
    return dict(dm=dm, ds=ds, kb=kb, L=L, eg=eg, rhs_u=v * beta, rhs_w=kb * eg, a=a,
                qd=qn * eg, kd=kn * ekd, ekd=ekd, cd=jnp.exp(gl))


def _dn_specs(S):
    def col(off):
        return pl.BlockSpec((1, S, HEAD), lambda b, h, off=off: (b, 0, off + h))

    def cw(off):
        return pl.BlockSpec((4, HEAD), lambda b, h, off=off: (0, off + h))

    per_head = pl.BlockSpec((1, 1, S, LANES), lambda b, h: (b, h, 0, 0))
    rowform = pl.BlockSpec((1, 1, S // CHUNK, 8, CHUNK), lambda b, h: (b, h, 0, 0, 0))
    gain = pl.BlockSpec((1, HEAD), lambda b, h: (0, 0))
    ins = [col(4), col(8), col(12), col(16), cw(0), cw(4), cw(8), per_head, per_head, rowform, gain]
    return ins, per_head


def _dn_act(x_ref, cw_ref, row, normalise, out_scale=1.0):
    _, act = _conv_silu(x_ref[0].astype(f32), cw_ref[...], row)
    if normalise:
        act = act * (lax.rsqrt(jnp.sum(act * act, axis=-1, keepdims=True) + EPS) * out_scale)
    return act


def _dn_group(qn_s, kn_s, v_s, gcb_ref, bb_ref, gcr_ref, g, ng):
    C = CHUNK
    rows = pl.ds(pl.multiple_of(g * (ng * C), ng * C), ng * C)
    ch = pl.ds(g * ng, ng)
    sh = (ng, C, HEAD)
    qn, kn, v = qn_s[rows, :].reshape(sh), kn_s[rows, :].reshape(sh), v_s[rows, :].reshape(sh)
    gcb3, beta = gcb_ref[0, 0, rows, :].reshape(sh), bb_ref[0, 0, rows, :].reshape(sh)
    gr = gcr_ref[0, 0, ch][:, 0:1, :]
    it = _dn_intra(qn, kn, v, gcb3, beta, gr)
    it.update(qn=qn, kn=kn, v=v, gcb=gcb3, beta=beta, gr=gr)
    return rows, ch, it


def _dn_fwd(proj, conv_w, gcb, betab, gcr, gain, name):
    B, S, _ = proj.shape
    n, C = S // CHUNK, CHUNK
    ng = min(8, n)
    ins, per_head = _dn_specs(S)

    def body(q_ref, k_ref, v_ref, z_ref, cq_ref, ck_ref, cv_ref, gcb_ref, bb_ref, gcr_ref, gain_ref,
             y_ref, st_ref, vn_ref, qn_s, kn_s, v_s, u_s, w_s, qd_s, kd_s, a_s, cd_s, o_s):
        row = lax.broadcasted_iota(jnp.int32, (S, HEAD), 0)
        qn_s[...] = _dn_act(q_ref, cq_ref, row, True, HEAD ** -0.5)
        kn_s[...] = _dn_act(k_ref, ck_ref, row, True)
        v_s[...] = _dn_act(v_ref, cv_ref, row, False)

        def group(g, carry):
            _, ch, it = _dn_group(qn_s, kn_s, v_s, gcb_ref, bb_ref, gcr_ref, g, ng)
            t = _tri_inv(it["L"])
            u_s[ch] = _bmm(t, it["rhs_u"], HI)
            w_s[ch] = _bmm(t, it["rhs_w"], HI).astype(bf16)
            qd_s[ch] = it["qd"].astype(bf16)
            kd_s[ch] = it["kd"].astype(bf16)
            a_s[ch] = it["a"].astype(bf16)
            cd_s[ch] = it["cd"]
            return carry

        lax.fori_loop(0, n // ng, group, 0)

        def step(i, st):
            sb = st.astype(bf16)
            st_ref[0, 0, i] = sb
            vn = (u_s[i] - _dot(w_s[i], sb)).astype(bf16)
            vn_ref[0, 0, pl.ds(pl.multiple_of(i * C, C), C), :] = vn
            o_s[i] = _dot(qd_s[i], sb) + _dot(a_s[i], vn)
            return st * cd_s[i] + _dot(kd_s[i], vn, TN)

        lax.fori_loop(0, n, step, jnp.zeros((HEAD, HEAD), f32))
        o = o_s[...].reshape(S, HEAD)
        zz = z_ref[0].astype(f32)
        rr = lax.rsqrt(jnp.mean(o * o, axis=-1, keepdims=True) + EPS)
        y_ref[0] = (o * rr * gain_ref[...] * (zz * _sigmoid(zz))).astype(bf16)

    seq = pltpu.VMEM((S, HEAD), f32)
    return pl.pallas_call(
        body, name=name, grid=(B, N_HEADS), in_specs=ins,
        out_specs=[pl.BlockSpec((1, S, HEAD), lambda b, h: (b, 0, h)),
                   pl.BlockSpec((1, 1, n, HEAD, HEAD), lambda b, h: (b, h, 0, 0, 0)), per_head],
        out_shape=[jax.ShapeDtypeStruct((B, S, BRANCH), bf16), jax.ShapeDtypeStruct((B, N_HEADS, n, HEAD, HEAD), bf16),
                   jax.ShapeDtypeStruct((B, N_HEADS, S, HEAD), bf16)],
        scratch_shapes=[seq, seq, seq, pltpu.VMEM((n, C, HEAD), f32), pltpu.VMEM((n, C, HEAD), bf16),
                        pltpu.VMEM((n, C, HEAD), bf16), pltpu.VMEM((n, C, HEAD), bf16), pltpu.VMEM((n, C, C), bf16),
                        pltpu.VMEM((n, 1, HEAD), f32), pltpu.VMEM((n, C, HEAD), f32)],
        compiler_params=_cp(("parallel", "parallel")),
    )(proj, proj, proj, proj, conv_w, conv_w, conv_w, gcb, betab, gcr, gain.reshape(1, HEAD))


def _rowsum(x):
    return jnp.sum(x, axis=-1, keepdims=True)


def _dn_bwd(proj, dy, conv_w, gcb, betab, gcr, gain, states, vnew, name):
    B, S, _ = proj.shape
    n, C = S // CHUNK, CHUNK
    ng = min(8, n)
    ins, per_head = _dn_specs(S)
    ins = ins + [pl.BlockSpec((1, S, HEAD), lambda b, h: (b, 0, h)),
                 pl.BlockSpec((1, 1, n, HEAD, HEAD), lambda b, h: (b, h, 0, 0, 0)), per_head]

    def body(q_ref, k_ref, v_ref, z_ref, cq_ref, ck_ref, cv_ref, gcb_ref, bb_ref, gcr_ref, gain_ref, dy_ref, st_ref, vn_ref,
             dq_ref, dk_ref, dv_ref, dz_ref, dg_ref, dbeta_ref, dconv_ref, dgain_ref,
             qn_s, kn_s, v_s, t_s, u_s, at_s, kd_s, qd_s, w_s, cd_s, do_s, dsp_s, dvn_s):
        row = lax.broadcasted_iota(jnp.int32, (S, HEAD), 0)
        qn_s[...] = _dn_act(q_ref, cq_ref, row, True, HEAD ** -0.5)
        kn_s[...] = _dn_act(k_ref, ck_ref, row, True)
        v_s[...] = _dn_act(v_ref, cv_ref, row, False)
        dgain_ref[...] = jnp.zeros_like(dgain_ref)
        gv = gain_ref[...]

        def group_fwd(g, carry):
            rows, ch, it = _dn_group(qn_s, kn_s, v_s, gcb_ref, bb_ref, gcr_ref, g, ng)
            t = _tri_inv(it["L"])
            ub = _bmm(t, it["rhs_u"], HI).astype(bf16)
            wb = _bmm(t, it["rhs_w"], HI).astype(bf16)
            ab, qdb = it["a"].astype(bf16), it["qd"].astype(bf16)
            vn = vn_ref[0, 0, rows, :].reshape(ng, C, HEAD)
            o = (_bmm(qdb, st_ref[0, 0, ch]) + _bmm(ab, vn)).reshape(ng * C, HEAD)
            zz = z_ref[0, rows, :].astype(f32)
            dyv = dy_ref[0, rows, :].astype(f32)
            rr = lax.rsqrt(jnp.mean(o * o, axis=-1, keepdims=True) + EPS)
            on = o * rr
            sz = _sigmoid(zz)
            dz_ref[0, rows, :] = (dyv * on * gv * (sz * (1.0 + zz * (1.0 - sz)))).astype(bf16)
            dnrm = dyv * (zz * sz)
            dgain_ref[0, 0] += _fold8(dnrm * on)
            doh = dnrm * gv
            do = rr * (doh - on * jnp.mean(doh * on, axis=-1, keepdims=True))
            t_s[ch] = t
            u_s[ch] = ub
            w_s[ch] = wb
            at_s[ch] = ab
            qd_s[ch] = qdb
            kd_s[ch] = it["kd"].astype(bf16)
            cd_s[ch] = it["cd"]
            do_s[ch] = do.reshape(ng, C, HEAD).astype(bf16)
            return carry

        lax.fori_loop(0, n // ng, group_fwd, 0)

        def step(t, dsp):
            i = n - 1 - t
            dspb = dsp.astype(bf16)
            dsp_s[i] = dspb
            dvn = _dot(at_s[i], do_s[i], TN) + _dot(kd_s[i], dspb)
            dvn_s[i] = dvn
            return dsp * cd_s[i] + _dot(qd_s[i], do_s[i], TN) - _dot(w_s[i], dvn.astype(bf16), TN)

        lax.fori_loop(0, n, step, jnp.zeros((HEAD, HEAD), f32))

        r = lax.broadcasted_iota(jnp.int32, (C, C), 0)
        c = lax.broadcasted_iota(jnp.int32, (C, C), 1)
        upper = r <= c

        def group_bwd(g, carry):
            rows, ch, it = _dn_group(qn_s, kn_s, v_s, gcb_ref, bb_ref, gcr_ref, g, ng)
            sh = (ng, C, HEAD)
            qn, kn, v, beta, gcb3, gr = it["qn"], it["kn"], it["v"], it["beta"], it["gcb"], it["gr"]
            sn = st_ref[0, 0, ch]
            vn = vn_ref[0, 0, rows, :].reshape(sh)
            dsp, dvn, dob = dsp_s[ch], dvn_s[ch], do_s[ch]
            t, ub, wb = t_s[ch], u_s[ch], w_s[ch]
            dvnb = dvn.astype(bf16)
            da = _bmm_nt(dob, vn)
            dat = _bmm_nt(vn, dob)
            dqd = _bmm_nt(dob, sn)
            dkd = _bmm_nt(vn, dsp)
            dcd = jnp.sum(jnp.sum(dsp.astype(f32) * sn.astype(f32), axis=2, keepdims=True), axis=1, keepdims=True)
            dw = -_bmm_nt(dvnb, sn)
            ru = _bmm_tn(t, dvn, HI)
            rw = _bmm_tn(t, dw, HI)
            rub, rwb = ru.astype(bf16), rw.astype(bf16)
            dL = -(_bmm_nt(rub, ub) + _bmm_nt(rwb, wb))
            dLt = -(_bmm_nt(ub, rub) + _bmm_nt(wb, rwb))
            knb, qnb, kbb = kn.astype(bf16), qn.astype(bf16), it["kb"].astype(bf16)
            dmt = jnp.where(upper, jnp.exp(jnp.where(upper, gr - gcb3[:, :, :C], 0.0)), 0.0)
            Lt = _bmm_nt(knb, kbb) * jnp.where(r < c, dmt, 0.0)
            At = _bmm_nt(knb, qnb) * dmt
            dgc = _rowsum(dL * it["L"] + da * it["a"]) - _rowsum(dLt * Lt + dat * At)
            dkk = (dL * it["ds"]).astype(bf16)
            dqk = (da * it["dm"]).astype(bf16)
            dkb = _bmm(dkk, knb) + rw * it["eg"]
            dkn = _bmm_tn(dkk, kbb) + _bmm_tn(dqk, qnb) + dkd * it["ekd"] + dkb * beta
            dqn = _bmm(dqk, knb) + dqd * it["eg"]
            tkd = _rowsum(dkd * it["kd"])
            dgl = jnp.sum(tkd, axis=1, keepdims=True) + dcd * it["cd"][:, :, 0:1]
            dgc = dgc + _rowsum(dqd * it["qd"]) - tkd + _rowsum(rw * it["rhs_w"])
            dbeta = _rowsum(ru * v) + _rowsum(dkb * kn)
            rowc = lax.broadcasted_iota(jnp.int32, (ng, C, 1), 1)
            dgc = dgc + jnp.where(rowc == C - 1, dgl, 0.0)
            rev = jnp.broadcast_to(upper.astype(f32), (ng, C, C))
            dg_ref[0, 0, rows, :] = _bmm(rev, jnp.broadcast_to(dgc, sh), HI).reshape(ng * C, LANES).astype(bf16)
            dbeta_ref[0, 0, rows, :] = jnp.broadcast_to(dbeta, sh).reshape(ng * C, LANES).astype(bf16)
            qn_s[rows, :] = dqn.reshape(ng * C, HEAD)
            kn_s[rows, :] = dkn.reshape(ng * C, HEAD)
            v_s[rows, :] = (ru * beta).reshape(ng * C, HEAD)
            return carry

        lax.fori_loop(0, n // ng, group_bwd, 0)

        def conv_back(x_ref, cw_ref, grad_s, out_ref, slot, normalise, out_scale):
            x = x_ref[0].astype(f32)
            w = cw_ref[...]
            pre, act = _conv_silu(x, w, row)
            dact = grad_s[...]
            if normalise:
                rn = lax.rsqrt(jnp.sum(act * act, axis=-1, keepdims=True) + EPS)
                unit = act * rn
                dact = (out_scale * rn) * (dact - unit * _rowsum(dact * unit))
            s = _sigmoid(pre)
            dc = dact * (s * (1.0 + pre * (1.0 - s)))
            out_ref[0] = (w[3:4] * dc + w[2:3] * _shift_up(dc, 1, row) + w[1:2] * _shift_up(dc, 2, row)
                          + w[0:1] * _shift_up(dc, 3, row)).astype(bf16)
            for tap in range(4):
                xs = x if tap == 3 else _shift_down(x, 3 - tap, row)
                dconv_ref[0, slot, tap:tap + 1, :] = jnp.sum(dc * xs, axis=0, keepdims=True)

        conv_back(q_ref, cq_ref, qn_s, dq_ref, 0, True, HEAD ** -0.5)
        conv_back(k_ref, ck_ref, kn_s, dk_ref, 1, True, 1.0)
        conv_back(v_ref, cv_ref, v_s, dv_ref, 2, False, 1.0)

    o512 = jax.ShapeDtypeStruct((B, S, BRANCH), bf16)
    s512 = pl.BlockSpec((1, S, HEAD), lambda b, h: (b, 0, h))
    ph = jax.ShapeDtypeStruct((B, N_HEADS, S, LANES), bf16)
    seq = pltpu.VMEM((S, HEAD), f32)
    cb = pltpu.VMEM((n, C, HEAD), bf16)
    return pl.pallas_call(
        body, name=name, grid=(B, N_HEADS), in_specs=ins,
        out_specs=[s512, s512, s512, s512, per_head, per_head,
                   pl.BlockSpec((1, 3, 4, HEAD), lambda b, h: (b, 0, 0, h)), pl.BlockSpec((1, 1, 8, HEAD), lambda b, h: (b, h, 0, 0))],
        out_shape=[o512, o512, o512, o512, ph, ph, jax.ShapeDtypeStruct((B, 3, 4, BRANCH), f32),
                   jax.ShapeDtypeStruct((B, N_HEADS, 8, HEAD), f32)],
        scratch_shapes=[seq, seq, seq, pltpu.VMEM((n, C, C), f32), cb, pltpu.VMEM((n, C, C), bf16), cb, cb, cb,
                        pltpu.VMEM((n, 1, HEAD), f32), cb, pltpu.VMEM((n, HEAD, HEAD), bf16), pltpu.VMEM((n, C, HEAD), f32)],
        compiler_params=_cp(("parallel", "parallel")),
    )(proj, proj, proj, proj, conv_w, conv_w, conv_w, gcb, betab, gcr, gain.reshape(1, HEAD), dy, states, vnew)


def _dn_post(ab, par, dg, dbeta, name):
    B, S, _ = ab.shape
    ts = _tile(S, (512, 256, 128))

    def body(ab_ref, par_ref, dg_ref, db_ref, dab_ref, acc_ref):
        x = ab_ref[0]
        lane = lax.broadcasted_iota(jnp.int32, x.shape, 1)
        dgs = jnp.zeros_like(x)
        dbs = jnp.zeros_like(x)
        for h in range(N_HEADS):
            dgs = jnp.where(lane == h, dg_ref[0, h], dgs)
            dbs = jnp.where(lane == N_HEADS + h, db_ref[0, h], dbs)
        nega = -jnp.exp(par_ref[0:1, :])
        pre = x + par_ref[1:2, :]
        da = dgs * nega * _sigmoid(pre)
        beta = _sigmoid(x)
        dab_ref[0] = (da + dbs * beta * (1.0 - beta)).astype(bf16)

        @pl.when((pl.program_id(0) == 0) & (pl.program_id(1) == 0))
        def _():
            acc_ref[...] = jnp.zeros_like(acc_ref)

        acc_ref[0] += _fold8(dgs * nega * _softplus(pre))
        acc_ref[1] += _fold8(da)

    return pl.pallas_call(
        body, name=name, grid=(B, S // ts),
        in_specs=[pl.BlockSpec((1, ts, LANES), lambda b, i: (b, i, 0)), pl.BlockSpec((8, LANES), lambda b, i: (0, 0)),
                  pl.BlockSpec((1, N_HEADS, ts, LANES), lambda b, i: (b, 0, i, 0)),
                  pl.BlockSpec((1, N_HEADS, ts, LANES), lambda b, i: (b, 0, i, 0))],
        out_specs=[pl.BlockSpec((1, ts, LANES), lambda b, i: (b, i, 0)), pl.BlockSpec((2, 8, LANES), lambda b, i: (0, 0, 0))],
        out_shape=[jax.ShapeDtypeStruct((B, S, LANES), bf16), jax.ShapeDtypeStruct((2, 8, LANES), f32)],
        compiler_params=_cp(("arbitrary", "arbitrary")),
    )(ab, par, dg, dbeta)


def _merge_specs(T, D, tm, tn, order):
    nj = D // tn

    def ij(f):
        return (lambda i, j: f(i, j)) if order == "ij" else (lambda j, i: f(i, j))

    ys = [pl.BlockSpec((tm, BRANCH), ij(lambda i, j: (i, 0))) for _ in range(3)]
    wb = pl.BlockSpec((3, BRANCH, tn), ij(lambda i, j: (0, 0, j)))
    gl = [pl.BlockSpec((tm, tn), ij(lambda i, j, k=k: (i, k * nj + j))) for k in range(3)]
    bg = [pl.BlockSpec((1, tn), ij(lambda i, j, k=k: (0, k * nj + j))) for k in range(3)]
    return ys, wb, gl, bg


def _merge_fwd(ys, wb, gl, b_gate, name):
    T, D = ys[0].shape[0], wb.shape[2]
    tm, tn = _tile(T, (512, 256, 128)), _tile(D, (512, 256, 128))
    sy, swb, sgl, sbg = _merge_specs(T, D, tm, tn, "ij")

    def body(y0, y1, y2, wb_ref, g0, g1, g2, b0, b1, b2, o_ref):
        acc = None
        for k, (y, g, b) in enumerate(((y0, g0, b0), (y1, g1, b1), (y2, g2, b2))):
            term = _sigmoid(g[...].astype(f32) + b[...]) * _dot(y[...], wb_ref[k])
            acc = term if acc is None else acc + term
        o_ref[...] = acc.astype(bf16)

    bg = b_gate.reshape(1, 3 * D)
    return pl.pallas_call(
        body, name=name, grid=(T // tm, D // tn), in_specs=sy + [swb] + sgl + sbg,
        out_specs=pl.BlockSpec((tm, tn), lambda i, j: (i, j)), out_shape=jax.ShapeDtypeStruct((T, D), bf16),
        compiler_params=_cp(("parallel", "parallel")),
    )(*ys, wb, gl, gl, gl, bg, bg, bg)


def _merge_bwd(dm, ys, wb, gl, b_gate, name):
    T, D = dm.shape
    tm, tn = _tile(T, (512, 256, 128)), _tile(D, (512, 256, 128))
    sy, swb, sgl, sbg = _merge_specs(T, D, tm, tn, "ji")

    def body(dm_ref, y0, y1, y2, wb_ref, g0, g1, g2, b0, b1, b2, dgl_ref, dbd_ref, dbg_ref):
        dmv = dm_ref[...].astype(f32)

        @pl.when(pl.program_id(1) == 0)
        def _():
            dbg_ref[...] = jnp.zeros_like(dbg_ref)

        for k, (y, g, b) in enumerate(((y0, g0, b0), (y1, g1, b1), (y2, g2, b2))):
            s = _sigmoid(g[...].astype(f32) + b[...])
            dg = dmv * _dot(y[...], wb_ref[k]) * s * (1.0 - s)
            dgl_ref[k] = dg.astype(bf16)
            dbd_ref[k] = (dmv * s).astype(bf16)
            dbg_ref[k] += _fold8(dg)

    bg = b_gate.reshape(1, 3 * D)
    o3 = jax.ShapeDtypeStruct((3, T, D), bf16)
    s3 = pl.BlockSpec((3, tm, tn), lambda j, i: (0, i, j))
    return pl.pallas_call(
        body, name=name, grid=(D // tn, T // tm),
        in_specs=[pl.BlockSpec((tm, tn), lambda j, i: (i, j))] + sy + [swb] + sgl + sbg,
        out_specs=[s3, s3, pl.BlockSpec((3, 8, tn), lambda j, i: (0, 0, j))],
        out_shape=[o3, o3, jax.ShapeDtypeStruct((3, 8, D), f32)],
        compiler_params=_cp(("parallel", "arbitrary")),
    )(dm, *ys, wb, gl, gl, gl, bg, bg, bg)


def _ffn_fwd(x, g, wg, wu, wd, tag):
    T, D = x.shape
    h = _rms_fwd(x, g, f"rms_{tag}")
    a, b, hm = _ffn_up(h, wg, wu, f"ffn_up_{tag}")
    y = _mm([("nn", hm, wd, {})], T, D, f32, f"ffn_down_{tag}", res=x, scale=0.5)
    return y, (x, h, a, b)


def _ffn_bwd(dy, saved, g, wg, wu, wd, tag):
    x, h, a, b = saved
    T, D = x.shape
    F = wg.shape[1]
    da, db, hm = _ffn_bwd_mid(dy, wd, a, b, f"ffn_mid_bwd_{tag}")
    dwd = _mm([("tn", hm, dy, {})], F, D, bf16, f"ffn_dwd_{tag}", scale=0.5)
    dwg = _mm([("tn", h, da, {})], D, F, bf16, f"ffn_dwg_{tag}")
    dwu = _mm([("tn", h, db, {})], D, F, bf16, f"ffn_dwu_{tag}")
    dh = _mm([("nt", da, wg, {}), ("nt", db, wu, {})], T, D, f32, f"ffn_dh_{tag}", tm=_tile(T, (256, 128)))
    dx, dg8 = _rms_bwd(dh, x, g, dy, f"rms_bwd_{tag}")
    return dx, dict(norm=jnp.sum(dg8, axis=0), wg=dwg, wu=dwu, wd=dwd)


def _layer_fwd(x, w, B, tag):
    T, D = x.shape
    S = T // B
    x1, sv0 = _ffn_fwd(x, w["ffn_norm"][0], w["wg"][0], w["wu"][0], w["wd"][0], f"pre_{tag}")
    h = _rms_fwd(x1, w["mix_norm"], f"rms_mix_{tag}")
    pm = _mm([("nn", h, w["w_main"], {})], T, 5 * BRANCH, bf16, f"proj_main_{tag}")
    ab = _mm([("nn", h, w["w_ab"], {})], T, LANES, f32, f"proj_ab_{tag}", tn=LANES)
    sb = _mm([("nn", h, w["w_sb"], {})], T, 3 * BRANCH, bf16, f"proj_sb_{tag}")
    gl = _mm([("nn", h, w["w_gates"], {})], T, 3 * D, bf16, f"proj_gates_{tag}")
    pm3, ab3, sb3 = pm.reshape(B, S, -1), ab.reshape(B, S, LANES), sb.reshape(B, S, -1)
    y_pool = _pool_fwd(pm3, w["pool_w"], w["pool_scale"], f"pool_{tag}")
    par = _dn_params(w["dn_A_log"], w["dn_dt_bias"])
    gcb, betab, gcr = _dn_prep(ab3, par, f"dn_prep_{tag}")
    y_dn, states, vnew = _dn_fwd(pm3, w["dn_conv"], gcb, betab, gcr, w["dn_out_norm"], f"dn_fwd_{tag}")
    y_sb, tot = _sb_fwd(sb3, f"sb_fwd_{tag}")
    ys = [y_pool.reshape(T, BRANCH), y_dn.reshape(T, BRANCH), y_sb.reshape(T, BRANCH)]
    merged = _merge_fwd(ys, w["w_branch"], gl, w["b_gate"], f"merge_{tag}")
    x2 = _mm([("nn", merged, w["w_out"], {})], T, D, f32, f"mix_out_{tag}", res=x1)
    x3, sv1 = _ffn_fwd(x2, w["ffn_norm"][1], w["wg"][1], w["wu"][1], w["wd"][1], f"post_{tag}")
    saved = dict(sv0=sv0, sv1=sv1, x1=x1, h=h, pm3=pm3, ab3=ab3, sb3=sb3, gl=gl, par=par, gcb=gcb, betab=betab, gcr=gcr,
                 states=states, vnew=vnew, tot=tot, ys=ys, merged=merged)
    return x3, saved


def _layer_bwd(dx3, w, sv, B, tag):
    T, D = dx3.shape
    S = T // B
    dx2, g1 = _ffn_bwd(dx3, sv["sv1"], w["ffn_norm"][1], w["wg"][1], w["wu"][1], w["wd"][1], f"post_{tag}")
    dmerged = _mm([("nt", dx2, w["w_out"], {})], T, D, bf16, f"mix_dmerged_{tag}")
    dw_out = _mm([("tn", sv["merged"], dx2, {})], D, D, bf16, f"mix_dwout_{tag}")
    ys = sv["ys"]
    dgl, dbd, dbg8 = _merge_bwd(dmerged, ys, w["w_branch"], sv["gl"], w["b_gate"], f"merge_bwd_{tag}")
    dys, dwb = [], []
    for k in range(3):
        dys.append(_mm([("nt", dbd, w["w_branch"], {"asel": k, "bsel": k})], T, BRANCH, bf16, f"branch_dy{k}_{tag}"))
        dwb.append(_mm([("tn", ys[k], dbd, {"bsel": k})], BRANCH, D, bf16, f"branch_dw{k}_{tag}"))
    pm3, ab3, sb3 = sv["pm3"], sv["ab3"], sv["sb3"]
    du, dpool_w, dsc8 = _pool_bwd(pm3, dys[0].reshape(B, S, BRANCH), w["pool_w"], w["pool_scale"], f"pool_bwd_{tag}")
    dq, dk, dv, dz, dg, dbeta, dconv, dgain = _dn_bwd(
        pm3, dys[1].reshape(B, S, BRANCH), w["dn_conv"], sv["gcb"], sv["betab"], sv["gcr"], w["dn_out_norm"],
        sv["states"], sv["vnew"], f"dn_bwd_{tag}")
    dab, dn_acc = _dn_post(ab3, sv["par"], dg, dbeta, f"dn_post_{tag}")
    dsq, dsk, dsv = _sb_bwd(sb3, dys[2].reshape(B, S, BRANCH), sv["tot"], f"sb_bwd_{tag}")
    main_parts = [t.reshape(T, BRANCH) for t in (du, dq, dk, dv, dz)]
    sb_parts = [t.reshape(T, BRANCH) for t in (dsq, dsk, dsv)]
    dab2 = dab.reshape(T, LANES)
    pairs = [("nt", t, w["w_main"], {"K": BRANCH, "kb": k}) for k, t in enumerate(main_parts)]
    pairs.append(("nt", dab2, w["w_ab"], {}))
    pairs += [("nt", t, w["w_sb"], {"K": BRANCH, "kb": k}) for k, t in enumerate(sb_parts)]
    pairs += [("nt", dgl, w["w_gates"], {"K": D, "kb": k, "asel": k}) for k in range(3)]
    dh = _mm(pairs, T, D, f32, f"mix_dh_{tag}", tm=_tile(T, (256, 128)))
    h = sv["h"]
    dw_cols = [_mm([("tn", h, t, {})], D, BRANCH, bf16, f"dwin_main{k}_{tag}") for k, t in enumerate(main_parts)]
    dw_cols.append(_mm([("tn", h, dab2, {})], D, LANES, bf16, f"dwin_ab_{tag}", tn=LANES)[:, :2 * N_HEADS])
    dw_cols += [_mm([("tn", h, t, {})], D, BRANCH, bf16, f"dwin_sb{k}_{tag}") for k, t in enumerate(sb_parts)]
    dw_cols += [_mm([("tn", h, dgl, {"bsel": k})], D, D, bf16, f"dwin_gate{k}_{tag}") for k in range(3)]
    dx1, dmix8 = _rms_bwd(dh, sv["x1"], w["mix_norm"], dx2, f"rms_mix_bwd_{tag}")
    dx0, g0 = _ffn_bwd(dx1, sv["sv0"], w["ffn_norm"][0], w["wg"][0], w["wu"][0], w["wd"][0], f"pre_{tag}")
    grads = dict(
        ffn_norm=jnp.stack([g0["norm"], g1["norm"]]), wg=[g0["wg"], g1["wg"]], wu=[g0["wu"], g1["wu"]], wd=[g0["wd"], g1["wd"]],
        mix_norm=jnp.sum(dmix8, axis=0), w_in=jnp.concatenate(dw_cols, axis=1), b_gate=jnp.sum(dbg8, axis=1).reshape(3 * D),
        pool_w=dpool_w, pool_scale=jnp.sum(dsc8, axis=0), dn_conv=jnp.sum(dconv, axis=0).transpose(1, 0, 2).reshape(4, 3 * BRANCH),
        dn_A_log=jnp.sum(dn_acc[0], axis=0)[:N_HEADS], dn_dt_bias=jnp.sum(dn_acc[1], axis=0)[:N_HEADS],
        dn_out_norm=jnp.sum(dgain, axis=(0, 1, 2)), w_branch=jnp.stack(dwb), w_out=dw_out)
    return dx0, grads


def _local_step(x, target, layers, final_norm, B):
    T, D = x.shape
    saved = []
    for l, w in enumerate(layers):
        x, sv = _layer_fwd(x, w, B, f"l{l}")
        saved.append(sv)
    dx, dfn8, ls8 = _final_loss(x, target, final_norm)
    grads = [None] * len(layers)
    for l in reversed(range(len(layers))):
        dx, grads[l] = _layer_bwd(dx, layers[l], saved[l], B, f"l{l}")
    return jnp.sum(ls8), dx, grads, jnp.sum(dfn8, axis=0)


def _adamw(w, g, m, v, name):
    shape = w.shape
    cols = shape[-1]
    rows = math.prod(shape[:-1]) if len(shape) > 1 else 1
    w2, g2, m2, v2 = (t.reshape(rows, cols) for t in (w, g, m, v))
    block_elems = 256 * 1024
    tr = rows if rows * cols <= block_elems else _tile(rows, [t for t in (512, 256, 128, 64, 32, 16, 8) if t * cols <= block_elems])

    def body(w_ref, g_ref, m_ref, v_ref, d_ref, mo_ref, vo_ref):
        gv = g_ref[...]
        mn = ADAM_B1 * m_ref[...] + (1.0 - ADAM_B1) * gv
        vn = ADAM_B2 * v_ref[...] + (1.0 - ADAM_B2) * (gv * gv)
        m_hat = mn / (1.0 - ADAM_B1 ** ADAM_STEP)
        v_hat = vn / (1.0 - ADAM_B2 ** ADAM_STEP)
        d_ref[...] = -ADAM_LR * (m_hat / (jnp.sqrt(v_hat) + ADAM_EPS) + ADAM_WD * w_ref[...])
        mo_ref[...] = mn
        vo_ref[...] = vn

    spec = pl.BlockSpec((tr, cols), lambda i: (i, 0))
    o = jax.ShapeDtypeStruct((rows, cols), f32)
    d, mo, vo = pl.pallas_call(
        body, name=name, grid=(rows // tr,), in_specs=[spec] * 4, out_specs=[spec] * 3, out_shape=[o, o, o],
        compiler_params=_cp(("parallel",)),
    )(w2, g2, m2, v2)
    return d.reshape(shape), mo.reshape(shape), vo.reshape(shape)


MESH = pl.DeviceIdType.MESH
_ANY = pl.BlockSpec(memory_space=pl.ANY)


def _place():
    x, y, c = lax.axis_index("x"), lax.axis_index("y"), lax.axis_index("c")
    return x, y, c, [(1 - x, y), (x, 1 - y), (1 - x, 1 - y)]


def _chip_index():
    return 2 * lax.axis_index("x") + lax.axis_index("y")


def _half(c, rh):
    return pl.ds(c * rh, rh)


def _remote(src, dst, ssem, rsem, to):
    return pltpu.make_async_remote_copy(src_ref=src, dst_ref=dst, send_sem=ssem, recv_sem=rsem, device_id=to,
                                        device_id_type=MESH)


def _gather_chips(shard, name):
    R, W = shard.shape
    rh = R // 2

    def body(x_ref, out_ref, ssem, rsem, fsend, frecv):
        x, y, c, chips = _place()
        me = 2 * x + y
        started = []
        for k, (px, py) in enumerate(chips):
            cp = _remote(x_ref.at[_half(c, rh)], out_ref.at[me, _half(c, rh)], ssem.at[k], rsem.at[k], (px, py, c))
            cp.start()
            started.append(cp)
        for k, (px, py) in enumerate(chips):
            block = out_ref.at[2 * px + py, _half(c, rh)]
            _remote(block, block, ssem.at[k], rsem.at[k], (px, py, c)).wait_recv()
            cp = _remote(block, block, fsend.at[k], frecv.at[k], (x, y, 1 - c))
            cp.start()
            started.append(cp)
        for k, (px, py) in enumerate(chips):
            block = out_ref.at[2 * px + py, _half(1 - c, rh)]
            _remote(block, block, fsend.at[k], frecv.at[k], (x, y, 1 - c)).wait_recv()
        for cp in started:
            cp.wait_send()

    out = pl.pallas_call(
        body, name=name, in_specs=[_ANY], out_specs=_ANY, out_shape=jax.ShapeDtypeStruct((N_CHIPS, R, W), shard.dtype),
        scratch_shapes=[pltpu.SemaphoreType.DMA((3,))] * 4,
    )(shard)
    return lax.dynamic_update_slice(out, shard[None], (_chip_index(), 0, 0))


def _pair_swap_halves(p, name):
    n, R, W = p.shape
    rh = R // 2

    def body(p_ref, out_ref, ssem, rsem):
        x, y, c, _ = _place()
        cp = _remote(p_ref.at[:, _half(1 - c, rh)], out_ref, ssem, rsem, (x, y, 1 - c))
        cp.start()
        cp.wait()

    return pl.pallas_call(
        body, name=name, in_specs=[_ANY], out_specs=_ANY, out_shape=jax.ShapeDtypeStruct((n, rh, W), p.dtype),
        scratch_shapes=[pltpu.SemaphoreType.DMA, pltpu.SemaphoreType.DMA],
    )(p)


def _pair_add(p, got, name):
    n, R, W = p.shape
    rh = R // 2
    tr = _tile(rh, (512, 256, 192, 128, 96, 64, 32, 16, 8))
    nb = rh // tr

    def body(c_ref, p_ref, g_ref, o_ref):
        o_ref[...] = (p_ref[...].astype(f32) + g_ref[...].astype(f32)).astype(o_ref.dtype)

    return pl.pallas_call(
        body, name=name,
        grid_spec=pltpu.PrefetchScalarGridSpec(
            num_scalar_prefetch=1, grid=(n, nb),
            in_specs=[pl.BlockSpec((1, tr, W), lambda j, i, c_ref: (j, c_ref[0] * nb + i, 0)),
                      pl.BlockSpec((1, tr, W), lambda j, i, c_ref: (j, i, 0))],
            out_specs=pl.BlockSpec((1, tr, W), lambda j, i, c_ref: (j, i, 0))),
        out_shape=jax.ShapeDtypeStruct((n, rh, W), p.dtype), compiler_params=_cp(("parallel", "parallel")),
    )(lax.axis_index("c").astype(jnp.int32).reshape(1), p, got)


def _scatter_chips(ps, name):
    n, R, W = ps.shape

    def body(p_ref, out_ref, ssem, rsem):
        x, y, c, chips = _place()
        me = 2 * x + y
        started = []
        for k, (px, py) in enumerate(chips):
            cp = _remote(p_ref.at[2 * px + py], out_ref.at[me], ssem.at[k], rsem.at[k], (px, py, c))
            cp.start()
            started.append(cp)
        for k, (px, py) in enumerate(chips):
            slot = out_ref.at[2 * px + py]
            _remote(slot, slot, ssem.at[k], rsem.at[k], (px, py, c)).wait_recv()
        for cp in started:
            cp.wait_send()

    out = pl.pallas_call(
        body, name=name, in_specs=[_ANY], out_specs=_ANY, out_shape=jax.ShapeDtypeStruct((n, R, W), ps.dtype),
        scratch_shapes=[pltpu.SemaphoreType.DMA((3,)), pltpu.SemaphoreType.DMA((3,))],
    )(ps)
    me = _chip_index()
    return lax.dynamic_update_slice(out, lax.dynamic_slice(ps, (me, 0, 0), (1, R, W)), (me, 0, 0))


def _sum_slots(r4, name):
    n, R, W = r4.shape
    tr = _tile(R, (512, 256, 192, 128, 96, 64, 32, 16, 8))

    def body(r_ref, o_ref):
        acc = r_ref[0].astype(f32)
        for k in range(1, n):
            acc = acc + r_ref[k].astype(f32)
        o_ref[...] = acc

    return pl.pallas_call(
        body, name=name, grid=(R // tr,), in_specs=[pl.BlockSpec((n, tr, W), lambda i: (0, i, 0))],
        out_specs=pl.BlockSpec((tr, W), lambda i: (i, 0)), out_shape=jax.ShapeDtypeStruct((R, W), f32),
        compiler_params=_cp(("parallel",)),
    )(r4)


def _pair_share(s, name):
    R, W = s.shape

    def body(s_ref, out_ref, ssem, rsem):
        x, y, c, _ = _place()
        cp = _remote(s_ref, out_ref.at[c], ssem, rsem, (x, y, 1 - c))
        cp.start()
        _remote(s_ref, out_ref.at[1 - c], ssem, rsem, (x, y, 1 - c)).wait_recv()
        cp.wait_send()

    out = pl.pallas_call(
        body, name=name, in_specs=[_ANY], out_specs=_ANY, out_shape=jax.ShapeDtypeStruct((2, R, W), s.dtype),
        scratch_shapes=[pltpu.SemaphoreType.DMA, pltpu.SemaphoreType.DMA],
    )(s)
    return lax.dynamic_update_slice(out, s[None], (lax.axis_index("c"), 0, 0))


def _reduce_to_chips(p, tag):
    n, R, W = p.shape
    got = _pair_swap_halves(p, f"rs_pair_swap_{tag}")
    ps = _pair_add(p, got, f"rs_pair_add_{tag}")
    r4 = _scatter_chips(ps, f"rs_scatter_{tag}")
    s = _sum_slots(r4, f"rs_sum_{tag}")
    return _pair_share(s, f"rs_share_{tag}").reshape(R, W)


W_IN_ROWS_PAD = 16


def _rows(a):
    return a.reshape(-1, PACK_W)


def _pad_rows(a, mult):
    r = (-a.shape[-2]) % mult
    return a if r == 0 else jnp.pad(a, [(0, 0)] * (a.ndim - 2) + [(0, r), (0, 0)])


def _pack_layer_shard(wg, wu, wd, w_in, w_branch, w_out):
    parts = [wg, wu, wd, w_in, w_branch, w_out]
    lead = wg.shape[:-3]
    out = []
    for a, nd in zip(parts, (3, 3, 3, 2, 3, 2)):
        flat = a.reshape(lead + (-1, PACK_W))
        out.append(_pad_rows(flat, W_IN_ROWS_PAD))
    return jnp.concatenate(out, axis=-2)


def _small_pack(pieces):
    flat, offs, r = [], [], 0
    for a in pieces:
        v = a.reshape(-1)
        pad = (-v.shape[0]) % PACK_W
        flat.append(jnp.pad(v, (0, pad)) if pad else v)
        offs.append(r)
        r += (v.shape[0] + pad) // PACK_W
    pack = jnp.concatenate(flat).reshape(r, PACK_W)
    return _pad_rows(pack, 16), offs


def _small_unpack(pack, offs, shapes):
    out = []
    for o, s in zip(offs, shapes):
        n = math.prod(s)
        rows = -(-n // PACK_W)
        out.append(pack[o:o + rows].reshape(-1)[:n].reshape(s))
    return out


SMALL_SHARDED = ("ffn_norm", "dn_conv")
SMALL_REPLICATED = ("mix_norm", "b_gate", "pool_w", "pool_scale", "dn_A_log", "dn_dt_bias", "dn_out_norm")


def kernel(x, ffn_norm, ffn_w_gate, ffn_w_up, ffn_w_down, mix_norm, w_in, b_gate, pool_w, pool_scale, dn_conv, dn_A_log, dn_dt_bias, dn_out_norm, w_branch, w_out, final_norm, loss_target, m_ffn_norm, m_ffn_w_gate, m_ffn_w_up, m_ffn_w_down, m_mix_norm, m_w_in, m_b_gate, m_pool_w, m_pool_scale, m_dn_conv, m_dn_A_log, m_dn_dt_bias, m_dn_out_norm, m_w_branch, m_w_out, m_final_norm, v_ffn_norm, v_ffn_w_gate, v_ffn_w_up, v_ffn_w_down, v_mix_norm, v_w_in, v_b_gate, v_pool_w, v_pool_scale, v_dn_conv, v_dn_A_log, v_dn_dt_bias, v_dn_out_norm, v_w_branch, v_w_out, v_final_norm):
    names = ("ffn_norm", "ffn_w_gate", "ffn_w_up", "ffn_w_down", "mix_norm", "w_in", "b_gate", "pool_w", "pool_scale", "dn_conv",
             "dn_A_log", "dn_dt_bias", "dn_out_norm", "w_branch", "w_out", "final_norm")
    wts = dict(zip(names, (ffn_norm, ffn_w_gate, ffn_w_up, ffn_w_down, mix_norm, w_in, b_gate, pool_w, pool_scale, dn_conv,
                           dn_A_log, dn_dt_bias, dn_out_norm, w_branch, w_out, final_norm)))
    ms = dict(zip(names, (m_ffn_norm, m_ffn_w_gate, m_ffn_w_up, m_ffn_w_down, m_mix_norm, m_w_in, m_b_gate, m_pool_w, m_pool_scale,
                          m_dn_conv, m_dn_A_log, m_dn_dt_bias, m_dn_out_norm, m_w_branch, m_w_out, m_final_norm)))
    vs = dict(zip(names, (v_ffn_norm, v_ffn_w_gate, v_ffn_w_up, v_ffn_w_down, v_mix_norm, v_w_in, v_b_gate, v_pool_w, v_pool_scale,
                          v_dn_conv, v_dn_A_log, v_dn_dt_bias, v_dn_out_norm, v_w_branch, v_w_out, v_final_norm)))
    B, S, D = x.shape
    T = B * S
    L = ffn_w_gate.shape[0]
    F = ffn_w_gate.shape[3] * N_CHIPS
    fs, ds_, cs = F // N_CHIPS, D // N_CHIPS, dn_conv.shape[2]
    p_in = w_in.shape[2] * N_CHIPS
    chip = 2 * lax.axis_index("x") + lax.axis_index("y")

    big = jnp.concatenate(
        [_pack_layer_shard(ffn_w_gate[l].astype(bf16), ffn_w_up[l].astype(bf16), ffn_w_down[l].astype(bf16), w_in[l].astype(bf16),
                           w_branch[l].astype(bf16), w_out[l].astype(bf16)) for l in range(L)], axis=0)
    rows_layer = big.shape[0] // L
    gathered = _gather_chips(big, "gather_weights")
    small_shard, soffs = _small_pack([ffn_norm, dn_conv])
    small_g = _gather_chips(small_shard, "gather_small")

    def chip_major(i, a):
        rows = -(-a.size // PACK_W)
        return small_g[:, soffs[i]:soffs[i] + rows].reshape(N_CHIPS, -1)[:, :a.size].reshape((N_CHIPS,) + a.shape)

    fn_full = jnp.moveaxis(chip_major(0, ffn_norm), 0, 2).reshape(L, 2, D)
    conv_full = jnp.moveaxis(chip_major(1, dn_conv), 0, 2).reshape(L, dn_conv.shape[1], N_CHIPS * cs)

    r_ffn = 2 * D * fs // PACK_W
    r_in = D * w_in.shape[2] // PACK_W
    r_in_pad = r_in + (-r_in) % W_IN_ROWS_PAD
    r_br = 3 * BRANCH * ds_ // PACK_W
    r_out = ds_ * D // PACK_W
    layers = []
    for l in range(L):
        g = gathered[:, l * rows_layer:(l + 1) * rows_layer]
        o = 0
        wg = g[:, o:o + r_ffn].reshape(N_CHIPS, 2, D, fs).transpose(1, 2, 0, 3).reshape(2, D, F); o += r_ffn
        wu = g[:, o:o + r_ffn].reshape(N_CHIPS, 2, D, fs).transpose(1, 2, 0, 3).reshape(2, D, F); o += r_ffn
        wd = g[:, o:o + r_ffn].reshape(N_CHIPS, 2, fs, D).transpose(1, 0, 2, 3).reshape(2, F, D); o += r_ffn
        wi = g[:, o:o + r_in].reshape(N_CHIPS, D, p_in // N_CHIPS).transpose(1, 0, 2).reshape(D, p_in); o += r_in_pad
        wb = g[:, o:o + r_br].reshape(N_CHIPS, 3, BRANCH, ds_).transpose(1, 2, 0, 3).reshape(3, BRANCH, D); o += r_br
        wo = g[:, o:o + r_out].reshape(D, D)
        c0, c1, c2 = 5 * BRANCH, 5 * BRANCH + 2 * N_HEADS, 8 * BRANCH + 2 * N_HEADS
        layers.append(dict(
            ffn_norm=fn_full[l], wg=[wg[0], wg[1]], wu=[wu[0], wu[1]], wd=[wd[0], wd[1]], mix_norm=mix_norm[l],
            w_main=wi[:, :c0], w_ab=jnp.pad(wi[:, c0:c1], ((0, 0), (0, LANES - 2 * N_HEADS))), w_sb=wi[:, c1:c2], w_gates=wi[:, c2:],
            b_gate=b_gate[l], pool_w=pool_w[l], pool_scale=pool_scale[l], dn_conv=conv_full[l], dn_A_log=dn_A_log[l],
            dn_dt_bias=dn_dt_bias[l], dn_out_norm=dn_out_norm[l], w_branch=wb, w_out=wo))

    sq, dx, grads, dfn = _local_step(x.reshape(T, D), loss_target.reshape(T, D), layers, final_norm, B)
    loss = lax.psum(sq, ("x", "y", "c")) * (0.5 / D)

    def by_chip(a, axis, n_lead):
        s = a.shape
        a = a.reshape(s[:axis] + (N_CHIPS, s[axis] // N_CHIPS) + s[axis + 1:])
        return jnp.moveaxis(a, axis, 0)

    gbig = jnp.concatenate(
        [_pack_layer_shard(by_chip(jnp.stack(g["wg"]), 2, 0), by_chip(jnp.stack(g["wu"]), 2, 0), by_chip(jnp.stack(g["wd"]), 1, 0),
                           by_chip(g["w_in"], 1, 0), by_chip(g["w_branch"], 2, 0), by_chip(g["w_out"], 0, 0)) for g in grads], axis=1)
    red = _reduce_to_chips(gbig, "big")
    small_names = SMALL_SHARDED + SMALL_REPLICATED
    pieces = [g[k] for g in grads for k in small_names] + [dfn]
    spack, offs = _small_pack(pieces)
    sred = _reduce_to_chips(jnp.broadcast_to(spack[None], (N_CHIPS,) + spack.shape), "small")
    small_red = _small_unpack(sred, offs, [p.shape for p in pieces])

    gw = {k: [] for k in names if k != "final_norm"}
    for l in range(L):
        g = red[l * rows_layer:(l + 1) * rows_layer]
        o = 0
        gw["ffn_w_gate"].append(g[o:o + r_ffn].reshape(2, D, fs)); o += r_ffn
        gw["ffn_w_up"].append(g[o:o + r_ffn].reshape(2, D, fs)); o += r_ffn
        gw["ffn_w_down"].append(g[o:o + r_ffn].reshape(2, fs, D)); o += r_ffn
        gw["w_in"].append(g[o:o + r_in].reshape(D, p_in // N_CHIPS)); o += r_in_pad
        gw["w_branch"].append(g[o:o + r_br].reshape(3, BRANCH, ds_)); o += r_br
        gw["w_out"].append(g[o:o + r_out].reshape(ds_, D))
        sm = dict(zip(small_names, small_red[l * len(small_names):(l + 1) * len(small_names)]))
        gw["ffn_norm"].append(lax.dynamic_slice_in_dim(sm["ffn_norm"], chip * ds_, ds_, axis=1))
        gw["dn_conv"].append(lax.dynamic_slice_in_dim(sm["dn_conv"], chip * cs, cs, axis=1))
        for k in SMALL_REPLICATED:
            gw[k].append(sm[k])
    gw = {k: jnp.stack(v) for k, v in gw.items()}
    gw["final_norm"] = small_red[-1]

    deltas, new_m, new_v = [], [], []
    for k in names:
        d, mo, vo = _adamw(wts[k], gw[k], ms[k], vs[k], f"adamw_{k}")
        deltas.append(d)
        new_m.append(mo)
        new_v.append(vo)
    return (loss, dx.reshape(B, S, D), *[gw[k] for k in names], *deltas, *new_m, *new_v)
```

```python
import functools
import math

import jax
import jax.numpy as jnp
from jax import lax
from jax.experimental import pallas as pl
from jax.experimental.pallas import tpu as pltpu

f32 = jnp.float32
bf16 = jnp.bfloat16
HI = lax.Precision.HIGHEST

EPS = 1e-6
HEAD = 128
N_HEADS = 4
BRANCH = 512
CHUNK = 64
SB_BLOCK = 128
SB_QUERIES = 512
POOL_WINDOWS = (2, 4, 8, 16)
N_CHIPS = 4
LANES = 128
PACK_W = 1024
ADAM_LR, ADAM_B1, ADAM_B2, ADAM_EPS, ADAM_WD, ADAM_STEP = 0.001, 0.9, 0.999, 1e-08, 0.01, 10
VMEM_LIMIT = 56 * 1024 * 1024

NN = (((1,), (0,)), ((), ()))
NT = (((1,), (1,)), ((), ()))
TN = (((0,), (0,)), ((), ()))


def _cp(sem=None):
    return pltpu.CompilerParams(dimension_semantics=sem, vmem_limit_bytes=VMEM_LIMIT)


def _tile(n, prefs):
    for p in prefs:
        if n % p == 0:
            return p
    return n


def _dot(a, b, dn=NN):
    return lax.dot_general(a, b, dn, preferred_element_type=f32)


def _sigmoid(x):
    return 1.0 / (1.0 + jnp.exp(-x))


def _softplus(x):
    return jnp.maximum(x, 0.0) + jnp.log1p(jnp.exp(-jnp.abs(x)))


def _mm(pairs, M, N, out_dtype, name, tm=None, tn=None, res=None, scale=1.0, out_chip=False):
    tm = tm or _tile(M, (512, 256, 128))
    tn = N // N_CHIPS if out_chip else (tn or _tile(N, (512, 256, 128)))
    specs, arrs, dns = [], [], []

    def lead(sel, shape, imap):
        if sel is None:
            return pl.BlockSpec(shape, imap)
        return pl.BlockSpec((None,) + shape, lambda i, j, sel=sel, imap=imap: (sel,) + imap(i, j))

    for form, a, b, o in pairs:
        ka, kb, moff, noff = o.get("ka", 0), o.get("kb", 0), o.get("moff", 0), o.get("noff", 0)
        asel, bsel = o.get("asel"), o.get("bsel")
        if form == "nn":
            K = o.get("K") or a.shape[-1]
            sa = lead(asel, (tm, K), lambda i, j, ka=ka, moff=moff: (i + moff, ka))
            sb = lead(bsel, (K, tn), lambda i, j, kb=kb, noff=noff: (kb, j + noff))
            dn = NN
        elif form == "nt":
            K = o.get("K") or a.shape[-1]
            sa = lead(asel, (tm, K), lambda i, j, ka=ka, moff=moff: (i + moff, ka))
            if o.get("b_by_chip"):
                sb = pl.BlockSpec((None, tn, K), lambda i, j, kb=kb, noff=noff: (j, noff, kb))
            else:
                sb = lead(bsel, (tn, K), lambda i, j, kb=kb, noff=noff: (j + noff, kb))
            dn = NT
        else:
            K = a.shape[-2]
            sa = lead(asel, (K, tm), lambda i, j, moff=moff: (0, i + moff))
            sb = lead(bsel, (K, tn), lambda i, j, noff=noff: (0, j + noff))
            dn = TN
        specs += [sa, sb]
        arrs += [a, b]
        dns.append(dn)
    if res is not None:
        specs.append(pl.BlockSpec((tm, tn), lambda i, j: (i, j)))
        arrs.append(res)
    n = len(pairs)

    def body(*refs):
        o_ref = refs[-1]
        acc = None
        for p in range(n):
            d = _dot(refs[2 * p][...].astype(bf16), refs[2 * p + 1][...].astype(bf16), dns[p])
            acc = d if acc is None else acc + d
        if scale != 1.0:
            acc = acc * scale
        if res is not None:
            acc = acc + refs[2 * n][...]
        o_ref[...] = acc.astype(o_ref.dtype)

    if out_chip:
        out_spec = pl.BlockSpec((None, tm, tn), lambda i, j: (j, i, 0))
        out_shape = jax.ShapeDtypeStruct((N_CHIPS, M, tn), out_dtype)
    else:
        out_spec = pl.BlockSpec((tm, tn), lambda i, j: (i, j))
        out_shape = jax.ShapeDtypeStruct((M, N), out_dtype)
    return pl.pallas_call(
        body, name=name, grid=(M // tm, N // tn), in_specs=specs, out_specs=out_spec, out_shape=out_shape,
        compiler_params=_cp(("parallel", "parallel")),
    )(*arrs)


def _rms_fwd(x, g, name):
    T, D = x.shape
    tm = _tile(T, (512, 256, 128))

    def body(x_ref, g_ref, h_ref):
        xv = x_ref[...]
        r = lax.rsqrt(jnp.mean(xv * xv, axis=-1, keepdims=True) + EPS)
        h_ref[...] = (xv * r * g_ref[...]).astype(bf16)

    return pl.pallas_call(
        body, name=name, grid=(T // tm,),
        in_specs=[pl.BlockSpec((tm, D), lambda i: (i, 0)), pl.BlockSpec((1, D), lambda i: (0, 0))],
        out_specs=pl.BlockSpec((tm, D), lambda i: (i, 0)),
        out_shape=jax.ShapeDtypeStruct((T, D), bf16), compiler_params=_cp(("parallel",)),
    )(x, g.reshape(1, D))


def _fold8(v):
    r, d = v.shape
    return jnp.sum(v.reshape(r // 8, 8, d), axis=0)


def _rms_bwd(dh, x, g, dres, name):
    T, D = x.shape
    tm = _tile(T, (512, 256, 128))

    def body(dh_ref, x_ref, g_ref, dres_ref, dx_ref, dg_ref):
        xv = x_ref[...]
        r = lax.rsqrt(jnp.mean(xv * xv, axis=-1, keepdims=True) + EPS)
        xh = xv * r
        dhv = dh_ref[...]
        dxh = dhv * g_ref[...]
        dx_ref[...] = dres_ref[...] + r * (dxh - xh * jnp.mean(dxh * xh, axis=-1, keepdims=True))

        @pl.when(pl.program_id(0) == 0)
        def _():
            dg_ref[...] = jnp.zeros_like(dg_ref)

        dg_ref[...] += _fold8(dhv * xh)

    return pl.pallas_call(
        body, name=name, grid=(T // tm,),
        in_specs=[pl.BlockSpec((tm, D), lambda i: (i, 0)), pl.BlockSpec((tm, D), lambda i: (i, 0)),
                  pl.BlockSpec((1, D), lambda i: (0, 0)), pl.BlockSpec((tm, D), lambda i: (i, 0))],
        out_specs=[pl.BlockSpec((tm, D), lambda i: (i, 0)), pl.BlockSpec((8, D), lambda i: (0, 0))],
        out_shape=[jax.ShapeDtypeStruct((T, D), f32), jax.ShapeDtypeStruct((8, D), f32)],
        compiler_params=_cp(("arbitrary",)),
    )(dh, x, g.reshape(1, D), dres)


def _final_loss(x, target, g):
    T, D = x.shape
    tm = _tile(T, (512, 256, 128))

    def body(x_ref, t_ref, g_ref, dx_ref, dg_ref, ls_ref):
        xv = x_ref[...]
        r = lax.rsqrt(jnp.mean(xv * xv, axis=-1, keepdims=True) + EPS)
        xh = xv * r
        gv = g_ref[...]
        e = xh * gv - t_ref[...]
        dy = e * (1.0 / D)
        dxh = dy * gv
        dx_ref[...] = r * (dxh - xh * jnp.mean(dxh * xh, axis=-1, keepdims=True))

        @pl.when(pl.program_id(0) == 0)
        def _():
            dg_ref[...] = jnp.zeros_like(dg_ref)
            ls_ref[...] = jnp.zeros_like(ls_ref)

        dg_ref[...] += _fold8(dy * xh)
        ls_ref[...] += _fold8(e * e)

    return pl.pallas_call(
        body, name="final_loss", grid=(T // tm,),
        in_specs=[pl.BlockSpec((tm, D), lambda i: (i, 0)), pl.BlockSpec((tm, D), lambda i: (i, 0)),
                  pl.BlockSpec((1, D), lambda i: (0, 0))],
        out_specs=[pl.BlockSpec((tm, D), lambda i: (i, 0)), pl.BlockSpec((8, D), lambda i: (0, 0)),
                   pl.BlockSpec((8, D), lambda i: (0, 0))],
        out_shape=[jax.ShapeDtypeStruct((T, D), f32), jax.ShapeDtypeStruct((8, D), f32),
                   jax.ShapeDtypeStruct((8, D), f32)],
        compiler_params=_cp(("arbitrary",)),
    )(x, target, g.reshape(1, D))


def _ffn_up(h, wgu, f, name):
    T, D = h.shape
    FP = wgu.shape[2]
    tm = _tile(T, (512, 256, 128))

    def body(h_ref, wg_ref, wu_ref, a_ref, b_ref, m_ref):
        hv = h_ref[...]
        a = _dot(hv, wg_ref[...])
        b = _dot(hv, wu_ref[...])
        a_ref[...] = a.astype(bf16)
        b_ref[...] = b.astype(bf16)
        m_ref[...] = (a * _sigmoid(a) * b).astype(bf16)

    o = jax.ShapeDtypeStruct((T, N_CHIPS * FP), bf16)
    ospec = pl.BlockSpec((tm, FP), lambda j, i: (i, j))
    return pl.pallas_call(
        body, name=name, grid=(N_CHIPS, T // tm),
        in_specs=[pl.BlockSpec((tm, D), lambda j, i: (i, 0)), pl.BlockSpec((None, D, FP), lambda j, i: (j, 2 * f, 0)),
                  pl.BlockSpec((None, D, FP), lambda j, i: (j, 2 * f + 1, 0))],
        out_specs=[ospec, ospec, ospec], out_shape=[o, o, o], compiler_params=_cp(("parallel", "parallel")),
    )(h, wgu, wgu)


def _ffn_bwd_mid(dx, m, a, b, f, name):
    T, D = dx.shape
    FP = a.shape[1] // N_CHIPS
    tm = _tile(T, (512, 256, 128))

    def body(dx_ref, wd_ref, a_ref, b_ref, da_ref, db_ref, m_ref):
        dm = 0.5 * _dot(dx_ref[...].astype(bf16), wd_ref[...], NT)
        av = a_ref[...].astype(f32)
        bv = b_ref[...].astype(f32)
        s = _sigmoid(av)
        silu = av * s
        da_ref[...] = (dm * bv * (s * (1.0 + av * (1.0 - s)))).astype(bf16)
        db_ref[...] = (dm * silu).astype(bf16)
        m_ref[...] = (silu * bv).astype(bf16)

    o = jax.ShapeDtypeStruct((T, N_CHIPS * FP), bf16)
    ospec = pl.BlockSpec((tm, FP), lambda j, i: (i, j))
    return pl.pallas_call(
        body, name=name, grid=(N_CHIPS, T // tm),
        in_specs=[pl.BlockSpec((tm, D), lambda j, i: (i, 0)), pl.BlockSpec((None, FP, D), lambda j, i: (j, f, 0)), ospec, ospec],
        out_specs=[ospec, ospec, ospec], out_shape=[o, o, o], compiler_params=_cp(("parallel", "parallel")),
    )(dx, m, a, b)


def _shift_down(x, k, row):
    return jnp.where(row < k, 0.0, pltpu.roll(x, k, 0))


def _shift_up(x, k, row):
    s = x.shape[0]
    return jnp.where(row >= s - k, 0.0, pltpu.roll(x, s - k, 0))


def _window_sum(x, g, row, shift):
    s2 = x + shift(x, 1, row)
    s4 = s2 + shift(s2, 2, row)
    s8 = s4 + shift(s4, 4, row)
    s16 = s8 + shift(s8, 8, row)
    return jnp.where(g == 0, s2, jnp.where(g == 1, s4, jnp.where(g == 2, s8, s16)))


def _pool_fwd(proj, pool_w, scale, name):
    B, S = proj.shape[0], proj.shape[1]
    G = len(POOL_WINDOWS)

    def body(u_ref, w_ref, sc_ref, y_ref):
        g = pl.program_id(1)
        u = u_ref[0].astype(f32)
        row = lax.broadcasted_iota(jnp.int32, u.shape, 0)
        win = _window_sum(u, g, row, _shift_down)
        cnt = jnp.minimum(row + 1, jnp.left_shift(2, g)).astype(f32)
        pooled = win / cnt - u
        mixed = _dot(pooled.astype(bf16), w_ref[0].astype(bf16))
        y_ref[0] = (mixed * sc_ref[...]).astype(bf16)

    return pl.pallas_call(
        body, name=name, grid=(B, G),
        in_specs=[pl.BlockSpec((1, S, HEAD), lambda b, g: (b, 0, g)), pl.BlockSpec((1, HEAD, HEAD), lambda b, g: (g, 0, 0)),
                  pl.BlockSpec((1, HEAD), lambda b, g: (0, g))],
        out_specs=pl.BlockSpec((1, S, HEAD), lambda b, g: (b, 0, g)),
        out_shape=jax.ShapeDtypeStruct((B, S, BRANCH), bf16), compiler_params=_cp(("parallel", "parallel")),
    )(proj, pool_w, scale.reshape(1, BRANCH))


def _pool_bwd(proj, dy, pool_w, scale, name):
    B, S = proj.shape[0], proj.shape[1]
    G = len(POOL_WINDOWS)

    def body(u_ref, dy_ref, w_ref, sc_ref, du_ref, dw_ref, dsc_ref):
        g = pl.program_id(0)
        u = u_ref[0].astype(f32)
        row = lax.broadcasted_iota(jnp.int32, u.shape, 0)
        cnt = jnp.minimum(row + 1, jnp.left_shift(2, g)).astype(f32)
        pooled = _window_sum(u, g, row, _shift_down) / cnt - u
        wv = w_ref[0].astype(bf16)
        mixed = _dot(pooled.astype(bf16), wv)
        dyv = dy_ref[0].astype(f32)
        dmix = (dyv * sc_ref[...]).astype(bf16)
        dpool = _dot(dmix, wv, NT)
        du_ref[0] = (_window_sum(dpool / cnt, g, row, _shift_up) - dpool).astype(bf16)

        @pl.when(pl.program_id(1) == 0)
        def _():
            dw_ref[...] = jnp.zeros_like(dw_ref)
            dsc_ref[...] = jnp.zeros_like(dsc_ref)

        dw_ref[0] += _dot(pooled.astype(bf16), dmix, TN)
        dsc_ref[...] += _fold8(dyv * mixed)

    return pl.pallas_call(
        body, name=name, grid=(G, B),
        in_specs=[pl.BlockSpec((1, S, HEAD), lambda g, b: (b, 0, g)), pl.BlockSpec((1, S, HEAD), lambda g, b: (b, 0, g)),
                  pl.BlockSpec((1, HEAD, HEAD), lambda g, b: (g, 0, 0)), pl.BlockSpec((1, HEAD), lambda g, b: (0, g))],
        out_specs=[pl.BlockSpec((1, S, HEAD), lambda g, b: (b, 0, g)), pl.BlockSpec((1, HEAD, HEAD), lambda g, b: (g, 0, 0)),
                   pl.BlockSpec((8, HEAD), lambda g, b: (0, g))],
        out_shape=[jax.ShapeDtypeStruct((B, S, BRANCH), bf16), jax.ShapeDtypeStruct((G, HEAD, HEAD), f32),
                   jax.ShapeDtypeStruct((8, BRANCH), f32)],
        compiler_params=_cp(("parallel", "arbitrary")),
    )(proj, dy, pool_w, scale.reshape(1, BRANCH))


def _split_dot(x, u):
    hi = x.astype(bf16)
    lo = (x - hi.astype(f32)).astype(bf16)
    return _dot(hi, u) + _dot(lo, u)


def _sb_fwd(sbqkv, name):
    B, S, _ = sbqkv.shape
    KB = SB_BLOCK
    TQ = _tile(S, (SB_QUERIES, KB))
    ns = TQ // KB
    nq = S // TQ
    scale = HEAD ** -0.5

    def body(q_ref, k_ref, v_ref, o_ref, tot_ref, run_s, acc_s):
        r = lax.broadcasted_iota(jnp.int32, (KB, KB), 0)
        c = lax.broadcasted_iota(jnp.int32, (KB, KB), 1)
        causal = c < r
        after = (r > c).astype(bf16)

        def sub(qa, krows, masked, run, acc):
            z = _dot(qa, k_ref[0, krows, :], NT) * scale
            sp = jnp.maximum(z, 0.0) + jnp.log(1.0 + jnp.exp(-jnp.abs(z)))
            ln = -sp
            if masked:
                ln = jnp.where(causal, ln, 0.0)
            w = jnp.exp(z - sp + _split_dot(ln, after) + run)
            if masked:
                w = jnp.where(causal, w, 0.0)
            acc = acc + _dot(w.astype(bf16), v_ref[0, krows, :])
            return run + jnp.sum(ln, axis=1, keepdims=True), acc

        def qloop(i, carry):
            base = pl.multiple_of(i * TQ, TQ)
            qi = q_ref[0, pl.ds(base, TQ), :]
            for a in range(ns):
                qa = qi[a * KB:(a + 1) * KB]
                run, acc = jnp.zeros((KB, LANES), f32), jnp.zeros((KB, HEAD), f32)
                for s in range(a, -1, -1):
                    run, acc = sub(qa, pl.ds(base + s * KB, KB), s == a, run, acc)
                run_s[a * KB:(a + 1) * KB, :] = run
                acc_s[a * KB:(a + 1) * KB, :] = acc

            def group(t, cr):
                g0 = pl.multiple_of((i - 1 - t) * TQ, TQ)
                for s in range(ns - 1, -1, -1):
                    cr = sub(qi, pl.ds(g0 + s * KB, KB), False, *cr)
                return cr

            run, acc = lax.fori_loop(0, i, group, (run_s[...], acc_s[...]))
            o_ref[0, pl.ds(base, TQ), :] = acc.astype(bf16)
            tot_ref[0, 0, pl.ds(base, TQ), :] = run
            return carry

        lax.fori_loop(0, nq, qloop, 0)

    def spec(off):
        return pl.BlockSpec((1, S, HEAD), lambda b, h, off=off: (b, 0, off + h))

    return pl.pallas_call(
        body, name=name, grid=(B, N_HEADS), in_specs=[spec(0), spec(N_HEADS), spec(2 * N_HEADS)],
        out_specs=[pl.BlockSpec((1, S, HEAD), lambda b, h: (b, 0, h)), pl.BlockSpec((1, 1, S, LANES), lambda b, h: (b, h, 0, 0))],
        out_shape=[jax.ShapeDtypeStruct((B, S, BRANCH), bf16), jax.ShapeDtypeStruct((B, N_HEADS, S, LANES), f32)],
        scratch_shapes=[pltpu.VMEM((TQ, LANES), f32), pltpu.VMEM((TQ, HEAD), f32)],
        compiler_params=_cp(("parallel", "parallel")),
    )(sbqkv, sbqkv, sbqkv)


def _sb_bwd(sbqkv, do, tot, name):
    B, S, _ = sbqkv.shape
    KB = SB_BLOCK
    TQ = _tile(S, (SB_QUERIES, KB))
    ns = TQ // KB
    nq = S // TQ
    scale = HEAD ** -0.5

    def body(q_ref, k_ref, v_ref, do_ref, tot_ref, dq_ref, dk_ref, dv_ref, dk_acc, dv_acc):
        r = lax.broadcasted_iota(jnp.int32, (KB, KB), 0)
        c = lax.broadcasted_iota(jnp.int32, (KB, KB), 1)
        causal = c < r
        after = (r > c).astype(bf16)
        before = (r < c).astype(bf16)
        dk_acc[...] = jnp.zeros_like(dk_acc)
        dv_acc[...] = jnp.zeros_like(dv_acc)

        def sub(qi, doi, total, rows, masked, cl, cp, dq):
            kj = k_ref[0, rows, :]
            vj = v_ref[0, rows, :]
            z = _dot(qi, kj, NT) * scale
            sp = jnp.maximum(z, 0.0) + jnp.log(1.0 + jnp.exp(-jnp.abs(z)))
            ln = -sp
            if masked:
                ln = jnp.where(causal, ln, 0.0)
            bs = jnp.sum(ln, axis=1, keepdims=True)
            w = jnp.exp(z - sp + _split_dot(ln, after) + (total - cl - bs))
            if masked:
                w = jnp.where(causal, w, 0.0)
            p = _dot(doi, vj, NT) * w
            qsum = cp + _split_dot(p, before)
            sig = jnp.exp(z - sp)
            dz = (p * (1.0 - sig) - qsum * sig) * scale
            if masked:
                dz = jnp.where(causal, dz, 0.0)
            dzb = dz.astype(bf16)
            dq = dq + _dot(dzb, kj)
            dk_acc[rows, :] += _dot(dzb, qi, TN)
            dv_acc[rows, :] += _dot(w.astype(bf16), doi, TN)
            return cl + bs, cp + jnp.sum(p, axis=1, keepdims=True), dq

        def qloop(i, carry):
            base = pl.multiple_of(i * TQ, TQ)
            rows = pl.ds(base, TQ)
            qi = q_ref[0, rows, :]
            doi = do_ref[0, rows, :]
            total = tot_ref[0, 0, rows, :][:, 0:1]
            zero = jnp.zeros((TQ, 1), f32)

            def group(g, st):
                g0 = pl.multiple_of(g * TQ, TQ)
                for s in range(ns):
                    st = sub(qi, doi, total, pl.ds(g0 + s * KB, KB), False, *st)
                return st

            cl, cp, dq = lax.fori_loop(0, i, group, (zero, zero, jnp.zeros((TQ, HEAD), f32)))
            for a in range(ns):
                ra = slice(a * KB, (a + 1) * KB)
                st = (cl[ra], cp[ra], dq[ra])
                for s in range(a + 1):
                    st = sub(qi[ra], doi[ra], total[ra], pl.ds(base + s * KB, KB), s == a, *st)
                dq_ref[0, pl.ds(base + a * KB, KB), :] = st[2].astype(bf16)
            return carry

        lax.fori_loop(0, nq, qloop, 0)
        dk_ref[0] = dk_acc[...].astype(bf16)
        dv_ref[0] = dv_acc[...].astype(bf16)

    def spec(off):
        return pl.BlockSpec((1, S, HEAD), lambda b, h, off=off: (b, 0, off + h))

    o = jax.ShapeDtypeStruct((B, S, BRANCH), bf16)
    return pl.pallas_call(
        body, name=name, grid=(B, N_HEADS),
        in_specs=[spec(0), spec(N_HEADS), spec(2 * N_HEADS), spec(0), pl.BlockSpec((1, 1, S, LANES), lambda b, h: (b, h, 0, 0))],
        out_specs=[spec(0), spec(0), spec(0)], out_shape=[o, o, o],
        scratch_shapes=[pltpu.VMEM((S, HEAD), f32), pltpu.VMEM((S, HEAD), f32)],
        compiler_params=_cp(("parallel", "parallel")),
    )(sbqkv, sbqkv, sbqkv, do, tot)


def _dn_params(a_log, dt_bias):
    p = jnp.zeros((8, LANES), f32)
    p = p.at[0, :N_HEADS].set(a_log)
    return p.at[1, :N_HEADS].set(dt_bias)


def _dn_prep(ab, par, name):
    B, S, _ = ab.shape
    R = 2 * CHUNK
    nt = S // R

    def body(ab_ref, par_ref, gcb_ref, bb_ref, gcr_ref):
        x = ab_ref[0]
        g = -jnp.exp(par_ref[0:1, :]) * _softplus(x + par_ref[1:2, :])
        r = lax.broadcasted_iota(jnp.int32, (R, R), 0)
        c = lax.broadcasted_iota(jnp.int32, (R, R), 1)
        tri = ((r >= c) & ((r >> 6) == (c >> 6))).astype(f32)
        cs = jnp.dot(tri, g, precision=HI, preferred_element_type=f32)
        beta = _sigmoid(x)
        cst = cs.T
        for h in range(N_HEADS):
            gcb_ref[0, h] = jnp.broadcast_to(cs[:, h:h + 1], (R, LANES))
            bb_ref[0, h] = jnp.broadcast_to(beta[:, N_HEADS + h:N_HEADS + h + 1], (R, LANES))
            gcr_ref[0, h, 0] = jnp.broadcast_to(cst[h:h + 1, 0:CHUNK], (8, CHUNK))
            gcr_ref[0, h, 1] = jnp.broadcast_to(cst[h:h + 1, CHUNK:R], (8, CHUNK))

    return pl.pallas_call(
        body, name=name, grid=(B, nt),
        in_specs=[pl.BlockSpec((1, R, LANES), lambda b, i: (b, i, 0)), pl.BlockSpec((8, LANES), lambda b, i: (0, 0))],
        out_specs=[pl.BlockSpec((1, N_HEADS, R, LANES), lambda b, i: (b, 0, i, 0)),
                   pl.BlockSpec((1, N_HEADS, R, LANES), lambda b, i: (b, 0, i, 0)),
                   pl.BlockSpec((1, N_HEADS, 2, 8, CHUNK), lambda b, i: (b, 0, i, 0, 0))],
        out_shape=[jax.ShapeDtypeStruct((B, N_HEADS, S, LANES), f32), jax.ShapeDtypeStruct((B, N_HEADS, S, LANES), f32),
                   jax.ShapeDtypeStruct((B, N_HEADS, S // CHUNK, 8, CHUNK), f32)],
        compiler_params=_cp(("parallel", "parallel")),
    )(ab, par)


def _bmm(a, b, prec=None):
    return jnp.einsum("nij,njk->nik", a, b, preferred_element_type=f32, precision=prec)


def _bmm_nt(a, b, prec=None):
    return jnp.einsum("nik,njk->nij", a, b, preferred_element_type=f32, precision=prec)


def _bmm_tn(a, b, prec=None):
    return jnp.einsum("nki,nkj->nij", a, b, preferred_element_type=f32, precision=prec)


def _tri_inv(L):
    C = L.shape[-1]
    r = lax.broadcasted_iota(jnp.int32, (C, C), 0)
    c = lax.broadcasted_iota(jnp.int32, (C, C), 1)
    eye = (r == c).astype(f32)
    bd16 = (r >> 4) == (c >> 4)
    bd32 = (r >> 5) == (c >> 5)
    mm = functools.partial(_bmm, prec=HI)
    n1 = -jnp.where(bd16, L, 0.0)
    n2 = mm(n1, n1)
    n4 = mm(n2, n2)
    n8 = mm(n4, n4)
    t = mm(mm(mm(eye + n1, eye + n2), eye + n4), eye + n8)
    t = t - mm(mm(t, jnp.where(bd32 & jnp.logical_not(bd16), L, 0.0)), t)
    t = t - mm(mm(t, jnp.where(bd32, 0.0, L)), t)
    return t


def _conv_silu(x, w, row):
    c = w[3:4] * x + w[2:3] * _shift_down(x, 1, row) + w[1:2] * _shift_down(x, 2, row) + w[0:1] * _shift_down(x, 3, row)
    return c, c * _sigmoid(c)


def _dn_intra(qn, kn, v, gcb, beta, gr):
    C = CHUNK
    r = lax.broadcasted_iota(jnp.int32, (C, C), 0)
    c = lax.broadcasted_iota(jnp.int32, (C, C), 1)
    incl = r >= c
    diff = gcb[:, :, :C] - gr
    dm = jnp.where(incl, jnp.exp(jnp.where(incl, diff, 0.0)), 0.0)
    ds = jnp.where(r > c, dm, 0.0)
    kb = kn * beta
    knb = kn.astype(bf16)
    L = _bmm_nt(kb.astype(bf16), knb) * ds
    eg = jnp.exp(gcb)
    a = _bmm_nt(qn.astype(bf16), knb) * dm
    gl = gcb[:, C - 1:C, :]
    ekd = jnp.exp(gl - gcb)
    return dict(dm=dm, ds=ds, kb=kb, L=L, eg=eg, rhs_u=v * beta, rhs_w=kb * eg, a=a,
                qd=qn * eg, kd=kn * ekd, ekd=ekd, cd=jnp.exp(gl))


def _dn_specs(S):
    def col(off):
        return pl.BlockSpec((1, S, HEAD), lambda b, h, off=off: (b, 0, off + h))

    def cw(off):
        return pl.BlockSpec((4, HEAD), lambda b, h, off=off: (0, off + h))

    per_head = pl.BlockSpec((1, 1, S, LANES), lambda b, h: (b, h, 0, 0))
    rowform = pl.BlockSpec((1, 1, S // CHUNK, 8, CHUNK), lambda b, h: (b, h, 0, 0, 0))
    gain = pl.BlockSpec((1, HEAD), lambda b, h: (0, 0))
    ins = [col(4), col(8), col(12), col(16), cw(0), cw(4), cw(8), per_head, per_head, rowform, gain]
    return ins, per_head


def _dn_act(x_ref, cw_ref, row, normalise, out_scale=1.0):
    _, act = _conv_silu(x_ref[0].astype(f32), cw_ref[...], row)
    if normalise:
        act = act * (lax.rsqrt(jnp.sum(act * act, axis=-1, keepdims=True) + EPS) * out_scale)
    return act


def _dn_group(qn_s, kn_s, v_s, gcb_ref, bb_ref, gcr_ref, g, ng):
    C = CHUNK
    rows = pl.ds(pl.multiple_of(g * (ng * C), ng * C), ng * C)
    ch = pl.ds(g * ng, ng)
    sh = (ng, C, HEAD)
    qn, kn, v = qn_s[rows, :].reshape(sh), kn_s[rows, :].reshape(sh), v_s[rows, :].reshape(sh)
    gcb3, beta = gcb_ref[0, 0, rows, :].reshape(sh), bb_ref[0, 0, rows, :].reshape(sh)
    gr = gcr_ref[0, 0, ch][:, 0:1, :]
    it = _dn_intra(qn, kn, v, gcb3, beta, gr)
    it.update(qn=qn, kn=kn, v=v, gcb=gcb3, beta=beta, gr=gr)
    return rows, ch, it


def _dn_fwd(proj, conv_w, gcb, betab, gcr, gain, name, carry=None):
    B, S, _ = proj.shape
    n, C = S // CHUNK, CHUNK
    ng = min(8, n)
    ins, per_head = _dn_specs(S)

    def body(q_ref, k_ref, v_ref, z_ref, cq_ref, ck_ref, cv_ref, gcb_ref, bb_ref, gcr_ref, gain_ref,
             y_ref, st_ref, vn_ref, qn_s, kn_s, v_s, u_s, w_s, qd_s, kd_s, a_s, cd_s, o_s):
        row = lax.broadcasted_iota(jnp.int32, (S, HEAD), 0)
        qn_s[...] = _dn_act(q_ref, cq_ref, row, True, HEAD ** -0.5)
        kn_s[...] = _dn_act(k_ref, ck_ref, row, True)
        v_s[...] = _dn_act(v_ref, cv_ref, row, False)

        def group(g, carry):
            _, ch, it = _dn_group(qn_s, kn_s, v_s, gcb_ref, bb_ref, gcr_ref, g, ng)
            t = _tri_inv(it["L"])
            u_s[ch] = _bmm(t, it["rhs_u"], HI)
            w_s[ch] = _bmm(t, it["rhs_w"], HI).astype(bf16)
            qd_s[ch] = it["qd"].astype(bf16)
            kd_s[ch] = it["kd"].astype(bf16)
            a_s[ch] = it["a"].astype(bf16)
            cd_s[ch] = it["cd"]
            return carry

        lax.fori_loop(0, n // ng, group, 0)

        def step(i, st):
            sb = st.astype(bf16)
            st_ref[0, 0, i] = sb
            vn = (u_s[i] - _dot(w_s[i], sb)).astype(bf16)
            vn_ref[0, 0, pl.ds(pl.multiple_of(i * C, C), C), :] = vn
            o_s[i] = _dot(qd_s[i], sb) + _dot(a_s[i], vn)
            return st * cd_s[i] + _dot(kd_s[i], vn, TN)

        lax.fori_loop(0, n, step, jnp.zeros((HEAD, HEAD), f32))
        o = o_s[...].reshape(S, HEAD)
        zz = z_ref[0].astype(f32)
        rr = lax.rsqrt(jnp.mean(o * o, axis=-1, keepdims=True) + EPS)
        y_ref[0] = (o * rr * gain_ref[...] * (zz * _sigmoid(zz))).astype(bf16)

    seq = pltpu.VMEM((S, HEAD), f32)
    body, cin, cout, cshapes, cscratch = _carried(body, len(ins), 3, (B, N_HEADS), carry)
    outs = pl.pallas_call(
        body, name=name, grid=(B, N_HEADS), in_specs=ins + cin,
        out_specs=[pl.BlockSpec((1, S, HEAD), lambda b, h: (b, 0, h)),
                   pl.BlockSpec((1, 1, n, HEAD, HEAD), lambda b, h: (b, h, 0, 0, 0)), per_head] + cout,
        out_shape=[jax.ShapeDtypeStruct((B, S, BRANCH), bf16), jax.ShapeDtypeStruct((B, N_HEADS, n, HEAD, HEAD), bf16),
                   jax.ShapeDtypeStruct((B, N_HEADS, S, HEAD), bf16)] + cshapes,
        scratch_shapes=[seq, seq, seq, pltpu.VMEM((n, C, HEAD), f32), pltpu.VMEM((n, C, HEAD), bf16),
                        pltpu.VMEM((n, C, HEAD), bf16), pltpu.VMEM((n, C, HEAD), bf16), pltpu.VMEM((n, C, C), bf16),
                        pltpu.VMEM((n, 1, HEAD), f32), pltpu.VMEM((n, C, HEAD), f32)] + cscratch,
        compiler_params=_cp(("arbitrary", "arbitrary") if carry else ("parallel", "parallel")),
    )(proj, proj, proj, proj, conv_w, conv_w, conv_w, gcb, betab, gcr, gain.reshape(1, HEAD), *(carry.ins if carry else []))
    return outs[:3], outs[3:]


def _rowsum(x):
    return jnp.sum(x, axis=-1, keepdims=True)


def _dn_bwd(proj, dy, conv_w, gcb, betab, gcr, gain, states, vnew, name, carry=None):
    B, S, _ = proj.shape
    n, C = S // CHUNK, CHUNK
    ng = min(8, n)
    ins, per_head = _dn_specs(S)
    ins = ins + [pl.BlockSpec((1, S, HEAD), lambda b, h: (b, 0, h)),
                 pl.BlockSpec((1, 1, n, HEAD, HEAD), lambda b, h: (b, h, 0, 0, 0)), per_head]

    def body(q_ref, k_ref, v_ref, z_ref, cq_ref, ck_ref, cv_ref, gcb_ref, bb_ref, gcr_ref, gain_ref, dy_ref, st_ref, vn_ref,
             dq_ref, dk_ref, dv_ref, dz_ref, dg_ref, dbeta_ref, dconv_ref, dgain_ref,
             qn_s, kn_s, v_s, t_s, u_s, at_s, kd_s, qd_s, w_s, cd_s, do_s, dsp_s, dvn_s):
        row = lax.broadcasted_iota(jnp.int32, (S, HEAD), 0)
        qn_s[...] = _dn_act(q_ref, cq_ref, row, True, HEAD ** -0.5)
        kn_s[...] = _dn_act(k_ref, ck_ref, row, True)
        v_s[...] = _dn_act(v_ref, cv_ref, row, False)
        dgain_ref[...] = jnp.zeros_like(dgain_ref)
        gv = gain_ref[...]

        def group_fwd(g, carry):
            rows, ch, it = _dn_group(qn_s, kn_s, v_s, gcb_ref, bb_ref, gcr_ref, g, ng)
            t = _tri_inv(it["L"])
            ub = _bmm(t, it["rhs_u"], HI).astype(bf16)
            wb = _bmm(t, it["rhs_w"], HI).astype(bf16)
            ab, qdb = it["a"].astype(bf16), it["qd"].astype(bf16)
            vn = vn_ref[0, 0, rows, :].reshape(ng, C, HEAD)
            o = (_bmm(qdb, st_ref[0, 0, ch]) + _bmm(ab, vn)).reshape(ng * C, HEAD)
            zz = z_ref[0, rows, :].astype(f32)
            dyv = dy_ref[0, rows, :].astype(f32)
            rr = lax.rsqrt(jnp.mean(o * o, axis=-1, keepdims=True) + EPS)
            on = o * rr
            sz = _sigmoid(zz)
            dz_ref[0, rows, :] = (dyv * on * gv * (sz * (1.0 + zz * (1.0 - sz)))).astype(bf16)
            dnrm = dyv * (zz * sz)
            dgain_ref[0, 0] += _fold8(dnrm * on)
            doh = dnrm * gv
            do = rr * (doh - on * jnp.mean(doh * on, axis=-1, keepdims=True))
            t_s[ch] = t
            u_s[ch] = ub
            w_s[ch] = wb
            at_s[ch] = ab
            qd_s[ch] = qdb
            kd_s[ch] = it["kd"].astype(bf16)
            cd_s[ch] = it["cd"]
            do_s[ch] = do.reshape(ng, C, HEAD).astype(bf16)
            return carry

        lax.fori_loop(0, n // ng, group_fwd, 0)

        def step(t, dsp):
            i = n - 1 - t
            dspb = dsp.astype(bf16)
            dsp_s[i] = dspb
            dvn = _dot(at_s[i], do_s[i], TN) + _dot(kd_s[i], dspb)
            dvn_s[i] = dvn
            return dsp * cd_s[i] + _dot(qd_s[i], do_s[i], TN) - _dot(w_s[i], dvn.astype(bf16), TN)

        lax.fori_loop(0, n, step, jnp.zeros((HEAD, HEAD), f32))

        r = lax.broadcasted_iota(jnp.int32, (C, C), 0)
        c = lax.broadcasted_iota(jnp.int32, (C, C), 1)
        upper = r <= c

        def group_bwd(g, carry):
            rows, ch, it = _dn_group(qn_s, kn_s, v_s, gcb_ref, bb_ref, gcr_ref, g, ng)
            sh = (ng, C, HEAD)
            qn, kn, v, beta, gcb3, gr = it["qn"], it["kn"], it["v"], it["beta"], it["gcb"], it["gr"]
            sn = st_ref[0, 0, ch]
            vn = vn_ref[0, 0, rows, :].reshape(sh)
            dsp, dvn, dob = dsp_s[ch], dvn_s[ch], do_s[ch]
            t, ub, wb = t_s[ch], u_s[ch], w_s[ch]
            dvnb = dvn.astype(bf16)
            da = _bmm_nt(dob, vn)
            dat = _bmm_nt(vn, dob)
            dqd = _bmm_nt(dob, sn)
            dkd = _bmm_nt(vn, dsp)
            dcd = jnp.sum(jnp.sum(dsp.astype(f32) * sn.astype(f32), axis=2, keepdims=True), axis=1, keepdims=True)
            dw = -_bmm_nt(dvnb, sn)
            ru = _bmm_tn(t, dvn, HI)
            rw = _bmm_tn(t, dw, HI)
            rub, rwb = ru.astype(bf16), rw.astype(bf16)
            dL = -(_bmm_nt(rub, ub) + _bmm_nt(rwb, wb))
            dLt = -(_bmm_nt(ub, rub) + _bmm_nt(wb, rwb))
            knb, qnb, kbb = kn.astype(bf16), qn.astype(bf16), it["kb"].astype(bf16)
            dmt = jnp.where(upper, jnp.exp(jnp.where(upper, gr - gcb3[:, :, :C], 0.0)), 0.0)
            Lt = _bmm_nt(knb, kbb) * jnp.where(r < c, dmt, 0.0)
            At = _bmm_nt(knb, qnb) * dmt
            dgc = _rowsum(dL * it["L"] + da * it["a"]) - _rowsum(dLt * Lt + dat * At)
            dkk = (dL * it["ds"]).astype(bf16)
            dqk = (da * it["dm"]).astype(bf16)
            dkb = _bmm(dkk, knb) + rw * it["eg"]
            dkn = _bmm_tn(dkk, kbb) + _bmm_tn(dqk, qnb) + dkd * it["ekd"] + dkb * beta
            dqn = _bmm(dqk, knb) + dqd * it["eg"]
            tkd = _rowsum(dkd * it["kd"])
            dgl = jnp.sum(tkd, axis=1, keepdims=True) + dcd * it["cd"][:, :, 0:1]
            dgc = dgc + _rowsum(dqd * it["qd"]) - tkd + _rowsum(rw * it["rhs_w"])
            dbeta = _rowsum(ru * v) + _rowsum(dkb * kn)
            rowc = lax.broadcasted_iota(jnp.int32, (ng, C, 1), 1)
            dgc = dgc + jnp.where(rowc == C - 1, dgl, 0.0)
            rev = jnp.broadcast_to(upper.astype(f32), (ng, C, C))
            dg_ref[0, 0, rows, :] = _bmm(rev, jnp.broadcast_to(dgc, sh), HI).reshape(ng * C, LANES).astype(bf16)
            dbeta_ref[0, 0, rows, :] = jnp.broadcast_to(dbeta, sh).reshape(ng * C, LANES).astype(bf16)
            qn_s[rows, :] = dqn.reshape(ng * C, HEAD)
            kn_s[rows, :] = dkn.reshape(ng * C, HEAD)
            v_s[rows, :] = (ru * beta).reshape(ng * C, HEAD)
            return carry

        lax.fori_loop(0, n // ng, group_bwd, 0)

        def conv_back(x_ref, cw_ref, grad_s, out_ref, slot, normalise, out_scale):
            x = x_ref[0].astype(f32)
            w = cw_ref[...]
            pre, act = _conv_silu(x, w, row)
            dact = grad_s[...]
            if normalise:
                rn = lax.rsqrt(jnp.sum(act * act, axis=-1, keepdims=True) + EPS)
                unit = act * rn
                dact = (out_scale * rn) * (dact - unit * _rowsum(dact * unit))
            s = _sigmoid(pre)
            dc = dact * (s * (1.0 + pre * (1.0 - s)))
            out_ref[0] = (w[3:4] * dc + w[2:3] * _shift_up(dc, 1, row) + w[1:2] * _shift_up(dc, 2, row)
                          + w[0:1] * _shift_up(dc, 3, row)).astype(bf16)
            for tap in range(4):
                xs = x if tap == 3 else _shift_down(x, 3 - tap, row)
                dconv_ref[0, slot, tap:tap + 1, :] = jnp.sum(dc * xs, axis=0, keepdims=True)

        conv_back(q_ref, cq_ref, qn_s, dq_ref, 0, True, HEAD ** -0.5)
        conv_back(k_ref, ck_ref, kn_s, dk_ref, 1, True, 1.0)
        conv_back(v_ref, cv_ref, v_s, dv_ref, 2, False, 1.0)

    o512 = jax.ShapeDtypeStruct((B, S, BRANCH), bf16)
    s512 = pl.BlockSpec((1, S, HEAD), lambda b, h: (b, 0, h))
    ph = jax.ShapeDtypeStruct((B, N_HEADS, S, LANES), bf16)
    seq = pltpu.VMEM((S, HEAD), f32)
    cb = pltpu.VMEM((n, C, HEAD), bf16)
    body, cin, cout, cshapes, cscratch = _carried(body, len(ins), 8, (B, N_HEADS), carry)
    outs = pl.pallas_call(
        body, name=name, grid=(B, N_HEADS), in_specs=ins + cin,
        out_specs=[s512, s512, s512, s512, per_head, per_head, pl.BlockSpec((1, 3, 4, HEAD), lambda b, h: (b, 0, 0, h)),
                   pl.BlockSpec((1, 1, 8, HEAD), lambda b, h: (b, h, 0, 0))] + cout,
        out_shape=[o512, o512, o512, o512, ph, ph, jax.ShapeDtypeStruct((B, 3, 4, BRANCH), f32),
                   jax.ShapeDtypeStruct((B, N_HEADS, 8, HEAD), f32)] + cshapes,
        scratch_shapes=[seq, seq, seq, pltpu.VMEM((n, C, C), f32), cb, pltpu.VMEM((n, C, C), bf16), cb, cb, cb,
                        pltpu.VMEM((n, 1, HEAD), f32), cb, pltpu.VMEM((n, HEAD, HEAD), bf16), pltpu.VMEM((n, C, HEAD), f32)]
        + cscratch,
        compiler_params=_cp(("arbitrary", "arbitrary") if carry else ("parallel", "parallel")),
    )(proj, proj, proj, proj, conv_w, conv_w, conv_w, gcb, betab, gcr, gain.reshape(1, HEAD), dy, states, vnew,
      *(carry.ins if carry else []))
    return outs[:8], outs[8:]


def _dn_post(ab, par, dg, dbeta, name):
    B, S, _ = ab.shape
    ts = _tile(S, (512, 256, 128))

    def body(ab_ref, par_ref, dg_ref, db_ref, dab_ref, acc_ref):
        x = ab_ref[0]
        lane = lax.broadcasted_iota(jnp.int32, x.shape, 1)
        dgs = jnp.zeros_like(x)
        dbs = jnp.zeros_like(x)
        for h in range(N_HEADS):
            dgs = jnp.where(lane == h, dg_ref[0, h], dgs)
            dbs = jnp.where(lane == N_HEADS + h, db_ref[0, h], dbs)
        nega = -jnp.exp(par_ref[0:1, :])
        pre = x + par_ref[1:2, :]
        da = dgs * nega * _sigmoid(pre)
        beta = _sigmoid(x)
        dab_ref[0] = (da + dbs * beta * (1.0 - beta)).astype(bf16)

        @pl.when((pl.program_id(0) == 0) & (pl.program_id(1) == 0))
        def _():
            acc_ref[...] = jnp.zeros_like(acc_ref)

        acc_ref[0] += _fold8(dgs * nega * _softplus(pre))
        acc_ref[1] += _fold8(da)

    return pl.pallas_call(
        body, name=name, grid=(B, S // ts),
        in_specs=[pl.BlockSpec((1, ts, LANES), lambda b, i: (b, i, 0)), pl.BlockSpec((8, LANES), lambda b, i: (0, 0)),
                  pl.BlockSpec((1, N_HEADS, ts, LANES), lambda b, i: (b, 0, i, 0)),
                  pl.BlockSpec((1, N_HEADS, ts, LANES), lambda b, i: (b, 0, i, 0))],
        out_specs=[pl.BlockSpec((1, ts, LANES), lambda b, i: (b, i, 0)), pl.BlockSpec((2, 8, LANES), lambda b, i: (0, 0, 0))],
        out_shape=[jax.ShapeDtypeStruct((B, S, LANES), bf16), jax.ShapeDtypeStruct((2, 8, LANES), f32)],
        compiler_params=_cp(("arbitrary", "arbitrary")),
    )(ab, par, dg, dbeta)


def _merge_specs(T, D, tm, tn, order):
    nj = D // tn

    def ij(f):
        return (lambda i, j: f(i, j)) if order == "ij" else (lambda j, i: f(i, j))

    ys = [pl.BlockSpec((tm, BRANCH), ij(lambda i, j: (i, 0))) for _ in range(3)]
    wb = pl.BlockSpec((3, BRANCH, tn), ij(lambda i, j: (0, 0, j)))
    gl = [pl.BlockSpec((tm, tn), ij(lambda i, j, k=k: (i, k * nj + j))) for k in range(3)]
    bg = [pl.BlockSpec((1, tn), ij(lambda i, j, k=k: (0, k * nj + j))) for k in range(3)]
    return ys, wb, gl, bg


def _merge_fwd(ys, wb, gl, b_gate, name):
    T, D = ys[0].shape[0], wb.shape[2]
    tm, tn = _tile(T, (512, 256, 128)), _tile(D, (512, 256, 128))
    sy, swb, sgl, sbg = _merge_specs(T, D, tm, tn, "ij")

    def body(y0, y1, y2, wb_ref, g0, g1, g2, b0, b1, b2, o_ref):
        acc = None
        for k, (y, g, b) in enumerate(((y0, g0, b0), (y1, g1, b1), (y2, g2, b2))):
            term = _sigmoid(g[...].astype(f32) + b[...]) * _dot(y[...], wb_ref[k])
            acc = term if acc is None else acc + term
        o_ref[...] = acc.astype(bf16)

    bg = b_gate.reshape(1, 3 * D)
    return pl.pallas_call(
        body, name=name, grid=(T // tm, D // tn), in_specs=sy + [swb] + sgl + sbg,
        out_specs=pl.BlockSpec((tm, tn), lambda i, j: (i, j)), out_shape=jax.ShapeDtypeStruct((T, D), bf16),
        compiler_params=_cp(("parallel", "parallel")),
    )(*ys, wb, gl, gl, gl, bg, bg, bg)


def _merge_bwd(dm, ys, wb, gl, b_gate, name):
    T, D = dm.shape
    tm, tn = _tile(T, (512, 256, 128)), _tile(D, (512, 256, 128))
    sy, swb, sgl, sbg = _merge_specs(T, D, tm, tn, "ji")

    def body(dm_ref, y0, y1, y2, wb_ref, g0, g1, g2, b0, b1, b2, dgl_ref, dbd_ref, dbg_ref):
        dmv = dm_ref[...].astype(f32)

        @pl.when(pl.program_id(1) == 0)
        def _():
            dbg_ref[...] = jnp.zeros_like(dbg_ref)

        for k, (y, g, b) in enumerate(((y0, g0, b0), (y1, g1, b1), (y2, g2, b2))):
            s = _sigmoid(g[...].astype(f32) + b[...])
            dg = dmv * _dot(y[...], wb_ref[k]) * s * (1.0 - s)
            dgl_ref[k] = dg.astype(bf16)
            dbd_ref[k] = (dmv * s).astype(bf16)
            dbg_ref[k] += _fold8(dg)

    bg = b_gate.reshape(1, 3 * D)
    o3 = jax.ShapeDtypeStruct((3, T, D), bf16)
    s3 = pl.BlockSpec((3, tm, tn), lambda j, i: (0, i, j))
    return pl.pallas_call(
        body, name=name, grid=(D // tn, T // tm),
        in_specs=[pl.BlockSpec((tm, tn), lambda j, i: (i, j))] + sy + [swb] + sgl + sbg,
        out_specs=[s3, s3, pl.BlockSpec((3, 8, tn), lambda j, i: (0, 0, j))],
        out_shape=[o3, o3, jax.ShapeDtypeStruct((3, 8, D), f32)],
        compiler_params=_cp(("parallel", "arbitrary")),
    )(dm, *ys, wb, gl, gl, gl, bg, bg, bg)


def _ffn_fwd(x, g, w, f, tag):
    T, D = x.shape
    FP = w["wgu"].shape[2]
    h = _rms_fwd(x, g, f"rms_{tag}")
    a, b, hm = _ffn_up(h, w["wgu"], f, f"ffn_up_{tag}")
    down = [("nn", hm, w["m"], {"K": FP, "ka": k, "bsel": k, "kb": f}) for k in range(N_CHIPS)]
    y = _mm(down, T, D, f32, f"ffn_down_{tag}", res=x, scale=0.5)
    return y, (x, h, a, b)


def _ffn_bwd(dy, saved, g, w, f, tag):
    x, h, a, b = saved
    T, D = x.shape
    FP = w["wgu"].shape[2]
    F4 = N_CHIPS * FP
    da, db, hm = _ffn_bwd_mid(dy, w["m"], a, b, f, f"ffn_mid_bwd_{tag}")
    dwd = _mm([("tn", hm, dy, {})], F4, D, bf16, f"ffn_dwd_{tag}", scale=0.5, tm=_tile(F4, (512, 256, 128)))
    dwg = _mm([("tn", h, da, {})], D, F4, bf16, f"ffn_dwg_{tag}", out_chip=True)
    dwu = _mm([("tn", h, db, {})], D, F4, bf16, f"ffn_dwu_{tag}", out_chip=True)
    tn = _tile(D, (512, 256, 128))
    pairs = [("nt", t, w["wgu"], {"K": FP, "ka": k, "bsel": k, "noff": (2 * f + u) * (D // tn)})
             for u, t in enumerate((da, db)) for k in range(N_CHIPS)]
    dh = _mm(pairs, T, D, f32, f"ffn_dh_{tag}", tm=_tile(T, (256, 128)), tn=tn)
    dx, dg8 = _rms_bwd(dh, x, g, dy, f"rms_bwd_{tag}")
    return dx, dict(norm=jnp.sum(dg8, axis=0), wgu=[dwg, dwu], wd=dwd.reshape(N_CHIPS, FP, D))


def _layer_fwd(x, w, B, tag, carry=None):
    T, D = x.shape
    S = T // B
    x1, sv0 = _ffn_fwd(x, w["ffn_norm"][0], w, 0, f"pre_{tag}")
    h = _rms_fwd(x1, w["mix_norm"], f"rms_mix_{tag}")
    pm = _mm([("nn", h, w["w_main"], {})], T, 5 * BRANCH, bf16, f"proj_main_{tag}")
    ab = _mm([("nn", h, w["w_ab"], {})], T, LANES, f32, f"proj_ab_{tag}", tn=LANES)
    sb = _mm([("nn", h, w["w_sb"], {})], T, 3 * BRANCH, bf16, f"proj_sb_{tag}")
    gl = _mm([("nn", h, w["w_gates"], {})], T, 3 * D, bf16, f"proj_gates_{tag}")
    pm3, ab3, sb3 = pm.reshape(B, S, -1), ab.reshape(B, S, LANES), sb.reshape(B, S, -1)
    y_pool = _pool_fwd(pm3, w["pool_w"], w["pool_scale"], f"pool_{tag}")
    par = _dn_params(w["dn_A_log"], w["dn_dt_bias"])
    gcb, betab, gcr = _dn_prep(ab3, par, f"dn_prep_{tag}")
    (y_dn, states, vnew), landed = _dn_fwd(pm3, w["dn_conv"], gcb, betab, gcr, w["dn_out_norm"], f"dn_fwd_{tag}", carry)
    y_sb, tot = _sb_fwd(sb3, f"sb_fwd_{tag}")
    ys = [y_pool.reshape(T, BRANCH), y_dn.reshape(T, BRANCH), y_sb.reshape(T, BRANCH)]
    merged = _merge_fwd(ys, w["w_branch"], gl, w["b_gate"], f"merge_{tag}")
    dc = D // N_CHIPS
    out_pairs = [("nn", merged, w["m"], {"K": dc, "ka": k, "bsel": k, "kb": _w_out_block(w)}) for k in range(N_CHIPS)]
    x2 = _mm(out_pairs, T, D, f32, f"mix_out_{tag}", res=x1)
    x3, sv1 = _ffn_fwd(x2, w["ffn_norm"][1], w, 1, f"post_{tag}")
    saved = dict(sv0=sv0, sv1=sv1, x1=x1, h=h, pm3=pm3, ab3=ab3, sb3=sb3, gl=gl, par=par, gcb=gcb, betab=betab, gcr=gcr,
                 states=states, vnew=vnew, tot=tot, ys=ys, merged=merged)
    return x3, saved, landed


def _layer_bwd(dx3, w, sv, B, tag, carry=None):
    T, D = dx3.shape
    S = T // B
    dx2, g1 = _ffn_bwd(dx3, sv["sv1"], w["ffn_norm"][1], w, 1, f"post_{tag}")
    dc = D // N_CHIPS
    dmerged = _mm([("nt", dx2, w["m"], {"b_by_chip": True, "noff": _w_out_block(w)})], T, D, bf16, f"mix_dmerged_{tag}", tn=dc)
    dw_out = _mm([("tn", sv["merged"], dx2, {})], D, D, bf16, f"mix_dwout_{tag}")
    ys = sv["ys"]
    dgl, dbd, dbg8 = _merge_bwd(dmerged, ys, w["w_branch"], sv["gl"], w["b_gate"], f"merge_bwd_{tag}")
    dys, dwb = [], []
    for k in range(3):
        dys.append(_mm([("nt", dbd, w["w_branch"], {"asel": k, "bsel": k})], T, BRANCH, bf16, f"branch_dy{k}_{tag}"))
        dwb.append(_mm([("tn", ys[k], dbd, {"bsel": k})], BRANCH, D, bf16, f"branch_dw{k}_{tag}"))
    pm3, ab3, sb3 = sv["pm3"], sv["ab3"], sv["sb3"]
    du, dpool_w, dsc8 = _pool_bwd(pm3, dys[0].reshape(B, S, BRANCH), w["pool_w"], w["pool_scale"], f"pool_bwd_{tag}")
    (dq, dk, dv, dz, dg, dbeta, dconv, dgain), landed = _dn_bwd(
        pm3, dys[1].reshape(B, S, BRANCH), w["dn_conv"], sv["gcb"], sv["betab"], sv["gcr"], w["dn_out_norm"],
        sv["states"], sv["vnew"], f"dn_bwd_{tag}", carry)
    dab, dn_acc = _dn_post(ab3, sv["par"], dg, dbeta, f"dn_post_{tag}")
    dsq, dsk, dsv = _sb_bwd(sb3, dys[2].reshape(B, S, BRANCH), sv["tot"], f"sb_bwd_{tag}")
    main_parts = [t.reshape(T, BRANCH) for t in (du, dq, dk, dv, dz)]
    sb_parts = [t.reshape(T, BRANCH) for t in (dsq, dsk, dsv)]
    dab2 = dab.reshape(T, LANES)
    pairs = [("nt", t, w["w_main"], {"K": BRANCH, "kb": k}) for k, t in enumerate(main_parts)]
    pairs.append(("nt", dab2, w["w_ab"], {}))
    pairs += [("nt", t, w["w_sb"], {"K": BRANCH, "kb": k}) for k, t in enumerate(sb_parts)]
    pairs += [("nt", dgl, w["w_gates"], {"K": D, "kb": k, "asel": k}) for k in range(3)]
    dh = _mm(pairs, T, D, f32, f"mix_dh_{tag}", tm=_tile(T, (256, 128)))
    h = sv["h"]
    dw_cols = [_mm([("tn", h, t, {})], D, BRANCH, bf16, f"dwin_main{k}_{tag}") for k, t in enumerate(main_parts)]
    dw_cols.append(_mm([("tn", h, dab2, {})], D, LANES, bf16, f"dwin_ab_{tag}", tn=LANES)[:, :2 * N_HEADS])
    dw_cols += [_mm([("tn", h, t, {})], D, BRANCH, bf16, f"dwin_sb{k}_{tag}") for k, t in enumerate(sb_parts)]
    dw_cols += [_mm([("tn", h, dgl, {"bsel": k})], D, D, bf16, f"dwin_gate{k}_{tag}") for k in range(3)]
    dx1, dmix8 = _rms_bwd(dh, sv["x1"], w["mix_norm"], dx2, f"rms_mix_bwd_{tag}")
    dx0, g0 = _ffn_bwd(dx1, sv["sv0"], w["ffn_norm"][0], w, 0, f"pre_{tag}")
    dwb = jnp.stack(dwb).reshape(3, BRANCH, N_CHIPS, dc).transpose(2, 0, 1, 3).reshape(N_CHIPS, -1, D)
    dw_in = jnp.concatenate(dw_cols, axis=1)
    pc = dw_in.shape[1] // N_CHIPS
    grads = dict(
        ffn_norm=jnp.stack([g0["norm"], g1["norm"]]),
        A=jnp.concatenate(g0["wgu"] + g1["wgu"], axis=1),
        M=jnp.concatenate([g0["wd"], g1["wd"], dw_out.reshape(N_CHIPS, dc, D), dwb], axis=1),
        C=jnp.stack([dw_in[:, k * pc:(k + 1) * pc] for k in range(N_CHIPS)]),
        mix_norm=jnp.sum(dmix8, axis=0), b_gate=jnp.sum(dbg8, axis=1).reshape(3 * D),
        pool_w=dpool_w, pool_scale=jnp.sum(dsc8, axis=0), dn_conv=jnp.sum(dconv, axis=0).transpose(1, 0, 2).reshape(4, 3 * BRANCH),
        dn_A_log=jnp.sum(dn_acc[0], axis=0)[:N_HEADS], dn_dt_bias=jnp.sum(dn_acc[1], axis=0)[:N_HEADS],
        dn_out_norm=jnp.sum(dgain, axis=(0, 1, 2)))
    return dx0, grads, landed


def _local_step(x, target, layers, final_norm, B):
    saved = []
    for l, w in enumerate(layers):
        x, sv, _ = _layer_fwd(x, w, B, f"l{l}")
        saved.append(sv)
    dx, dfn8, ls8 = _final_loss(x, target, final_norm)
    grads = [None] * len(layers)
    for l in reversed(range(len(layers))):
        dx, grads[l], _ = _layer_bwd(dx, layers[l], saved[l], B, f"l{l}")
    return jnp.sum(ls8), dx, grads, jnp.sum(dfn8, axis=0)


def _adamw(w, g, m, v, name):
    shape = w.shape
    cols = shape[-1]
    rows = math.prod(shape[:-1]) if len(shape) > 1 else 1
    w2, g2, m2, v2 = (t.reshape(rows, cols) for t in (w, g, m, v))
    block_elems = 256 * 1024
    tr = rows if rows * cols <= block_elems else _tile(rows, [t for t in (512, 256, 128, 64, 32, 16, 8) if t * cols <= block_elems])

    def body(w_ref, g_ref, m_ref, v_ref, d_ref, mo_ref, vo_ref):
        gv = g_ref[...]
        mn = ADAM_B1 * m_ref[...] + (1.0 - ADAM_B1) * gv
        vn = ADAM_B2 * v_ref[...] + (1.0 - ADAM_B2) * (gv * gv)
        m_hat = mn / (1.0 - ADAM_B1 ** ADAM_STEP)
        v_hat = vn / (1.0 - ADAM_B2 ** ADAM_STEP)
        d_ref[...] = -ADAM_LR * (m_hat / (jnp.sqrt(v_hat) + ADAM_EPS) + ADAM_WD * w_ref[...])
        mo_ref[...] = mn
        vo_ref[...] = vn

    spec = pl.BlockSpec((tr, cols), lambda i: (i, 0))
    o = jax.ShapeDtypeStruct((rows, cols), f32)
    d, mo, vo = pl.pallas_call(
        body, name=name, grid=(rows // tr,), in_specs=[spec] * 4, out_specs=[spec] * 3, out_shape=[o, o, o],
        compiler_params=_cp(("parallel",)),
    )(w2, g2, m2, v2)
    return d.reshape(shape), mo.reshape(shape), vo.reshape(shape)


MESH = pl.DeviceIdType.MESH
_ANY = pl.BlockSpec(memory_space=pl.ANY)


def _place():
    x, y, c = lax.axis_index("x"), lax.axis_index("y"), lax.axis_index("c")
    return x, y, c, [(1 - x, y), (x, 1 - y), (1 - x, 1 - y)]


def _chip_index():
    return 2 * lax.axis_index("x") + lax.axis_index("y")


def _half(c, rh):
    return pl.ds(c * rh, rh)


def _remote(src, dst, ssem, rsem, to):
    return pltpu.make_async_remote_copy(src_ref=src, dst_ref=dst, send_sem=ssem, recv_sem=rsem, device_id=to,
                                        device_id_type=MESH)


class _ChipExchange:
    def __init__(self, kind, ins):
        self.kind, self.ins = kind, list(ins)
        self.n = len(self.ins)
        self.out_shapes = [jax.ShapeDtypeStruct((N_CHIPS,) + a.shape[-2:], a.dtype) for a in self.ins]
        self.scratch = [pltpu.SemaphoreType.DMA((self.n, 3)), pltpu.SemaphoreType.DMA((self.n, 3))]

    def _copies(self, in_refs, out_refs, ssem, rsem):
        x, y, c, chips = _place()
        me = 2 * x + y
        pairs = []
        for o, (src, dst) in enumerate(zip(in_refs, out_refs)):
            for k, (px, py) in enumerate(chips):
                peer = 2 * px + py
                if self.kind == "gather":
                    rows = _half(c, src.shape[0] // 2)
                    out, land = (src.at[rows], dst.at[me, rows]), dst.at[peer, rows]
                else:
                    out, land = (src.at[peer], dst.at[me]), dst.at[peer]
                pairs.append((_remote(out[0], out[1], ssem.at[o, k], rsem.at[o, k], (px, py, c)),
                              _remote(land, land, ssem.at[o, k], rsem.at[o, k], (px, py, c))))
        return pairs

    def start(self, in_refs, out_refs, ssem, rsem):
        for mine, _ in self._copies(in_refs, out_refs, ssem, rsem):
            mine.start()

    def wait(self, in_refs, out_refs, ssem, rsem):
        for mine, landing in self._copies(in_refs, out_refs, ssem, rsem):
            landing.wait_recv()
            mine.wait_send()

    def standalone(self, name):
        n = self.n

        def body(*refs):
            ins, outs, (ssem, rsem) = refs[:n], refs[n:2 * n], refs[2 * n:]
            self.start(ins, outs, ssem, rsem)
            self.wait(ins, outs, ssem, rsem)

        return pl.pallas_call(body, name=name, in_specs=[_ANY] * n, out_specs=[_ANY] * n, out_shape=self.out_shapes,
                              scratch_shapes=self.scratch)(*self.ins)


def _carried(body, n_in, n_out, grid, carry):
    if carry is None:
        return body, [], [], [], []
    n = carry.n

    def wrapped(*refs):
        ins, cin = refs[:n_in], refs[n_in:n_in + n]
        outs, cout = refs[n_in + n:n_in + n + n_out], refs[n_in + n + n_out:n_in + 2 * n + n_out]
        scratch, (ssem, rsem) = refs[n_in + 2 * n + n_out:-2], refs[-2:]
        step = pl.program_id(0) * grid[1] + pl.program_id(1)

        @pl.when(step == 0)
        def _():
            carry.start(cin, cout, ssem, rsem)

        body(*ins, *outs, *scratch)

        @pl.when(step == grid[0] * grid[1] - 1)
        def _():
            carry.wait(cin, cout, ssem, rsem)

    return wrapped, [_ANY] * n, [_ANY] * n, carry.out_shapes, carry.scratch


def _gather_finish(shards, landed, name):
    n = len(shards)

    def body(*refs):
        outs, (ssem, rsem) = refs[n:2 * n], refs[2 * n:]
        x, y, c, chips = _place()
        started = []
        for o, buf in enumerate(outs):
            rh = buf.shape[1] // 2
            for k, (px, py) in enumerate(chips):
                block = buf.at[2 * px + py, _half(c, rh)]
                cp = _remote(block, block, ssem.at[o, k], rsem.at[o, k], (x, y, 1 - c))
                cp.start()
                started.append(cp)
        for o, buf in enumerate(outs):
            rh = buf.shape[1] // 2
            for k, (px, py) in enumerate(chips):
                block = buf.at[2 * px + py, _half(1 - c, rh)]
                _remote(block, block, ssem.at[o, k], rsem.at[o, k], (x, y, 1 - c)).wait_recv()
        for cp in started:
            cp.wait_send()

    outs = pl.pallas_call(
        body, name=name, in_specs=[_ANY] * n, out_specs=[_ANY] * n,
        out_shape=[jax.ShapeDtypeStruct(a.shape, a.dtype) for a in landed], input_output_aliases={i: i for i in range(n)},
        scratch_shapes=[pltpu.SemaphoreType.DMA((n, 3)), pltpu.SemaphoreType.DMA((n, 3))],
    )(*landed)
    me = _chip_index()
    return [lax.dynamic_update_slice(g, s[None], (me, 0, 0)) for g, s in zip(outs, shards)]


def _gather_chips(shards, name):
    landed = _ChipExchange("gather", shards).standalone(f"{name}_ici")
    return _gather_finish(shards, landed, f"{name}_pass")


def _pair_swap_halves(ps, name):
    n = len(ps)

    def body(*refs):
        ins, outs, (ssem, rsem) = refs[:n], refs[n:2 * n], refs[2 * n:]
        x, y, c, _ = _place()
        cps = [_remote(p.at[:, _half(1 - c, p.shape[1] // 2)], out, ssem.at[o], rsem.at[o], (x, y, 1 - c))
               for o, (p, out) in enumerate(zip(ins, outs))]
        for cp in cps:
            cp.start()
        for cp in cps:
            cp.wait()

    return pl.pallas_call(
        body, name=name, in_specs=[_ANY] * n, out_specs=[_ANY] * n,
        out_shape=[jax.ShapeDtypeStruct((p.shape[0], p.shape[1] // 2, p.shape[2]), p.dtype) for p in ps],
        scratch_shapes=[pltpu.SemaphoreType.DMA((n,)), pltpu.SemaphoreType.DMA((n,))],
    )(*ps)


def _pair_add(p, got, name):
    n, R, W = p.shape
    rh = R // 2
    tr = _tile(rh, (512, 256, 192, 128, 96, 64, 32, 16, 8))
    nb = rh // tr

    def body(c_ref, p_ref, g_ref, o_ref):
        o_ref[...] = (p_ref[...].astype(f32) + g_ref[...].astype(f32)).astype(o_ref.dtype)

    return pl.pallas_call(
        body, name=name,
        grid_spec=pltpu.PrefetchScalarGridSpec(
            num_scalar_prefetch=1, grid=(n, nb),
            in_specs=[pl.BlockSpec((1, tr, W), lambda j, i, c_ref: (j, c_ref[0] * nb + i, 0)),
                      pl.BlockSpec((1, tr, W), lambda j, i, c_ref: (j, i, 0))],
            out_specs=pl.BlockSpec((1, tr, W), lambda j, i, c_ref: (j, i, 0))),
        out_shape=jax.ShapeDtypeStruct((n, rh, W), p.dtype), compiler_params=_cp(("parallel", "parallel")),
    )(lax.axis_index("c").astype(jnp.int32).reshape(1), p, got)


def _own_slot_filled(landed, ps):
    me = _chip_index()
    return [lax.dynamic_update_slice(out, lax.dynamic_slice(p, (me, 0, 0), (1,) + p.shape[1:]), (me, 0, 0))
            for out, p in zip(landed, ps)]


def _sum_slots(r4, name):
    n, R, W = r4.shape
    tr = _tile(R, (512, 256, 192, 128, 96, 64, 32, 16, 8))

    def body(r_ref, o_ref):
        acc = r_ref[0].astype(f32)
        for k in range(1, n):
            acc = acc + r_ref[k].astype(f32)
        o_ref[...] = acc

    return pl.pallas_call(
        body, name=name, grid=(R // tr,), in_specs=[pl.BlockSpec((n, tr, W), lambda i: (0, i, 0))],
        out_specs=pl.BlockSpec((tr, W), lambda i: (i, 0)), out_shape=jax.ShapeDtypeStruct((R, W), f32),
        compiler_params=_cp(("parallel",)),
    )(r4)


def _pair_share(ss, name):
    n = len(ss)

    def body(*refs):
        ins, outs, (ssem, rsem) = refs[:n], refs[n:2 * n], refs[2 * n:]
        x, y, c, _ = _place()
        cps = [_remote(s, out.at[c], ssem.at[o], rsem.at[o], (x, y, 1 - c)) for o, (s, out) in enumerate(zip(ins, outs))]
        for cp in cps:
            cp.start()
        for o, (s, out) in enumerate(zip(ins, outs)):
            _remote(s, out.at[1 - c], ssem.at[o], rsem.at[o], (x, y, 1 - c)).wait_recv()
        for cp in cps:
            cp.wait_send()

    outs = pl.pallas_call(
        body, name=name, in_specs=[_ANY] * n, out_specs=[_ANY] * n,
        out_shape=[jax.ShapeDtypeStruct((2,) + s.shape, s.dtype) for s in ss],
        scratch_shapes=[pltpu.SemaphoreType.DMA((n,)), pltpu.SemaphoreType.DMA((n,))],
    )(*ss)
    c = lax.axis_index("c")
    return [lax.dynamic_update_slice(out, s[None], (c, 0, 0)).reshape(2 * s.shape[0], s.shape[1]) for out, s in zip(outs, ss)]


def _reduce_begin(ps, tag):
    got = _pair_swap_halves(ps, f"rs_pair_swap_{tag}")
    return [_pair_add(p, g, f"rs_pair_add{o}_{tag}") for o, (p, g) in enumerate(zip(ps, got))]


def _reduce_finish(pair_sums, landed, tag):
    r4 = _own_slot_filled(landed, pair_sums)
    return _pair_share([_sum_slots(r, f"rs_sum{o}_{tag}") for o, r in enumerate(r4)], f"rs_share_{tag}")


def _reduce_to_chips(ps, tag):
    pair_sums = _reduce_begin(ps, tag)
    return _reduce_finish(pair_sums, _ChipExchange("scatter", pair_sums).standalone(f"rs_scatter_{tag}"), tag)


def _pad_rows(a, mult):
    r = (-a.shape[-2]) % mult
    return a if r == 0 else jnp.pad(a, [(0, 0)] * (a.ndim - 2) + [(0, r), (0, 0)])


def _hidden_pad(fs):
    return -(-fs // LANES) * LANES


def _w_out_block(w):
    return 2 * w["wgu"].shape[2] // (w["m"].shape[2] // N_CHIPS)


def _pack_wgu(wg, wu):
    fs = wg.shape[-1]
    t = jnp.stack([wg, wu], axis=1).astype(bf16)
    return jnp.pad(t, ((0, 0), (0, 0), (0, 0), (0, _hidden_pad(fs) - fs))).reshape(-1, _hidden_pad(fs))


def _pack_m(wd, w_out, w_branch):
    fs, D = wd.shape[1:]
    wdp = jnp.pad(wd.astype(bf16), ((0, 0), (0, _hidden_pad(fs) - fs), (0, 0))).reshape(-1, D)
    return jnp.concatenate([wdp, w_out.astype(bf16), w_branch.astype(bf16).reshape(-1, D)], axis=0)


def _w_in_cols(cg, lo, hi):
    p = cg.shape[2]
    parts = [cg[k][:, max(lo, k * p) - k * p:min(hi, (k + 1) * p) - k * p] for k in range(N_CHIPS)
             if max(lo, k * p) < min(hi, (k + 1) * p)]
    return parts[0] if len(parts) == 1 else jnp.concatenate(parts, axis=1)


def _layer_weights(ag, mg, cg, small):
    D = mg.shape[2]
    dc = D // N_CHIPS
    fp2 = 2 * ag.shape[2]
    wb = mg[:, fp2 + dc:].reshape(N_CHIPS, 3, BRANCH, dc).transpose(1, 2, 0, 3).reshape(3, BRANCH, D)
    c0, c1, c2 = 5 * BRANCH, 5 * BRANCH + 2 * N_HEADS, 8 * BRANCH + 2 * N_HEADS
    w = dict(small)
    w.update(wgu=ag, m=mg, w_branch=wb, w_main=_w_in_cols(cg, 0, c0),
             w_ab=jnp.pad(_w_in_cols(cg, c0, c1), ((0, 0), (0, LANES - 2 * N_HEADS))), w_sb=_w_in_cols(cg, c1, c2),
             w_gates=_w_in_cols(cg, c2, N_CHIPS * cg.shape[2]))
    return w


def _small_pack(pieces):
    flat, offs, r = [], [], 0
    for a in pieces:
        v = a.reshape(-1)
        pad = (-v.shape[0]) % PACK_W
        flat.append(jnp.pad(v, (0, pad)) if pad else v)
        offs.append(r)
        r += (v.shape[0] + pad) // PACK_W
    pack = jnp.concatenate(flat).reshape(r, PACK_W)
    return _pad_rows(pack, 16), offs


def _small_unpack(pack, offs, shapes):
    out = []
    for o, s in zip(offs, shapes):
        n = math.prod(s)
        rows = -(-n // PACK_W)
        out.append(pack[o:o + rows].reshape(-1)[:n].reshape(s))
    return out


SMALL_SHARDED = ("ffn_norm", "dn_conv")
SMALL_REPLICATED = ("mix_norm", "b_gate", "pool_w", "pool_scale", "dn_A_log", "dn_dt_bias", "dn_out_norm")


def kernel(x, ffn_norm, ffn_w_gate, ffn_w_up, ffn_w_down, mix_norm, w_in, b_gate, pool_w, pool_scale, dn_conv, dn_A_log, dn_dt_bias, dn_out_norm, w_branch, w_out, final_norm, loss_target, m_ffn_norm, m_ffn_w_gate, m_ffn_w_up, m_ffn_w_down, m_mix_norm, m_w_in, m_b_gate, m_pool_w, m_pool_scale, m_dn_conv, m_dn_A_log, m_dn_dt_bias, m_dn_out_norm, m_w_branch, m_w_out, m_final_norm, v_ffn_norm, v_ffn_w_gate, v_ffn_w_up, v_ffn_w_down, v_mix_norm, v_w_in, v_b_gate, v_pool_w, v_pool_scale, v_dn_conv, v_dn_A_log, v_dn_dt_bias, v_dn_out_norm, v_w_branch, v_w_out, v_final_norm):
    names = ("ffn_norm", "ffn_w_gate", "ffn_w_up", "ffn_w_down", "mix_norm", "w_in", "b_gate", "pool_w", "pool_scale", "dn_conv",
             "dn_A_log", "dn_dt_bias", "dn_out_norm", "w_branch", "w_out", "final_norm")
    wts = dict(zip(names, (ffn_norm, ffn_w_gate, ffn_w_up, ffn_w_down, mix_norm, w_in, b_gate, pool_w, pool_scale, dn_conv,
                           dn_A_log, dn_dt_bias, dn_out_norm, w_branch, w_out, final_norm)))
    ms = dict(zip(names, (m_ffn_norm, m_ffn_w_gate, m_ffn_w_up, m_ffn_w_down, m_mix_norm, m_w_in, m_b_gate, m_pool_w, m_pool_scale,
                          m_dn_conv, m_dn_A_log, m_dn_dt_bias, m_dn_out_norm, m_w_branch, m_w_out, m_final_norm)))
    vs = dict(zip(names, (v_ffn_norm, v_ffn_w_gate, v_ffn_w_up, v_ffn_w_down, v_mix_norm, v_w_in, v_b_gate, v_pool_w, v_pool_scale,
                          v_dn_conv, v_dn_A_log, v_dn_dt_bias, v_dn_out_norm, v_w_branch, v_w_out, v_final_norm)))
    B, S, D = x.shape
    T = B * S
    L = ffn_w_gate.shape[0]
    fs, ds_, cs = ffn_w_gate.shape[3], D // N_CHIPS, dn_conv.shape[2]
    fp = _hidden_pad(fs)
    chip = _chip_index()

    shards = [[_pack_wgu(ffn_w_gate[l], ffn_w_up[l]), _pack_m(ffn_w_down[l], w_out[l], w_branch[l]), w_in[l].astype(bf16)]
              for l in range(L)]
    small_shard, soffs = _small_pack([ffn_norm, dn_conv])
    small_g = _gather_chips([small_shard], "gather_small")[0]

    def chip_major(i, a):
        rows = -(-a.size // PACK_W)
        return small_g[:, soffs[i]:soffs[i] + rows].reshape(N_CHIPS, -1)[:, :a.size].reshape((N_CHIPS,) + a.shape)

    fn_full = jnp.moveaxis(chip_major(0, ffn_norm), 0, 2).reshape(L, 2, D)
    conv_full = jnp.moveaxis(chip_major(1, dn_conv), 0, 2).reshape(L, dn_conv.shape[1], N_CHIPS * cs)

    def small_params(l):
        return dict(ffn_norm=fn_full[l], mix_norm=mix_norm[l], b_gate=b_gate[l], pool_w=pool_w[l], pool_scale=pool_scale[l],
                    dn_conv=conv_full[l], dn_A_log=dn_A_log[l], dn_dt_bias=dn_dt_bias[l], dn_out_norm=dn_out_norm[l])

    xt = x.reshape(T, D)
    gathered = _gather_chips(shards[0], "gather_l0")
    layers, saved = [], []
    for l in range(L):
        w = _layer_weights(*gathered, small_params(l))
        carry = _ChipExchange("gather", shards[l + 1]) if l + 1 < L else None
        xt, sv, landed = _layer_fwd(xt, w, B, f"l{l}", carry)
        layers.append(w)
        saved.append(sv)
        if carry is not None:
            gathered = _gather_finish(shards[l + 1], landed, f"gather_pass_l{l + 1}")
    dx, dfn8, ls8 = _final_loss(xt, loss_target.reshape(T, D), final_norm)
    loss = lax.psum(jnp.sum(ls8), ("x", "y", "c")) * (0.5 / D)

    grads, red, pending = [None] * L, [None] * L, None
    for l in reversed(range(L)):
        carry = _ChipExchange("scatter", pending) if pending is not None else None
        dx, grads[l], landed = _layer_bwd(dx, layers[l], saved[l], B, f"l{l}", carry)
        if carry is not None:
            red[l + 1] = _reduce_finish(pending, landed, f"l{l + 1}")
        pending = _reduce_begin([grads[l]["A"], grads[l]["M"], grads[l]["C"]], f"l{l}")
    red[0] = _reduce_finish(pending, _ChipExchange("scatter", pending).standalone("rs_scatter_l0"), "l0")

    small_names = SMALL_SHARDED + SMALL_REPLICATED
    pieces = [g[k] for g in grads for k in small_names] + [jnp.sum(dfn8, axis=0)]
    spack, offs = _small_pack(pieces)
    sred = _reduce_to_chips([jnp.broadcast_to(spack[None], (N_CHIPS,) + spack.shape)], "small")[0]
    small_red = _small_unpack(sred, offs, [p.shape for p in pieces])

    gw = {k: [] for k in names if k != "final_norm"}
    for l in range(L):
        ga, gm, gc = red[l]
        gu = ga.reshape(2, 2, D, fp)[..., :fs]
        gw["ffn_w_gate"].append(gu[:, 0])
        gw["ffn_w_up"].append(gu[:, 1])
        gw["ffn_w_down"].append(gm[:2 * fp].reshape(2, fp, D)[:, :fs])
        gw["w_out"].append(gm[2 * fp:2 * fp + ds_])
        gw["w_branch"].append(gm[2 * fp + ds_:].reshape(3, BRANCH, ds_))
        gw["w_in"].append(gc)
        sm = dict(zip(small_names, small_red[l * len(small_names):(l + 1) * len(small_names)]))
        gw["ffn_norm"].append(lax.dynamic_slice_in_dim(sm["ffn_norm"], chip * ds_, ds_, axis=1))
        gw["dn_conv"].append(lax.dynamic_slice_in_dim(sm["dn_conv"], chip * cs, cs, axis=1))
        for k in SMALL_REPLICATED:
            gw[k].append(sm[k])
    gw = {k: jnp.stack(v) for k, v in gw.items()}
    gw["final_norm"] = small_red[-1]

    deltas, new_m, new_v = [], [], []
    for k in names:
        d, mo, vo = _adamw(wts[k], gw[k], ms[k], vs[k], f"adamw_{k}")
        deltas.append(d)
        new_m.append(mo)
        new_v.append(vo)
    return (loss, dx.reshape(B, S, D), *[gw[k] for k in names], *deltas, *new_m, *new_v)
```

```python
import functools
import math

import jax
import jax.numpy as jnp
from jax import lax
from jax.experimental import pallas as pl
from jax.experimental.pallas import tpu as pltpu

f32 = jnp.float32
bf16 = jnp.bfloat16
HI = lax.Precision.HIGHEST
MID = lax.Precision.HIGH

EPS = 1e-6
HEAD = 128
N_HEADS = 4
BRANCH = 512
CHUNK = 64
SB_BLOCK = 128
SB_QUERIES = 512
POOL_WINDOWS = (2, 4, 8, 16)
N_CHIPS = 4
LANES = 128
PACK_W = 1024
ADAM_LR, ADAM_B1, ADAM_B2, ADAM_EPS, ADAM_WD, ADAM_STEP = 0.001, 0.9, 0.999, 1e-08, 0.01, 10
VMEM_LIMIT = 56 * 1024 * 1024

NN = (((1,), (0,)), ((), ()))
NT = (((1,), (1,)), ((), ()))
TN = (((0,), (0,)), ((), ()))


def _cp(sem=None):
    return pltpu.CompilerParams(dimension_semantics=sem, vmem_limit_bytes=VMEM_LIMIT)


def _tile(n, prefs):
    for p in prefs:
        if n % p == 0:
            return p
    return n


def _row_tile(rows, width, max_elems):
    for d in range(rows, 0, -1):
        if rows % d == 0 and (d % 16 == 0 or d == rows) and d * width <= max_elems:
            return d
    return rows


def _dot(a, b, dn=NN):
    return lax.dot_general(a, b, dn, preferred_element_type=f32)


def _sigmoid(x):
    return 1.0 / (1.0 + jnp.exp(-x))


def _softplus(x):
    return jnp.maximum(x, 0.0) + jnp.log1p(jnp.exp(-jnp.abs(x)))


def _mm(pairs, M, N, out_dtype, name, tm=None, tn=None, res=None, scale=1.0, out_chip=False):
    tm = tm or _tile(M, (512, 256, 128))
    tn = N // N_CHIPS if out_chip else (tn or _tile(N, (512, 256, 128)))
    specs, arrs, dns = [], [], []

    def lead(sel, shape, imap):
        if sel is None:
            return pl.BlockSpec(shape, imap)
        return pl.BlockSpec((None,) + shape, lambda i, j, sel=sel, imap=imap: (sel,) + imap(i, j))

    for form, a, b, o in pairs:
        ka, kb, moff, noff = o.get("ka", 0), o.get("kb", 0), o.get("moff", 0), o.get("noff", 0)
        asel, bsel = o.get("asel"), o.get("bsel")
        if form == "nn":
            K = o.get("K") or a.shape[-1]
            sa = lead(asel, (tm, K), lambda i, j, ka=ka, moff=moff: (i + moff, ka))
            sb = lead(bsel, (K, tn), lambda i, j, kb=kb, noff=noff: (kb, j + noff))
            dn = NN
        elif form == "nt":
            K = o.get("K") or a.shape[-1]
            sa = lead(asel, (tm, K), lambda i, j, ka=ka, moff=moff: (i + moff, ka))
            if o.get("b_by_chip"):
                sb = pl.BlockSpec((None, tn, K), lambda i, j, kb=kb, noff=noff: (j, noff, kb))
            else:
                sb = lead(bsel, (tn, K), lambda i, j, kb=kb, noff=noff: (j + noff, kb))
            dn = NT
        else:
            K = a.shape[-2]
            sa = lead(asel, (K, tm), lambda i, j, moff=moff: (0, i + moff))
            sb = lead(bsel, (K, tn), lambda i, j, noff=noff: (0, j + noff))
            dn = TN
        specs += [sa, sb]
        arrs += [a, b]
        dns.append(dn)
    if res is not None:
        specs.append(pl.BlockSpec((tm, tn), lambda i, j: (i, j)))
        arrs.append(res)
    n = len(pairs)

    def body(*refs):
        o_ref = refs[-1]
        acc = None
        for p in range(n):
            d = _dot(refs[2 * p][...].astype(bf16), refs[2 * p + 1][...].astype(bf16), dns[p])
            acc = d if acc is None else acc + d
        if scale != 1.0:
            acc = acc * scale
        if res is not None:
            acc = acc + refs[2 * n][...]
        o_ref[...] = acc.astype(o_ref.dtype)

    if out_chip:
        out_spec = pl.BlockSpec((None, tm, tn), lambda i, j: (j, i, 0))
        out_shape = jax.ShapeDtypeStruct((N_CHIPS, M, tn), out_dtype)
    else:
        out_spec = pl.BlockSpec((tm, tn), lambda i, j: (i, j))
        out_shape = jax.ShapeDtypeStruct((M, N), out_dtype)
    return pl.pallas_call(
        body, name=name, grid=(M // tm, N // tn), in_specs=specs, out_specs=out_spec, out_shape=out_shape,
        compiler_params=_cp(("parallel", "parallel")),
    )(*arrs)


def _rms_fwd(x, g, name):
    T, D = x.shape
    tm = _tile(T, (512, 256, 128))

    def body(x_ref, g_ref, h_ref):
        xv = x_ref[...]
        r = lax.rsqrt(jnp.mean(xv * xv, axis=-1, keepdims=True) + EPS)
        h_ref[...] = (xv * r * g_ref[...]).astype(bf16)

    return pl.pallas_call(
        body, name=name, grid=(T // tm,),
        in_specs=[pl.BlockSpec((tm, D), lambda i: (i, 0)), pl.BlockSpec((1, D), lambda i: (0, 0))],
        out_specs=pl.BlockSpec((tm, D), lambda i: (i, 0)),
        out_shape=jax.ShapeDtypeStruct((T, D), bf16), compiler_params=_cp(("parallel",)),
    )(x, g.reshape(1, D))


def _fold8(v):
    r, d = v.shape
    return jnp.sum(v.reshape(r // 8, 8, d), axis=0)


def _rms_bwd(dh, x, g, dres, name):
    T, D = x.shape
    tm = _tile(T, (512, 256, 128))

    def body(dh_ref, x_ref, g_ref, dres_ref, dx_ref, dg_ref):
        xv = x_ref[...]
        r = lax.rsqrt(jnp.mean(xv * xv, axis=-1, keepdims=True) + EPS)
        xh = xv * r
        dhv = dh_ref[...]
        dxh = dhv * g_ref[...]
        dx_ref[...] = dres_ref[...] + r * (dxh - xh * jnp.mean(dxh * xh, axis=-1, keepdims=True))

        @pl.when(pl.program_id(0) == 0)
        def _():
            dg_ref[...] = jnp.zeros_like(dg_ref)

        dg_ref[...] += _fold8(dhv * xh)

    return pl.pallas_call(
        body, name=name, grid=(T // tm,),
        in_specs=[pl.BlockSpec((tm, D), lambda i: (i, 0)), pl.BlockSpec((tm, D), lambda i: (i, 0)),
                  pl.BlockSpec((1, D), lambda i: (0, 0)), pl.BlockSpec((tm, D), lambda i: (i, 0))],
        out_specs=[pl.BlockSpec((tm, D), lambda i: (i, 0)), pl.BlockSpec((8, D), lambda i: (0, 0))],
        out_shape=[jax.ShapeDtypeStruct((T, D), f32), jax.ShapeDtypeStruct((8, D), f32)],
        compiler_params=_cp(("arbitrary",)),
    )(dh, x, g.reshape(1, D), dres)


def _final_loss(x, target, g):
    T, D = x.shape
    tm = _tile(T, (512, 256, 128))

    def body(x_ref, t_ref, g_ref, dx_ref, dg_ref, ls_ref):
        xv = x_ref[...]
        r = lax.rsqrt(jnp.mean(xv * xv, axis=-1, keepdims=True) + EPS)
        xh = xv * r
        gv = g_ref[...]
        e = xh * gv - t_ref[...]
        dy = e * (1.0 / D)
        dxh = dy * gv
        dx_ref[...] = r * (dxh - xh * jnp.mean(dxh * xh, axis=-1, keepdims=True))

        @pl.when(pl.program_id(0) == 0)
        def _():
            dg_ref[...] = jnp.zeros_like(dg_ref)
            ls_ref[...] = jnp.zeros_like(ls_ref)

        dg_ref[...] += _fold8(dy * xh)
        ls_ref[...] += _fold8(e * e)

    return pl.pallas_call(
        body, name="final_loss", grid=(T // tm,),
        in_specs=[pl.BlockSpec((tm, D), lambda i: (i, 0)), pl.BlockSpec((tm, D), lambda i: (i, 0)),
                  pl.BlockSpec((1, D), lambda i: (0, 0))],
        out_specs=[pl.BlockSpec((tm, D), lambda i: (i, 0)), pl.BlockSpec((8, D), lambda i: (0, 0)),
                   pl.BlockSpec((8, D), lambda i: (0, 0))],
        out_shape=[jax.ShapeDtypeStruct((T, D), f32), jax.ShapeDtypeStruct((8, D), f32),
                   jax.ShapeDtypeStruct((8, D), f32)],
        compiler_params=_cp(("arbitrary",)),
    )(x, target, g.reshape(1, D))


def _ffn_up(h, wgu, f, name, carry=None):
    T, D = h.shape
    FP = wgu.shape[2]
    tm = _tile(T, (512, 256, 128))

    def body(h_ref, wg_ref, wu_ref, a_ref, b_ref, m_ref):
        hv = h_ref[...]
        a = _dot(hv, wg_ref[...])
        b = _dot(hv, wu_ref[...])
        a_ref[...] = a.astype(bf16)
        b_ref[...] = b.astype(bf16)
        m_ref[...] = (a * _sigmoid(a) * b).astype(bf16)

    o = jax.ShapeDtypeStruct((T, N_CHIPS * FP), bf16)
    ospec = pl.BlockSpec((tm, FP), lambda j, i: (i, j))
    grid = (N_CHIPS, T // tm)
    body, cin, cout, cshapes, cscratch = _carried(body, 3, 3, grid, carry)
    outs = pl.pallas_call(
        body, name=name, grid=grid,
        in_specs=[pl.BlockSpec((tm, D), lambda j, i: (i, 0)), pl.BlockSpec((None, D, FP), lambda j, i: (j, 2 * f, 0)),
                  pl.BlockSpec((None, D, FP), lambda j, i: (j, 2 * f + 1, 0))] + cin,
        out_specs=[ospec, ospec, ospec] + cout, out_shape=[o, o, o] + cshapes, scratch_shapes=cscratch,
        compiler_params=_cp(("arbitrary", "arbitrary") if carry else ("parallel", "parallel")),
    )(h, wgu, wgu, *(carry.ins if carry else []))
    return outs[:3], outs[3:]


def _ffn_bwd_mid(dx, m, a, b, f, name):
    T, D = dx.shape
    FP = a.shape[1] // N_CHIPS
    tm = _tile(T, (512, 256, 128))

    def body(dx_ref, wd_ref, a_ref, b_ref, da_ref, db_ref, m_ref):
        dm = 0.5 * _dot(dx_ref[...].astype(bf16), wd_ref[...], NT)
        av = a_ref[...].astype(f32)
        bv = b_ref[...].astype(f32)
        s = _sigmoid(av)
        silu = av * s
        da_ref[...] = (dm * bv * (s * (1.0 + av * (1.0 - s)))).astype(bf16)
        db_ref[...] = (dm * silu).astype(bf16)
        m_ref[...] = (silu * bv).astype(bf16)

    o = jax.ShapeDtypeStruct((T, N_CHIPS * FP), bf16)
    ospec = pl.BlockSpec((tm, FP), lambda j, i: (i, j))
    return pl.pallas_call(
        body, name=name, grid=(N_CHIPS, T // tm),
        in_specs=[pl.BlockSpec((tm, D), lambda j, i: (i, 0)), pl.BlockSpec((None, FP, D), lambda j, i: (j, f, 0)), ospec, ospec],
        out_specs=[ospec, ospec, ospec], out_shape=[o, o, o], compiler_params=_cp(("parallel", "parallel")),
    )(dx, m, a, b)


def _shift_down(x, k, row):
    return jnp.where(row < k, 0.0, pltpu.roll(x, k, 0))


def _shift_up(x, k, row):
    s = x.shape[0]
    return jnp.where(row >= s - k, 0.0, pltpu.roll(x, s - k, 0))


def _window_sum(x, g, row, shift):
    s2 = x + shift(x, 1, row)
    s4 = s2 + shift(s2, 2, row)
    s8 = s4 + shift(s4, 4, row)
    s16 = s8 + shift(s8, 8, row)
    return jnp.where(g == 0, s2, jnp.where(g == 1, s4, jnp.where(g == 2, s8, s16)))


def _pool_fwd(proj, pool_w, scale, name):
    B, S = proj.shape[0], proj.shape[1]
    G = len(POOL_WINDOWS)

    def body(u_ref, w_ref, sc_ref, y_ref):
        g = pl.program_id(1)
        u = u_ref[0].astype(f32)
        row = lax.broadcasted_iota(jnp.int32, u.shape, 0)
        win = _window_sum(u, g, row, _shift_down)
        cnt = jnp.minimum(row + 1, jnp.left_shift(2, g)).astype(f32)
        pooled = win / cnt - u
        mixed = _dot(pooled.astype(bf16), w_ref[0].astype(bf16))
        y_ref[0] = (mixed * sc_ref[...]).astype(bf16)

    return pl.pallas_call(
        body, name=name, grid=(B, G),
        in_specs=[pl.BlockSpec((1, S, HEAD), lambda b, g: (b, 0, g)), pl.BlockSpec((1, HEAD, HEAD), lambda b, g: (g, 0, 0)),
                  pl.BlockSpec((1, HEAD), lambda b, g: (0, g))],
        out_specs=pl.BlockSpec((1, S, HEAD), lambda b, g: (b, 0, g)),
        out_shape=jax.ShapeDtypeStruct((B, S, BRANCH), bf16), compiler_params=_cp(("parallel", "parallel")),
    )(proj, pool_w, scale.reshape(1, BRANCH))


def _pool_bwd(proj, dy, pool_w, scale, name):
    B, S = proj.shape[0], proj.shape[1]
    G = len(POOL_WINDOWS)

    def body(u_ref, dy_ref, w_ref, sc_ref, du_ref, dw_ref, dsc_ref):
        g = pl.program_id(0)
        u = u_ref[0].astype(f32)
        row = lax.broadcasted_iota(jnp.int32, u.shape, 0)
        cnt = jnp.minimum(row + 1, jnp.left_shift(2, g)).astype(f32)
        pooled = _window_sum(u, g, row, _shift_down) / cnt - u
        wv = w_ref[0].astype(bf16)
        mixed = _dot(pooled.astype(bf16), wv)
        dyv = dy_ref[0].astype(f32)
        dmix = (dyv * sc_ref[...]).astype(bf16)
        dpool = _dot(dmix, wv, NT)
        du_ref[0] = (_window_sum(dpool / cnt, g, row, _shift_up) - dpool).astype(bf16)

        @pl.when(pl.program_id(1) == 0)
        def _():
            dw_ref[...] = jnp.zeros_like(dw_ref)
            dsc_ref[...] = jnp.zeros_like(dsc_ref)

        dw_ref[0] += _dot(pooled.astype(bf16), dmix, TN)
        dsc_ref[...] += _fold8(dyv * mixed)

    return pl.pallas_call(
        body, name=name, grid=(G, B),
        in_specs=[pl.BlockSpec((1, S, HEAD), lambda g, b: (b, 0, g)), pl.BlockSpec((1, S, HEAD), lambda g, b: (b, 0, g)),
                  pl.BlockSpec((1, HEAD, HEAD), lambda g, b: (g, 0, 0)), pl.BlockSpec((1, HEAD), lambda g, b: (0, g))],
        out_specs=[pl.BlockSpec((1, S, HEAD), lambda g, b: (b, 0, g)), pl.BlockSpec((1, HEAD, HEAD), lambda g, b: (g, 0, 0)),
                   pl.BlockSpec((8, HEAD), lambda g, b: (0, g))],
        out_shape=[jax.ShapeDtypeStruct((B, S, BRANCH), bf16), jax.ShapeDtypeStruct((G, HEAD, HEAD), f32),
                   jax.ShapeDtypeStruct((8, BRANCH), f32)],
        compiler_params=_cp(("parallel", "arbitrary")),
    )(proj, dy, pool_w, scale.reshape(1, BRANCH))


def _split_dot(x, u):
    hi = x.astype(bf16)
    lo = (x - hi.astype(f32)).astype(bf16)
    return _dot(hi, u) + _dot(lo, u)


def _sb_fwd(sbqkv, name, carry=None):
    B, S, _ = sbqkv.shape
    KB = SB_BLOCK
    TQ = _tile(S, (SB_QUERIES, KB))
    ns = TQ // KB
    nq = S // TQ
    scale = HEAD ** -0.5

    def body(q_ref, k_ref, v_ref, o_ref, tot_ref, run_s, acc_s):
        r = lax.broadcasted_iota(jnp.int32, (KB, KB), 0)
        c = lax.broadcasted_iota(jnp.int32, (KB, KB), 1)
        causal = c < r
        after = (r > c).astype(bf16)

        def sub(qa, krows, masked, run, acc):
            z = _dot(qa, k_ref[0, krows, :], NT) * scale
            sp = jnp.maximum(z, 0.0) + jnp.log(1.0 + jnp.exp(-jnp.abs(z)))
            ln = -sp
            if masked:
                ln = jnp.where(causal, ln, 0.0)
            w = jnp.exp(z - sp + _split_dot(ln, after) + run)
            if masked:
                w = jnp.where(causal, w, 0.0)
            acc = acc + _dot(w.astype(bf16), v_ref[0, krows, :])
            return run + jnp.sum(ln, axis=1, keepdims=True), acc

        def qloop(i, carry):
            base = pl.multiple_of(i * TQ, TQ)
            qi = q_ref[0, pl.ds(base, TQ), :]
            for a in range(ns):
                qa = qi[a * KB:(a + 1) * KB]
                run, acc = jnp.zeros((KB, LANES), f32), jnp.zeros((KB, HEAD), f32)
                for s in range(a, -1, -1):
                    run, acc = sub(qa, pl.ds(base + s * KB, KB), s == a, run, acc)
                run_s[a * KB:(a + 1) * KB, :] = run
                acc_s[a * KB:(a + 1) * KB, :] = acc

            def group(t, cr):
                g0 = pl.multiple_of((i - 1 - t) * TQ, TQ)
                for s in range(ns - 1, -1, -1):
                    cr = sub(qi, pl.ds(g0 + s * KB, KB), False, *cr)
                return cr

            run, acc = lax.fori_loop(0, i, group, (run_s[...], acc_s[...]))
            o_ref[0, pl.ds(base, TQ), :] = acc.astype(bf16)
            tot_ref[0, 0, pl.ds(base, TQ), :] = run
            return carry

        lax.fori_loop(0, nq, qloop, 0)

    def spec(off):
        return pl.BlockSpec((1, S, HEAD), lambda b, h, off=off: (b, 0, off + h))

    body, cin, cout, cshapes, cscratch = _carried(body, 3, 2, (B, N_HEADS), carry)
    outs = pl.pallas_call(
        body, name=name, grid=(B, N_HEADS), in_specs=[spec(0), spec(N_HEADS), spec(2 * N_HEADS)] + cin,
        out_specs=[pl.BlockSpec((1, S, HEAD), lambda b, h: (b, 0, h)),
                   pl.BlockSpec((1, 1, S, LANES), lambda b, h: (b, h, 0, 0))] + cout,
        out_shape=[jax.ShapeDtypeStruct((B, S, BRANCH), bf16), jax.ShapeDtypeStruct((B, N_HEADS, S, LANES), f32)] + cshapes,
        scratch_shapes=[pltpu.VMEM((TQ, LANES), f32), pltpu.VMEM((TQ, HEAD), f32)] + cscratch,
        compiler_params=_cp(("arbitrary", "arbitrary") if carry else ("parallel", "parallel")),
    )(sbqkv, sbqkv, sbqkv, *(carry.ins if carry else []))
    return outs[:2], outs[2:]


def _sb_bwd(sbqkv, do, tot, name, carry=None):
    B, S, _ = sbqkv.shape
    KB = SB_BLOCK
    TQ = _tile(S, (SB_QUERIES, KB))
    ns = TQ // KB
    nq = S // TQ
    scale = HEAD ** -0.5

    def body(q_ref, k_ref, v_ref, do_ref, tot_ref, dq_ref, dk_ref, dv_ref, dk_acc, dv_acc):
        r = lax.broadcasted_iota(jnp.int32, (KB, KB), 0)
        c = lax.broadcasted_iota(jnp.int32, (KB, KB), 1)
        causal = c < r
        after = (r > c).astype(bf16)
        before = (r < c).astype(bf16)
        dk_acc[...] = jnp.zeros_like(dk_acc)
        dv_acc[...] = jnp.zeros_like(dv_acc)

        def sub(qi, doi, total, rows, masked, cl, cp, dq):
            kj = k_ref[0, rows, :]
            vj = v_ref[0, rows, :]
            z = _dot(qi, kj, NT) * scale
            sp = jnp.maximum(z, 0.0) + jnp.log(1.0 + jnp.exp(-jnp.abs(z)))
            ln = -sp
            if masked:
                ln = jnp.where(causal, ln, 0.0)
            bs = jnp.sum(ln, axis=1, keepdims=True)
            w = jnp.exp(z - sp + _split_dot(ln, after) + (total - cl - bs))
            if masked:
                w = jnp.where(causal, w, 0.0)
            p = _dot(doi, vj, NT) * w
            qsum = cp + _split_dot(p, before)
            sig = jnp.exp(z - sp)
            dz = (p * (1.0 - sig) - qsum * sig) * scale
            if masked:
                dz = jnp.where(causal, dz, 0.0)
            dzb = dz.astype(bf16)
            dq = dq + _dot(dzb, kj)
            dk_acc[rows, :] += _dot(dzb, qi, TN)
            dv_acc[rows, :] += _dot(w.astype(bf16), doi, TN)
            return cl + bs, cp + jnp.sum(p, axis=1, keepdims=True), dq

        def qloop(i, carry):
            base = pl.multiple_of(i * TQ, TQ)
            rows = pl.ds(base, TQ)
            qi = q_ref[0, rows, :]
            doi = do_ref[0, rows, :]
            total = tot_ref[0, 0, rows, :][:, 0:1]
            zero = jnp.zeros((TQ, 1), f32)

            def group(g, st):
                g0 = pl.multiple_of(g * TQ, TQ)
                for s in range(ns):
                    st = sub(qi, doi, total, pl.ds(g0 + s * KB, KB), False, *st)
                return st

            cl, cp, dq = lax.fori_loop(0, i, group, (zero, zero, jnp.zeros((TQ, HEAD), f32)))
            for a in range(ns):
                ra = slice(a * KB, (a + 1) * KB)
                st = (cl[ra], cp[ra], dq[ra])
                for s in range(a + 1):
                    st = sub(qi[ra], doi[ra], total[ra], pl.ds(base + s * KB, KB), s == a, *st)
                dq_ref[0, pl.ds(base + a * KB, KB), :] = st[2].astype(bf16)
            return carry

        lax.fori_loop(0, nq, qloop, 0)
        dk_ref[0] = dk_acc[...].astype(bf16)
        dv_ref[0] = dv_acc[...].astype(bf16)

    def spec(off):
        return pl.BlockSpec((1, S, HEAD), lambda b, h, off=off: (b, 0, off + h))

    o = jax.ShapeDtypeStruct((B, S, BRANCH), bf16)
    body, cin, cout, cshapes, cscratch = _carried(body, 5, 3, (B, N_HEADS), carry)
    outs = pl.pallas_call(
        body, name=name, grid=(B, N_HEADS),
        in_specs=[spec(0), spec(N_HEADS), spec(2 * N_HEADS), spec(0),
                  pl.BlockSpec((1, 1, S, LANES), lambda b, h: (b, h, 0, 0))] + cin,
        out_specs=[spec(0), spec(0), spec(0)] + cout, out_shape=[o, o, o] + cshapes,
        scratch_shapes=[pltpu.VMEM((S, HEAD), f32), pltpu.VMEM((S, HEAD), f32)] + cscratch,
        compiler_params=_cp(("arbitrary", "arbitrary") if carry else ("parallel", "parallel")),
    )(sbqkv, sbqkv, sbqkv, do, tot, *(carry.ins if carry else []))
    return outs[:3], outs[3:]


def _dn_params(a_log, dt_bias):
    p = jnp.zeros((8, LANES), f32)
    p = p.at[0, :N_HEADS].set(a_log)
    return p.at[1, :N_HEADS].set(dt_bias)


def _dn_prep(ab, par, name):
    B, S, _ = ab.shape
    R = 2 * CHUNK
    nt = S // R

    def body(ab_ref, par_ref, gcb_ref, bb_ref, gcr_ref):
        x = ab_ref[0]
        g = -jnp.exp(par_ref[0:1, :]) * _softplus(x + par_ref[1:2, :])
        r = lax.broadcasted_iota(jnp.int32, (R, R), 0)
        c = lax.broadcasted_iota(jnp.int32, (R, R), 1)
        tri = ((r >= c) & ((r >> 6) == (c >> 6))).astype(f32)
        cs = jnp.dot(tri, g, precision=HI, preferred_element_type=f32)
        beta = _sigmoid(x)
        cst = cs.T
        for h in range(N_HEADS):
            gcb_ref[0, h] = jnp.broadcast_to(cs[:, h:h + 1], (R, LANES))
            bb_ref[0, h] = jnp.broadcast_to(beta[:, N_HEADS + h:N_HEADS + h + 1], (R, LANES))
            gcr_ref[0, h, 0] = jnp.broadcast_to(cst[h:h + 1, 0:CHUNK], (8, CHUNK))
            gcr_ref[0, h, 1] = jnp.broadcast_to(cst[h:h + 1, CHUNK:R], (8, CHUNK))

    return pl.pallas_call(
        body, name=name, grid=(B, nt),
        in_specs=[pl.BlockSpec((1, R, LANES), lambda b, i: (b, i, 0)), pl.BlockSpec((8, LANES), lambda b, i: (0, 0))],
        out_specs=[pl.BlockSpec((1, N_HEADS, R, LANES), lambda b, i: (b, 0, i, 0)),
                   pl.BlockSpec((1, N_HEADS, R, LANES), lambda b, i: (b, 0, i, 0)),
                   pl.BlockSpec((1, N_HEADS, 2, 8, CHUNK), lambda b, i: (b, 0, i, 0, 0))],
        out_shape=[jax.ShapeDtypeStruct((B, N_HEADS, S, LANES), f32), jax.ShapeDtypeStruct((B, N_HEADS, S, LANES), f32),
                   jax.ShapeDtypeStruct((B, N_HEADS, S // CHUNK, 8, CHUNK), f32)],
        compiler_params=_cp(("parallel", "parallel")),
    )(ab, par)


def _bmm(a, b, prec=None):
    return jnp.einsum("nij,njk->nik", a, b, preferred_element_type=f32, precision=prec)


def _bmm_nt(a, b, prec=None):
    return jnp.einsum("nik,njk->nij", a, b, preferred_element_type=f32, precision=prec)


def _bmm_tn(a, b, prec=None):
    return jnp.einsum("nki,nkj->nij", a, b, preferred_element_type=f32, precision=prec)


def _tri_inv(L):
    C = L.shape[-1]
    r = lax.broadcasted_iota(jnp.int32, (C, C), 0)
    c = lax.broadcasted_iota(jnp.int32, (C, C), 1)
    eye = (r == c).astype(f32)
    bd16 = (r >> 4) == (c >> 4)
    bd32 = (r >> 5) == (c >> 5)
    mm = functools.partial(_bmm, prec=MID)
    n1 = -jnp.where(bd16, L, 0.0)
    n2 = mm(n1, n1)
    n4 = mm(n2, n2)
    n8 = mm(n4, n4)
    t = mm(mm(mm(eye + n1, eye + n2), eye + n4), eye + n8)
    t = t - mm(mm(t, jnp.where(bd32 & jnp.logical_not(bd16), L, 0.0)), t)
    t = t - mm(mm(t, jnp.where(bd32, 0.0, L)), t)
    return t


def _conv_silu(x, w, row):
    c = w[3:4] * x + w[2:3] * _shift_down(x, 1, row) + w[1:2] * _shift_down(x, 2, row) + w[0:1] * _shift_down(x, 3, row)
    return c, c * _sigmoid(c)


def _dn_intra(qn, kn, v, gcb, beta, gr):
    C = CHUNK
    r = lax.broadcasted_iota(jnp.int32, (C, C), 0)
    c = lax.broadcasted_iota(jnp.int32, (C, C), 1)
    incl = r >= c
    diff = gcb[:, :, :C] - gr
    dm = jnp.where(incl, jnp.exp(jnp.where(incl, diff, 0.0)), 0.0)
    ds = jnp.where(r > c, dm, 0.0)
    kb = kn * beta
    knb = kn.astype(bf16)
    L = _bmm_nt(kb.astype(bf16), knb) * ds
    eg = jnp.exp(gcb)
    a = _bmm_nt(qn.astype(bf16), knb) * dm
    gl = gcb[:, C - 1:C, :]
    ekd = jnp.exp(gl - gcb)
    return dict(dm=dm, ds=ds, kb=kb, L=L, eg=eg, rhs_u=v * beta, rhs_w=kb * eg, a=a,
                qd=qn * eg, kd=kn * ekd, ekd=ekd, cd=jnp.exp(gl))


def _dn_specs(S):
    def col(off):
        return pl.BlockSpec((1, S, HEAD), lambda b, h, off=off: (b, 0, off + h))

    def cw(off):
        return pl.BlockSpec((4, HEAD), lambda b, h, off=off: (0, off + h))

    per_head = pl.BlockSpec((1, 1, S, LANES), lambda b, h: (b, h, 0, 0))
    rowform = pl.BlockSpec((1, 1, S // CHUNK, 8, CHUNK), lambda b, h: (b, h, 0, 0, 0))
    gain = pl.BlockSpec((1, HEAD), lambda b, h: (0, 0))
    ins = [col(4), col(8), col(12), col(16), cw(0), cw(4), cw(8), per_head, per_head, rowform, gain]
    return ins, per_head


def _dn_act(x_ref, cw_ref, row, normalise, out_scale=1.0):
    _, act = _conv_silu(x_ref[0].astype(f32), cw_ref[...], row)
    if normalise:
        act = act * (lax.rsqrt(jnp.sum(act * act, axis=-1, keepdims=True) + EPS) * out_scale)
    return act


def _dn_group(qn_s, kn_s, v_s, gcb_ref, bb_ref, gcr_ref, g, ng):
    C = CHUNK
    rows = pl.ds(pl.multiple_of(g * (ng * C), ng * C), ng * C)
    ch = pl.ds(g * ng, ng)
    sh = (ng, C, HEAD)
    qn, kn, v = qn_s[rows, :].reshape(sh), kn_s[rows, :].reshape(sh), v_s[rows, :].reshape(sh)
    gcb3, beta = gcb_ref[0, 0, rows, :].reshape(sh), bb_ref[0, 0, rows, :].reshape(sh)
    gr = gcr_ref[0, 0, ch][:, 0:1, :]
    it = _dn_intra(qn, kn, v, gcb3, beta, gr)
    it.update(qn=qn, kn=kn, v=v, gcb=gcb3, beta=beta, gr=gr)
    return rows, ch, it


def _dn_fwd(proj, conv_w, gcb, betab, gcr, gain, name, carry=None):
    B, S, _ = proj.shape
    n, C = S // CHUNK, CHUNK
    ng = min(8, n)
    ins, per_head = _dn_specs(S)

    def body(q_ref, k_ref, v_ref, z_ref, cq_ref, ck_ref, cv_ref, gcb_ref, bb_ref, gcr_ref, gain_ref,
             y_ref, st_ref, vn_ref, qn_s, kn_s, v_s, u_s, w_s, qd_s, a_s, cd_s, g_s, h_s):
        row = lax.broadcasted_iota(jnp.int32, (S, HEAD), 0)
        qn_s[...] = _dn_act(q_ref, cq_ref, row, True, HEAD ** -0.5)
        kn_s[...] = _dn_act(k_ref, ck_ref, row, True)
        v_s[...] = _dn_act(v_ref, cv_ref, row, False)

        def group(g, carry):
            _, ch, it = _dn_group(qn_s, kn_s, v_s, gcb_ref, bb_ref, gcr_ref, g, ng)
            t = _tri_inv(it["L"])
            u = _bmm(t, it["rhs_u"], MID)
            wb = _bmm(t, it["rhs_w"], MID).astype(bf16)
            kdb = it["kd"].astype(bf16)
            u_s[ch] = u
            w_s[ch] = wb
            qd_s[ch] = it["qd"].astype(bf16)
            a_s[ch] = it["a"].astype(bf16)
            cd_s[ch] = it["cd"]
            g_s[ch] = _bmm_tn(kdb, wb).astype(bf16)
            h_s[ch] = _bmm_tn(kdb, u.astype(bf16))
            return carry

        lax.fori_loop(0, n // ng, group, 0)

        def step(i, st):
            sb = st.astype(bf16)
            st_ref[0, 0, i] = sb
            return st * cd_s[i] - _dot(g_s[i], sb) + h_s[i]

        lax.fori_loop(0, n, step, jnp.zeros((HEAD, HEAD), f32))

        def group_out(g, carry):
            rows = pl.ds(pl.multiple_of(g * (ng * C), ng * C), ng * C)
            ch = pl.ds(g * ng, ng)
            sn = st_ref[0, 0, ch]
            vn = (u_s[ch] - _bmm(w_s[ch], sn)).astype(bf16)
            vn_ref[0, 0, rows, :] = vn.reshape(ng * C, HEAD)
            o = (_bmm(qd_s[ch], sn) + _bmm(a_s[ch], vn)).reshape(ng * C, HEAD)
            zz = z_ref[0, rows, :].astype(f32)
            rr = lax.rsqrt(jnp.mean(o * o, axis=-1, keepdims=True) + EPS)
            y_ref[0, rows, :] = (o * rr * gain_ref[...] * (zz * _sigmoid(zz))).astype(bf16)
            return carry

        lax.fori_loop(0, n // ng, group_out, 0)

    seq = pltpu.VMEM((S, HEAD), f32)
    body, cin, cout, cshapes, cscratch = _carried(body, len(ins), 3, (B, N_HEADS), carry)
    outs = pl.pallas_call(
        body, name=name, grid=(B, N_HEADS), in_specs=ins + cin,
        out_specs=[pl.BlockSpec((1, S, HEAD), lambda b, h: (b, 0, h)),
                   pl.BlockSpec((1, 1, n, HEAD, HEAD), lambda b, h: (b, h, 0, 0, 0)), per_head] + cout,
        out_shape=[jax.ShapeDtypeStruct((B, S, BRANCH), bf16), jax.ShapeDtypeStruct((B, N_HEADS, n, HEAD, HEAD), bf16),
                   jax.ShapeDtypeStruct((B, N_HEADS, S, HEAD), bf16)] + cshapes,
        scratch_shapes=[seq, seq, seq, pltpu.VMEM((n, C, HEAD), f32), pltpu.VMEM((n, C, HEAD), bf16),
                        pltpu.VMEM((n, C, HEAD), bf16), pltpu.VMEM((n, C, C), bf16), pltpu.VMEM((n, 1, HEAD), f32),
                        pltpu.VMEM((n, HEAD, HEAD), bf16), pltpu.VMEM((n, HEAD, HEAD), f32)] + cscratch,
        compiler_params=_cp(("arbitrary", "arbitrary") if carry else ("parallel", "parallel")),
    )(proj, proj, proj, proj, conv_w, conv_w, conv_w, gcb, betab, gcr, gain.reshape(1, HEAD), *(carry.ins if carry else []))
    return outs[:3], outs[3:]


def _rowsum(x):
    return jnp.sum(x, axis=-1, keepdims=True)


def _dn_bwd(proj, dy, conv_w, gcb, betab, gcr, gain, states, vnew, name, carry=None):
    B, S, _ = proj.shape
    n, C = S // CHUNK, CHUNK
    ng = min(8, n)
    ins, per_head = _dn_specs(S)
    ins = ins + [pl.BlockSpec((1, S, HEAD), lambda b, h: (b, 0, h)),
                 pl.BlockSpec((1, 1, n, HEAD, HEAD), lambda b, h: (b, h, 0, 0, 0)), per_head]

    def body(q_ref, k_ref, v_ref, z_ref, cq_ref, ck_ref, cv_ref, gcb_ref, bb_ref, gcr_ref, gain_ref, dy_ref, st_ref, vn_ref,
             dq_ref, dk_ref, dv_ref, dz_ref, dg_ref, dbeta_ref, dconv_ref, dgain_ref,
             qn_s, kn_s, v_s, t_s, u_s, w_s, cd_s, do_s, dsp_s, dvn_s, g_s, q_s):
        row = lax.broadcasted_iota(jnp.int32, (S, HEAD), 0)
        qn_s[...] = _dn_act(q_ref, cq_ref, row, True, HEAD ** -0.5)
        kn_s[...] = _dn_act(k_ref, ck_ref, row, True)
        v_s[...] = _dn_act(v_ref, cv_ref, row, False)
        dgain_ref[...] = jnp.zeros_like(dgain_ref)
        gv = gain_ref[...]

        def group_fwd(g, carry):
            rows, ch, it = _dn_group(qn_s, kn_s, v_s, gcb_ref, bb_ref, gcr_ref, g, ng)
            t = _tri_inv(it["L"])
            ub = _bmm(t, it["rhs_u"], MID).astype(bf16)
            wb = _bmm(t, it["rhs_w"], MID).astype(bf16)
            ab, qdb = it["a"].astype(bf16), it["qd"].astype(bf16)
            vn = vn_ref[0, 0, rows, :].reshape(ng, C, HEAD)
            o = (_bmm(qdb, st_ref[0, 0, ch]) + _bmm(ab, vn)).reshape(ng * C, HEAD)
            zz = z_ref[0, rows, :].astype(f32)
            dyv = dy_ref[0, rows, :].astype(f32)
            rr = lax.rsqrt(jnp.mean(o * o, axis=-1, keepdims=True) + EPS)
            on = o * rr
            sz = _sigmoid(zz)
            dz_ref[0, rows, :] = (dyv * on * gv * (sz * (1.0 + zz * (1.0 - sz)))).astype(bf16)
            dnrm = dyv * (zz * sz)
            dgain_ref[0, 0] += _fold8(dnrm * on)
            doh = dnrm * gv
            do = rr * (doh - on * jnp.mean(doh * on, axis=-1, keepdims=True))
            dob = do.reshape(ng, C, HEAD).astype(bf16)
            atdo = _bmm_tn(ab, dob)
            t_s[ch] = t
            u_s[ch] = ub
            w_s[ch] = wb
            cd_s[ch] = it["cd"]
            do_s[ch] = dob
            dvn_s[ch] = atdo
            g_s[ch] = _bmm_tn(it["kd"].astype(bf16), wb).astype(bf16)
            q_s[ch] = _bmm_tn(qdb, dob) - _bmm_tn(wb, atdo.astype(bf16))
            return carry

        lax.fori_loop(0, n // ng, group_fwd, 0)

        def step(t, dsp):
            i = n - 1 - t
            dspb = dsp.astype(bf16)
            dsp_s[i] = dspb
            return dsp * cd_s[i] - _dot(g_s[i], dspb, TN) + q_s[i]

        lax.fori_loop(0, n, step, jnp.zeros((HEAD, HEAD), f32))

        r = lax.broadcasted_iota(jnp.int32, (C, C), 0)
        c = lax.broadcasted_iota(jnp.int32, (C, C), 1)
        upper = r <= c

        def group_bwd(g, carry):
            rows, ch, it = _dn_group(qn_s, kn_s, v_s, gcb_ref, bb_ref, gcr_ref, g, ng)
            sh = (ng, C, HEAD)
            qn, kn, v, beta, gcb3, gr = it["qn"], it["kn"], it["v"], it["beta"], it["gcb"], it["gr"]
            sn = st_ref[0, 0, ch]
            vn = vn_ref[0, 0, rows, :].reshape(sh)
            dsp, dob = dsp_s[ch], do_s[ch]
            dvn = dvn_s[ch] + _bmm(it["kd"].astype(bf16), dsp)
            t, ub, wb = t_s[ch], u_s[ch], w_s[ch]
            dvnb = dvn.astype(bf16)
            da = _bmm_nt(dob, vn)
            dat = _bmm_nt(vn, dob)
            dqd = _bmm_nt(dob, sn)
            dkd = _bmm_nt(vn, dsp)
            dcd = jnp.sum(jnp.sum(dsp.astype(f32) * sn.astype(f32), axis=2, keepdims=True), axis=1, keepdims=True)
            dw = -_bmm_nt(dvnb, sn)
            ru = _bmm_tn(t, dvn, MID)
            rw = _bmm_tn(t, dw, MID)
            rub, rwb = ru.astype(bf16), rw.astype(bf16)
            dL = -(_bmm_nt(rub, ub) + _bmm_nt(rwb, wb))
            dLt = -(_bmm_nt(ub, rub) + _bmm_nt(wb, rwb))
            knb, qnb, kbb = kn.astype(bf16), qn.astype(bf16), it["kb"].astype(bf16)
            dmt = jnp.where(upper, jnp.exp(jnp.where(upper, gr - gcb3[:, :, :C], 0.0)), 0.0)
            Lt = _bmm_nt(knb, kbb) * jnp.where(r < c, dmt, 0.0)
            At = _bmm_nt(knb, qnb) * dmt
            dgc = _rowsum(dL * it["L"] + da * it["a"]) - _rowsum(dLt * Lt + dat * At)
            dkk = (dL * it["ds"]).astype(bf16)
            dqk = (da * it["dm"]).astype(bf16)
            dkb = _bmm(dkk, knb) + rw * it["eg"]
            dkn = _bmm_tn(dkk, kbb) + _bmm_tn(dqk, qnb) + dkd * it["ekd"] + dkb * beta
            dqn = _bmm(dqk, knb) + dqd * it["eg"]
            tkd = _rowsum(dkd * it["kd"])
            dgl = jnp.sum(tkd, axis=1, keepdims=True) + dcd * it["cd"][:, :, 0:1]
            dgc = dgc + _rowsum(dqd * it["qd"]) - tkd + _rowsum(rw * it["rhs_w"])
            dbeta = _rowsum(ru * v) + _rowsum(dkb * kn)
            rowc = lax.broadcasted_iota(jnp.int32, (ng, C, 1), 1)
            dgc = dgc + jnp.where(rowc == C - 1, dgl, 0.0)
            rev = jnp.broadcast_to(upper.astype(f32), (ng, C, C))
            dg_ref[0, 0, rows, :] = _bmm(rev, jnp.broadcast_to(dgc, sh), HI).reshape(ng * C, LANES).astype(bf16)
            dbeta_ref[0, 0, rows, :] = jnp.broadcast_to(dbeta, sh).reshape(ng * C, LANES).astype(bf16)
            qn_s[rows, :] = dqn.reshape(ng * C, HEAD)
            kn_s[rows, :] = dkn.reshape(ng * C, HEAD)
            v_s[rows, :] = (ru * beta).reshape(ng * C, HEAD)
            return carry

        lax.fori_loop(0, n // ng, group_bwd, 0)

        def conv_back(x_ref, cw_ref, grad_s, out_ref, slot, normalise, out_scale):
            x = x_ref[0].astype(f32)
            w = cw_ref[...]
            pre, act = _conv_silu(x, w, row)
            dact = grad_s[...]
            if normalise:
                rn = lax.rsqrt(jnp.sum(act * act, axis=-1, keepdims=True) + EPS)
                unit = act * rn
                dact = (out_scale * rn) * (dact - unit * _rowsum(dact * unit))
            s = _sigmoid(pre)
            dc = dact * (s * (1.0 + pre * (1.0 - s)))
            out_ref[0] = (w[3:4] * dc + w[2:3] * _shift_up(dc, 1, row) + w[1:2] * _shift_up(dc, 2, row)
                          + w[0:1] * _shift_up(dc, 3, row)).astype(bf16)
            for tap in range(4):
                xs = x if tap == 3 else _shift_down(x, 3 - tap, row)
                dconv_ref[0, slot, tap:tap + 1, :] = jnp.sum(dc * xs, axis=0, keepdims=True)

        conv_back(q_ref, cq_ref, qn_s, dq_ref, 0, True, HEAD ** -0.5)
        conv_back(k_ref, ck_ref, kn_s, dk_ref, 1, True, 1.0)
        conv_back(v_ref, cv_ref, v_s, dv_ref, 2, False, 1.0)

    o512 = jax.ShapeDtypeStruct((B, S, BRANCH), bf16)
    s512 = pl.BlockSpec((1, S, HEAD), lambda b, h: (b, 0, h))
    ph = jax.ShapeDtypeStruct((B, N_HEADS, S, LANES), bf16)
    seq = pltpu.VMEM((S, HEAD), f32)
    cb = pltpu.VMEM((n, C, HEAD), bf16)
    body, cin, cout, cshapes, cscratch = _carried(body, len(ins), 8, (B, N_HEADS), carry)
    outs = pl.pallas_call(
        body, name=name, grid=(B, N_HEADS), in_specs=ins + cin,
        out_specs=[s512, s512, s512, s512, per_head, per_head, pl.BlockSpec((1, 3, 4, HEAD), lambda b, h: (b, 0, 0, h)),
                   pl.BlockSpec((1, 1, 8, HEAD), lambda b, h: (b, h, 0, 0))] + cout,
        out_shape=[o512, o512, o512, o512, ph, ph, jax.ShapeDtypeStruct((B, 3, 4, BRANCH), f32),
                   jax.ShapeDtypeStruct((B, N_HEADS, 8, HEAD), f32)] + cshapes,
        scratch_shapes=[seq, seq, seq, pltpu.VMEM((n, C, C), f32), cb, cb, pltpu.VMEM((n, 1, HEAD), f32), cb,
                        pltpu.VMEM((n, HEAD, HEAD), bf16), pltpu.VMEM((n, C, HEAD), f32), pltpu.VMEM((n, HEAD, HEAD), bf16),
                        pltpu.VMEM((n, HEAD, HEAD), f32)] + cscratch,
        compiler_params=_cp(("arbitrary", "arbitrary") if carry else ("parallel", "parallel")),
    )(proj, proj, proj, proj, conv_w, conv_w, conv_w, gcb, betab, gcr, gain.reshape(1, HEAD), dy, states, vnew,
      *(carry.ins if carry else []))
    return outs[:8], outs[8:]


def _dn_post(ab, par, dg, dbeta, name):
    B, S, _ = ab.shape
    ts = _tile(S, (512, 256, 128))

    def body(ab_ref, par_ref, dg_ref, db_ref, dab_ref, acc_ref):
        x = ab_ref[0]
        lane = lax.broadcasted_iota(jnp.int32, x.shape, 1)
        dgs = jnp.zeros_like(x)
        dbs = jnp.zeros_like(x)
        for h in range(N_HEADS):
            dgs = jnp.where(lane == h, dg_ref[0, h], dgs)
            dbs = jnp.where(lane == N_HEADS + h, db_ref[0, h], dbs)
        nega = -jnp.exp(par_ref[0:1, :])
        pre = x + par_ref[1:2, :]
        da = dgs * nega * _sigmoid(pre)
        beta = _sigmoid(x)
        dab_ref[0] = (da + dbs * beta * (1.0 - beta)).astype(bf16)

        @pl.when((pl.program_id(0) == 0) & (pl.program_id(1) == 0))
        def _():
            acc_ref[...] = jnp.zeros_like(acc_ref)

        acc_ref[0] += _fold8(dgs * nega * _softplus(pre))
        acc_ref[1] += _fold8(da)

    return pl.pallas_call(
        body, name=name, grid=(B, S // ts),
        in_specs=[pl.BlockSpec((1, ts, LANES), lambda b, i: (b, i, 0)), pl.BlockSpec((8, LANES), lambda b, i: (0, 0)),
                  pl.BlockSpec((1, N_HEADS, ts, LANES), lambda b, i: (b, 0, i, 0)),
                  pl.BlockSpec((1, N_HEADS, ts, LANES), lambda b, i: (b, 0, i, 0))],
        out_specs=[pl.BlockSpec((1, ts, LANES), lambda b, i: (b, i, 0)), pl.BlockSpec((2, 8, LANES), lambda b, i: (0, 0, 0))],
        out_shape=[jax.ShapeDtypeStruct((B, S, LANES), bf16), jax.ShapeDtypeStruct((2, 8, LANES), f32)],
        compiler_params=_cp(("arbitrary", "arbitrary")),
    )(ab, par, dg, dbeta)


def _merge_specs(T, D, tm, tn, order):
    nj = D // tn

    def ij(f):
        return (lambda i, j: f(i, j)) if order == "ij" else (lambda j, i: f(i, j))

    ys = [pl.BlockSpec((tm, BRANCH), ij(lambda i, j: (i, 0))) for _ in range(3)]
    wb = pl.BlockSpec((3, BRANCH, tn), ij(lambda i, j: (0, 0, j)))
    gl = [pl.BlockSpec((tm, tn), ij(lambda i, j, k=k: (i, k * nj + j))) for k in range(3)]
    bg = [pl.BlockSpec((1, tn), ij(lambda i, j, k=k: (0, k * nj + j))) for k in range(3)]
    return ys, wb, gl, bg


def _merge_fwd(ys, wb, gl, b_gate, name):
    T, D = ys[0].shape[0], wb.shape[2]
    tm, tn = _tile(T, (512, 256, 128)), _tile(D, (512, 256, 128))
    sy, swb, sgl, sbg = _merge_specs(T, D, tm, tn, "ij")

    def body(y0, y1, y2, wb_ref, g0, g1, g2, b0, b1, b2, o_ref):
        acc = None
        for k, (y, g, b) in enumerate(((y0, g0, b0), (y1, g1, b1), (y2, g2, b2))):
            term = _sigmoid(g[...].astype(f32) + b[...]) * _dot(y[...], wb_ref[k])
            acc = term if acc is None else acc + term
        o_ref[...] = acc.astype(bf16)

    bg = b_gate.reshape(1, 3 * D)
    return pl.pallas_call(
        body, name=name, grid=(T // tm, D // tn), in_specs=sy + [swb] + sgl + sbg,
        out_specs=pl.BlockSpec((tm, tn), lambda i, j: (i, j)), out_shape=jax.ShapeDtypeStruct((T, D), bf16),
        compiler_params=_cp(("parallel", "parallel")),
    )(*ys, wb, gl, gl, gl, bg, bg, bg)


def _merge_bwd(dm, ys, wb, gl, b_gate, name):
    T, D = dm.shape
    tm, tn = _tile(T, (512, 256, 128)), _tile(D, (512, 256, 128))
    sy, swb, sgl, sbg = _merge_specs(T, D, tm, tn, "ji")

    def body(dm_ref, y0, y1, y2, wb_ref, g0, g1, g2, b0, b1, b2, dgl_ref, dbd_ref, dbg_ref):
        dmv = dm_ref[...].astype(f32)

        @pl.when(pl.program_id(1) == 0)
        def _():
            dbg_ref[...] = jnp.zeros_like(dbg_ref)

        for k, (y, g, b) in enumerate(((y0, g0, b0), (y1, g1, b1), (y2, g2, b2))):
            s = _sigmoid(g[...].astype(f32) + b[...])
            dg = dmv * _dot(y[...], wb_ref[k]) * s * (1.0 - s)
            dgl_ref[k] = dg.astype(bf16)
            dbd_ref[k] = (dmv * s).astype(bf16)
            dbg_ref[k] += _fold8(dg)

    bg = b_gate.reshape(1, 3 * D)
    o3 = jax.ShapeDtypeStruct((3, T, D), bf16)
    s3 = pl.BlockSpec((3, tm, tn), lambda j, i: (0, i, j))
    return pl.pallas_call(
        body, name=name, grid=(D // tn, T // tm),
        in_specs=[pl.BlockSpec((tm, tn), lambda j, i: (i, j))] + sy + [swb] + sgl + sbg,
        out_specs=[s3, s3, pl.BlockSpec((3, 8, tn), lambda j, i: (0, 0, j))],
        out_shape=[o3, o3, jax.ShapeDtypeStruct((3, 8, D), f32)],
        compiler_params=_cp(("parallel", "arbitrary")),
    )(dm, *ys, wb, gl, gl, gl, bg, bg, bg)


def _ffn_fwd(x, g, w, f, tag, carry=None):
    T, D = x.shape
    FP = w["wgu"].shape[2]
    h = _rms_fwd(x, g, f"rms_{tag}")
    (a, b, hm), landed = _ffn_up(h, w["wgu"], f, f"ffn_up_{tag}", carry)
    down = [("nn", hm, w["m"], {"K": FP, "ka": k, "bsel": k, "kb": f}) for k in range(N_CHIPS)]
    y = _mm(down, T, D, f32, f"ffn_down_{tag}", res=x, scale=0.5)
    return y, (x, h, a, b), landed


def _ffn_bwd(dy, saved, g, w, f, tag):
    x, h, a, b = saved
    T, D = x.shape
    FP = w["wgu"].shape[2]
    F4 = N_CHIPS * FP
    da, db, hm = _ffn_bwd_mid(dy, w["m"], a, b, f, f"ffn_mid_bwd_{tag}")
    dwd = _mm([("tn", hm, dy, {})], F4, D, bf16, f"ffn_dwd_{tag}", scale=0.5, tm=_tile(F4, (512, 256, 128)))
    dwg = _mm([("tn", h, da, {})], D, F4, bf16, f"ffn_dwg_{tag}", out_chip=True)
    dwu = _mm([("tn", h, db, {})], D, F4, bf16, f"ffn_dwu_{tag}", out_chip=True)
    tn = _tile(D, (512, 256, 128))
    pairs = [("nt", t, w["wgu"], {"K": FP, "ka": k, "bsel": k, "noff": (2 * f + u) * (D // tn)})
             for u, t in enumerate((da, db)) for k in range(N_CHIPS)]
    dh = _mm(pairs, T, D, f32, f"ffn_dh_{tag}", tm=_tile(T, (256, 128)), tn=tn)
    dx, dg8 = _rms_bwd(dh, x, g, dy, f"rms_bwd_{tag}")
    return dx, dict(norm=jnp.sum(dg8, axis=0), wgu=[dwg, dwu], wd=dwd.reshape(N_CHIPS, FP, D))


def _layer_fwd(x, w, B, tag, carry=(None, None, None)):
    T, D = x.shape
    S = T // B
    x1, sv0, landed_ffn = _ffn_fwd(x, w["ffn_norm"][0], w, 0, f"pre_{tag}", carry[2])
    h = _rms_fwd(x1, w["mix_norm"], f"rms_mix_{tag}")
    pm = _mm([("nn", h, w["w_main"], {})], T, 5 * BRANCH, bf16, f"proj_main_{tag}")
    ab = _mm([("nn", h, w["w_ab"], {})], T, LANES, f32, f"proj_ab_{tag}", tn=LANES)
    sb = _mm([("nn", h, w["w_sb"], {})], T, 3 * BRANCH, bf16, f"proj_sb_{tag}")
    gl = _mm([("nn", h, w["w_gates"], {})], T, 3 * D, bf16, f"proj_gates_{tag}")
    pm3, ab3, sb3 = pm.reshape(B, S, -1), ab.reshape(B, S, LANES), sb.reshape(B, S, -1)
    y_pool = _pool_fwd(pm3, w["pool_w"], w["pool_scale"], f"pool_{tag}")
    par = _dn_params(w["dn_A_log"], w["dn_dt_bias"])
    gcb, betab, gcr = _dn_prep(ab3, par, f"dn_prep_{tag}")
    (y_dn, states, vnew), landed_dn = _dn_fwd(pm3, w["dn_conv"], gcb, betab, gcr, w["dn_out_norm"], f"dn_fwd_{tag}", carry[0])
    (y_sb, tot), landed_sb = _sb_fwd(sb3, f"sb_fwd_{tag}", carry[1])
    ys = [y_pool.reshape(T, BRANCH), y_dn.reshape(T, BRANCH), y_sb.reshape(T, BRANCH)]
    merged = _merge_fwd(ys, w["w_branch"], gl, w["b_gate"], f"merge_{tag}")
    dc = D // N_CHIPS
    out_pairs = [("nn", merged, w["m"], {"K": dc, "ka": k, "bsel": k, "kb": _w_out_block(w)}) for k in range(N_CHIPS)]
    x2 = _mm(out_pairs, T, D, f32, f"mix_out_{tag}", res=x1)
    x3, sv1, _ = _ffn_fwd(x2, w["ffn_norm"][1], w, 1, f"post_{tag}")
    saved = dict(sv0=sv0, sv1=sv1, x1=x1, h=h, pm3=pm3, ab3=ab3, sb3=sb3, gl=gl, par=par, gcb=gcb, betab=betab, gcr=gcr,
                 states=states, vnew=vnew, tot=tot, ys=ys, merged=merged)
    return x3, saved, (landed_dn, landed_sb, landed_ffn)


def _layer_bwd(dx3, w, sv, B, tag, carry=(None, None)):
    T, D = dx3.shape
    S = T // B
    dx2, g1 = _ffn_bwd(dx3, sv["sv1"], w["ffn_norm"][1], w, 1, f"post_{tag}")
    dc = D // N_CHIPS
    dmerged = _mm([("nt", dx2, w["m"], {"b_by_chip": True, "noff": _w_out_block(w)})], T, D, bf16, f"mix_dmerged_{tag}", tn=dc)
    dw_out = _mm([("tn", sv["merged"], dx2, {})], D, D, bf16, f"mix_dwout_{tag}")
    ys = sv["ys"]
    dgl, dbd, dbg8 = _merge_bwd(dmerged, ys, w["w_branch"], sv["gl"], w["b_gate"], f"merge_bwd_{tag}")
    dys, dwb = [], []
    for k in range(3):
        dys.append(_mm([("nt", dbd, w["w_branch"], {"asel": k, "bsel": k})], T, BRANCH, bf16, f"branch_dy{k}_{tag}"))
        dwb.append(_mm([("tn", ys[k], dbd, {"bsel": k})], BRANCH, D, bf16, f"branch_dw{k}_{tag}"))
    pm3, ab3, sb3 = sv["pm3"], sv["ab3"], sv["sb3"]
    du, dpool_w, dsc8 = _pool_bwd(pm3, dys[0].reshape(B, S, BRANCH), w["pool_w"], w["pool_scale"], f"pool_bwd_{tag}")
    (dq, dk, dv, dz, dg, dbeta, dconv, dgain), landed = _dn_bwd(
        pm3, dys[1].reshape(B, S, BRANCH), w["dn_conv"], sv["gcb"], sv["betab"], sv["gcr"], w["dn_out_norm"],
        sv["states"], sv["vnew"], f"dn_bwd_{tag}", carry[0])
    dab, dn_acc = _dn_post(ab3, sv["par"], dg, dbeta, f"dn_post_{tag}")
    (dsq, dsk, dsv), landed_sb = _sb_bwd(sb3, dys[2].reshape(B, S, BRANCH), sv["tot"], f"sb_bwd_{tag}", carry[1])
    main_parts = [t.reshape(T, BRANCH) for t in (du, dq, dk, dv, dz)]
    sb_parts = [t.reshape(T, BRANCH) for t in (dsq, dsk, dsv)]
    dab2 = dab.reshape(T, LANES)
    pairs = [("nt", t, w["w_main"], {"K": BRANCH, "kb": k}) for k, t in enumerate(main_parts)]
    pairs.append(("nt", dab2, w["w_ab"], {}))
    pairs += [("nt", t, w["w_sb"], {"K": BRANCH, "kb": k}) for k, t in enumerate(sb_parts)]
    pairs += [("nt", dgl, w["w_gates"], {"K": D, "kb": k, "asel": k}) for k in range(3)]
    dh = _mm(pairs, T, D, f32, f"mix_dh_{tag}", tm=_tile(T, (256, 128)))
    h = sv["h"]
    dw_cols = [_mm([("tn", h, t, {})], D, BRANCH, bf16, f"dwin_main{k}_{tag}") for k, t in enumerate(main_parts)]
    dw_cols.append(_mm([("tn", h, dab2, {})], D, LANES, bf16, f"dwin_ab_{tag}", tn=LANES)[:, :2 * N_HEADS])
    dw_cols += [_mm([("tn", h, t, {})], D, BRANCH, bf16, f"dwin_sb{k}_{tag}") for k, t in enumerate(sb_parts)]
    dw_cols += [_mm([("tn", h, dgl, {"bsel": k})], D, D, bf16, f"dwin_gate{k}_{tag}") for k in range(3)]
    dx1, dmix8 = _rms_bwd(dh, sv["x1"], w["mix_norm"], dx2, f"rms_mix_bwd_{tag}")
    dx0, g0 = _ffn_bwd(dx1, sv["sv0"], w["ffn_norm"][0], w, 0, f"pre_{tag}")
    dwb = jnp.stack(dwb).reshape(3, BRANCH, N_CHIPS, dc).transpose(2, 0, 1, 3).reshape(N_CHIPS, -1, D)
    dw_in = jnp.concatenate(dw_cols, axis=1)
    pc = dw_in.shape[1] // N_CHIPS
    grads = dict(
        ffn_norm=jnp.stack([g0["norm"], g1["norm"]]),
        A=jnp.concatenate(g0["wgu"] + g1["wgu"], axis=1),
        M=jnp.concatenate([g0["wd"], g1["wd"], dw_out.reshape(N_CHIPS, dc, D), dwb], axis=1),
        C=jnp.stack([dw_in[:, k * pc:(k + 1) * pc] for k in range(N_CHIPS)]),
        mix_norm=jnp.sum(dmix8, axis=0), b_gate=jnp.sum(dbg8, axis=1).reshape(3 * D),
        pool_w=dpool_w, pool_scale=jnp.sum(dsc8, axis=0), dn_conv=jnp.sum(dconv, axis=0).transpose(1, 0, 2).reshape(4, 3 * BRANCH),
        dn_A_log=jnp.sum(dn_acc[0], axis=0)[:N_HEADS], dn_dt_bias=jnp.sum(dn_acc[1], axis=0)[:N_HEADS],
        dn_out_norm=jnp.sum(dgain, axis=(0, 1, 2)))
    return dx0, grads, (landed, landed_sb)


def _local_step(x, target, layers, final_norm, B):
    saved = []
    for l, w in enumerate(layers):
        x, sv, _ = _layer_fwd(x, w, B, f"l{l}")
        saved.append(sv)
    dx, dfn8, ls8 = _final_loss(x, target, final_norm)
    grads = [None] * len(layers)
    for l in reversed(range(len(layers))):
        dx, grads[l], _ = _layer_bwd(dx, layers[l], saved[l], B, f"l{l}")
    return jnp.sum(ls8), dx, grads, jnp.sum(dfn8, axis=0)


def _adamw(w, g, m, v, name):
    shape = w.shape
    cols = shape[-1]
    rows = math.prod(shape[:-1]) if len(shape) > 1 else 1
    w2, g2, m2, v2 = (t.reshape(rows, cols) for t in (w, g, m, v))
    block_elems = 256 * 1024
    tr = rows if rows * cols <= block_elems else _tile(rows, [t for t in (512, 256, 128, 64, 32, 16, 8) if t * cols <= block_elems])

    def body(w_ref, g_ref, m_ref, v_ref, d_ref, mo_ref, vo_ref):
        gv = g_ref[...]
        mn = ADAM_B1 * m_ref[...] + (1.0 - ADAM_B1) * gv
        vn = ADAM_B2 * v_ref[...] + (1.0 - ADAM_B2) * (gv * gv)
        m_hat = mn / (1.0 - ADAM_B1 ** ADAM_STEP)
        v_hat = vn / (1.0 - ADAM_B2 ** ADAM_STEP)
        d_ref[...] = -ADAM_LR * (m_hat / (jnp.sqrt(v_hat) + ADAM_EPS) + ADAM_WD * w_ref[...])
        mo_ref[...] = mn
        vo_ref[...] = vn

    spec = pl.BlockSpec((tr, cols), lambda i: (i, 0))
    o = jax.ShapeDtypeStruct((rows, cols), f32)
    d, mo, vo = pl.pallas_call(
        body, name=name, grid=(rows // tr,), in_specs=[spec] * 4, out_specs=[spec] * 3, out_shape=[o, o, o],
        compiler_params=_cp(("parallel",)),
    )(w2, g2, m2, v2)
    return d.reshape(shape), mo.reshape(shape), vo.reshape(shape)


MESH = pl.DeviceIdType.MESH
_ANY = pl.BlockSpec(memory_space=pl.ANY)


def _place():
    x, y, c = lax.axis_index("x"), lax.axis_index("y"), lax.axis_index("c")
    return x, y, c, [(1 - x, y), (x, 1 - y), (1 - x, 1 - y)]


def _chip_index():
    return 2 * lax.axis_index("x") + lax.axis_index("y")


def _half(c, rh):
    return pl.ds(c * rh, rh)


def _remote(src, dst, ssem, rsem, to):
    return pltpu.make_async_remote_copy(src_ref=src, dst_ref=dst, send_sem=ssem, recv_sem=rsem, device_id=to,
                                        device_id_type=MESH)


class _ChipExchange:
    def __init__(self, kind, ins):
        self.kind, self.ins = kind, list(ins)
        self.n = len(self.ins)
        self.out_shapes = [jax.ShapeDtypeStruct((N_CHIPS,) + a.shape[-2:], a.dtype) for a in self.ins]
        self.scratch = [pltpu.SemaphoreType.DMA((self.n, 3)), pltpu.SemaphoreType.DMA((self.n, 3))]

    def _copies(self, in_refs, out_refs, ssem, rsem):
        x, y, c, chips = _place()
        me = 2 * x + y
        pairs = []
        for o, (src, dst) in enumerate(zip(in_refs, out_refs)):
            for k, (px, py) in enumerate(chips):
                peer = 2 * px + py
                if self.kind == "gather":
                    rows = _half(c, src.shape[0] // 2)
                    out, land = (src.at[rows], dst.at[me, rows]), dst.at[peer, rows]
                else:
                    out, land = (src.at[peer], dst.at[me]), dst.at[peer]
                pairs.append((_remote(out[0], out[1], ssem.at[o, k], rsem.at[o, k], (px, py, c)),
                              _remote(land, land, ssem.at[o, k], rsem.at[o, k], (px, py, c))))
        return pairs

    def start(self, in_refs, out_refs, ssem, rsem):
        for mine, _ in self._copies(in_refs, out_refs, ssem, rsem):
            mine.start()

    def wait(self, in_refs, out_refs, ssem, rsem):
        for mine, landing in self._copies(in_refs, out_refs, ssem, rsem):
            landing.wait_recv()
            mine.wait_send()

    def standalone(self, name):
        n = self.n

        def body(*refs):
            ins, outs, (ssem, rsem) = refs[:n], refs[n:2 * n], refs[2 * n:]
            self.start(ins, outs, ssem, rsem)
            self.wait(ins, outs, ssem, rsem)

        return pl.pallas_call(body, name=name, in_specs=[_ANY] * n, out_specs=[_ANY] * n, out_shape=self.out_shapes,
                              scratch_shapes=self.scratch)(*self.ins)


def _carried(body, n_in, n_out, grid, carry):
    if carry is None:
        return body, [], [], [], []
    n = carry.n

    def wrapped(*refs):
        ins, cin = refs[:n_in], refs[n_in:n_in + n]
        outs, cout = refs[n_in + n:n_in + n + n_out], refs[n_in + n + n_out:n_in + 2 * n + n_out]
        scratch, (ssem, rsem) = refs[n_in + 2 * n + n_out:-2], refs[-2:]
        step = pl.program_id(0) * grid[1] + pl.program_id(1)

        @pl.when(step == 0)
        def _():
            carry.start(cin, cout, ssem, rsem)

        body(*ins, *outs, *scratch)

        @pl.when(step == grid[0] * grid[1] - 1)
        def _():
            carry.wait(cin, cout, ssem, rsem)

    return wrapped, [_ANY] * n, [_ANY] * n, carry.out_shapes, carry.scratch


def _gather_finish(shards, landed, name):
    n = len(shards)

    def body(*refs):
        outs, (ssem, rsem) = refs[n:2 * n], refs[2 * n:]
        x, y, c, chips = _place()
        started = []
        for o, buf in enumerate(outs):
            rh = buf.shape[1] // 2
            for k, (px, py) in enumerate(chips):
                block = buf.at[2 * px + py, _half(c, rh)]
                cp = _remote(block, block, ssem.at[o, k], rsem.at[o, k], (x, y, 1 - c))
                cp.start()
                started.append(cp)
        for o, buf in enumerate(outs):
            rh = buf.shape[1] // 2
            for k, (px, py) in enumerate(chips):
                block = buf.at[2 * px + py, _half(1 - c, rh)]
                _remote(block, block, ssem.at[o, k], rsem.at[o, k], (x, y, 1 - c)).wait_recv()
        for cp in started:
            cp.wait_send()

    outs = pl.pallas_call(
        body, name=name, in_specs=[_ANY] * n, out_specs=[_ANY] * n,
        out_shape=[jax.ShapeDtypeStruct(a.shape, a.dtype) for a in landed], input_output_aliases={i: i for i in range(n)},
        scratch_shapes=[pltpu.SemaphoreType.DMA((n, 3)), pltpu.SemaphoreType.DMA((n, 3))],
    )(*landed)
    me = _chip_index()
    return [lax.dynamic_update_slice(g, s[None], (me, 0, 0)) for g, s in zip(outs, shards)]


def _gather_chips(shards, name):
    landed = _ChipExchange("gather", shards).standalone(f"{name}_ici")
    return _gather_finish(shards, landed, f"{name}_pass")


def _pair_swap_halves(ps, name):
    n = len(ps)

    def body(*refs):
        ins, outs, (ssem, rsem) = refs[:n], refs[n:2 * n], refs[2 * n:]
        x, y, c, _ = _place()
        cps = [_remote(p.at[:, _half(1 - c, p.shape[1] // 2)], out, ssem.at[o], rsem.at[o], (x, y, 1 - c))
               for o, (p, out) in enumerate(zip(ins, outs))]
        for cp in cps:
            cp.start()
        for cp in cps:
            cp.wait()

    return pl.pallas_call(
        body, name=name, in_specs=[_ANY] * n, out_specs=[_ANY] * n,
        out_shape=[jax.ShapeDtypeStruct((p.shape[0], p.shape[1] // 2, p.shape[2]), p.dtype) for p in ps],
        scratch_shapes=[pltpu.SemaphoreType.DMA((n,)), pltpu.SemaphoreType.DMA((n,))],
    )(*ps)


def _pair_add(p, got, name):
    n, R, W = p.shape
    rh = R // 2
    tr = _row_tile(rh, W, 2 * 1024 * 1024)
    nb = rh // tr

    def body(c_ref, p_ref, g_ref, o_ref):
        o_ref[...] = (p_ref[...].astype(f32) + g_ref[...].astype(f32)).astype(o_ref.dtype)

    return pl.pallas_call(
        body, name=name,
        grid_spec=pltpu.PrefetchScalarGridSpec(
            num_scalar_prefetch=1, grid=(n, nb),
            in_specs=[pl.BlockSpec((1, tr, W), lambda j, i, c_ref: (j, c_ref[0] * nb + i, 0)),
                      pl.BlockSpec((1, tr, W), lambda j, i, c_ref: (j, i, 0))],
            out_specs=pl.BlockSpec((1, tr, W), lambda j, i, c_ref: (j, i, 0))),
        out_shape=jax.ShapeDtypeStruct((n, rh, W), p.dtype), compiler_params=_cp(("parallel", "parallel")),
    )(lax.axis_index("c").astype(jnp.int32).reshape(1), p, got)


def _own_slot_filled(landed, ps):
    me = _chip_index()
    return [lax.dynamic_update_slice(out, lax.dynamic_slice(p, (me, 0, 0), (1,) + p.shape[1:]), (me, 0, 0))
            for out, p in zip(landed, ps)]


def _sum_slots(r4, name):
    n, R, W = r4.shape
    tr = _row_tile(R, W, 1024 * 1024)

    def body(r_ref, o_ref):
        acc = r_ref[0].astype(f32)
        for k in range(1, n):
            acc = acc + r_ref[k].astype(f32)
        o_ref[...] = acc

    return pl.pallas_call(
        body, name=name, grid=(R // tr,), in_specs=[pl.BlockSpec((n, tr, W), lambda i: (0, i, 0))],
        out_specs=pl.BlockSpec((tr, W), lambda i: (i, 0)), out_shape=jax.ShapeDtypeStruct((R, W), f32),
        compiler_params=_cp(("parallel",)),
    )(r4)


def _pair_share(ss, name):
    n = len(ss)

    def body(*refs):
        ins, outs, (ssem, rsem) = refs[:n], refs[n:2 * n], refs[2 * n:]
        x, y, c, _ = _place()
        cps = [_remote(s, out.at[c], ssem.at[o], rsem.at[o], (x, y, 1 - c)) for o, (s, out) in enumerate(zip(ins, outs))]
        for cp in cps:
            cp.start()
        for o, (s, out) in enumerate(zip(ins, outs)):
            _remote(s, out.at[1 - c], ssem.at[o], rsem.at[o], (x, y, 1 - c)).wait_recv()
        for cp in cps:
            cp.wait_send()

    outs = pl.pallas_call(
        body, name=name, in_specs=[_ANY] * n, out_specs=[_ANY] * n,
        out_shape=[jax.ShapeDtypeStruct((2,) + s.shape, s.dtype) for s in ss],
        scratch_shapes=[pltpu.SemaphoreType.DMA((n,)), pltpu.SemaphoreType.DMA((n,))],
    )(*ss)
    c = lax.axis_index("c")
    return [lax.dynamic_update_slice(out, s[None], (c, 0, 0)).reshape(2 * s.shape[0], s.shape[1]) for out, s in zip(outs, ss)]


def _reduce_begin(ps, tag):
    got = _pair_swap_halves(ps, f"rs_pair_swap_{tag}")
    return [_pair_add(p, g, f"rs_pair_add{o}_{tag}") for o, (p, g) in enumerate(zip(ps, got))]


def _reduce_finish(pair_sums, landed, tag):
    r4 = _own_slot_filled(landed, pair_sums)
    return _pair_share([_sum_slots(r, f"rs_sum{o}_{tag}") for o, r in enumerate(r4)], f"rs_share_{tag}")


def _reduce_to_chips(ps, tag):
    pair_sums = _reduce_begin(ps, tag)
    return _reduce_finish(pair_sums, _ChipExchange("scatter", pair_sums).standalone(f"rs_scatter_{tag}"), tag)


def _pad_rows(a, mult):
    r = (-a.shape[-2]) % mult
    return a if r == 0 else jnp.pad(a, [(0, 0)] * (a.ndim - 2) + [(0, r), (0, 0)])


def _hidden_pad(fs):
    return -(-fs // LANES) * LANES


def _w_out_block(w):
    return 2 * w["wgu"].shape[2] // (w["m"].shape[2] // N_CHIPS)


def _pack_wgu(wg, wu):
    fs = wg.shape[-1]
    t = jnp.stack([wg, wu], axis=1).astype(bf16)
    return jnp.pad(t, ((0, 0), (0, 0), (0, 0), (0, _hidden_pad(fs) - fs))).reshape(-1, _hidden_pad(fs))


def _pack_m(wd, w_out, w_branch):
    fs, D = wd.shape[1:]
    wdp = jnp.pad(wd.astype(bf16), ((0, 0), (0, _hidden_pad(fs) - fs), (0, 0))).reshape(-1, D)
    return jnp.concatenate([wdp, w_out.astype(bf16), w_branch.astype(bf16).reshape(-1, D)], axis=0)


def _w_in_cols(cg, lo, hi):
    p = cg.shape[2]
    parts = [cg[k][:, max(lo, k * p) - k * p:min(hi, (k + 1) * p) - k * p] for k in range(N_CHIPS)
             if max(lo, k * p) < min(hi, (k + 1) * p)]
    return parts[0] if len(parts) == 1 else jnp.concatenate(parts, axis=1)


def _layer_weights(ag, mg, cg, small):
    D = mg.shape[2]
    dc = D // N_CHIPS
    fp2 = 2 * ag.shape[2]
    wb = mg[:, fp2 + dc:].reshape(N_CHIPS, 3, BRANCH, dc).transpose(1, 2, 0, 3).reshape(3, BRANCH, D)
    c0, c1, c2 = 5 * BRANCH, 5 * BRANCH + 2 * N_HEADS, 8 * BRANCH + 2 * N_HEADS
    w = dict(small)
    w.update(wgu=ag, m=mg, w_branch=wb, w_main=_w_in_cols(cg, 0, c0),
             w_ab=jnp.pad(_w_in_cols(cg, c0, c1), ((0, 0), (0, LANES - 2 * N_HEADS))), w_sb=_w_in_cols(cg, c1, c2),
             w_gates=_w_in_cols(cg, c2, N_CHIPS * cg.shape[2]))
    return w


def _small_pack(pieces):
    flat, offs, r = [], [], 0
    for a in pieces:
        v = a.reshape(-1)
        pad = (-v.shape[0]) % PACK_W
        flat.append(jnp.pad(v, (0, pad)) if pad else v)
        offs.append(r)
        r += (v.shape[0] + pad) // PACK_W
    pack = jnp.concatenate(flat).reshape(r, PACK_W)
    return _pad_rows(pack, 16), offs


def _small_unpack(pack, offs, shapes):
    out = []
    for o, s in zip(offs, shapes):
        n = math.prod(s)
        rows = -(-n // PACK_W)
        out.append(pack[o:o + rows].reshape(-1)[:n].reshape(s))
    return out


SMALL_SHARDED = ("ffn_norm", "dn_conv")
SMALL_REPLICATED = ("mix_norm", "b_gate", "pool_w", "pool_scale", "dn_A_log", "dn_dt_bias", "dn_out_norm")


def kernel(x, ffn_norm, ffn_w_gate, ffn_w_up, ffn_w_down, mix_norm, w_in, b_gate, pool_w, pool_scale, dn_conv, dn_A_log, dn_dt_bias, dn_out_norm, w_branch, w_out, final_norm, loss_target, m_ffn_norm, m_ffn_w_gate, m_ffn_w_up, m_ffn_w_down, m_mix_norm, m_w_in, m_b_gate, m_pool_w, m_pool_scale, m_dn_conv, m_dn_A_log, m_dn_dt_bias, m_dn_out_norm, m_w_branch, m_w_out, m_final_norm, v_ffn_norm, v_ffn_w_gate, v_ffn_w_up, v_ffn_w_down, v_mix_norm, v_w_in, v_b_gate, v_pool_w, v_pool_scale, v_dn_conv, v_dn_A_log, v_dn_dt_bias, v_dn_out_norm, v_w_branch, v_w_out, v_final_norm):
    names = ("ffn_norm", "ffn_w_gate", "ffn_w_up", "ffn_w_down", "mix_norm", "w_in", "b_gate", "pool_w", "pool_scale", "dn_conv",
             "dn_A_log", "dn_dt_bias", "dn_out_norm", "w_branch", "w_out", "final_norm")
    wts = dict(zip(names, (ffn_norm, ffn_w_gate, ffn_w_up, ffn_w_down, mix_norm, w_in, b_gate, pool_w, pool_scale, dn_conv,
                           dn_A_log, dn_dt_bias, dn_out_norm, w_branch, w_out, final_norm)))
    ms = dict(zip(names, (m_ffn_norm, m_ffn_w_gate, m_ffn_w_up, m_ffn_w_down, m_mix_norm, m_w_in, m_b_gate, m_pool_w, m_pool_scale,
                          m_dn_conv, m_dn_A_log, m_dn_dt_bias, m_dn_out_norm, m_w_branch, m_w_out, m_final_norm)))
    vs = dict(zip(names, (v_ffn_norm, v_ffn_w_gate, v_ffn_w_up, v_ffn_w_down, v_mix_norm, v_w_in, v_b_gate, v_pool_w, v_pool_scale,
                          v_dn_conv, v_dn_A_log, v_dn_dt_bias, v_dn_out_norm, v_w_branch, v_w_out, v_final_norm)))
    B, S, D = x.shape
    T = B * S
    L = ffn_w_gate.shape[0]
    fs, ds_, cs = ffn_w_gate.shape[3], D // N_CHIPS, dn_conv.shape[2]
    fp = _hidden_pad(fs)
    chip = _chip_index()

    shards = [[_pack_wgu(ffn_w_gate[l], ffn_w_up[l]), _pack_m(ffn_w_down[l], w_out[l], w_branch[l]), w_in[l].astype(bf16)]
              for l in range(L)]
    small_shard, soffs = _small_pack([ffn_norm, dn_conv])
    small_g = _gather_chips([small_shard], "gather_small")[0]

    def chip_major(i, a):
        rows = -(-a.size // PACK_W)
        return small_g[:, soffs[i]:soffs[i] + rows].reshape(N_CHIPS, -1)[:, :a.size].reshape((N_CHIPS,) + a.shape)

    fn_full = jnp.moveaxis(chip_major(0, ffn_norm), 0, 2).reshape(L, 2, D)
    conv_full = jnp.moveaxis(chip_major(1, dn_conv), 0, 2).reshape(L, dn_conv.shape[1], N_CHIPS * cs)

    def small_params(l):
        return dict(ffn_norm=fn_full[l], mix_norm=mix_norm[l], b_gate=b_gate[l], pool_w=pool_w[l], pool_scale=pool_scale[l],
                    dn_conv=conv_full[l], dn_A_log=dn_A_log[l], dn_dt_bias=dn_dt_bias[l], dn_out_norm=dn_out_norm[l])

    xt = x.reshape(T, D)
    gathered = _gather_chips(shards[0], "gather_l0")
    layers, saved = [], []
    for l in range(L):
        w = _layer_weights(*gathered, small_params(l))
        more = l + 1 < L
        carry = [_ChipExchange("gather", [s]) if more else None for s in shards[min(l + 1, L - 1)]]
        xt, sv, (landed_dn, landed_sb, landed_ffn) = _layer_fwd(xt, w, B, f"l{l}", (carry[0], carry[1], carry[2]))
        layers.append(w)
        saved.append(sv)
        if more:
            gathered = _gather_finish(shards[l + 1], [landed_dn[0], landed_sb[0], landed_ffn[0]], f"gather_pass_l{l + 1}")
    dx, dfn8, ls8 = _final_loss(xt, loss_target.reshape(T, D), final_norm)
    loss = lax.psum(jnp.sum(ls8), ("x", "y", "c")) * (0.5 / D)

    grads, red, pending = [None] * L, [None] * L, None
    for l in reversed(range(L)):
        carry = (None, None) if pending is None else (_ChipExchange("scatter", pending[:2]), _ChipExchange("scatter", pending[2:]))
        dx, grads[l], (landed_dn, landed_sb) = _layer_bwd(dx, layers[l], saved[l], B, f"l{l}", carry)
        if pending is not None:
            red[l + 1] = _reduce_finish(pending, list(landed_dn) + list(landed_sb), f"l{l + 1}")
        pending = _reduce_begin([grads[l]["A"], grads[l]["M"], grads[l]["C"]], f"l{l}")
    red[0] = _reduce_finish(pending, _ChipExchange("scatter", pending).standalone("rs_scatter_l0"), "l0")

    small_names = SMALL_SHARDED + SMALL_REPLICATED
    pieces = [g[k] for g in grads for k in small_names] + [jnp.sum(dfn8, axis=0)]
    spack, offs = _small_pack(pieces)
    sred = _reduce_to_chips([jnp.broadcast_to(spack[None], (N_CHIPS,) + spack.shape)], "small")[0]
    small_red = _small_unpack(sred, offs, [p.shape for p in pieces])

    gw = {k: [] for k in names if k != "final_norm"}
    for l in range(L):
        ga, gm, gc = red[l]
        gu = ga.reshape(2, 2, D, fp)[..., :fs]
        gw["ffn_w_gate"].append(gu[:, 0])
        gw["ffn_w_up"].append(gu[:, 1])
        gw["ffn_w_down"].append(gm[:2 * fp].reshape(2, fp, D)[:, :fs])
        gw["w_out"].append(gm[2 * fp:2 * fp + ds_])
        gw["w_branch"].append(gm[2 * fp + ds_:].reshape(3, BRANCH, ds_))
        gw["w_in"].append(gc)
        sm = dict(zip(small_names, small_red[l * len(small_names):(l + 1) * len(small_names)]))
        gw["ffn_norm"].append(lax.dynamic_slice_in_dim(sm["ffn_norm"], chip * ds_, ds_, axis=1))
        gw["dn_conv"].append(lax.dynamic_slice_in_dim(sm["dn_conv"], chip * cs, cs, axis=1))
        for k in SMALL_REPLICATED:
            gw[k].append(sm[k])
    gw = {k: jnp.stack(v) for k, v in gw.items()}
    gw["final_norm"] = small_red[-1]

    deltas, new_m, new_v = [], [], []
    for k in names:
        d, mo, vo = _adamw(wts[k], gw[k], ms[k], vs[k], f"adamw_{k}")
        deltas.append(d)
        new_m.append(mo)
        new_v.append(vo)
    return (loss, dx.reshape(B, S, D), *[gw[k] for k in names], *deltas, *new_m, *new_v)
```

```python
import functools
import math

import jax
import jax.numpy as jnp
from jax import lax
from jax.experimental import pallas as pl
from jax.experimental.pallas import tpu as pltpu

f32 = jnp.float32
bf16 = jnp.bfloat16
HI = lax.Precision.HIGHEST
MID = lax.Precision.HIGH

EPS = 1e-6
HEAD = 128
N_HEADS = 4
BRANCH = 512
CHUNK = 64
SB_BLOCK = 128
SB_QUERIES = 512
POOL_WINDOWS = (2, 4, 8, 16)
N_CHIPS = 4
LANES = 128
PACK_W = 1024
ADAM_LR, ADAM_B1, ADAM_B2, ADAM_EPS, ADAM_WD, ADAM_STEP = 0.001, 0.9, 0.999, 1e-08, 0.01, 10
VMEM_LIMIT = 56 * 1024 * 1024
MM_VMEM_BUDGET = 36 * 1024 * 1024

NN = (((1,), (0,)), ((), ()))
NT = (((1,), (1,)), ((), ()))
TN = (((0,), (0,)), ((), ()))


def _cp(sem=None):
    return pltpu.CompilerParams(dimension_semantics=sem, vmem_limit_bytes=VMEM_LIMIT)


def _tile(n, prefs):
    for p in prefs:
        if n % p == 0:
            return p
    return n


def _row_tile(rows, width, max_elems):
    for d in range(rows, 0, -1):
        if rows % d == 0 and (d % 16 == 0 or d == rows) and d * width <= max_elems:
            return d
    return rows


def _dot(a, b, dn=NN):
    return lax.dot_general(a, b, dn, preferred_element_type=f32)


def _sigmoid(x):
    return 0.5 * jnp.tanh(0.5 * x) + 0.5


def _softplus(x):
    return jnp.maximum(x, 0.0) + jnp.log1p(jnp.exp(-jnp.abs(x)))


def _mm(pairs, M, N, out_dtype, name, tm=None, tn=None, res=None, scale=1.0, out_chip=False):
    tn = N // N_CHIPS if out_chip else (tn or _tile(N, (512, 256, 128)))

    def vmem_bytes(rows):
        total = rows * tn * (2 * jnp.dtype(out_dtype).itemsize + (8 if res is not None else 0) + 8)
        for form, a, b, o in pairs:
            K = a.shape[-2] if form == "tn" else (o.get("K") or a.shape[-1])
            total += 2 * K * (rows * a.dtype.itemsize + tn * b.dtype.itemsize)
        return total

    tm = tm or next((t for t in (1024, 512, 256, 128) if M % t == 0 and vmem_bytes(t) <= MM_VMEM_BUDGET), M)
    specs, arrs, dns = [], [], []

    def lead(sel, shape, imap):
        if sel is None:
            return pl.BlockSpec(shape, imap)
        return pl.BlockSpec((None,) + shape, lambda i, j, sel=sel, imap=imap: (sel,) + imap(i, j))

    for form, a, b, o in pairs:
        ka, kb, moff, noff = o.get("ka", 0), o.get("kb", 0), o.get("moff", 0), o.get("noff", 0)
        asel, bsel = o.get("asel"), o.get("bsel")
        if form == "nn":
            K = o.get("K") or a.shape[-1]
            sa = lead(asel, (tm, K), lambda i, j, ka=ka, moff=moff: (i + moff, ka))
            sb = lead(bsel, (K, tn), lambda i, j, kb=kb, noff=noff: (kb, j + noff))
            dn = NN
        elif form == "nt":
            K = o.get("K") or a.shape[-1]
            sa = lead(asel, (tm, K), lambda i, j, ka=ka, moff=moff: (i + moff, ka))
            if o.get("b_by_chip"):
                sb = pl.BlockSpec((None, tn, K), lambda i, j, kb=kb, noff=noff: (j, noff, kb))
            else:
                sb = lead(bsel, (tn, K), lambda i, j, kb=kb, noff=noff: (j + noff, kb))
            dn = NT
        else:
            K = a.shape[-2]
            sa = lead(asel, (K, tm), lambda i, j, moff=moff: (0, i + moff))
            sb = lead(bsel, (K, tn), lambda i, j, noff=noff: (0, j + noff))
            dn = TN
        specs += [sa, sb]
        arrs += [a, b]
        dns.append(dn)
    if res is not None:
        specs.append(pl.BlockSpec((tm, tn), lambda i, j: (i, j)))
        arrs.append(res)
    n = len(pairs)

    def body(*refs):
        o_ref = refs[-1]
        acc = None
        for p in range(n):
            d = _dot(refs[2 * p][...].astype(bf16), refs[2 * p + 1][...].astype(bf16), dns[p])
            acc = d if acc is None else acc + d
        if scale != 1.0:
            acc = acc * scale
        if res is not None:
            acc = acc + refs[2 * n][...]
        o_ref[...] = acc.astype(o_ref.dtype)

    if out_chip:
        out_spec = pl.BlockSpec((None, tm, tn), lambda i, j: (j, i, 0))
        out_shape = jax.ShapeDtypeStruct((N_CHIPS, M, tn), out_dtype)
    else:
        out_spec = pl.BlockSpec((tm, tn), lambda i, j: (i, j))
        out_shape = jax.ShapeDtypeStruct((M, N), out_dtype)
    return pl.pallas_call(
        body, name=name, grid=(M // tm, N // tn), in_specs=specs, out_specs=out_spec, out_shape=out_shape,
        compiler_params=_cp(("parallel", "parallel")),
    )(*arrs)


def _rms_fwd(x, g, name):
    T, D = x.shape
    tm = _tile(T, (512, 256, 128))

    def body(x_ref, g_ref, h_ref):
        xv = x_ref[...]
        r = lax.rsqrt(jnp.mean(xv * xv, axis=-1, keepdims=True) + EPS)
        h_ref[...] = (xv * r * g_ref[...]).astype(bf16)

    return pl.pallas_call(
        body, name=name, grid=(T // tm,),
        in_specs=[pl.BlockSpec((tm, D), lambda i: (i, 0)), pl.BlockSpec((1, D), lambda i: (0, 0))],
        out_specs=pl.BlockSpec((tm, D), lambda i: (i, 0)),
        out_shape=jax.ShapeDtypeStruct((T, D), bf16), compiler_params=_cp(("parallel",)),
    )(x, g.reshape(1, D))


def _fold8(v):
    r, d = v.shape
    return jnp.sum(v.reshape(r // 8, 8, d), axis=0)


def _rms_bwd(dh, x, g, dres, name):
    T, D = x.shape
    tm = _tile(T, (512, 256, 128))

    def body(dh_ref, x_ref, g_ref, dres_ref, dx_ref, dg_ref):
        xv = x_ref[...]
        r = lax.rsqrt(jnp.mean(xv * xv, axis=-1, keepdims=True) + EPS)
        xh = xv * r
        dhv = dh_ref[...]
        dxh = dhv * g_ref[...]
        dx_ref[...] = dres_ref[...] + r * (dxh - xh * jnp.mean(dxh * xh, axis=-1, keepdims=True))

        @pl.when(pl.program_id(0) == 0)
        def _():
            dg_ref[...] = jnp.zeros_like(dg_ref)

        dg_ref[...] += _fold8(dhv * xh)

    return pl.pallas_call(
        body, name=name, grid=(T // tm,),
        in_specs=[pl.BlockSpec((tm, D), lambda i: (i, 0)), pl.BlockSpec((tm, D), lambda i: (i, 0)),
                  pl.BlockSpec((1, D), lambda i: (0, 0)), pl.BlockSpec((tm, D), lambda i: (i, 0))],
        out_specs=[pl.BlockSpec((tm, D), lambda i: (i, 0)), pl.BlockSpec((8, D), lambda i: (0, 0))],
        out_shape=[jax.ShapeDtypeStruct((T, D), f32), jax.ShapeDtypeStruct((8, D), f32)],
        compiler_params=_cp(("arbitrary",)),
    )(dh, x, g.reshape(1, D), dres)


def _final_loss(x, target, g):
    T, D = x.shape
    tm = _tile(T, (512, 256, 128))

    def body(x_ref, t_ref, g_ref, dx_ref, dg_ref, ls_ref):
        xv = x_ref[...]
        r = lax.rsqrt(jnp.mean(xv * xv, axis=-1, keepdims=True) + EPS)
        xh = xv * r
        gv = g_ref[...]
        e = xh * gv - t_ref[...]
        dy = e * (1.0 / D)
        dxh = dy * gv
        dx_ref[...] = r * (dxh - xh * jnp.mean(dxh * xh, axis=-1, keepdims=True))

        @pl.when(pl.program_id(0) == 0)
        def _():
            dg_ref[...] = jnp.zeros_like(dg_ref)
            ls_ref[...] = jnp.zeros_like(ls_ref)

        dg_ref[...] += _fold8(dy * xh)
        ls_ref[...] += _fold8(e * e)

    return pl.pallas_call(
        body, name="final_loss", grid=(T // tm,),
        in_specs=[pl.BlockSpec((tm, D), lambda i: (i, 0)), pl.BlockSpec((tm, D), lambda i: (i, 0)),
                  pl.BlockSpec((1, D), lambda i: (0, 0))],
        out_specs=[pl.BlockSpec((tm, D), lambda i: (i, 0)), pl.BlockSpec((8, D), lambda i: (0, 0)),
                   pl.BlockSpec((8, D), lambda i: (0, 0))],
        out_shape=[jax.ShapeDtypeStruct((T, D), f32), jax.ShapeDtypeStruct((8, D), f32),
                   jax.ShapeDtypeStruct((8, D), f32)],
        compiler_params=_cp(("arbitrary",)),
    )(x, target, g.reshape(1, D))


def _ffn_up(h, wgu, f, name, carry=None):
    T, D = h.shape
    FP = wgu.shape[2]
    tm = _tile(T, (1024, 512, 256, 128))

    def body(h_ref, wg_ref, wu_ref, a_ref, b_ref, m_ref):
        hv = h_ref[...]
        a = _dot(hv, wg_ref[...])
        b = _dot(hv, wu_ref[...])
        a_ref[...] = a.astype(bf16)
        b_ref[...] = b.astype(bf16)
        m_ref[...] = (a * _sigmoid(a) * b).astype(bf16)

    o = jax.ShapeDtypeStruct((T, N_CHIPS * FP), bf16)
    ospec = pl.BlockSpec((tm, FP), lambda j, i: (i, j))
    grid = (N_CHIPS, T // tm)
    body, cin, cout, cshapes, cscratch = _carried(body, 3, 3, grid, carry)
    outs = pl.pallas_call(
        body, name=name, grid=grid,
        in_specs=[pl.BlockSpec((tm, D), lambda j, i: (i, 0)), pl.BlockSpec((None, D, FP), lambda j, i: (j, 2 * f, 0)),
                  pl.BlockSpec((None, D, FP), lambda j, i: (j, 2 * f + 1, 0))] + cin,
        out_specs=[ospec, ospec, ospec] + cout, out_shape=[o, o, o] + cshapes, scratch_shapes=cscratch,
        compiler_params=_cp(("arbitrary", "arbitrary") if carry else ("parallel", "parallel")),
    )(h, wgu, wgu, *(carry.ins if carry else []))
    return outs[:3], outs[3:]


def _ffn_bwd_mid(dx, m, a, b, f, name):
    T, D = dx.shape
    FP = a.shape[1] // N_CHIPS
    tm = _tile(T, (1024, 512, 256, 128))

    def body(dx_ref, wd_ref, a_ref, b_ref, da_ref, db_ref, m_ref):
        dm = 0.5 * _dot(dx_ref[...].astype(bf16), wd_ref[...], NT)
        av = a_ref[...].astype(f32)
        bv = b_ref[...].astype(f32)
        s = _sigmoid(av)
        silu = av * s
        da_ref[...] = (dm * bv * (s * (1.0 + av * (1.0 - s)))).astype(bf16)
        db_ref[...] = (dm * silu).astype(bf16)
        m_ref[...] = (silu * bv).astype(bf16)

    o = jax.ShapeDtypeStruct((T, N_CHIPS * FP), bf16)
    ospec = pl.BlockSpec((tm, FP), lambda j, i: (i, j))
    return pl.pallas_call(
        body, name=name, grid=(N_CHIPS, T // tm),
        in_specs=[pl.BlockSpec((tm, D), lambda j, i: (i, 0)), pl.BlockSpec((None, FP, D), lambda j, i: (j, f, 0)), ospec, ospec],
        out_specs=[ospec, ospec, ospec], out_shape=[o, o, o], compiler_params=_cp(("parallel", "parallel")),
    )(dx, m, a, b)


def _shift_down(x, k, row):
    return jnp.where(row < k, 0.0, pltpu.roll(x, k, 0))


def _shift_up(x, k, row):
    s = x.shape[0]
    return jnp.where(row >= s - k, 0.0, pltpu.roll(x, s - k, 0))


def _window_sum(x, g, row, shift):
    s2 = x + shift(x, 1, row)
    s4 = s2 + shift(s2, 2, row)
    s8 = s4 + shift(s4, 4, row)
    s16 = s8 + shift(s8, 8, row)
    return jnp.where(g == 0, s2, jnp.where(g == 1, s4, jnp.where(g == 2, s8, s16)))


def _pool_fwd(proj, pool_w, scale, name):
    B, S = proj.shape[0], proj.shape[1]
    G = len(POOL_WINDOWS)

    def body(u_ref, w_ref, sc_ref, y_ref):
        g = pl.program_id(1)
        u = u_ref[0].astype(f32)
        row = lax.broadcasted_iota(jnp.int32, u.shape, 0)
        win = _window_sum(u, g, row, _shift_down)
        cnt = jnp.minimum(row + 1, jnp.left_shift(2, g)).astype(f32)
        pooled = win / cnt - u
        mixed = _dot(pooled.astype(bf16), w_ref[0].astype(bf16))
        y_ref[0] = (mixed * sc_ref[...]).astype(bf16)

    return pl.pallas_call(
        body, name=name, grid=(B, G),
        in_specs=[pl.BlockSpec((1, S, HEAD), lambda b, g: (b, 0, g)), pl.BlockSpec((1, HEAD, HEAD), lambda b, g: (g, 0, 0)),
                  pl.BlockSpec((1, HEAD), lambda b, g: (0, g))],
        out_specs=pl.BlockSpec((1, S, HEAD), lambda b, g: (b, 0, g)),
        out_shape=jax.ShapeDtypeStruct((B, S, BRANCH), bf16), compiler_params=_cp(("parallel", "parallel")),
    )(proj, pool_w, scale.reshape(1, BRANCH))


def _pool_bwd(proj, dy, pool_w, scale, name):
    B, S = proj.shape[0], proj.shape[1]
    G = len(POOL_WINDOWS)

    def body(u_ref, dy_ref, w_ref, sc_ref, du_ref, dw_ref, dsc_ref):
        g = pl.program_id(0)
        u = u_ref[0].astype(f32)
        row = lax.broadcasted_iota(jnp.int32, u.shape, 0)
        cnt = jnp.minimum(row + 1, jnp.left_shift(2, g)).astype(f32)
        pooled = _window_sum(u, g, row, _shift_down) / cnt - u
        wv = w_ref[0].astype(bf16)
        mixed = _dot(pooled.astype(bf16), wv)
        dyv = dy_ref[0].astype(f32)
        dmix = (dyv * sc_ref[...]).astype(bf16)
        dpool = _dot(dmix, wv, NT)
        du_ref[0] = (_window_sum(dpool / cnt, g, row, _shift_up) - dpool).astype(bf16)

        @pl.when(pl.program_id(1) == 0)
        def _():
            dw_ref[...] = jnp.zeros_like(dw_ref)
            dsc_ref[...] = jnp.zeros_like(dsc_ref)

        dw_ref[0] += _dot(pooled.astype(bf16), dmix, TN)
        dsc_ref[...] += _fold8(dyv * mixed)

    return pl.pallas_call(
        body, name=name, grid=(G, B),
        in_specs=[pl.BlockSpec((1, S, HEAD), lambda g, b: (b, 0, g)), pl.BlockSpec((1, S, HEAD), lambda g, b: (b, 0, g)),
                  pl.BlockSpec((1, HEAD, HEAD), lambda g, b: (g, 0, 0)), pl.BlockSpec((1, HEAD), lambda g, b: (0, g))],
        out_specs=[pl.BlockSpec((1, S, HEAD), lambda g, b: (b, 0, g)), pl.BlockSpec((1, HEAD, HEAD), lambda g, b: (g, 0, 0)),
                   pl.BlockSpec((8, HEAD), lambda g, b: (0, g))],
        out_shape=[jax.ShapeDtypeStruct((B, S, BRANCH), bf16), jax.ShapeDtypeStruct((G, HEAD, HEAD), f32),
                   jax.ShapeDtypeStruct((8, BRANCH), f32)],
        compiler_params=_cp(("parallel", "arbitrary")),
    )(proj, dy, pool_w, scale.reshape(1, BRANCH))


def _split_dot(x, u):
    hi = x.astype(bf16)
    lo = (x - hi.astype(f32)).astype(bf16)
    return _dot(hi, u) + _dot(lo, u)


def _sb_fwd(sbqkv, name, carry=None):
    B, S, _ = sbqkv.shape
    KB = SB_BLOCK
    TQ = _tile(S, (SB_QUERIES, KB))
    ns = TQ // KB
    nq = S // TQ
    scale = HEAD ** -0.5

    def body(q_ref, k_ref, v_ref, o_ref, tot_ref, run_s, acc_s):
        r = lax.broadcasted_iota(jnp.int32, (KB, KB), 0)
        c = lax.broadcasted_iota(jnp.int32, (KB, KB), 1)
        causal = c < r
        after = (r > c).astype(bf16)

        def sub(qa, krows, masked, run, acc):
            z = _dot(qa, k_ref[0, krows, :], NT) * scale
            sp = jnp.maximum(z, 0.0) + jnp.log(1.0 + jnp.exp(-jnp.abs(z)))
            ln = -sp
            if masked:
                ln = jnp.where(causal, ln, 0.0)
            w = jnp.exp(z - sp + _split_dot(ln, after) + run)
            if masked:
                w = jnp.where(causal, w, 0.0)
            acc = acc + _dot(w.astype(bf16), v_ref[0, krows, :])
            return run + jnp.sum(ln, axis=1, keepdims=True), acc

        def qloop(i, carry):
            base = pl.multiple_of(i * TQ, TQ)
            qi = q_ref[0, pl.ds(base, TQ), :]
            for a in range(ns):
                qa = qi[a * KB:(a + 1) * KB]
                run, acc = jnp.zeros((KB, LANES), f32), jnp.zeros((KB, HEAD), f32)
                for s in range(a, -1, -1):
                    run, acc = sub(qa, pl.ds(base + s * KB, KB), s == a, run, acc)
                run_s[a * KB:(a + 1) * KB, :] = run
                acc_s[a * KB:(a + 1) * KB, :] = acc

            def group(t, cr):
                g0 = pl.multiple_of((i - 1 - t) * TQ, TQ)
                for s in range(ns - 1, -1, -1):
                    cr = sub(qi, pl.ds(g0 + s * KB, KB), False, *cr)
                return cr

            run, acc = lax.fori_loop(0, i, group, (run_s[...], acc_s[...]))
            o_ref[0, pl.ds(base, TQ), :] = acc.astype(bf16)
            tot_ref[0, 0, pl.ds(base, TQ), :] = run
            return carry

        lax.fori_loop(0, nq, qloop, 0)

    def spec(off):
        return pl.BlockSpec((1, S, HEAD), lambda b, h, off=off: (b, 0, off + h))

    body, cin, cout, cshapes, cscratch = _carried(body, 3, 2, (B, N_HEADS), carry)
    outs = pl.pallas_call(
        body, name=name, grid=(B, N_HEADS), in_specs=[spec(0), spec(N_HEADS), spec(2 * N_HEADS)] + cin,
        out_specs=[pl.BlockSpec((1, S, HEAD), lambda b, h: (b, 0, h)),
                   pl.BlockSpec((1, 1, S, LANES), lambda b, h: (b, h, 0, 0))] + cout,
        out_shape=[jax.ShapeDtypeStruct((B, S, BRANCH), bf16), jax.ShapeDtypeStruct((B, N_HEADS, S, LANES), f32)] + cshapes,
        scratch_shapes=[pltpu.VMEM((TQ, LANES), f32), pltpu.VMEM((TQ, HEAD), f32)] + cscratch,
        compiler_params=_cp(("arbitrary", "arbitrary") if carry else ("parallel", "parallel")),
    )(sbqkv, sbqkv, sbqkv, *(carry.ins if carry else []))
    return outs[:2], outs[2:]


def _sb_bwd(sbqkv, do, tot, name, carry=None):
    B, S, _ = sbqkv.shape
    KB = SB_BLOCK
    TQ = _tile(S, (SB_QUERIES, KB))
    ns = TQ // KB
    nq = S // TQ
    scale = HEAD ** -0.5

    def body(q_ref, k_ref, v_ref, do_ref, tot_ref, dq_ref, dk_ref, dv_ref, dk_acc, dv_acc):
        r = lax.broadcasted_iota(jnp.int32, (KB, KB), 0)
        c = lax.broadcasted_iota(jnp.int32, (KB, KB), 1)
        causal = c < r
        after = (r > c).astype(bf16)
        before = (r < c).astype(bf16)
        dk_acc[...] = jnp.zeros_like(dk_acc)
        dv_acc[...] = jnp.zeros_like(dv_acc)

        def sub(qi, doi, total, rows, masked, cl, cp, dq):
            kj = k_ref[0, rows, :]
            vj = v_ref[0, rows, :]
            z = _dot(qi, kj, NT) * scale
            sp = jnp.maximum(z, 0.0) + jnp.log(1.0 + jnp.exp(-jnp.abs(z)))
            ln = -sp
            if masked:
                ln = jnp.where(causal, ln, 0.0)
            bs = jnp.sum(ln, axis=1, keepdims=True)
            w = jnp.exp(z - sp + _split_dot(ln, after) + (total - cl - bs))
            if masked:
                w = jnp.where(causal, w, 0.0)
            p = _dot(doi, vj, NT) * w
            qsum = cp + _split_dot(p, before)
            sig = jnp.exp(z - sp)
            dz = (p * (1.0 - sig) - qsum * sig) * scale
            if masked:
                dz = jnp.where(causal, dz, 0.0)
            dzb = dz.astype(bf16)
            dq = dq + _dot(dzb, kj)
            dk_acc[rows, :] += _dot(dzb, qi, TN)
            dv_acc[rows, :] += _dot(w.astype(bf16), doi, TN)
            return cl + bs, cp + jnp.sum(p, axis=1, keepdims=True), dq

        def qloop(i, carry):
            base = pl.multiple_of(i * TQ, TQ)
            rows = pl.ds(base, TQ)
            qi = q_ref[0, rows, :]
            doi = do_ref[0, rows, :]
            total = tot_ref[0, 0, rows, :][:, 0:1]
            zero = jnp.zeros((TQ, 1), f32)

            def group(g, st):
                g0 = pl.multiple_of(g * TQ, TQ)
                for s in range(ns):
                    st = sub(qi, doi, total, pl.ds(g0 + s * KB, KB), False, *st)
                return st

            cl, cp, dq = lax.fori_loop(0, i, group, (zero, zero, jnp.zeros((TQ, HEAD), f32)))
            for a in range(ns):
                ra = slice(a * KB, (a + 1) * KB)
                st = (cl[ra], cp[ra], dq[ra])
                for s in range(a + 1):
                    st = sub(qi[ra], doi[ra], total[ra], pl.ds(base + s * KB, KB), s == a, *st)
                dq_ref[0, pl.ds(base + a * KB, KB), :] = st[2].astype(bf16)
            return carry

        lax.fori_loop(0, nq, qloop, 0)
        dk_ref[0] = dk_acc[...].astype(bf16)
        dv_ref[0] = dv_acc[...].astype(bf16)

    def spec(off):
        return pl.BlockSpec((1, S, HEAD), lambda b, h, off=off: (b, 0, off + h))

    o = jax.ShapeDtypeStruct((B, S, BRANCH), bf16)
    body, cin, cout, cshapes, cscratch = _carried(body, 5, 3, (B, N_HEADS), carry)
    outs = pl.pallas_call(
        body, name=name, grid=(B, N_HEADS),
        in_specs=[spec(0), spec(N_HEADS), spec(2 * N_HEADS), spec(0),
                  pl.BlockSpec((1, 1, S, LANES), lambda b, h: (b, h, 0, 0))] + cin,
        out_specs=[spec(0), spec(0), spec(0)] + cout, out_shape=[o, o, o] + cshapes,
        scratch_shapes=[pltpu.VMEM((S, HEAD), f32), pltpu.VMEM((S, HEAD), f32)] + cscratch,
        compiler_params=_cp(("arbitrary", "arbitrary") if carry else ("parallel", "parallel")),
    )(sbqkv, sbqkv, sbqkv, do, tot, *(carry.ins if carry else []))
    return outs[:3], outs[3:]


def _dn_params(a_log, dt_bias):
    p = jnp.zeros((8, LANES), f32)
    p = p.at[0, :N_HEADS].set(a_log)
    return p.at[1, :N_HEADS].set(dt_bias)


def _dn_prep(ab, par, name):
    B, S, _ = ab.shape
    R = 2 * CHUNK
    nt = S // R

    def body(ab_ref, par_ref, gcb_ref, bb_ref, gcr_ref):
        x = ab_ref[0]
        g = -jnp.exp(par_ref[0:1, :]) * _softplus(x + par_ref[1:2, :])
        r = lax.broadcasted_iota(jnp.int32, (R, R), 0)
        c = lax.broadcasted_iota(jnp.int32, (R, R), 1)
        tri = ((r >= c) & ((r >> 6) == (c >> 6))).astype(f32)
        cs = jnp.dot(tri, g, precision=HI, preferred_element_type=f32)
        beta = _sigmoid(x)
        cst = cs.T
        for h in range(N_HEADS):
            gcb_ref[0, h] = jnp.broadcast_to(cs[:, h:h + 1], (R, LANES))
            bb_ref[0, h] = jnp.broadcast_to(beta[:, N_HEADS + h:N_HEADS + h + 1], (R, LANES))
            gcr_ref[0, h, 0] = jnp.broadcast_to(cst[h:h + 1, 0:CHUNK], (8, CHUNK))
            gcr_ref[0, h, 1] = jnp.broadcast_to(cst[h:h + 1, CHUNK:R], (8, CHUNK))

    return pl.pallas_call(
        body, name=name, grid=(B, nt),
        in_specs=[pl.BlockSpec((1, R, LANES), lambda b, i: (b, i, 0)), pl.BlockSpec((8, LANES), lambda b, i: (0, 0))],
        out_specs=[pl.BlockSpec((1, N_HEADS, R, LANES), lambda b, i: (b, 0, i, 0)),
                   pl.BlockSpec((1, N_HEADS, R, LANES), lambda b, i: (b, 0, i, 0)),
                   pl.BlockSpec((1, N_HEADS, 2, 8, CHUNK), lambda b, i: (b, 0, i, 0, 0))],
        out_shape=[jax.ShapeDtypeStruct((B, N_HEADS, S, LANES), f32), jax.ShapeDtypeStruct((B, N_HEADS, S, LANES), f32),
                   jax.ShapeDtypeStruct((B, N_HEADS, S // CHUNK, 8, CHUNK), f32)],
        compiler_params=_cp(("parallel", "parallel")),
    )(ab, par)


def _bmm(a, b, prec=None):
    return jnp.einsum("nij,njk->nik", a, b, preferred_element_type=f32, precision=prec)


def _bmm_nt(a, b, prec=None):
    return jnp.einsum("nik,njk->nij", a, b, preferred_element_type=f32, precision=prec)


def _bmm_tn(a, b, prec=None):
    return jnp.einsum("nki,nkj->nij", a, b, preferred_element_type=f32, precision=prec)


def _tri_inv(L):
    C = L.shape[-1]
    r = lax.broadcasted_iota(jnp.int32, (C, C), 0)
    c = lax.broadcasted_iota(jnp.int32, (C, C), 1)
    eye = (r == c).astype(f32)
    bd16 = (r >> 4) == (c >> 4)
    bd32 = (r >> 5) == (c >> 5)
    mm = functools.partial(_bmm, prec=MID)
    n1 = -jnp.where(bd16, L, 0.0)
    n2 = mm(n1, n1)
    n4 = mm(n2, n2)
    n8 = mm(n4, n4)
    t = mm(mm(mm(eye + n1, eye + n2), eye + n4), eye + n8)
    t = t - mm(mm(t, jnp.where(bd32 & jnp.logical_not(bd16), L, 0.0)), t)
    t = t - mm(mm(t, jnp.where(bd32, 0.0, L)), t)
    return t


def _conv_silu(x, w, row):
    c = w[3:4] * x + w[2:3] * _shift_down(x, 1, row) + w[1:2] * _shift_down(x, 2, row) + w[0:1] * _shift_down(x, 3, row)
    return c, c * _sigmoid(c)


def _dn_intra(qn, kn, v, gcb, beta, gr):
    C = CHUNK
    r = lax.broadcasted_iota(jnp.int32, (C, C), 0)
    c = lax.broadcasted_iota(jnp.int32, (C, C), 1)
    incl = r >= c
    diff = gcb[:, :, :C] - gr
    dm = jnp.where(incl, jnp.exp(jnp.where(incl, diff, 0.0)), 0.0)
    ds = jnp.where(r > c, dm, 0.0)
    kb = kn * beta
    knb = kn.astype(bf16)
    L = _bmm_nt(kb.astype(bf16), knb) * ds
    eg = jnp.exp(gcb)
    a = _bmm_nt(qn.astype(bf16), knb) * dm
    gl = gcb[:, C - 1:C, :]
    ekd = jnp.exp(gl - gcb)
    return dict(dm=dm, ds=ds, kb=kb, L=L, eg=eg, rhs_u=v * beta, rhs_w=kb * eg, a=a,
                qd=qn * eg, kd=kn * ekd, ekd=ekd, cd=jnp.exp(gl))


def _dn_specs(S):
    def col(off):
        return pl.BlockSpec((1, S, HEAD), lambda b, h, off=off: (b, 0, off + h))

    def cw(off):
        return pl.BlockSpec((4, HEAD), lambda b, h, off=off: (0, off + h))

    per_head = pl.BlockSpec((1, 1, S, LANES), lambda b, h: (b, h, 0, 0))
    rowform = pl.BlockSpec((1, 1, S // CHUNK, 8, CHUNK), lambda b, h: (b, h, 0, 0, 0))
    gain = pl.BlockSpec((1, HEAD), lambda b, h: (0, 0))
    ins = [col(4), col(8), col(12), col(16), cw(0), cw(4), cw(8), per_head, per_head, rowform, gain]
    return ins, per_head


def _dn_act(x_ref, cw_ref, row, normalise, out_scale=1.0):
    _, act = _conv_silu(x_ref[0].astype(f32), cw_ref[...], row)
    if normalise:
        act = act * (lax.rsqrt(jnp.sum(act * act, axis=-1, keepdims=True) + EPS) * out_scale)
    return act


def _dn_group(qn_s, kn_s, v_s, gcb_ref, bb_ref, gcr_ref, g, ng):
    C = CHUNK
    rows = pl.ds(pl.multiple_of(g * (ng * C), ng * C), ng * C)
    ch = pl.ds(g * ng, ng)
    sh = (ng, C, HEAD)
    qn, kn, v = qn_s[rows, :].reshape(sh), kn_s[rows, :].reshape(sh), v_s[rows, :].reshape(sh)
    gcb3, beta = gcb_ref[0, 0, rows, :].reshape(sh), bb_ref[0, 0, rows, :].reshape(sh)
    gr = gcr_ref[0, 0, ch][:, 0:1, :]
    it = _dn_intra(qn, kn, v, gcb3, beta, gr)
    it.update(qn=qn, kn=kn, v=v, gcb=gcb3, beta=beta, gr=gr)
    return rows, ch, it


def _dn_fwd(proj, conv_w, gcb, betab, gcr, gain, name, carry=None):
    B, S, _ = proj.shape
    n, C = S // CHUNK, CHUNK
    ng = min(8, n)
    ins, per_head = _dn_specs(S)

    def body(q_ref, k_ref, v_ref, z_ref, cq_ref, ck_ref, cv_ref, gcb_ref, bb_ref, gcr_ref, gain_ref,
             y_ref, st_ref, vn_ref, qn_s, kn_s, v_s, u_s, w_s, qd_s, a_s, cd_s, g_s, h_s):
        row = lax.broadcasted_iota(jnp.int32, (S, HEAD), 0)
        qn_s[...] = _dn_act(q_ref, cq_ref, row, True, HEAD ** -0.5)
        kn_s[...] = _dn_act(k_ref, ck_ref, row, True)
        v_s[...] = _dn_act(v_ref, cv_ref, row, False)

        def group(g, carry):
            _, ch, it = _dn_group(qn_s, kn_s, v_s, gcb_ref, bb_ref, gcr_ref, g, ng)
            t = _tri_inv(it["L"])
            u = _bmm(t, it["rhs_u"], MID)
            wb = _bmm(t, it["rhs_w"], MID).astype(bf16)
            kdb = it["kd"].astype(bf16)
            u_s[ch] = u
            w_s[ch] = wb
            qd_s[ch] = it["qd"].astype(bf16)
            a_s[ch] = it["a"].astype(bf16)
            cd_s[ch] = it["cd"]
            g_s[ch] = _bmm_tn(kdb, wb).astype(bf16)
            h_s[ch] = _bmm_tn(kdb, u.astype(bf16))
            return carry

        lax.fori_loop(0, n // ng, group, 0)

        def step(i, st):
            sb = st.astype(bf16)
            st_ref[0, 0, i] = sb
            return st * cd_s[i] - _dot(g_s[i], sb) + h_s[i]

        lax.fori_loop(0, n, step, jnp.zeros((HEAD, HEAD), f32))

        def group_out(g, carry):
            rows = pl.ds(pl.multiple_of(g * (ng * C), ng * C), ng * C)
            ch = pl.ds(g * ng, ng)
            sn = st_ref[0, 0, ch]
            vn = (u_s[ch] - _bmm(w_s[ch], sn)).astype(bf16)
            vn_ref[0, 0, rows, :] = vn.reshape(ng * C, HEAD)
            o = (_bmm(qd_s[ch], sn) + _bmm(a_s[ch], vn)).reshape(ng * C, HEAD)
            zz = z_ref[0, rows, :].astype(f32)
            rr = lax.rsqrt(jnp.mean(o * o, axis=-1, keepdims=True) + EPS)
            y_ref[0, rows, :] = (o * rr * gain_ref[...] * (zz * _sigmoid(zz))).astype(bf16)
            return carry

        lax.fori_loop(0, n // ng, group_out, 0)

    seq = pltpu.VMEM((S, HEAD), f32)
    body, cin, cout, cshapes, cscratch = _carried(body, len(ins), 3, (B, N_HEADS), carry)
    outs = pl.pallas_call(
        body, name=name, grid=(B, N_HEADS), in_specs=ins + cin,
        out_specs=[pl.BlockSpec((1, S, HEAD), lambda b, h: (b, 0, h)),
                   pl.BlockSpec((1, 1, n, HEAD, HEAD), lambda b, h: (b, h, 0, 0, 0)), per_head] + cout,
        out_shape=[jax.ShapeDtypeStruct((B, S, BRANCH), bf16), jax.ShapeDtypeStruct((B, N_HEADS, n, HEAD, HEAD), bf16),
                   jax.ShapeDtypeStruct((B, N_HEADS, S, HEAD), bf16)] + cshapes,
        scratch_shapes=[seq, seq, seq, pltpu.VMEM((n, C, HEAD), f32), pltpu.VMEM((n, C, HEAD), bf16),
                        pltpu.VMEM((n, C, HEAD), bf16), pltpu.VMEM((n, C, C), bf16), pltpu.VMEM((n, 1, HEAD), f32),
                        pltpu.VMEM((n, HEAD, HEAD), bf16), pltpu.VMEM((n, HEAD, HEAD), f32)] + cscratch,
        compiler_params=_cp(("arbitrary", "arbitrary") if carry else ("parallel", "parallel")),
    )(proj, proj, proj, proj, conv_w, conv_w, conv_w, gcb, betab, gcr, gain.reshape(1, HEAD), *(carry.ins if carry else []))
    return outs[:3], outs[3:]


def _rowsum(x):
    return jnp.sum(x, axis=-1, keepdims=True)


def _dn_bwd(proj, dy, conv_w, gcb, betab, gcr, gain, states, vnew, name, carry=None):
    B, S, _ = proj.shape
    n, C = S // CHUNK, CHUNK
    ng = min(8, n)
    ins, per_head = _dn_specs(S)
    ins = ins + [pl.BlockSpec((1, S, HEAD), lambda b, h: (b, 0, h)),
                 pl.BlockSpec((1, 1, n, HEAD, HEAD), lambda b, h: (b, h, 0, 0, 0)), per_head]

    def body(q_ref, k_ref, v_ref, z_ref, cq_ref, ck_ref, cv_ref, gcb_ref, bb_ref, gcr_ref, gain_ref, dy_ref, st_ref, vn_ref,
             dq_ref, dk_ref, dv_ref, dz_ref, dg_ref, dbeta_ref, dconv_ref, dgain_ref,
             qn_s, kn_s, v_s, t_s, u_s, w_s, cd_s, do_s, dsp_s, dvn_s, g_s, q_s):
        row = lax.broadcasted_iota(jnp.int32, (S, HEAD), 0)
        qn_s[...] = _dn_act(q_ref, cq_ref, row, True, HEAD ** -0.5)
        kn_s[...] = _dn_act(k_ref, ck_ref, row, True)
        v_s[...] = _dn_act(v_ref, cv_ref, row, False)
        dgain_ref[...] = jnp.zeros_like(dgain_ref)
        gv = gain_ref[...]

        def group_fwd(g, carry):
            rows, ch, it = _dn_group(qn_s, kn_s, v_s, gcb_ref, bb_ref, gcr_ref, g, ng)
            t = _tri_inv(it["L"])
            ub = _bmm(t, it["rhs_u"], MID).astype(bf16)
            wb = _bmm(t, it["rhs_w"], MID).astype(bf16)
            ab, qdb = it["a"].astype(bf16), it["qd"].astype(bf16)
            vn = vn_ref[0, 0, rows, :].reshape(ng, C, HEAD)
            o = (_bmm(qdb, st_ref[0, 0, ch]) + _bmm(ab, vn)).reshape(ng * C, HEAD)
            zz = z_ref[0, rows, :].astype(f32)
            dyv = dy_ref[0, rows, :].astype(f32)
            rr = lax.rsqrt(jnp.mean(o * o, axis=-1, keepdims=True) + EPS)
            on = o * rr
            sz = _sigmoid(zz)
            dz_ref[0, rows, :] = (dyv * on * gv * (sz * (1.0 + zz * (1.0 - sz)))).astype(bf16)
            dnrm = dyv * (zz * sz)
            dgain_ref[0, 0] += _fold8(dnrm * on)
            doh = dnrm * gv
            do = rr * (doh - on * jnp.mean(doh * on, axis=-1, keepdims=True))
            dob = do.reshape(ng, C, HEAD).astype(bf16)
            atdo = _bmm_tn(ab, dob)
            t_s[ch] = t
            u_s[ch] = ub
            w_s[ch] = wb
            cd_s[ch] = it["cd"]
            do_s[ch] = dob
            dvn_s[ch] = atdo
            g_s[ch] = _bmm_tn(it["kd"].astype(bf16), wb).astype(bf16)
            q_s[ch] = _bmm_tn(qdb, dob) - _bmm_tn(wb, atdo.astype(bf16))
            return carry

        lax.fori_loop(0, n // ng, group_fwd, 0)

        def step(t, dsp):
            i = n - 1 - t
            dspb = dsp.astype(bf16)
            dsp_s[i] = dspb
            return dsp * cd_s[i] - _dot(g_s[i], dspb, TN) + q_s[i]

        lax.fori_loop(0, n, step, jnp.zeros((HEAD, HEAD), f32))

        r = lax.broadcasted_iota(jnp.int32, (C, C), 0)
        c = lax.broadcasted_iota(jnp.int32, (C, C), 1)
        upper = r <= c

        def group_bwd(g, carry):
            rows, ch, it = _dn_group(qn_s, kn_s, v_s, gcb_ref, bb_ref, gcr_ref, g, ng)
            sh = (ng, C, HEAD)
            qn, kn, v, beta, gcb3, gr = it["qn"], it["kn"], it["v"], it["beta"], it["gcb"], it["gr"]
            sn = st_ref[0, 0, ch]
            vn = vn_ref[0, 0, rows, :].reshape(sh)
            dsp, dob = dsp_s[ch], do_s[ch]
            dvn = dvn_s[ch] + _bmm(it["kd"].astype(bf16), dsp)
            t, ub, wb = t_s[ch], u_s[ch], w_s[ch]
            dvnb = dvn.astype(bf16)
            da = _bmm_nt(dob, vn)
            dat = _bmm_nt(vn, dob)
            dqd = _bmm_nt(dob, sn)
            dkd = _bmm_nt(vn, dsp)
            dcd = jnp.sum(jnp.sum(dsp.astype(f32) * sn.astype(f32), axis=2, keepdims=True), axis=1, keepdims=True)
            dw = -_bmm_nt(dvnb, sn)
            ru = _bmm_tn(t, dvn, MID)
            rw = _bmm_tn(t, dw, MID)
            rub, rwb = ru.astype(bf16), rw.astype(bf16)
            dL = -(_bmm_nt(rub, ub) + _bmm_nt(rwb, wb))
            dLt = -(_bmm_nt(ub, rub) + _bmm_nt(wb, rwb))
            knb, qnb, kbb = kn.astype(bf16), qn.astype(bf16), it["kb"].astype(bf16)
            dmt = jnp.where(upper, jnp.exp(jnp.where(upper, gr - gcb3[:, :, :C], 0.0)), 0.0)
            Lt = _bmm_nt(knb, kbb) * jnp.where(r < c, dmt, 0.0)
            At = _bmm_nt(knb, qnb) * dmt
            dgc = _rowsum(dL * it["L"] + da * it["a"]) - _rowsum(dLt * Lt + dat * At)
            dkk = (dL * it["ds"]).astype(bf16)
            dqk = (da * it["dm"]).astype(bf16)
            dkb = _bmm(dkk, knb) + rw * it["eg"]
            dkn = _bmm_tn(dkk, kbb) + _bmm_tn(dqk, qnb) + dkd * it["ekd"] + dkb * beta
            dqn = _bmm(dqk, knb) + dqd * it["eg"]
            tkd = _rowsum(dkd * it["kd"])
            dgl = jnp.sum(tkd, axis=1, keepdims=True) + dcd * it["cd"][:, :, 0:1]
            dgc = dgc + _rowsum(dqd * it["qd"]) - tkd + _rowsum(rw * it["rhs_w"])
            dbeta = _rowsum(ru * v) + _rowsum(dkb * kn)
            rowc = lax.broadcasted_iota(jnp.int32, (ng, C, 1), 1)
            dgc = dgc + jnp.where(rowc == C - 1, dgl, 0.0)
            rev = jnp.broadcast_to(upper.astype(f32), (ng, C, C))
            dg_ref[0, 0, rows, :] = _bmm(rev, jnp.broadcast_to(dgc, sh), HI).reshape(ng * C, LANES).astype(bf16)
            dbeta_ref[0, 0, rows, :] = jnp.broadcast_to(dbeta, sh).reshape(ng * C, LANES).astype(bf16)
            qn_s[rows, :] = dqn.reshape(ng * C, HEAD)
            kn_s[rows, :] = dkn.reshape(ng * C, HEAD)
            v_s[rows, :] = (ru * beta).reshape(ng * C, HEAD)
            return carry

        lax.fori_loop(0, n // ng, group_bwd, 0)

        def conv_back(x_ref, cw_ref, grad_s, out_ref, slot, normalise, out_scale):
            x = x_ref[0].astype(f32)
            w = cw_ref[...]
            pre, act = _conv_silu(x, w, row)
            dact = grad_s[...]
            if normalise:
                rn = lax.rsqrt(jnp.sum(act * act, axis=-1, keepdims=True) + EPS)
                unit = act * rn
                dact = (out_scale * rn) * (dact - unit * _rowsum(dact * unit))
            s = _sigmoid(pre)
            dc = dact * (s * (1.0 + pre * (1.0 - s)))
            out_ref[0] = (w[3:4] * dc + w[2:3] * _shift_up(dc, 1, row) + w[1:2] * _shift_up(dc, 2, row)
                          + w[0:1] * _shift_up(dc, 3, row)).astype(bf16)
            for tap in range(4):
                xs = x if tap == 3 else _shift_down(x, 3 - tap, row)
                dconv_ref[0, slot, tap:tap + 1, :] = jnp.sum(dc * xs, axis=0, keepdims=True)

        conv_back(q_ref, cq_ref, qn_s, dq_ref, 0, True, HEAD ** -0.5)
        conv_back(k_ref, ck_ref, kn_s, dk_ref, 1, True, 1.0)
        conv_back(v_ref, cv_ref, v_s, dv_ref, 2, False, 1.0)

    o512 = jax.ShapeDtypeStruct((B, S, BRANCH), bf16)
    s512 = pl.BlockSpec((1, S, HEAD), lambda b, h: (b, 0, h))
    ph = jax.ShapeDtypeStruct((B, N_HEADS, S, LANES), bf16)
    seq = pltpu.VMEM((S, HEAD), f32)
    cb = pltpu.VMEM((n, C, HEAD), bf16)
    body, cin, cout, cshapes, cscratch = _carried(body, len(ins), 8, (B, N_HEADS), carry)
    outs = pl.pallas_call(
        body, name=name, grid=(B, N_HEADS), in_specs=ins + cin,
        out_specs=[s512, s512, s512, s512, per_head, per_head, pl.BlockSpec((1, 3, 4, HEAD), lambda b, h: (b, 0, 0, h)),
                   pl.BlockSpec((1, 1, 8, HEAD), lambda b, h: (b, h, 0, 0))] + cout,
        out_shape=[o512, o512, o512, o512, ph, ph, jax.ShapeDtypeStruct((B, 3, 4, BRANCH), f32),
                   jax.ShapeDtypeStruct((B, N_HEADS, 8, HEAD), f32)] + cshapes,
        scratch_shapes=[seq, seq, seq, pltpu.VMEM((n, C, C), f32), cb, cb, pltpu.VMEM((n, 1, HEAD), f32), cb,
                        pltpu.VMEM((n, HEAD, HEAD), bf16), pltpu.VMEM((n, C, HEAD), f32), pltpu.VMEM((n, HEAD, HEAD), bf16),
                        pltpu.VMEM((n, HEAD, HEAD), f32)] + cscratch,
        compiler_params=_cp(("arbitrary", "arbitrary") if carry else ("parallel", "parallel")),
    )(proj, proj, proj, proj, conv_w, conv_w, conv_w, gcb, betab, gcr, gain.reshape(1, HEAD), dy, states, vnew,
      *(carry.ins if carry else []))
    return outs[:8], outs[8:]


def _dn_post(ab, par, dg, dbeta, name):
    B, S, _ = ab.shape
    ts = _tile(S, (512, 256, 128))

    def body(ab_ref, par_ref, dg_ref, db_ref, dab_ref, acc_ref):
        x = ab_ref[0]
        lane = lax.broadcasted_iota(jnp.int32, x.shape, 1)
        dgs = jnp.zeros_like(x)
        dbs = jnp.zeros_like(x)
        for h in range(N_HEADS):
            dgs = jnp.where(lane == h, dg_ref[0, h], dgs)
            dbs = jnp.where(lane == N_HEADS + h, db_ref[0, h], dbs)
        nega = -jnp.exp(par_ref[0:1, :])
        pre = x + par_ref[1:2, :]
        da = dgs * nega * _sigmoid(pre)
        beta = _sigmoid(x)
        dab_ref[0] = (da + dbs * beta * (1.0 - beta)).astype(bf16)

        @pl.when((pl.program_id(0) == 0) & (pl.program_id(1) == 0))
        def _():
            acc_ref[...] = jnp.zeros_like(acc_ref)

        acc_ref[0] += _fold8(dgs * nega * _softplus(pre))
        acc_ref[1] += _fold8(da)

    return pl.pallas_call(
        body, name=name, grid=(B, S // ts),
        in_specs=[pl.BlockSpec((1, ts, LANES), lambda b, i: (b, i, 0)), pl.BlockSpec((8, LANES), lambda b, i: (0, 0)),
                  pl.BlockSpec((1, N_HEADS, ts, LANES), lambda b, i: (b, 0, i, 0)),
                  pl.BlockSpec((1, N_HEADS, ts, LANES), lambda b, i: (b, 0, i, 0))],
        out_specs=[pl.BlockSpec((1, ts, LANES), lambda b, i: (b, i, 0)), pl.BlockSpec((2, 8, LANES), lambda b, i: (0, 0, 0))],
        out_shape=[jax.ShapeDtypeStruct((B, S, LANES), bf16), jax.ShapeDtypeStruct((2, 8, LANES), f32)],
        compiler_params=_cp(("arbitrary", "arbitrary")),
    )(ab, par, dg, dbeta)


def _merge_specs(T, D, tm, tn, order):
    nj = D // tn

    def ij(f):
        return (lambda i, j: f(i, j)) if order == "ij" else (lambda j, i: f(i, j))

    ys = [pl.BlockSpec((tm, BRANCH), ij(lambda i, j: (i, 0))) for _ in range(3)]
    wb = pl.BlockSpec((3, BRANCH, tn), ij(lambda i, j: (0, 0, j)))
    gl = [pl.BlockSpec((tm, tn), ij(lambda i, j, k=k: (i, k * nj + j))) for k in range(3)]
    bg = [pl.BlockSpec((1, tn), ij(lambda i, j, k=k: (0, k * nj + j))) for k in range(3)]
    return ys, wb, gl, bg


def _merge_fwd(ys, wb, gl, b_gate, name):
    T, D = ys[0].shape[0], wb.shape[2]
    tm, tn = _tile(T, (512, 256, 128)), _tile(D, (512, 256, 128))
    sy, swb, sgl, sbg = _merge_specs(T, D, tm, tn, "ij")

    def body(y0, y1, y2, wb_ref, g0, g1, g2, b0, b1, b2, o_ref):
        acc = None
        for k, (y, g, b) in enumerate(((y0, g0, b0), (y1, g1, b1), (y2, g2, b2))):
            term = _sigmoid(g[...].astype(f32) + b[...]) * _dot(y[...], wb_ref[k])
            acc = term if acc is None else acc + term
        o_ref[...] = acc.astype(bf16)

    bg = b_gate.reshape(1, 3 * D)
    return pl.pallas_call(
        body, name=name, grid=(T // tm, D // tn), in_specs=sy + [swb] + sgl + sbg,
        out_specs=pl.BlockSpec((tm, tn), lambda i, j: (i, j)), out_shape=jax.ShapeDtypeStruct((T, D), bf16),
        compiler_params=_cp(("parallel", "parallel")),
    )(*ys, wb, gl, gl, gl, bg, bg, bg)


def _merge_bwd(dm, ys, wb, gl, b_gate, name):
    T, D = dm.shape
    tm, tn = _tile(T, (512, 256, 128)), _tile(D, (512, 256, 128))
    sy, swb, sgl, sbg = _merge_specs(T, D, tm, tn, "ji")

    def body(dm_ref, y0, y1, y2, wb_ref, g0, g1, g2, b0, b1, b2, dgl_ref, dbd_ref, dbg_ref):
        dmv = dm_ref[...].astype(f32)

        @pl.when(pl.program_id(1) == 0)
        def _():
            dbg_ref[...] = jnp.zeros_like(dbg_ref)

        for k, (y, g, b) in enumerate(((y0, g0, b0), (y1, g1, b1), (y2, g2, b2))):
            s = _sigmoid(g[...].astype(f32) + b[...])
            dg = dmv * _dot(y[...], wb_ref[k]) * s * (1.0 - s)
            dgl_ref[k] = dg.astype(bf16)
            dbd_ref[k] = (dmv * s).astype(bf16)
            dbg_ref[k] += _fold8(dg)

    bg = b_gate.reshape(1, 3 * D)
    o3 = jax.ShapeDtypeStruct((3, T, D), bf16)
    s3 = pl.BlockSpec((3, tm, tn), lambda j, i: (0, i, j))
    return pl.pallas_call(
        body, name=name, grid=(D // tn, T // tm),
        in_specs=[pl.BlockSpec((tm, tn), lambda j, i: (i, j))] + sy + [swb] + sgl + sbg,
        out_specs=[s3, s3, pl.BlockSpec((3, 8, tn), lambda j, i: (0, 0, j))],
        out_shape=[o3, o3, jax.ShapeDtypeStruct((3, 8, D), f32)],
        compiler_params=_cp(("parallel", "arbitrary")),
    )(dm, *ys, wb, gl, gl, gl, bg, bg, bg)


def _ffn_fwd(x, g, w, f, tag, carry=None):
    T, D = x.shape
    FP = w["wgu"].shape[2]
    h = _rms_fwd(x, g, f"rms_{tag}")
    (a, b, hm), landed = _ffn_up(h, w["wgu"], f, f"ffn_up_{tag}", carry)
    down = [("nn", hm, w["m"], {"K": FP, "ka": k, "bsel": k, "kb": f}) for k in range(N_CHIPS)]
    y = _mm(down, T, D, f32, f"ffn_down_{tag}", res=x, scale=0.5)
    return y, (x, h, a, b), landed


def _ffn_bwd(dy, saved, g, w, f, tag):
    x, h, a, b = saved
    T, D = x.shape
    FP = w["wgu"].shape[2]
    F4 = N_CHIPS * FP
    da, db, hm = _ffn_bwd_mid(dy, w["m"], a, b, f, f"ffn_mid_bwd_{tag}")
    dwd = _mm([("tn", hm, dy, {})], F4, D, bf16, f"ffn_dwd_{tag}", scale=0.5)
    dwg = _mm([("tn", h, da, {})], D, F4, bf16, f"ffn_dwg_{tag}", out_chip=True)
    dwu = _mm([("tn", h, db, {})], D, F4, bf16, f"ffn_dwu_{tag}", out_chip=True)
    tn = _tile(D, (512, 256, 128))
    pairs = [("nt", t, w["wgu"], {"K": FP, "ka": k, "bsel": k, "noff": (2 * f + u) * (D // tn)})
             for u, t in enumerate((da, db)) for k in range(N_CHIPS)]
    dh = _mm(pairs, T, D, f32, f"ffn_dh_{tag}", tn=tn)
    dx, dg8 = _rms_bwd(dh, x, g, dy, f"rms_bwd_{tag}")
    return dx, dict(norm=jnp.sum(dg8, axis=0), wgu=[dwg, dwu], wd=dwd.reshape(N_CHIPS, FP, D))


def _layer_fwd(x, w, B, tag, carry=(None, None, None)):
    T, D = x.shape
    S = T // B
    x1, sv0, landed_ffn = _ffn_fwd(x, w["ffn_norm"][0], w, 0, f"pre_{tag}", carry[2])
    h = _rms_fwd(x1, w["mix_norm"], f"rms_mix_{tag}")
    pm = _mm([("nn", h, w["w_main"], {})], T, 5 * BRANCH, bf16, f"proj_main_{tag}")
    ab = _mm([("nn", h, w["w_ab"], {})], T, LANES, f32, f"proj_ab_{tag}", tn=LANES)
    sb = _mm([("nn", h, w["w_sb"], {})], T, 3 * BRANCH, bf16, f"proj_sb_{tag}")
    gl = _mm([("nn", h, w["w_gates"], {})], T, 3 * D, bf16, f"proj_gates_{tag}")
    pm3, ab3, sb3 = pm.reshape(B, S, -1), ab.reshape(B, S, LANES), sb.reshape(B, S, -1)
    y_pool = _pool_fwd(pm3, w["pool_w"], w["pool_scale"], f"pool_{tag}")
    par = _dn_params(w["dn_A_log"], w["dn_dt_bias"])
    gcb, betab, gcr = _dn_prep(ab3, par, f"dn_prep_{tag}")
    (y_dn, states, vnew), landed_dn = _dn_fwd(pm3, w["dn_conv"], gcb, betab, gcr, w["dn_out_norm"], f"dn_fwd_{tag}", carry[0])
    (y_sb, tot), landed_sb = _sb_fwd(sb3, f"sb_fwd_{tag}", carry[1])
    ys = [y_pool.reshape(T, BRANCH), y_dn.reshape(T, BRANCH), y_sb.reshape(T, BRANCH)]
    merged = _merge_fwd(ys, w["w_branch"], gl, w["b_gate"], f"merge_{tag}")
    dc = D // N_CHIPS
    out_pairs = [("nn", merged, w["m"], {"K": dc, "ka": k, "bsel": k, "kb": _w_out_block(w)}) for k in range(N_CHIPS)]
    x2 = _mm(out_pairs, T, D, f32, f"mix_out_{tag}", res=x1)
    x3, sv1, _ = _ffn_fwd(x2, w["ffn_norm"][1], w, 1, f"post_{tag}")
    saved = dict(sv0=sv0, sv1=sv1, x1=x1, h=h, pm3=pm3, ab3=ab3, sb3=sb3, gl=gl, par=par, gcb=gcb, betab=betab, gcr=gcr,
                 states=states, vnew=vnew, tot=tot, ys=ys, merged=merged)
    return x3, saved, (landed_dn, landed_sb, landed_ffn)


def _layer_bwd(dx3, w, sv, B, tag, carry=(None, None)):
    T, D = dx3.shape
    S = T // B
    dx2, g1 = _ffn_bwd(dx3, sv["sv1"], w["ffn_norm"][1], w, 1, f"post_{tag}")
    dc = D // N_CHIPS
    dmerged = _mm([("nt", dx2, w["m"], {"b_by_chip": True, "noff": _w_out_block(w)})], T, D, bf16, f"mix_dmerged_{tag}", tn=dc)
    dw_out = _mm([("tn", sv["merged"], dx2, {})], D, D, bf16, f"mix_dwout_{tag}")
    ys = sv["ys"]
    dgl, dbd, dbg8 = _merge_bwd(dmerged, ys, w["w_branch"], sv["gl"], w["b_gate"], f"merge_bwd_{tag}")
    dys, dwb = [], []
    for k in range(3):
        dys.append(_mm([("nt", dbd, w["w_branch"], {"asel": k, "bsel": k})], T, BRANCH, bf16, f"branch_dy{k}_{tag}"))
        dwb.append(_mm([("tn", ys[k], dbd, {"bsel": k})], BRANCH, D, bf16, f"branch_dw{k}_{tag}"))
    pm3, ab3, sb3 = sv["pm3"], sv["ab3"], sv["sb3"]
    du, dpool_w, dsc8 = _pool_bwd(pm3, dys[0].reshape(B, S, BRANCH), w["pool_w"], w["pool_scale"], f"pool_bwd_{tag}")
    (dq, dk, dv, dz, dg, dbeta, dconv, dgain), landed = _dn_bwd(
        pm3, dys[1].reshape(B, S, BRANCH), w["dn_conv"], sv["gcb"], sv["betab"], sv["gcr"], w["dn_out_norm"],
        sv["states"], sv["vnew"], f"dn_bwd_{tag}", carry[0])
    dab, dn_acc = _dn_post(ab3, sv["par"], dg, dbeta, f"dn_post_{tag}")
    (dsq, dsk, dsv), landed_sb = _sb_bwd(sb3, dys[2].reshape(B, S, BRANCH), sv["tot"], f"sb_bwd_{tag}", carry[1])
    main_parts = [t.reshape(T, BRANCH) for t in (du, dq, dk, dv, dz)]
    sb_parts = [t.reshape(T, BRANCH) for t in (dsq, dsk, dsv)]
    dab2 = dab.reshape(T, LANES)
    pairs = [("nt", t, w["w_main"], {"K": BRANCH, "kb": k}) for k, t in enumerate(main_parts)]
    pairs.append(("nt", dab2, w["w_ab"], {}))
    pairs += [("nt", t, w["w_sb"], {"K": BRANCH, "kb": k}) for k, t in enumerate(sb_parts)]
    pairs += [("nt", dgl, w["w_gates"], {"K": D, "kb": k, "asel": k}) for k in range(3)]
    dh = _mm(pairs, T, D, f32, f"mix_dh_{tag}")
    h = sv["h"]
    dw_cols = [_mm([("tn", h, t, {})], D, BRANCH, bf16, f"dwin_main{k}_{tag}") for k, t in enumerate(main_parts)]
    dw_cols.append(_mm([("tn", h, dab2, {})], D, LANES, bf16, f"dwin_ab_{tag}", tn=LANES)[:, :2 * N_HEADS])
    dw_cols += [_mm([("tn", h, t, {})], D, BRANCH, bf16, f"dwin_sb{k}_{tag}") for k, t in enumerate(sb_parts)]
    dw_cols += [_mm([("tn", h, dgl, {"bsel": k})], D, D, bf16, f"dwin_gate{k}_{tag}") for k in range(3)]
    dx1, dmix8 = _rms_bwd(dh, sv["x1"], w["mix_norm"], dx2, f"rms_mix_bwd_{tag}")
    dx0, g0 = _ffn_bwd(dx1, sv["sv0"], w["ffn_norm"][0], w, 0, f"pre_{tag}")
    dwb = jnp.stack(dwb).reshape(3, BRANCH, N_CHIPS, dc).transpose(2, 0, 1, 3).reshape(N_CHIPS, -1, D)
    dw_in = jnp.concatenate(dw_cols, axis=1)
    pc = dw_in.shape[1] // N_CHIPS
    grads = dict(
        ffn_norm=jnp.stack([g0["norm"], g1["norm"]]),
        A=jnp.concatenate(g0["wgu"] + g1["wgu"], axis=1),
        M=jnp.concatenate([g0["wd"], g1["wd"], dw_out.reshape(N_CHIPS, dc, D), dwb], axis=1),
        C=jnp.stack([dw_in[:, k * pc:(k + 1) * pc] for k in range(N_CHIPS)]),
        mix_norm=jnp.sum(dmix8, axis=0), b_gate=jnp.sum(dbg8, axis=1).reshape(3 * D),
        pool_w=dpool_w, pool_scale=jnp.sum(dsc8, axis=0), dn_conv=jnp.sum(dconv, axis=0).transpose(1, 0, 2).reshape(4, 3 * BRANCH),
        dn_A_log=jnp.sum(dn_acc[0], axis=0)[:N_HEADS], dn_dt_bias=jnp.sum(dn_acc[1], axis=0)[:N_HEADS],
        dn_out_norm=jnp.sum(dgain, axis=(0, 1, 2)))
    return dx0, grads, (landed, landed_sb)


def _local_step(x, target, layers, final_norm, B):
    saved = []
    for l, w in enumerate(layers):
        x, sv, _ = _layer_fwd(x, w, B, f"l{l}")
        saved.append(sv)
    dx, dfn8, ls8 = _final_loss(x, target, final_norm)
    grads = [None] * len(layers)
    for l in reversed(range(len(layers))):
        dx, grads[l], _ = _layer_bwd(dx, layers[l], saved[l], B, f"l{l}")
    return jnp.sum(ls8), dx, grads, jnp.sum(dfn8, axis=0)


def _adamw(w, g, m, v, name):
    shape = w.shape
    cols = shape[-1]
    rows = math.prod(shape[:-1]) if len(shape) > 1 else 1
    w2, g2, m2, v2 = (t.reshape(rows, cols) for t in (w, g, m, v))
    block_elems = 256 * 1024
    tr = rows if rows * cols <= block_elems else _tile(rows, [t for t in (512, 256, 128, 64, 32, 16, 8) if t * cols <= block_elems])

    def body(w_ref, g_ref, m_ref, v_ref, d_ref, mo_ref, vo_ref):
        gv = g_ref[...]
        mn = ADAM_B1 * m_ref[...] + (1.0 - ADAM_B1) * gv
        vn = ADAM_B2 * v_ref[...] + (1.0 - ADAM_B2) * (gv * gv)
        m_hat = mn / (1.0 - ADAM_B1 ** ADAM_STEP)
        v_hat = vn / (1.0 - ADAM_B2 ** ADAM_STEP)
        d_ref[...] = -ADAM_LR * (m_hat / (jnp.sqrt(v_hat) + ADAM_EPS) + ADAM_WD * w_ref[...])
        mo_ref[...] = mn
        vo_ref[...] = vn

    spec = pl.BlockSpec((tr, cols), lambda i: (i, 0))
    o = jax.ShapeDtypeStruct((rows, cols), f32)
    d, mo, vo = pl.pallas_call(
        body, name=name, grid=(rows // tr,), in_specs=[spec] * 4, out_specs=[spec] * 3, out_shape=[o, o, o],
        compiler_params=_cp(("parallel",)),
    )(w2, g2, m2, v2)
    return d.reshape(shape), mo.reshape(shape), vo.reshape(shape)


MESH = pl.DeviceIdType.MESH
_ANY = pl.BlockSpec(memory_space=pl.ANY)


def _place():
    x, y, c = lax.axis_index("x"), lax.axis_index("y"), lax.axis_index("c")
    return x, y, c, [(1 - x, y), (x, 1 - y), (1 - x, 1 - y)]


def _chip_index():
    return 2 * lax.axis_index("x") + lax.axis_index("y")


def _half(c, rh):
    return pl.ds(c * rh, rh)


def _remote(src, dst, ssem, rsem, to):
    return pltpu.make_async_remote_copy(src_ref=src, dst_ref=dst, send_sem=ssem, recv_sem=rsem, device_id=to,
                                        device_id_type=MESH)


class _ChipExchange:
    def __init__(self, kind, ins):
        self.kind, self.ins = kind, list(ins)
        self.n = len(self.ins)
        self.out_shapes = [jax.ShapeDtypeStruct((N_CHIPS,) + a.shape[-2:], a.dtype) for a in self.ins]
        self.scratch = [pltpu.SemaphoreType.DMA((self.n, 3)), pltpu.SemaphoreType.DMA((self.n, 3))]

    def _copies(self, in_refs, out_refs, ssem, rsem):
        x, y, c, chips = _place()
        me = 2 * x + y
        pairs = []
        for o, (src, dst) in enumerate(zip(in_refs, out_refs)):
            for k, (px, py) in enumerate(chips):
                peer = 2 * px + py
                if self.kind == "gather":
                    rows = _half(c, src.shape[0] // 2)
                    out, land = (src.at[rows], dst.at[me, rows]), dst.at[peer, rows]
                else:
                    out, land = (src.at[peer], dst.at[me]), dst.at[peer]
                pairs.append((_remote(out[0], out[1], ssem.at[o, k], rsem.at[o, k], (px, py, c)),
                              _remote(land, land, ssem.at[o, k], rsem.at[o, k], (px, py, c))))
        return pairs

    def start(self, in_refs, out_refs, ssem, rsem):
        for mine, _ in self._copies(in_refs, out_refs, ssem, rsem):
            mine.start()

    def wait(self, in_refs, out_refs, ssem, rsem):
        for mine, landing in self._copies(in_refs, out_refs, ssem, rsem):
            landing.wait_recv()
            mine.wait_send()

    def standalone(self, name):
        n = self.n

        def body(*refs):
            ins, outs, (ssem, rsem) = refs[:n], refs[n:2 * n], refs[2 * n:]
            self.start(ins, outs, ssem, rsem)
            self.wait(ins, outs, ssem, rsem)

        return pl.pallas_call(body, name=name, in_specs=[_ANY] * n, out_specs=[_ANY] * n, out_shape=self.out_shapes,
                              scratch_shapes=self.scratch)(*self.ins)


def _carried(body, n_in, n_out, grid, carry):
    if carry is None:
        return body, [], [], [], []
    n = carry.n

    def wrapped(*refs):
        ins, cin = refs[:n_in], refs[n_in:n_in + n]
        outs, cout = refs[n_in + n:n_in + n + n_out], refs[n_in + n + n_out:n_in + 2 * n + n_out]
        scratch, (ssem, rsem) = refs[n_in + 2 * n + n_out:-2], refs[-2:]
        step = pl.program_id(0) * grid[1] + pl.program_id(1)

        @pl.when(step == 0)
        def _():
            carry.start(cin, cout, ssem, rsem)

        body(*ins, *outs, *scratch)

        @pl.when(step == grid[0] * grid[1] - 1)
        def _():
            carry.wait(cin, cout, ssem, rsem)

    return wrapped, [_ANY] * n, [_ANY] * n, carry.out_shapes, carry.scratch


def _gather_finish(shards, landed, name):
    n = len(shards)

    def body(*refs):
        outs, (ssem, rsem) = refs[n:2 * n], refs[2 * n:]
        x, y, c, chips = _place()
        started = []
        for o, buf in enumerate(outs):
            rh = buf.shape[1] // 2
            for k, (px, py) in enumerate(chips):
                block = buf.at[2 * px + py, _half(c, rh)]
                cp = _remote(block, block, ssem.at[o, k], rsem.at[o, k], (x, y, 1 - c))
                cp.start()
                started.append(cp)
        for o, buf in enumerate(outs):
            rh = buf.shape[1] // 2
            for k, (px, py) in enumerate(chips):
                block = buf.at[2 * px + py, _half(1 - c, rh)]
                _remote(block, block, ssem.at[o, k], rsem.at[o, k], (x, y, 1 - c)).wait_recv()
        for cp in started:
            cp.wait_send()

    outs = pl.pallas_call(
        body, name=name, in_specs=[_ANY] * n, out_specs=[_ANY] * n,
        out_shape=[jax.ShapeDtypeStruct(a.shape, a.dtype) for a in landed], input_output_aliases={i: i for i in range(n)},
        scratch_shapes=[pltpu.SemaphoreType.DMA((n, 3)), pltpu.SemaphoreType.DMA((n, 3))],
    )(*landed)
    me = _chip_index()
    return [lax.dynamic_update_slice(g, s[None], (me, 0, 0)) for g, s in zip(outs, shards)]


def _gather_chips(shards, name):
    landed = _ChipExchange("gather", shards).standalone(f"{name}_ici")
    return _gather_finish(shards, landed, f"{name}_pass")


def _pair_swap_halves(ps, name):
    n = len(ps)

    def body(*refs):
        ins, outs, (ssem, rsem) = refs[:n], refs[n:2 * n], refs[2 * n:]
        x, y, c, _ = _place()
        cps = [_remote(p.at[:, _half(1 - c, p.shape[1] // 2)], out, ssem.at[o], rsem.at[o], (x, y, 1 - c))
               for o, (p, out) in enumerate(zip(ins, outs))]
        for cp in cps:
            cp.start()
        for cp in cps:
            cp.wait()

    return pl.pallas_call(
        body, name=name, in_specs=[_ANY] * n, out_specs=[_ANY] * n,
        out_shape=[jax.ShapeDtypeStruct((p.shape[0], p.shape[1] // 2, p.shape[2]), p.dtype) for p in ps],
        scratch_shapes=[pltpu.SemaphoreType.DMA((n,)), pltpu.SemaphoreType.DMA((n,))],
    )(*ps)


def _pair_add(p, got, name):
    n, R, W = p.shape
    rh = R // 2
    tr = _row_tile(rh, W, 2 * 1024 * 1024)
    nb = rh // tr

    def body(c_ref, p_ref, g_ref, o_ref):
        o_ref[...] = (p_ref[...].astype(f32) + g_ref[...].astype(f32)).astype(o_ref.dtype)

    return pl.pallas_call(
        body, name=name,
        grid_spec=pltpu.PrefetchScalarGridSpec(
            num_scalar_prefetch=1, grid=(n, nb),
            in_specs=[pl.BlockSpec((1, tr, W), lambda j, i, c_ref: (j, c_ref[0] * nb + i, 0)),
                      pl.BlockSpec((1, tr, W), lambda j, i, c_ref: (j, i, 0))],
            out_specs=pl.BlockSpec((1, tr, W), lambda j, i, c_ref: (j, i, 0))),
        out_shape=jax.ShapeDtypeStruct((n, rh, W), p.dtype), compiler_params=_cp(("parallel", "parallel")),
    )(lax.axis_index("c").astype(jnp.int32).reshape(1), p, got)


def _own_slot_filled(landed, ps):
    me = _chip_index()
    return [lax.dynamic_update_slice(out, lax.dynamic_slice(p, (me, 0, 0), (1,) + p.shape[1:]), (me, 0, 0))
            for out, p in zip(landed, ps)]


def _sum_slots(r4, name):
    n, R, W = r4.shape
    tr = _row_tile(R, W, 1024 * 1024)

    def body(r_ref, o_ref):
        acc = r_ref[0].astype(f32)
        for k in range(1, n):
            acc = acc + r_ref[k].astype(f32)
        o_ref[...] = acc

    return pl.pallas_call(
        body, name=name, grid=(R // tr,), in_specs=[pl.BlockSpec((n, tr, W), lambda i: (0, i, 0))],
        out_specs=pl.BlockSpec((tr, W), lambda i: (i, 0)), out_shape=jax.ShapeDtypeStruct((R, W), f32),
        compiler_params=_cp(("parallel",)),
    )(r4)


def _pair_share(ss, name):
    n = len(ss)

    def body(*refs):
        ins, outs, (ssem, rsem) = refs[:n], refs[n:2 * n], refs[2 * n:]
        x, y, c, _ = _place()
        cps = [_remote(s, out.at[c], ssem.at[o], rsem.at[o], (x, y, 1 - c)) for o, (s, out) in enumerate(zip(ins, outs))]
        for cp in cps:
            cp.start()
        for o, (s, out) in enumerate(zip(ins, outs)):
            _remote(s, out.at[1 - c], ssem.at[o], rsem.at[o], (x, y, 1 - c)).wait_recv()
        for cp in cps:
            cp.wait_send()

    outs = pl.pallas_call(
        body, name=name, in_specs=[_ANY] * n, out_specs=[_ANY] * n,
        out_shape=[jax.ShapeDtypeStruct((2,) + s.shape, s.dtype) for s in ss],
        scratch_shapes=[pltpu.SemaphoreType.DMA((n,)), pltpu.SemaphoreType.DMA((n,))],
    )(*ss)
    c = lax.axis_index("c")
    return [lax.dynamic_update_slice(out, s[None], (c, 0, 0)).reshape(2 * s.shape[0], s.shape[1]) for out, s in zip(outs, ss)]


def _reduce_begin(ps, tag):
    got = _pair_swap_halves(ps, f"rs_pair_swap_{tag}")
    return [_pair_add(p, g, f"rs_pair_add{o}_{tag}") for o, (p, g) in enumerate(zip(ps, got))]


def _reduce_finish(pair_sums, landed, tag):
    r4 = _own_slot_filled(landed, pair_sums)
    return _pair_share([_sum_slots(r, f"rs_sum{o}_{tag}") for o, r in enumerate(r4)], f"rs_share_{tag}")


def _reduce_to_chips(ps, tag):
    pair_sums = _reduce_begin(ps, tag)
    return _reduce_finish(pair_sums, _ChipExchange("scatter", pair_sums).standalone(f"rs_scatter_{tag}"), tag)


def _pad_rows(a, mult):
    r = (-a.shape[-2]) % mult
    return a if r == 0 else jnp.pad(a, [(0, 0)] * (a.ndim - 2) + [(0, r), (0, 0)])


def _hidden_pad(fs):
    return -(-fs // LANES) * LANES


def _w_out_block(w):
    return 2 * w["wgu"].shape[2] // (w["m"].shape[2] // N_CHIPS)


def _pack_wgu(wg, wu):
    fs = wg.shape[-1]
    t = jnp.stack([wg, wu], axis=1).astype(bf16)
    return jnp.pad(t, ((0, 0), (0, 0), (0, 0), (0, _hidden_pad(fs) - fs))).reshape(-1, _hidden_pad(fs))


def _pack_m(wd, w_out, w_branch):
    fs, D = wd.shape[1:]
    wdp = jnp.pad(wd.astype(bf16), ((0, 0), (0, _hidden_pad(fs) - fs), (0, 0))).reshape(-1, D)
    return jnp.concatenate([wdp, w_out.astype(bf16), w_branch.astype(bf16).reshape(-1, D)], axis=0)


def _w_in_cols(cg, lo, hi):
    p = cg.shape[2]
    parts = [cg[k][:, max(lo, k * p) - k * p:min(hi, (k + 1) * p) - k * p] for k in range(N_CHIPS)
             if max(lo, k * p) < min(hi, (k + 1) * p)]
    return parts[0] if len(parts) == 1 else jnp.concatenate(parts, axis=1)


def _layer_weights(ag, mg, cg, small):
    D = mg.shape[2]
    dc = D // N_CHIPS
    fp2 = 2 * ag.shape[2]
    wb = mg[:, fp2 + dc:].reshape(N_CHIPS, 3, BRANCH, dc).transpose(1, 2, 0, 3).reshape(3, BRANCH, D)
    c0, c1, c2 = 5 * BRANCH, 5 * BRANCH + 2 * N_HEADS, 8 * BRANCH + 2 * N_HEADS
    w = dict(small)
    w.update(wgu=ag, m=mg, w_branch=wb, w_main=_w_in_cols(cg, 0, c0),
             w_ab=jnp.pad(_w_in_cols(cg, c0, c1), ((0, 0), (0, LANES - 2 * N_HEADS))), w_sb=_w_in_cols(cg, c1, c2),
             w_gates=_w_in_cols(cg, c2, N_CHIPS * cg.shape[2]))
    return w


def _small_pack(pieces):
    flat, offs, r = [], [], 0
    for a in pieces:
        v = a.reshape(-1)
        pad = (-v.shape[0]) % PACK_W
        flat.append(jnp.pad(v, (0, pad)) if pad else v)
        offs.append(r)
        r += (v.shape[0] + pad) // PACK_W
    pack = jnp.concatenate(flat).reshape(r, PACK_W)
    return _pad_rows(pack, 16), offs


def _small_unpack(pack, offs, shapes):
    out = []
    for o, s in zip(offs, shapes):
        n = math.prod(s)
        rows = -(-n // PACK_W)
        out.append(pack[o:o + rows].reshape(-1)[:n].reshape(s))
    return out


SMALL_SHARDED = ("ffn_norm", "dn_conv")
SMALL_REPLICATED = ("mix_norm", "b_gate", "pool_w", "pool_scale", "dn_A_log", "dn_dt_bias", "dn_out_norm")


def kernel(x, ffn_norm, ffn_w_gate, ffn_w_up, ffn_w_down, mix_norm, w_in, b_gate, pool_w, pool_scale, dn_conv, dn_A_log, dn_dt_bias, dn_out_norm, w_branch, w_out, final_norm, loss_target, m_ffn_norm, m_ffn_w_gate, m_ffn_w_up, m_ffn_w_down, m_mix_norm, m_w_in, m_b_gate, m_pool_w, m_pool_scale, m_dn_conv, m_dn_A_log, m_dn_dt_bias, m_dn_out_norm, m_w_branch, m_w_out, m_final_norm, v_ffn_norm, v_ffn_w_gate, v_ffn_w_up, v_ffn_w_down, v_mix_norm, v_w_in, v_b_gate, v_pool_w, v_pool_scale, v_dn_conv, v_dn_A_log, v_dn_dt_bias, v_dn_out_norm, v_w_branch, v_w_out, v_final_norm):
    names = ("ffn_norm", "ffn_w_gate", "ffn_w_up", "ffn_w_down", "mix_norm", "w_in", "b_gate", "pool_w", "pool_scale", "dn_conv",
             "dn_A_log", "dn_dt_bias", "dn_out_norm", "w_branch", "w_out", "final_norm")
    wts = dict(zip(names, (ffn_norm, ffn_w_gate, ffn_w_up, ffn_w_down, mix_norm, w_in, b_gate, pool_w, pool_scale, dn_conv,
                           dn_A_log, dn_dt_bias, dn_out_norm, w_branch, w_out, final_norm)))
    ms = dict(zip(names, (m_ffn_norm, m_ffn_w_gate, m_ffn_w_up, m_ffn_w_down, m_mix_norm, m_w_in, m_b_gate, m_pool_w, m_pool_scale,
                          m_dn_conv, m_dn_A_log, m_dn_dt_bias, m_dn_out_norm, m_w_branch, m_w_out, m_final_norm)))
    vs = dict(zip(names, (v_ffn_norm, v_ffn_w_gate, v_ffn_w_up, v_ffn_w_down, v_mix_norm, v_w_in, v_b_gate, v_pool_w, v_pool_scale,
                          v_dn_conv, v_dn_A_log, v_dn_dt_bias, v_dn_out_norm, v_w_branch, v_w_out, v_final_norm)))
    B, S, D = x.shape
    T = B * S
    L = ffn_w_gate.shape[0]
    fs, ds_, cs = ffn_w_gate.shape[3], D // N_CHIPS, dn_conv.shape[2]
    fp = _hidden_pad(fs)
    chip = _chip_index()

    shards = [[_pack_wgu(ffn_w_gate[l], ffn_w_up[l]), _pack_m(ffn_w_down[l], w_out[l], w_branch[l]), w_in[l].astype(bf16)]
              for l in range(L)]
    small_shard, soffs = _small_pack([ffn_norm, dn_conv])
    small_g = _gather_chips([small_shard], "gather_small")[0]

    def chip_major(i, a):
        rows = -(-a.size // PACK_W)
        return small_g[:, soffs[i]:soffs[i] + rows].reshape(N_CHIPS, -1)[:, :a.size].reshape((N_CHIPS,) + a.shape)

    fn_full = jnp.moveaxis(chip_major(0, ffn_norm), 0, 2).reshape(L, 2, D)
    conv_full = jnp.moveaxis(chip_major(1, dn_conv), 0, 2).reshape(L, dn_conv.shape[1], N_CHIPS * cs)

    def small_params(l):
        return dict(ffn_norm=fn_full[l], mix_norm=mix_norm[l], b_gate=b_gate[l], pool_w=pool_w[l], pool_scale=pool_scale[l],
                    dn_conv=conv_full[l], dn_A_log=dn_A_log[l], dn_dt_bias=dn_dt_bias[l], dn_out_norm=dn_out_norm[l])

    xt = x.reshape(T, D)
    gathered = _gather_chips(shards[0], "gather_l0")
    layers, saved = [], []
    for l in range(L):
        w = _layer_weights(*gathered, small_params(l))
        more = l + 1 < L
        carry = [_ChipExchange("gather", [s]) if more else None for s in shards[min(l + 1, L - 1)]]
        xt, sv, (landed_dn, landed_sb, landed_ffn) = _layer_fwd(xt, w, B, f"l{l}", (carry[0], carry[1], carry[2]))
        layers.append(w)
        saved.append(sv)
        if more:
            gathered = _gather_finish(shards[l + 1], [landed_dn[0], landed_sb[0], landed_ffn[0]], f"gather_pass_l{l + 1}")
    dx, dfn8, ls8 = _final_loss(xt, loss_target.reshape(T, D), final_norm)
    loss = lax.psum(jnp.sum(ls8), ("x", "y", "c")) * (0.5 / D)

    grads, red, pending = [None] * L, [None] * L, None
    for l in reversed(range(L)):
        carry = (None, None) if pending is None else (_ChipExchange("scatter", pending[:2]), _ChipExchange("scatter", pending[2:]))
        dx, grads[l], (landed_dn, landed_sb) = _layer_bwd(dx, layers[l], saved[l], B, f"l{l}", carry)
        if pending is not None:
            red[l + 1] = _reduce_finish(pending, list(landed_dn) + list(landed_sb), f"l{l + 1}")
        pending = _reduce_begin([grads[l]["A"], grads[l]["M"], grads[l]["C"]], f"l{l}")
    red[0] = _reduce_finish(pending, _ChipExchange("scatter", pending).standalone("rs_scatter_l0"), "l0")

    small_names = SMALL_SHARDED + SMALL_REPLICATED
    pieces = [g[k] for g in grads for k in small_names] + [jnp.sum(dfn8, axis=0)]
    spack, offs = _small_pack(pieces)
    sred = _reduce_to_chips([jnp.broadcast_to(spack[None], (N_CHIPS,) + spack.shape)], "small")[0]
    small_red = _small_unpack(sred, offs, [p.shape for p in pieces])

    gw = {k: [] for k in names if k != "final_norm"}
    for l in range(L):
        ga, gm, gc = red[l]
        gu = ga.reshape(2, 2, D, fp)[..., :fs]
        gw["ffn_w_gate"].append(gu[:, 0])
        gw["ffn_w_up"].append(gu[:, 1])
        gw["ffn_w_down"].append(gm[:2 * fp].reshape(2, fp, D)[:, :fs])
        gw["w_out"].append(gm[2 * fp:2 * fp + ds_])
        gw["w_branch"].append(gm[2 * fp + ds_:].reshape(3, BRANCH, ds_))
        gw["w_in"].append(gc)
        sm = dict(zip(small_names, small_red[l * len(small_names):(l + 1) * len(small_names)]))
        gw["ffn_norm"].append(lax.dynamic_slice_in_dim(sm["ffn_norm"], chip * ds_, ds_, axis=1))
        gw["dn_conv"].append(lax.dynamic_slice_in_dim(sm["dn_conv"], chip * cs, cs, axis=1))
        for k in SMALL_REPLICATED:
            gw[k].append(sm[k])
    gw = {k: jnp.stack(v) for k, v in gw.items()}
    gw["final_norm"] = small_red[-1]

    deltas, new_m, new_v = [], [], []
    for k in names:
        d, mo, vo = _adamw(wts[k], gw[k], ms[k], vs[k], f"adamw_{k}")
        deltas.append(d)
        new_m.append(mo)
        new_v.append(vo)
    return (loss, dx.reshape(B, S, D), *[gw[k] for k in names], *deltas, *new_m, *new_v)
```

```python
import functools
import math

import jax
import jax.numpy as jnp
from jax import lax
from jax.experimental import pallas as pl
from jax.experimental.pallas import tpu as pltpu

f32 = jnp.float32
bf16 = jnp.bfloat16
HI = lax.Precision.HIGHEST
MID = lax.Precision.HIGH

EPS = 1e-6
HEAD = 128
N_HEADS = 4
BRANCH = 512
CHUNK = 64
SB_BLOCK = 128
SB_QUERIES = 512
POOL_WINDOWS = (2, 4, 8, 16)
N_CHIPS = 4
LANES = 128
PACK_W = 1024
ADAM_LR, ADAM_B1, ADAM_B2, ADAM_EPS, ADAM_WD, ADAM_STEP = 0.001, 0.9, 0.999, 1e-08, 0.01, 10
VMEM_LIMIT = 56 * 1024 * 1024
MM_VMEM_BUDGET = 36 * 1024 * 1024

NN = (((1,), (0,)), ((), ()))
NT = (((1,), (1,)), ((), ()))
TN = (((0,), (0,)), ((), ()))


def _cp(sem=None):
    return pltpu.CompilerParams(dimension_semantics=sem, vmem_limit_bytes=VMEM_LIMIT)


def _tile(n, prefs):
    for p in prefs:
        if n % p == 0:
            return p
    return n


def _row_tile(rows, width, max_elems):
    for d in range(rows, 0, -1):
        if rows % d == 0 and (d % 16 == 0 or d == rows) and d * width <= max_elems:
            return d
    return rows


def _dot(a, b, dn=NN):
    return lax.dot_general(a, b, dn, preferred_element_type=f32)


def _sigmoid(x):
    return 0.5 * jnp.tanh(0.5 * x) + 0.5


def _softplus(x):
    return jnp.maximum(x, 0.0) + jnp.log1p(jnp.exp(-jnp.abs(x)))


def _mm(pairs, M, N, out_dtype, name, tm=None, tn=None, res=None, scale=1.0, out_chip=False):
    tn = N // N_CHIPS if out_chip else (tn or _tile(N, (512, 256, 128)))

    def vmem_bytes(rows):
        total = rows * tn * (2 * jnp.dtype(out_dtype).itemsize + (8 if res is not None else 0) + 8)
        for form, a, b, o in pairs:
            K = a.shape[-2] if form == "tn" else (o.get("K") or a.shape[-1])
            total += 2 * K * (rows * a.dtype.itemsize + tn * b.dtype.itemsize)
        return total

    tm = tm or next((t for t in (1024, 512, 256, 128) if M % t == 0 and vmem_bytes(t) <= MM_VMEM_BUDGET), M)
    specs, arrs, dns = [], [], []

    def lead(sel, shape, imap):
        if sel is None:
            return pl.BlockSpec(shape, imap)
        return pl.BlockSpec((None,) + shape, lambda i, j, sel=sel, imap=imap: (sel,) + imap(i, j))

    for form, a, b, o in pairs:
        ka, kb, moff, noff = o.get("ka", 0), o.get("kb", 0), o.get("moff", 0), o.get("noff", 0)
        asel, bsel = o.get("asel"), o.get("bsel")
        if form == "nn":
            K = o.get("K") or a.shape[-1]
            sa = lead(asel, (tm, K), lambda i, j, ka=ka, moff=moff: (i + moff, ka))
            sb = lead(bsel, (K, tn), lambda i, j, kb=kb, noff=noff: (kb, j + noff))
            dn = NN
        elif form == "nt":
            K = o.get("K") or a.shape[-1]
            sa = lead(asel, (tm, K), lambda i, j, ka=ka, moff=moff: (i + moff, ka))
            if o.get("b_by_chip"):
                sb = pl.BlockSpec((None, tn, K), lambda i, j, kb=kb, noff=noff: (j, noff, kb))
            else:
                sb = lead(bsel, (tn, K), lambda i, j, kb=kb, noff=noff: (j + noff, kb))
            dn = NT
        else:
            K = a.shape[-2]
            sa = lead(asel, (K, tm), lambda i, j, moff=moff: (0, i + moff))
            sb = lead(bsel, (K, tn), lambda i, j, noff=noff: (0, j + noff))
            dn = TN
        specs += [sa, sb]
        arrs += [a, b]
        dns.append(dn)
    if res is not None:
        specs.append(pl.BlockSpec((tm, tn), lambda i, j: (i, j)))
        arrs.append(res)
    n = len(pairs)

    def body(*refs):
        o_ref = refs[-1]
        acc = None
        for p in range(n):
            d = _dot(refs[2 * p][...].astype(bf16), refs[2 * p + 1][...].astype(bf16), dns[p])
            acc = d if acc is None else acc + d
        if scale != 1.0:
            acc = acc * scale
        if res is not None:
            acc = acc + refs[2 * n][...]
        o_ref[...] = acc.astype(o_ref.dtype)

    if out_chip:
        out_spec = pl.BlockSpec((None, tm, tn), lambda i, j: (j, i, 0))
        out_shape = jax.ShapeDtypeStruct((N_CHIPS, M, tn), out_dtype)
    else:
        out_spec = pl.BlockSpec((tm, tn), lambda i, j: (i, j))
        out_shape = jax.ShapeDtypeStruct((M, N), out_dtype)
    return pl.pallas_call(
        body, name=name, grid=(M // tm, N // tn), in_specs=specs, out_specs=out_spec, out_shape=out_shape,
        compiler_params=_cp(("parallel", "parallel")),
    )(*arrs)


def _rms_fwd(x, g, name):
    T, D = x.shape
    tm = _tile(T, (512, 256, 128))

    def body(x_ref, g_ref, h_ref):
        xv = x_ref[...]
        r = lax.rsqrt(jnp.mean(xv * xv, axis=-1, keepdims=True) + EPS)
        h_ref[...] = (xv * r * g_ref[...]).astype(bf16)

    return pl.pallas_call(
        body, name=name, grid=(T // tm,),
        in_specs=[pl.BlockSpec((tm, D), lambda i: (i, 0)), pl.BlockSpec((1, D), lambda i: (0, 0))],
        out_specs=pl.BlockSpec((tm, D), lambda i: (i, 0)),
        out_shape=jax.ShapeDtypeStruct((T, D), bf16), compiler_params=_cp(("parallel",)),
    )(x, g.reshape(1, D))


def _fold8(v):
    r, d = v.shape
    return jnp.sum(v.reshape(r // 8, 8, d), axis=0)


def _rms_bwd(dh, x, g, dres, name):
    T, D = x.shape
    tm = _tile(T, (512, 256, 128))

    def body(dh_ref, x_ref, g_ref, dres_ref, dx_ref, dxb_ref, dg_ref):
        xv = x_ref[...]
        r = lax.rsqrt(jnp.mean(xv * xv, axis=-1, keepdims=True) + EPS)
        xh = xv * r
        dhv = dh_ref[...]
        dxh = dhv * g_ref[...]
        dx = dres_ref[...] + r * (dxh - xh * jnp.mean(dxh * xh, axis=-1, keepdims=True))
        dx_ref[...] = dx
        dxb_ref[...] = dx.astype(bf16)

        @pl.when(pl.program_id(0) == 0)
        def _():
            dg_ref[...] = jnp.zeros_like(dg_ref)

        dg_ref[...] += _fold8(dhv * xh)

    row = pl.BlockSpec((tm, D), lambda i: (i, 0))
    dx, dxb, dg8 = pl.pallas_call(
        body, name=name, grid=(T // tm,), in_specs=[row, row, pl.BlockSpec((1, D), lambda i: (0, 0)), row],
        out_specs=[row, row, pl.BlockSpec((8, D), lambda i: (0, 0))],
        out_shape=[jax.ShapeDtypeStruct((T, D), f32), jax.ShapeDtypeStruct((T, D), bf16), jax.ShapeDtypeStruct((8, D), f32)],
        compiler_params=_cp(("arbitrary",)),
    )(dh, x, g.reshape(1, D), dres)
    return (dx, dxb), dg8


def _final_loss(x, target, g):
    T, D = x.shape
    tm = _tile(T, (512, 256, 128))

    def body(x_ref, t_ref, g_ref, dx_ref, dxb_ref, dg_ref, ls_ref):
        xv = x_ref[...]
        r = lax.rsqrt(jnp.mean(xv * xv, axis=-1, keepdims=True) + EPS)
        xh = xv * r
        gv = g_ref[...]
        e = xh * gv - t_ref[...]
        dy = e * (1.0 / D)
        dxh = dy * gv
        dx = r * (dxh - xh * jnp.mean(dxh * xh, axis=-1, keepdims=True))
        dx_ref[...] = dx
        dxb_ref[...] = dx.astype(bf16)

        @pl.when(pl.program_id(0) == 0)
        def _():
            dg_ref[...] = jnp.zeros_like(dg_ref)
            ls_ref[...] = jnp.zeros_like(ls_ref)

        dg_ref[...] += _fold8(dy * xh)
        ls_ref[...] += _fold8(e * e)

    row = pl.BlockSpec((tm, D), lambda i: (i, 0))
    acc = pl.BlockSpec((8, D), lambda i: (0, 0))
    dx, dxb, dg8, ls8 = pl.pallas_call(
        body, name="final_loss", grid=(T // tm,), in_specs=[row, row, pl.BlockSpec((1, D), lambda i: (0, 0))],
        out_specs=[row, row, acc, acc],
        out_shape=[jax.ShapeDtypeStruct((T, D), f32), jax.ShapeDtypeStruct((T, D), bf16), jax.ShapeDtypeStruct((8, D), f32),
                   jax.ShapeDtypeStruct((8, D), f32)],
        compiler_params=_cp(("arbitrary",)),
    )(x, target, g.reshape(1, D))
    return (dx, dxb), dg8, ls8


def _ffn_up(h, wgu, f, name, carry=None):
    T, D = h.shape
    FP = wgu.shape[2]
    tm = _tile(T, (1024, 512, 256, 128))

    def body(h_ref, wg_ref, wu_ref, a_ref, b_ref, m_ref):
        hv = h_ref[...]
        a = _dot(hv, wg_ref[...])
        b = _dot(hv, wu_ref[...])
        a_ref[...] = a.astype(bf16)
        b_ref[...] = b.astype(bf16)
        m_ref[...] = (a * _sigmoid(a) * b).astype(bf16)

    o = jax.ShapeDtypeStruct((T, N_CHIPS * FP), bf16)
    ospec = pl.BlockSpec((tm, FP), lambda j, i: (i, j))
    grid = (N_CHIPS, T // tm)
    body, cin, cout, cshapes, cscratch = _carried(body, 3, 3, grid, carry)
    outs = pl.pallas_call(
        body, name=name, grid=grid,
        in_specs=[pl.BlockSpec((tm, D), lambda j, i: (i, 0)), pl.BlockSpec((None, D, FP), lambda j, i: (j, 2 * f, 0)),
                  pl.BlockSpec((None, D, FP), lambda j, i: (j, 2 * f + 1, 0))] + cin,
        out_specs=[ospec, ospec, ospec] + cout, out_shape=[o, o, o] + cshapes, scratch_shapes=cscratch,
        compiler_params=_cp(("arbitrary", "arbitrary") if carry else ("parallel", "parallel")),
    )(h, wgu, wgu, *(carry.ins if carry else []))
    return outs[:3], outs[3:]


def _ffn_bwd_mid(dx, m, a, b, f, name):
    T, D = dx.shape
    FP = a.shape[1] // N_CHIPS
    tm = _tile(T, (1024, 512, 256, 128))

    def body(dx_ref, wd_ref, a_ref, b_ref, da_ref, db_ref, m_ref):
        dm = 0.5 * _dot(dx_ref[...].astype(bf16), wd_ref[...], NT)
        av = a_ref[...].astype(f32)
        bv = b_ref[...].astype(f32)
        s = _sigmoid(av)
        silu = av * s
        da_ref[...] = (dm * bv * (s * (1.0 + av * (1.0 - s)))).astype(bf16)
        db_ref[...] = (dm * silu).astype(bf16)
        m_ref[...] = (silu * bv).astype(bf16)

    o = jax.ShapeDtypeStruct((T, N_CHIPS * FP), bf16)
    ospec = pl.BlockSpec((tm, FP), lambda j, i: (i, j))
    return pl.pallas_call(
        body, name=name, grid=(N_CHIPS, T // tm),
        in_specs=[pl.BlockSpec((tm, D), lambda j, i: (i, 0)), pl.BlockSpec((None, FP, D), lambda j, i: (j, f, 0)), ospec, ospec],
        out_specs=[ospec, ospec, ospec], out_shape=[o, o, o], compiler_params=_cp(("parallel", "parallel")),
    )(dx, m, a, b)


def _shift_down(x, k, row):
    return jnp.where(row < k, 0.0, pltpu.roll(x, k, 0))


def _shift_up(x, k, row):
    s = x.shape[0]
    return jnp.where(row >= s - k, 0.0, pltpu.roll(x, s - k, 0))


def _window_sum(x, g, row, shift):
    s2 = x + shift(x, 1, row)
    s4 = s2 + shift(s2, 2, row)
    s8 = s4 + shift(s4, 4, row)
    s16 = s8 + shift(s8, 8, row)
    return jnp.where(g == 0, s2, jnp.where(g == 1, s4, jnp.where(g == 2, s8, s16)))


def _pool_fwd(proj, pool_w, scale, name):
    B, S = proj.shape[0], proj.shape[1]
    G = len(POOL_WINDOWS)

    def body(u_ref, w_ref, sc_ref, y_ref):
        g = pl.program_id(1)
        u = u_ref[0].astype(f32)
        row = lax.broadcasted_iota(jnp.int32, u.shape, 0)
        win = _window_sum(u, g, row, _shift_down)
        cnt = jnp.minimum(row + 1, jnp.left_shift(2, g)).astype(f32)
        pooled = win / cnt - u
        mixed = _dot(pooled.astype(bf16), w_ref[0].astype(bf16))
        y_ref[0] = (mixed * sc_ref[...]).astype(bf16)

    return pl.pallas_call(
        body, name=name, grid=(B, G),
        in_specs=[pl.BlockSpec((1, S, HEAD), lambda b, g: (b, 0, g)), pl.BlockSpec((1, HEAD, HEAD), lambda b, g: (g, 0, 0)),
                  pl.BlockSpec((1, HEAD), lambda b, g: (0, g))],
        out_specs=pl.BlockSpec((1, S, HEAD), lambda b, g: (b, 0, g)),
        out_shape=jax.ShapeDtypeStruct((B, S, BRANCH), bf16), compiler_params=_cp(("parallel", "parallel")),
    )(proj, pool_w, scale.reshape(1, BRANCH))


def _pool_bwd(proj, dy, pool_w, scale, name):
    B, S = proj.shape[0], proj.shape[1]
    G = len(POOL_WINDOWS)

    def body(u_ref, dy_ref, w_ref, sc_ref, du_ref, dw_ref, dsc_ref):
        g = pl.program_id(0)
        u = u_ref[0].astype(f32)
        row = lax.broadcasted_iota(jnp.int32, u.shape, 0)
        cnt = jnp.minimum(row + 1, jnp.left_shift(2, g)).astype(f32)
        pooled = _window_sum(u, g, row, _shift_down) / cnt - u
        wv = w_ref[0].astype(bf16)
        mixed = _dot(pooled.astype(bf16), wv)
        dyv = dy_ref[0].astype(f32)
        dmix = (dyv * sc_ref[...]).astype(bf16)
        dpool = _dot(dmix, wv, NT)
        du_ref[0] = (_window_sum(dpool / cnt, g, row, _shift_up) - dpool).astype(bf16)

        @pl.when(pl.program_id(1) == 0)
        def _():
            dw_ref[...] = jnp.zeros_like(dw_ref)
            dsc_ref[...] = jnp.zeros_like(dsc_ref)

        dw_ref[0] += _dot(pooled.astype(bf16), dmix, TN)
        dsc_ref[...] += _fold8(dyv * mixed)

    return pl.pallas_call(
        body, name=name, grid=(G, B),
        in_specs=[pl.BlockSpec((1, S, HEAD), lambda g, b: (b, 0, g)), pl.BlockSpec((1, S, HEAD), lambda g, b: (b, 0, g)),
                  pl.BlockSpec((1, HEAD, HEAD), lambda g, b: (g, 0, 0)), pl.BlockSpec((1, HEAD), lambda g, b: (0, g))],
        out_specs=[pl.BlockSpec((1, S, HEAD), lambda g, b: (b, 0, g)), pl.BlockSpec((1, HEAD, HEAD), lambda g, b: (g, 0, 0)),
                   pl.BlockSpec((8, HEAD), lambda g, b: (0, g))],
        out_shape=[jax.ShapeDtypeStruct((B, S, BRANCH), bf16), jax.ShapeDtypeStruct((G, HEAD, HEAD), f32),
                   jax.ShapeDtypeStruct((8, BRANCH), f32)],
        compiler_params=_cp(("parallel", "arbitrary")),
    )(proj, dy, pool_w, scale.reshape(1, BRANCH))


def _split_dot(x, u):
    hi = x.astype(bf16)
    lo = (x - hi.astype(f32)).astype(bf16)
    return _dot(hi, u) + _dot(lo, u)


def _sb_fwd(sbqkv, name, carry=None):
    B, S, _ = sbqkv.shape
    KB = SB_BLOCK
    TQ = _tile(S, (SB_QUERIES, KB))
    ns = TQ // KB
    nq = S // TQ
    scale = HEAD ** -0.5

    def body(q_ref, k_ref, v_ref, o_ref, tot_ref, run_s, acc_s):
        r = lax.broadcasted_iota(jnp.int32, (KB, KB), 0)
        c = lax.broadcasted_iota(jnp.int32, (KB, KB), 1)
        causal = c < r
        after = (r > c).astype(bf16)

        def sub(qa, krows, masked, run, acc):
            z = _dot(qa, k_ref[0, krows, :], NT) * scale
            sp = jnp.maximum(z, 0.0) + jnp.log(1.0 + jnp.exp(-jnp.abs(z)))
            ln = -sp
            if masked:
                ln = jnp.where(causal, ln, 0.0)
            w = jnp.exp(z - sp + _split_dot(ln, after) + run)
            if masked:
                w = jnp.where(causal, w, 0.0)
            acc = acc + _dot(w.astype(bf16), v_ref[0, krows, :])
            return run + jnp.sum(ln, axis=1, keepdims=True), acc

        def qloop(i, carry):
            base = pl.multiple_of(i * TQ, TQ)
            qi = q_ref[0, pl.ds(base, TQ), :]
            for a in range(ns):
                qa = qi[a * KB:(a + 1) * KB]
                run, acc = jnp.zeros((KB, LANES), f32), jnp.zeros((KB, HEAD), f32)
                for s in range(a, -1, -1):
                    run, acc = sub(qa, pl.ds(base + s * KB, KB), s == a, run, acc)
                run_s[a * KB:(a + 1) * KB, :] = run
                acc_s[a * KB:(a + 1) * KB, :] = acc

            def group(t, cr):
                g0 = pl.multiple_of((i - 1 - t) * TQ, TQ)
                for s in range(ns - 1, -1, -1):
                    cr = sub(qi, pl.ds(g0 + s * KB, KB), False, *cr)
                return cr

            run, acc = lax.fori_loop(0, i, group, (run_s[...], acc_s[...]))
            o_ref[0, pl.ds(base, TQ), :] = acc.astype(bf16)
            tot_ref[0, 0, pl.ds(base, TQ), :] = run
            return carry

        lax.fori_loop(0, nq, qloop, 0)

    def spec(off):
        return pl.BlockSpec((1, S, HEAD), lambda b, h, off=off: (b, 0, off + h))

    body, cin, cout, cshapes, cscratch = _carried(body, 3, 2, (B, N_HEADS), carry)
    outs = pl.pallas_call(
        body, name=name, grid=(B, N_HEADS), in_specs=[spec(0), spec(N_HEADS), spec(2 * N_HEADS)] + cin,
        out_specs=[pl.BlockSpec((1, S, HEAD), lambda b, h: (b, 0, h)),
                   pl.BlockSpec((1, 1, S, LANES), lambda b, h: (b, h, 0, 0))] + cout,
        out_shape=[jax.ShapeDtypeStruct((B, S, BRANCH), bf16), jax.ShapeDtypeStruct((B, N_HEADS, S, LANES), f32)] + cshapes,
        scratch_shapes=[pltpu.VMEM((TQ, LANES), f32), pltpu.VMEM((TQ, HEAD), f32)] + cscratch,
        compiler_params=_cp(("arbitrary", "arbitrary") if carry else ("parallel", "parallel")),
    )(sbqkv, sbqkv, sbqkv, *(carry.ins if carry else []))
    return outs[:2], outs[2:]


def _sb_bwd(sbqkv, do, tot, name, carry=None):
    B, S, _ = sbqkv.shape
    KB = SB_BLOCK
    TQ = _tile(S, (SB_QUERIES, KB))
    ns = TQ // KB
    nq = S // TQ
    scale = HEAD ** -0.5

    def body(q_ref, k_ref, v_ref, do_ref, tot_ref, dq_ref, dk_ref, dv_ref, dk_acc, dv_acc):
        r = lax.broadcasted_iota(jnp.int32, (KB, KB), 0)
        c = lax.broadcasted_iota(jnp.int32, (KB, KB), 1)
        wide_r = lax.broadcasted_iota(jnp.int32, (TQ, KB), 0)
        wide_c = lax.broadcasted_iota(jnp.int32, (TQ, KB), 1)
        after = (r > c).astype(bf16)
        before = (r < c).astype(bf16)
        dk_acc[...] = jnp.zeros_like(dk_acc)
        dv_acc[...] = jnp.zeros_like(dv_acc)

        def sub(qi, doi, total, rows, mask, cl, cp, dq):
            kj = k_ref[0, rows, :]
            vj = v_ref[0, rows, :]
            z = _dot(qi, kj, NT) * scale
            sp = jnp.maximum(z, 0.0) + jnp.log(1.0 + jnp.exp(-jnp.abs(z)))
            ln = -sp
            if mask is not None:
                ln = jnp.where(mask, ln, 0.0)
            bs = jnp.sum(ln, axis=1, keepdims=True)
            w = jnp.exp(z - sp + _split_dot(ln, after) + (total - cl - bs))
            if mask is not None:
                w = jnp.where(mask, w, 0.0)
            p = _dot(doi, vj, NT) * w
            qsum = cp + _split_dot(p, before)
            sig = jnp.exp(z - sp)
            dz = (p - (p + qsum) * sig) * scale
            if mask is not None:
                dz = jnp.where(mask, dz, 0.0)
            dzb = dz.astype(bf16)
            dq = dq + _dot(dzb, kj)
            dk_acc[rows, :] += _dot(dzb, qi, TN)
            dv_acc[rows, :] += _dot(w.astype(bf16), doi, TN)
            return cl + bs, cp + jnp.sum(p, axis=1, keepdims=True), dq

        def qloop(i, carry):
            base = pl.multiple_of(i * TQ, TQ)
            rows = pl.ds(base, TQ)
            qi = q_ref[0, rows, :]
            doi = do_ref[0, rows, :]
            total = tot_ref[0, 0, rows, :][:, 0:1]
            zero = jnp.zeros((TQ, 1), f32)

            def group(g, st):
                g0 = pl.multiple_of(g * TQ, TQ)
                for s in range(ns):
                    st = sub(qi, doi, total, pl.ds(g0 + s * KB, KB), None, *st)
                return st

            st = lax.fori_loop(0, i, group, (zero, zero, jnp.zeros((TQ, HEAD), f32)))
            for s in range(ns):
                st = sub(qi, doi, total, pl.ds(base + s * KB, KB), (s * KB + wide_c) < wide_r, *st)
            dq_ref[0, rows, :] = st[2].astype(bf16)
            return carry

        lax.fori_loop(0, nq, qloop, 0)
        dk_ref[0] = dk_acc[...].astype(bf16)
        dv_ref[0] = dv_acc[...].astype(bf16)

    def spec(off):
        return pl.BlockSpec((1, S, HEAD), lambda b, h, off=off: (b, 0, off + h))

    o = jax.ShapeDtypeStruct((B, S, BRANCH), bf16)
    body, cin, cout, cshapes, cscratch = _carried(body, 5, 3, (B, N_HEADS), carry)
    outs = pl.pallas_call(
        body, name=name, grid=(B, N_HEADS),
        in_specs=[spec(0), spec(N_HEADS), spec(2 * N_HEADS), spec(0),
                  pl.BlockSpec((1, 1, S, LANES), lambda b, h: (b, h, 0, 0))] + cin,
        out_specs=[spec(0), spec(0), spec(0)] + cout, out_shape=[o, o, o] + cshapes,
        scratch_shapes=[pltpu.VMEM((S, HEAD), f32), pltpu.VMEM((S, HEAD), f32)] + cscratch,
        compiler_params=_cp(("arbitrary", "arbitrary") if carry else ("parallel", "parallel")),
    )(sbqkv, sbqkv, sbqkv, do, tot, *(carry.ins if carry else []))
    return outs[:3], outs[3:]


def _dn_params(a_log, dt_bias):
    p = jnp.zeros((8, LANES), f32)
    p = p.at[0, :N_HEADS].set(a_log)
    return p.at[1, :N_HEADS].set(dt_bias)


def _dn_prep(ab, par, name):
    B, S, _ = ab.shape
    R = 2 * CHUNK
    nt = S // R

    def body(ab_ref, par_ref, gcb_ref, bb_ref, gcr_ref):
        x = ab_ref[0]
        g = -jnp.exp(par_ref[0:1, :]) * _softplus(x + par_ref[1:2, :])
        r = lax.broadcasted_iota(jnp.int32, (R, R), 0)
        c = lax.broadcasted_iota(jnp.int32, (R, R), 1)
        tri = ((r >= c) & ((r >> 6) == (c >> 6))).astype(f32)
        cs = jnp.dot(tri, g, precision=HI, preferred_element_type=f32)
        beta = _sigmoid(x)
        cst = cs.T
        for h in range(N_HEADS):
            gcb_ref[0, h] = jnp.broadcast_to(cs[:, h:h + 1], (R, LANES))
            bb_ref[0, h] = jnp.broadcast_to(beta[:, N_HEADS + h:N_HEADS + h + 1], (R, LANES))
            gcr_ref[0, h, 0] = jnp.broadcast_to(cst[h:h + 1, 0:CHUNK], (8, CHUNK))
            gcr_ref[0, h, 1] = jnp.broadcast_to(cst[h:h + 1, CHUNK:R], (8, CHUNK))

    return pl.pallas_call(
        body, name=name, grid=(B, nt),
        in_specs=[pl.BlockSpec((1, R, LANES), lambda b, i: (b, i, 0)), pl.BlockSpec((8, LANES), lambda b, i: (0, 0))],
        out_specs=[pl.BlockSpec((1, N_HEADS, R, LANES), lambda b, i: (b, 0, i, 0)),
                   pl.BlockSpec((1, N_HEADS, R, LANES), lambda b, i: (b, 0, i, 0)),
                   pl.BlockSpec((1, N_HEADS, 2, 8, CHUNK), lambda b, i: (b, 0, i, 0, 0))],
        out_shape=[jax.ShapeDtypeStruct((B, N_HEADS, S, LANES), f32), jax.ShapeDtypeStruct((B, N_HEADS, S, LANES), f32),
                   jax.ShapeDtypeStruct((B, N_HEADS, S // CHUNK, 8, CHUNK), f32)],
        compiler_params=_cp(("parallel", "parallel")),
    )(ab, par)


def _bmm(a, b, prec=None):
    return jnp.einsum("nij,njk->nik", a, b, preferred_element_type=f32, precision=prec)


def _bmm_nt(a, b, prec=None):
    return jnp.einsum("nik,njk->nij", a, b, preferred_element_type=f32, precision=prec)


def _bmm_tn(a, b, prec=None):
    return jnp.einsum("nki,nkj->nij", a, b, preferred_element_type=f32, precision=prec)


def _tri_inv(L):
    C = L.shape[-1]
    r = lax.broadcasted_iota(jnp.int32, (C, C), 0)
    c = lax.broadcasted_iota(jnp.int32, (C, C), 1)
    eye = (r == c).astype(f32)
    bd16 = (r >> 4) == (c >> 4)
    bd32 = (r >> 5) == (c >> 5)
    mm = functools.partial(_bmm, prec=MID)
    n1 = -jnp.where(bd16, L, 0.0)
    n2 = mm(n1, n1)
    n4 = mm(n2, n2)
    n8 = mm(n4, n4)
    t = mm(mm(mm(eye + n1, eye + n2), eye + n4), eye + n8)
    t = t - mm(mm(t, jnp.where(bd32 & jnp.logical_not(bd16), L, 0.0)), t)
    t = t - mm(mm(t, jnp.where(bd32, 0.0, L)), t)
    return t


def _conv_silu(x, w, row):
    c = w[3:4] * x + w[2:3] * _shift_down(x, 1, row) + w[1:2] * _shift_down(x, 2, row) + w[0:1] * _shift_down(x, 3, row)
    return c, c * _sigmoid(c)


def _dn_intra(qn, kn, v, gcb, beta, gr):
    C = CHUNK
    r = lax.broadcasted_iota(jnp.int32, (C, C), 0)
    c = lax.broadcasted_iota(jnp.int32, (C, C), 1)
    incl = r >= c
    diff = gcb[:, :, :C] - gr
    dm = jnp.where(incl, jnp.exp(jnp.where(incl, diff, 0.0)), 0.0)
    ds = jnp.where(r > c, dm, 0.0)
    kb = kn * beta
    knb = kn.astype(bf16)
    L = _bmm_nt(kb.astype(bf16), knb) * ds
    eg = jnp.exp(gcb)
    a = _bmm_nt(qn.astype(bf16), knb) * dm
    gl = gcb[:, C - 1:C, :]
    ekd = jnp.exp(gl - gcb)
    return dict(dm=dm, ds=ds, kb=kb, L=L, eg=eg, rhs_u=v * beta, rhs_w=kb * eg, a=a,
                qd=qn * eg, kd=kn * ekd, ekd=ekd, cd=jnp.exp(gl))


def _dn_specs(S):
    def col(off):
        return pl.BlockSpec((1, S, HEAD), lambda b, h, off=off: (b, 0, off + h))

    def cw(off):
        return pl.BlockSpec((4, HEAD), lambda b, h, off=off: (0, off + h))

    per_head = pl.BlockSpec((1, 1, S, LANES), lambda b, h: (b, h, 0, 0))
    rowform = pl.BlockSpec((1, 1, S // CHUNK, 8, CHUNK), lambda b, h: (b, h, 0, 0, 0))
    gain = pl.BlockSpec((1, HEAD), lambda b, h: (0, 0))
    ins = [col(4), col(8), col(12), col(16), cw(0), cw(4), cw(8), per_head, per_head, rowform, gain]
    return ins, per_head


def _dn_act(x_ref, cw_ref, row, normalise, out_scale=1.0):
    _, act = _conv_silu(x_ref[0].astype(f32), cw_ref[...], row)
    if normalise:
        act = act * (lax.rsqrt(jnp.sum(act * act, axis=-1, keepdims=True) + EPS) * out_scale)
    return act


def _dn_group(qn_s, kn_s, v_s, gcb_ref, bb_ref, gcr_ref, g, ng):
    C = CHUNK
    rows = pl.ds(pl.multiple_of(g * (ng * C), ng * C), ng * C)
    ch = pl.ds(g * ng, ng)
    sh = (ng, C, HEAD)
    qn, kn, v = qn_s[rows, :].reshape(sh), kn_s[rows, :].reshape(sh), v_s[rows, :].reshape(sh)
    gcb3, beta = gcb_ref[0, 0, rows, :].reshape(sh), bb_ref[0, 0, rows, :].reshape(sh)
    gr = gcr_ref[0, 0, ch][:, 0:1, :]
    it = _dn_intra(qn, kn, v, gcb3, beta, gr)
    it.update(qn=qn, kn=kn, v=v, gcb=gcb3, beta=beta, gr=gr)
    return rows, ch, it


def _dn_fwd(proj, conv_w, gcb, betab, gcr, gain, name, carry=None):
    B, S, _ = proj.shape
    n, C = S // CHUNK, CHUNK
    ng = min(8, n)
    ins, per_head = _dn_specs(S)

    def body(q_ref, k_ref, v_ref, z_ref, cq_ref, ck_ref, cv_ref, gcb_ref, bb_ref, gcr_ref, gain_ref,
             y_ref, st_ref, vn_ref, qn_s, kn_s, v_s, u_s, w_s, qd_s, a_s, cd_s, g_s, h_s):
        row = lax.broadcasted_iota(jnp.int32, (S, HEAD), 0)
        qn_s[...] = _dn_act(q_ref, cq_ref, row, True, HEAD ** -0.5)
        kn_s[...] = _dn_act(k_ref, ck_ref, row, True)
        v_s[...] = _dn_act(v_ref, cv_ref, row, False)

        def group(g, carry):
            _, ch, it = _dn_group(qn_s, kn_s, v_s, gcb_ref, bb_ref, gcr_ref, g, ng)
            t = _tri_inv(it["L"])
            u = _bmm(t, it["rhs_u"], MID)
            wb = _bmm(t, it["rhs_w"], MID).astype(bf16)
            kdb = it["kd"].astype(bf16)
            u_s[ch] = u
            w_s[ch] = wb
            qd_s[ch] = it["qd"].astype(bf16)
            a_s[ch] = it["a"].astype(bf16)
            cd_s[ch] = it["cd"]
            g_s[ch] = _bmm_tn(kdb, wb).astype(bf16)
            h_s[ch] = _bmm_tn(kdb, u.astype(bf16))
            return carry

        lax.fori_loop(0, n // ng, group, 0)

        def step(i, st):
            sb = st.astype(bf16)
            st_ref[0, 0, i] = sb
            return st * cd_s[i] - _dot(g_s[i], sb) + h_s[i]

        lax.fori_loop(0, n, step, jnp.zeros((HEAD, HEAD), f32))

        def group_out(g, carry):
            rows = pl.ds(pl.multiple_of(g * (ng * C), ng * C), ng * C)
            ch = pl.ds(g * ng, ng)
            sn = st_ref[0, 0, ch]
            vn = (u_s[ch] - _bmm(w_s[ch], sn)).astype(bf16)
            vn_ref[0, 0, rows, :] = vn.reshape(ng * C, HEAD)
            o = (_bmm(qd_s[ch], sn) + _bmm(a_s[ch], vn)).reshape(ng * C, HEAD)
            zz = z_ref[0, rows, :].astype(f32)
            rr = lax.rsqrt(jnp.mean(o * o, axis=-1, keepdims=True) + EPS)
            y_ref[0, rows, :] = (o * rr * gain_ref[...] * (zz * _sigmoid(zz))).astype(bf16)
            return carry

        lax.fori_loop(0, n // ng, group_out, 0)

    seq = pltpu.VMEM((S, HEAD), f32)
    body, cin, cout, cshapes, cscratch = _carried(body, len(ins), 3, (B, N_HEADS), carry)
    outs = pl.pallas_call(
        body, name=name, grid=(B, N_HEADS), in_specs=ins + cin,
        out_specs=[pl.BlockSpec((1, S, HEAD), lambda b, h: (b, 0, h)),
                   pl.BlockSpec((1, 1, n, HEAD, HEAD), lambda b, h: (b, h, 0, 0, 0)), per_head] + cout,
        out_shape=[jax.ShapeDtypeStruct((B, S, BRANCH), bf16), jax.ShapeDtypeStruct((B, N_HEADS, n, HEAD, HEAD), bf16),
                   jax.ShapeDtypeStruct((B, N_HEADS, S, HEAD), bf16)] + cshapes,
        scratch_shapes=[seq, seq, seq, pltpu.VMEM((n, C, HEAD), f32), pltpu.VMEM((n, C, HEAD), bf16),
                        pltpu.VMEM((n, C, HEAD), bf16), pltpu.VMEM((n, C, C), bf16), pltpu.VMEM((n, 1, HEAD), f32),
                        pltpu.VMEM((n, HEAD, HEAD), bf16), pltpu.VMEM((n, HEAD, HEAD), f32)] + cscratch,
        compiler_params=_cp(("arbitrary", "arbitrary") if carry else ("parallel", "parallel")),
    )(proj, proj, proj, proj, conv_w, conv_w, conv_w, gcb, betab, gcr, gain.reshape(1, HEAD), *(carry.ins if carry else []))
    return outs[:3], outs[3:]


def _rowsum(x):
    return jnp.sum(x, axis=-1, keepdims=True)


def _dn_bwd(proj, dy, conv_w, gcb, betab, gcr, gain, states, vnew, name, carry=None):
    B, S, _ = proj.shape
    n, C = S // CHUNK, CHUNK
    ng = min(8, n)
    ins, per_head = _dn_specs(S)
    ins = ins + [pl.BlockSpec((1, S, HEAD), lambda b, h: (b, 0, h)),
                 pl.BlockSpec((1, 1, n, HEAD, HEAD), lambda b, h: (b, h, 0, 0, 0)), per_head]

    def body(q_ref, k_ref, v_ref, z_ref, cq_ref, ck_ref, cv_ref, gcb_ref, bb_ref, gcr_ref, gain_ref, dy_ref, st_ref, vn_ref,
             dq_ref, dk_ref, dv_ref, dz_ref, dg_ref, dbeta_ref, dconv_ref, dgain_ref,
             qn_s, kn_s, v_s, t_s, u_s, w_s, cd_s, do_s, dsp_s, dvn_s, g_s, q_s):
        row = lax.broadcasted_iota(jnp.int32, (S, HEAD), 0)
        qn_s[...] = _dn_act(q_ref, cq_ref, row, True, HEAD ** -0.5)
        kn_s[...] = _dn_act(k_ref, ck_ref, row, True)
        v_s[...] = _dn_act(v_ref, cv_ref, row, False)
        dgain_ref[...] = jnp.zeros_like(dgain_ref)
        gv = gain_ref[...]

        def group_fwd(g, carry):
            rows, ch, it = _dn_group(qn_s, kn_s, v_s, gcb_ref, bb_ref, gcr_ref, g, ng)
            t = _tri_inv(it["L"])
            ub = _bmm(t, it["rhs_u"], MID).astype(bf16)
            wb = _bmm(t, it["rhs_w"], MID).astype(bf16)
            ab, qdb = it["a"].astype(bf16), it["qd"].astype(bf16)
            vn = vn_ref[0, 0, rows, :].reshape(ng, C, HEAD)
            o = (_bmm(qdb, st_ref[0, 0, ch]) + _bmm(ab, vn)).reshape(ng * C, HEAD)
            zz = z_ref[0, rows, :].astype(f32)
            dyv = dy_ref[0, rows, :].astype(f32)
            rr = lax.rsqrt(jnp.mean(o * o, axis=-1, keepdims=True) + EPS)
            on = o * rr
            sz = _sigmoid(zz)
            dz_ref[0, rows, :] = (dyv * on * gv * (sz * (1.0 + zz * (1.0 - sz)))).astype(bf16)
            dnrm = dyv * (zz * sz)
            dgain_ref[0, 0] += _fold8(dnrm * on)
            doh = dnrm * gv
            do = rr * (doh - on * jnp.mean(doh * on, axis=-1, keepdims=True))
            dob = do.reshape(ng, C, HEAD).astype(bf16)
            atdo = _bmm_tn(ab, dob)
            t_s[ch] = t
            u_s[ch] = ub
            w_s[ch] = wb
            cd_s[ch] = it["cd"]
            do_s[ch] = dob
            dvn_s[ch] = atdo
            g_s[ch] = _bmm_tn(it["kd"].astype(bf16), wb).astype(bf16)
            q_s[ch] = _bmm_tn(qdb, dob) - _bmm_tn(wb, atdo.astype(bf16))
            return carry

        lax.fori_loop(0, n // ng, group_fwd, 0)

        def step(t, dsp):
            i = n - 1 - t
            dspb = dsp.astype(bf16)
            dsp_s[i] = dspb
            return dsp * cd_s[i] - _dot(g_s[i], dspb, TN) + q_s[i]

        lax.fori_loop(0, n, step, jnp.zeros((HEAD, HEAD), f32))

        r = lax.broadcasted_iota(jnp.int32, (C, C), 0)
        c = lax.broadcasted_iota(jnp.int32, (C, C), 1)
        upper = r <= c

        def group_bwd(g, carry):
            rows, ch, it = _dn_group(qn_s, kn_s, v_s, gcb_ref, bb_ref, gcr_ref, g, ng)
            sh = (ng, C, HEAD)
            qn, kn, v, beta, gcb3, gr = it["qn"], it["kn"], it["v"], it["beta"], it["gcb"], it["gr"]
            sn = st_ref[0, 0, ch]
            vn = vn_ref[0, 0, rows, :].reshape(sh)
            dsp, dob = dsp_s[ch], do_s[ch]
            dvn = dvn_s[ch] + _bmm(it["kd"].astype(bf16), dsp)
            t, ub, wb = t_s[ch], u_s[ch], w_s[ch]
            dvnb = dvn.astype(bf16)
            da = _bmm_nt(dob, vn)
            dat = _bmm_nt(vn, dob)
            dqd = _bmm_nt(dob, sn)
            dkd = _bmm_nt(vn, dsp)
            dcd = jnp.sum(jnp.sum(dsp.astype(f32) * sn.astype(f32), axis=2, keepdims=True), axis=1, keepdims=True)
            dw = -_bmm_nt(dvnb, sn)
            ru = _bmm_tn(t, dvn, MID)
            rw = _bmm_tn(t, dw, MID)
            rub, rwb = ru.astype(bf16), rw.astype(bf16)
            dL = -(_bmm_nt(rub, ub) + _bmm_nt(rwb, wb))
            dLt = -(_bmm_nt(ub, rub) + _bmm_nt(wb, rwb))
            knb, qnb, kbb = kn.astype(bf16), qn.astype(bf16), it["kb"].astype(bf16)
            dmt = jnp.where(upper, jnp.exp(jnp.where(upper, gr - gcb3[:, :, :C], 0.0)), 0.0)
            Lt = _bmm_nt(knb, kbb) * jnp.where(r < c, dmt, 0.0)
            At = _bmm_nt(knb, qnb) * dmt
            dgc = _rowsum(dL * it["L"] + da * it["a"]) - _rowsum(dLt * Lt + dat * At)
            dkk = (dL * it["ds"]).astype(bf16)
            dqk = (da * it["dm"]).astype(bf16)
            dkb = _bmm(dkk, knb) + rw * it["eg"]
            dkn = _bmm_tn(dkk, kbb) + _bmm_tn(dqk, qnb) + dkd * it["ekd"] + dkb * beta
            dqn = _bmm(dqk, knb) + dqd * it["eg"]
            tkd = _rowsum(dkd * it["kd"])
            dgl = jnp.sum(tkd, axis=1, keepdims=True) + dcd * it["cd"][:, :, 0:1]
            dgc = dgc + _rowsum(dqd * it["qd"]) - tkd + _rowsum(rw * it["rhs_w"])
            dbeta = _rowsum(ru * v) + _rowsum(dkb * kn)
            rowc = lax.broadcasted_iota(jnp.int32, (ng, C, 1), 1)
            dgc = dgc + jnp.where(rowc == C - 1, dgl, 0.0)
            rev = jnp.broadcast_to(upper.astype(f32), (ng, C, C))
            dg_ref[0, 0, rows, :] = _bmm(rev, jnp.broadcast_to(dgc, sh), HI).reshape(ng * C, LANES).astype(bf16)
            dbeta_ref[0, 0, rows, :] = jnp.broadcast_to(dbeta, sh).reshape(ng * C, LANES).astype(bf16)
            qn_s[rows, :] = dqn.reshape(ng * C, HEAD)
            kn_s[rows, :] = dkn.reshape(ng * C, HEAD)
            v_s[rows, :] = (ru * beta).reshape(ng * C, HEAD)
            return carry

        lax.fori_loop(0, n // ng, group_bwd, 0)

        def conv_back(x_ref, cw_ref, grad_s, out_ref, slot, normalise, out_scale):
            x = x_ref[0].astype(f32)
            w = cw_ref[...]
            pre, act = _conv_silu(x, w, row)
            dact = grad_s[...]
            if normalise:
                rn = lax.rsqrt(jnp.sum(act * act, axis=-1, keepdims=True) + EPS)
                unit = act * rn
                dact = (out_scale * rn) * (dact - unit * _rowsum(dact * unit))
            s = _sigmoid(pre)
            dc = dact * (s * (1.0 + pre * (1.0 - s)))
            out_ref[0] = (w[3:4] * dc + w[2:3] * _shift_up(dc, 1, row) + w[1:2] * _shift_up(dc, 2, row)
                          + w[0:1] * _shift_up(dc, 3, row)).astype(bf16)
            for tap in range(4):
                xs = x if tap == 3 else _shift_down(x, 3 - tap, row)
                dconv_ref[0, slot, tap:tap + 1, :] = jnp.sum(dc * xs, axis=0, keepdims=True)

        conv_back(q_ref, cq_ref, qn_s, dq_ref, 0, True, HEAD ** -0.5)
        conv_back(k_ref, ck_ref, kn_s, dk_ref, 1, True, 1.0)
        conv_back(v_ref, cv_ref, v_s, dv_ref, 2, False, 1.0)

    o512 = jax.ShapeDtypeStruct((B, S, BRANCH), bf16)
    s512 = pl.BlockSpec((1, S, HEAD), lambda b, h: (b, 0, h))
    ph = jax.ShapeDtypeStruct((B, N_HEADS, S, LANES), bf16)
    seq = pltpu.VMEM((S, HEAD), f32)
    cb = pltpu.VMEM((n, C, HEAD), bf16)
    body, cin, cout, cshapes, cscratch = _carried(body, len(ins), 8, (B, N_HEADS), carry)
    outs = pl.pallas_call(
        body, name=name, grid=(B, N_HEADS), in_specs=ins + cin,
        out_specs=[s512, s512, s512, s512, per_head, per_head, pl.BlockSpec((1, 3, 4, HEAD), lambda b, h: (b, 0, 0, h)),
                   pl.BlockSpec((1, 1, 8, HEAD), lambda b, h: (b, h, 0, 0))] + cout,
        out_shape=[o512, o512, o512, o512, ph, ph, jax.ShapeDtypeStruct((B, 3, 4, BRANCH), f32),
                   jax.ShapeDtypeStruct((B, N_HEADS, 8, HEAD), f32)] + cshapes,
        scratch_shapes=[seq, seq, seq, pltpu.VMEM((n, C, C), f32), cb, cb, pltpu.VMEM((n, 1, HEAD), f32), cb,
                        pltpu.VMEM((n, HEAD, HEAD), bf16), pltpu.VMEM((n, C, HEAD), f32), pltpu.VMEM((n, HEAD, HEAD), bf16),
                        pltpu.VMEM((n, HEAD, HEAD), f32)] + cscratch,
        compiler_params=_cp(("arbitrary", "arbitrary") if carry else ("parallel", "parallel")),
    )(proj, proj, proj, proj, conv_w, conv_w, conv_w, gcb, betab, gcr, gain.reshape(1, HEAD), dy, states, vnew,
      *(carry.ins if carry else []))
    return outs[:8], outs[8:]


def _dn_post(ab, par, dg, dbeta, name):
    B, S, _ = ab.shape
    ts = _tile(S, (512, 256, 128))

    def body(ab_ref, par_ref, dg_ref, db_ref, dab_ref, acc_ref):
        x = ab_ref[0]
        lane = lax.broadcasted_iota(jnp.int32, x.shape, 1)
        dgs = jnp.zeros_like(x)
        dbs = jnp.zeros_like(x)
        for h in range(N_HEADS):
            dgs = jnp.where(lane == h, dg_ref[0, h], dgs)
            dbs = jnp.where(lane == N_HEADS + h, db_ref[0, h], dbs)
        nega = -jnp.exp(par_ref[0:1, :])
        pre = x + par_ref[1:2, :]
        da = dgs * nega * _sigmoid(pre)
        beta = _sigmoid(x)
        dab_ref[0] = (da + dbs * beta * (1.0 - beta)).astype(bf16)

        @pl.when((pl.program_id(0) == 0) & (pl.program_id(1) == 0))
        def _():
            acc_ref[...] = jnp.zeros_like(acc_ref)

        acc_ref[0] += _fold8(dgs * nega * _softplus(pre))
        acc_ref[1] += _fold8(da)

    return pl.pallas_call(
        body, name=name, grid=(B, S // ts),
        in_specs=[pl.BlockSpec((1, ts, LANES), lambda b, i: (b, i, 0)), pl.BlockSpec((8, LANES), lambda b, i: (0, 0)),
                  pl.BlockSpec((1, N_HEADS, ts, LANES), lambda b, i: (b, 0, i, 0)),
                  pl.BlockSpec((1, N_HEADS, ts, LANES), lambda b, i: (b, 0, i, 0))],
        out_specs=[pl.BlockSpec((1, ts, LANES), lambda b, i: (b, i, 0)), pl.BlockSpec((2, 8, LANES), lambda b, i: (0, 0, 0))],
        out_shape=[jax.ShapeDtypeStruct((B, S, LANES), bf16), jax.ShapeDtypeStruct((2, 8, LANES), f32)],
        compiler_params=_cp(("arbitrary", "arbitrary")),
    )(ab, par, dg, dbeta)


def _merge_specs(T, D, tm, tn, order):
    nj = D // tn

    def ij(f):
        return (lambda i, j: f(i, j)) if order == "ij" else (lambda j, i: f(i, j))

    ys = [pl.BlockSpec((tm, BRANCH), ij(lambda i, j: (i, 0))) for _ in range(3)]
    wb = pl.BlockSpec((3, BRANCH, tn), ij(lambda i, j: (0, 0, j)))
    gl = [pl.BlockSpec((tm, tn), ij(lambda i, j, k=k: (i, k * nj + j))) for k in range(3)]
    bg = [pl.BlockSpec((1, tn), ij(lambda i, j, k=k: (0, k * nj + j))) for k in range(3)]
    return ys, wb, gl, bg


def _merge_fwd(ys, wb, gl, b_gate, name):
    T, D = ys[0].shape[0], wb.shape[2]
    tm, tn = _tile(T, (512, 256, 128)), _tile(D, (512, 256, 128))
    sy, swb, sgl, sbg = _merge_specs(T, D, tm, tn, "ij")

    def body(y0, y1, y2, wb_ref, g0, g1, g2, b0, b1, b2, o_ref):
        acc = None
        for k, (y, g, b) in enumerate(((y0, g0, b0), (y1, g1, b1), (y2, g2, b2))):
            term = _sigmoid(g[...].astype(f32) + b[...]) * _dot(y[...], wb_ref[k])
            acc = term if acc is None else acc + term
        o_ref[...] = acc.astype(bf16)

    bg = b_gate.reshape(1, 3 * D)
    return pl.pallas_call(
        body, name=name, grid=(T // tm, D // tn), in_specs=sy + [swb] + sgl + sbg,
        out_specs=pl.BlockSpec((tm, tn), lambda i, j: (i, j)), out_shape=jax.ShapeDtypeStruct((T, D), bf16),
        compiler_params=_cp(("parallel", "parallel")),
    )(*ys, wb, gl, gl, gl, bg, bg, bg)


def _merge_bwd(dm, ys, wb, gl, b_gate, name):
    T, D = dm.shape
    tm, tn = _tile(T, (512, 256, 128)), _tile(D, (512, 256, 128))
    sy, swb, sgl, sbg = _merge_specs(T, D, tm, tn, "ji")

    def body(dm_ref, y0, y1, y2, wb_ref, g0, g1, g2, b0, b1, b2, dgl_ref, dbd_ref, dbg_ref):
        dmv = dm_ref[...].astype(f32)

        @pl.when(pl.program_id(1) == 0)
        def _():
            dbg_ref[...] = jnp.zeros_like(dbg_ref)

        for k, (y, g, b) in enumerate(((y0, g0, b0), (y1, g1, b1), (y2, g2, b2))):
            s = _sigmoid(g[...].astype(f32) + b[...])
            dg = dmv * _dot(y[...], wb_ref[k]) * s * (1.0 - s)
            dgl_ref[k] = dg.astype(bf16)
            dbd_ref[k] = (dmv * s).astype(bf16)
            dbg_ref[k] += _fold8(dg)

    bg = b_gate.reshape(1, 3 * D)
    o3 = jax.ShapeDtypeStruct((3, T, D), bf16)
    s3 = pl.BlockSpec((3, tm, tn), lambda j, i: (0, i, j))
    return pl.pallas_call(
        body, name=name, grid=(D // tn, T // tm),
        in_specs=[pl.BlockSpec((tm, tn), lambda j, i: (i, j))] + sy + [swb] + sgl + sbg,
        out_specs=[s3, s3, pl.BlockSpec((3, 8, tn), lambda j, i: (0, 0, j))],
        out_shape=[o3, o3, jax.ShapeDtypeStruct((3, 8, D), f32)],
        compiler_params=_cp(("parallel", "arbitrary")),
    )(dm, *ys, wb, gl, gl, gl, bg, bg, bg)


def _ffn_fwd(x, g, w, f, tag, carry=None):
    T, D = x.shape
    FP = w["wgu"].shape[2]
    h = _rms_fwd(x, g, f"rms_{tag}")
    (a, b, hm), landed = _ffn_up(h, w["wgu"], f, f"ffn_up_{tag}", carry)
    down = [("nn", hm, w["m"], {"K": FP, "ka": k, "bsel": k, "kb": f}) for k in range(N_CHIPS)]
    y = _mm(down, T, D, f32, f"ffn_down_{tag}", res=x, scale=0.5)
    return y, (x, h, a, b), landed


def _ffn_bwd(dy2, saved, g, w, f, tag):
    x, h, a, b = saved
    dy, dyb = dy2
    T, D = x.shape
    FP = w["wgu"].shape[2]
    F4 = N_CHIPS * FP
    da, db, hm = _ffn_bwd_mid(dyb, w["m"], a, b, f, f"ffn_mid_bwd_{tag}")
    dwd = _mm([("tn", hm, dyb, {})], F4, D, bf16, f"ffn_dwd_{tag}", scale=0.5)
    dwg = _mm([("tn", h, da, {})], D, F4, bf16, f"ffn_dwg_{tag}", out_chip=True)
    dwu = _mm([("tn", h, db, {})], D, F4, bf16, f"ffn_dwu_{tag}", out_chip=True)
    tn = _tile(D, (512, 256, 128))
    pairs = [("nt", t, w["wgu"], {"K": FP, "ka": k, "bsel": k, "noff": (2 * f + u) * (D // tn)})
             for u, t in enumerate((da, db)) for k in range(N_CHIPS)]
    dh = _mm(pairs, T, D, f32, f"ffn_dh_{tag}", tn=tn)
    dx, dg8 = _rms_bwd(dh, x, g, dy, f"rms_bwd_{tag}")
    return dx, dict(norm=jnp.sum(dg8, axis=0), wgu=[dwg, dwu], wd=dwd.reshape(N_CHIPS, FP, D))


def _layer_fwd(x, w, B, tag, carry=(None, None, None)):
    T, D = x.shape
    S = T // B
    x1, sv0, landed_ffn = _ffn_fwd(x, w["ffn_norm"][0], w, 0, f"pre_{tag}", carry[2])
    h = _rms_fwd(x1, w["mix_norm"], f"rms_mix_{tag}")
    pm = _mm([("nn", h, w["w_main"], {})], T, 5 * BRANCH, bf16, f"proj_main_{tag}")
    ab = _mm([("nn", h, w["w_ab"], {})], T, LANES, f32, f"proj_ab_{tag}", tn=LANES)
    sb = _mm([("nn", h, w["w_sb"], {})], T, 3 * BRANCH, bf16, f"proj_sb_{tag}")
    gl = _mm([("nn", h, w["w_gates"], {})], T, 3 * D, bf16, f"proj_gates_{tag}")
    pm3, ab3, sb3 = pm.reshape(B, S, -1), ab.reshape(B, S, LANES), sb.reshape(B, S, -1)
    y_pool = _pool_fwd(pm3, w["pool_w"], w["pool_scale"], f"pool_{tag}")
    par = _dn_params(w["dn_A_log"], w["dn_dt_bias"])
    gcb, betab, gcr = _dn_prep(ab3, par, f"dn_prep_{tag}")
    (y_dn, states, vnew), landed_dn = _dn_fwd(pm3, w["dn_conv"], gcb, betab, gcr, w["dn_out_norm"], f"dn_fwd_{tag}", carry[0])
    (y_sb, tot), landed_sb = _sb_fwd(sb3, f"sb_fwd_{tag}", carry[1])
    ys = [y_pool.reshape(T, BRANCH), y_dn.reshape(T, BRANCH), y_sb.reshape(T, BRANCH)]
    merged = _merge_fwd(ys, w["w_branch"], gl, w["b_gate"], f"merge_{tag}")
    dc = D // N_CHIPS
    out_pairs = [("nn", merged, w["m"], {"K": dc, "ka": k, "bsel": k, "kb": _w_out_block(w)}) for k in range(N_CHIPS)]
    x2 = _mm(out_pairs, T, D, f32, f"mix_out_{tag}", res=x1)
    x3, sv1, _ = _ffn_fwd(x2, w["ffn_norm"][1], w, 1, f"post_{tag}")
    saved = dict(sv0=sv0, sv1=sv1, x1=x1, h=h, pm3=pm3, ab3=ab3, sb3=sb3, gl=gl, par=par, gcb=gcb, betab=betab, gcr=gcr,
                 states=states, vnew=vnew, tot=tot, ys=ys, merged=merged)
    return x3, saved, (landed_dn, landed_sb, landed_ffn)


def _layer_bwd(dx3, w, sv, B, tag, carry=(None, None)):
    T, D = dx3[0].shape
    S = T // B
    (dx2, dx2b), g1 = _ffn_bwd(dx3, sv["sv1"], w["ffn_norm"][1], w, 1, f"post_{tag}")
    dc = D // N_CHIPS
    dmerged = _mm([("nt", dx2b, w["m"], {"b_by_chip": True, "noff": _w_out_block(w)})], T, D, bf16, f"mix_dmerged_{tag}", tn=dc)
    dw_out = _mm([("tn", sv["merged"], dx2b, {})], D, D, bf16, f"mix_dwout_{tag}")
    ys = sv["ys"]
    dgl, dbd, dbg8 = _merge_bwd(dmerged, ys, w["w_branch"], sv["gl"], w["b_gate"], f"merge_bwd_{tag}")
    dys, dwb = [], []
    for k in range(3):
        dys.append(_mm([("nt", dbd, w["w_branch"], {"asel": k, "bsel": k})], T, BRANCH, bf16, f"branch_dy{k}_{tag}"))
        dwb.append(_mm([("tn", ys[k], dbd, {"bsel": k})], BRANCH, D, bf16, f"branch_dw{k}_{tag}"))
    pm3, ab3, sb3 = sv["pm3"], sv["ab3"], sv["sb3"]
    du, dpool_w, dsc8 = _pool_bwd(pm3, dys[0].reshape(B, S, BRANCH), w["pool_w"], w["pool_scale"], f"pool_bwd_{tag}")
    (dq, dk, dv, dz, dg, dbeta, dconv, dgain), landed = _dn_bwd(
        pm3, dys[1].reshape(B, S, BRANCH), w["dn_conv"], sv["gcb"], sv["betab"], sv["gcr"], w["dn_out_norm"],
        sv["states"], sv["vnew"], f"dn_bwd_{tag}", carry[0])
    dab, dn_acc = _dn_post(ab3, sv["par"], dg, dbeta, f"dn_post_{tag}")
    (dsq, dsk, dsv), landed_sb = _sb_bwd(sb3, dys[2].reshape(B, S, BRANCH), sv["tot"], f"sb_bwd_{tag}", carry[1])
    main_parts = [t.reshape(T, BRANCH) for t in (du, dq, dk, dv, dz)]
    sb_parts = [t.reshape(T, BRANCH) for t in (dsq, dsk, dsv)]
    dab2 = dab.reshape(T, LANES)
    pairs = [("nt", t, w["w_main"], {"K": BRANCH, "kb": k}) for k, t in enumerate(main_parts)]
    pairs.append(("nt", dab2, w["w_ab"], {}))
    pairs += [("nt", t, w["w_sb"], {"K": BRANCH, "kb": k}) for k, t in enumerate(sb_parts)]
    pairs += [("nt", dgl, w["w_gates"], {"K": D, "kb": k, "asel": k}) for k in range(3)]
    dh = _mm(pairs, T, D, f32, f"mix_dh_{tag}")
    h = sv["h"]
    dw_cols = [_mm([("tn", h, t, {})], D, BRANCH, bf16, f"dwin_main{k}_{tag}") for k, t in enumerate(main_parts)]
    dw_cols.append(_mm([("tn", h, dab2, {})], D, LANES, bf16, f"dwin_ab_{tag}", tn=LANES)[:, :2 * N_HEADS])
    dw_cols += [_mm([("tn", h, t, {})], D, BRANCH, bf16, f"dwin_sb{k}_{tag}") for k, t in enumerate(sb_parts)]
    dw_cols += [_mm([("tn", h, dgl, {"bsel": k})], D, D, bf16, f"dwin_gate{k}_{tag}") for k in range(3)]
    dx1, dmix8 = _rms_bwd(dh, sv["x1"], w["mix_norm"], dx2, f"rms_mix_bwd_{tag}")
    dx0, g0 = _ffn_bwd(dx1, sv["sv0"], w["ffn_norm"][0], w, 0, f"pre_{tag}")
    dwb = jnp.stack(dwb).reshape(3, BRANCH, N_CHIPS, dc).transpose(2, 0, 1, 3).reshape(N_CHIPS, -1, D)
    dw_in = jnp.concatenate(dw_cols, axis=1)
    pc = dw_in.shape[1] // N_CHIPS
    grads = dict(
        ffn_norm=jnp.stack([g0["norm"], g1["norm"]]),
        A=jnp.concatenate(g0["wgu"] + g1["wgu"], axis=1),
        M=jnp.concatenate([g0["wd"], g1["wd"], dw_out.reshape(N_CHIPS, dc, D), dwb], axis=1),
        C=jnp.stack([dw_in[:, k * pc:(k + 1) * pc] for k in range(N_CHIPS)]),
        mix_norm=jnp.sum(dmix8, axis=0), b_gate=jnp.sum(dbg8, axis=1).reshape(3 * D),
        pool_w=dpool_w, pool_scale=jnp.sum(dsc8, axis=0), dn_conv=jnp.sum(dconv, axis=0).transpose(1, 0, 2).reshape(4, 3 * BRANCH),
        dn_A_log=jnp.sum(dn_acc[0], axis=0)[:N_HEADS], dn_dt_bias=jnp.sum(dn_acc[1], axis=0)[:N_HEADS],
        dn_out_norm=jnp.sum(dgain, axis=(0, 1, 2)))
    return dx0, grads, (landed, landed_sb)


def _local_step(x, target, layers, final_norm, B):
    saved = []
    for l, w in enumerate(layers):
        x, sv, _ = _layer_fwd(x, w, B, f"l{l}")
        saved.append(sv)
    dx, dfn8, ls8 = _final_loss(x, target, final_norm)
    grads = [None] * len(layers)
    for l in reversed(range(len(layers))):
        dx, grads[l], _ = _layer_bwd(dx, layers[l], saved[l], B, f"l{l}")
    return jnp.sum(ls8), dx[0], grads, jnp.sum(dfn8, axis=0)


def _adamw(w, g, m, v, name):
    shape = w.shape
    cols = shape[-1]
    rows = math.prod(shape[:-1]) if len(shape) > 1 else 1
    w2, g2, m2, v2 = (t.reshape(rows, cols) for t in (w, g, m, v))
    block_elems = 256 * 1024
    tr = rows if rows * cols <= block_elems else _tile(rows, [t for t in (512, 256, 128, 64, 32, 16, 8) if t * cols <= block_elems])

    def body(w_ref, g_ref, m_ref, v_ref, d_ref, mo_ref, vo_ref):
        gv = g_ref[...]
        mn = ADAM_B1 * m_ref[...] + (1.0 - ADAM_B1) * gv
        vn = ADAM_B2 * v_ref[...] + (1.0 - ADAM_B2) * (gv * gv)
        m_hat = mn / (1.0 - ADAM_B1 ** ADAM_STEP)
        v_hat = vn / (1.0 - ADAM_B2 ** ADAM_STEP)
        d_ref[...] = -ADAM_LR * (m_hat / (jnp.sqrt(v_hat) + ADAM_EPS) + ADAM_WD * w_ref[...])
        mo_ref[...] = mn
        vo_ref[...] = vn

    spec = pl.BlockSpec((tr, cols), lambda i: (i, 0))
    o = jax.ShapeDtypeStruct((rows, cols), f32)
    d, mo, vo = pl.pallas_call(
        body, name=name, grid=(rows // tr,), in_specs=[spec] * 4, out_specs=[spec] * 3, out_shape=[o, o, o],
        compiler_params=_cp(("parallel",)),
    )(w2, g2, m2, v2)
    return d.reshape(shape), mo.reshape(shape), vo.reshape(shape)


MESH = pl.DeviceIdType.MESH
_ANY = pl.BlockSpec(memory_space=pl.ANY)


def _place():
    x, y, c = lax.axis_index("x"), lax.axis_index("y"), lax.axis_index("c")
    return x, y, c, [(1 - x, y), (x, 1 - y), (1 - x, 1 - y)]


def _chip_index():
    return 2 * lax.axis_index("x") + lax.axis_index("y")


def _half(c, rh):
    return pl.ds(c * rh, rh)


def _remote(src, dst, ssem, rsem, to):
    return pltpu.make_async_remote_copy(src_ref=src, dst_ref=dst, send_sem=ssem, recv_sem=rsem, device_id=to,
                                        device_id_type=MESH)


class _ChipExchange:
    def __init__(self, kind, ins):
        self.kind, self.ins = kind, list(ins)
        self.n = len(self.ins)
        self.out_shapes = [jax.ShapeDtypeStruct((N_CHIPS,) + a.shape[-2:], a.dtype) for a in self.ins]
        self.scratch = [pltpu.SemaphoreType.DMA((self.n, 3)), pltpu.SemaphoreType.DMA((self.n, 3))]

    def _copies(self, in_refs, out_refs, ssem, rsem):
        x, y, c, chips = _place()
        me = 2 * x + y
        pairs = []
        for o, (src, dst) in enumerate(zip(in_refs, out_refs)):
            for k, (px, py) in enumerate(chips):
                peer = 2 * px + py
                if self.kind == "gather":
                    rows = _half(c, src.shape[0] // 2)
                    out, land = (src.at[rows], dst.at[me, rows]), dst.at[peer, rows]
                else:
                    out, land = (src.at[peer], dst.at[me]), dst.at[peer]
                pairs.append((_remote(out[0], out[1], ssem.at[o, k], rsem.at[o, k], (px, py, c)),
                              _remote(land, land, ssem.at[o, k], rsem.at[o, k], (px, py, c))))
        return pairs

    def start(self, in_refs, out_refs, ssem, rsem):
        for mine, _ in self._copies(in_refs, out_refs, ssem, rsem):
            mine.start()

    def wait(self, in_refs, out_refs, ssem, rsem):
        for mine, landing in self._copies(in_refs, out_refs, ssem, rsem):
            landing.wait_recv()
            mine.wait_send()

    def standalone(self, name):
        n = self.n

        def body(*refs):
            ins, outs, (ssem, rsem) = refs[:n], refs[n:2 * n], refs[2 * n:]
            self.start(ins, outs, ssem, rsem)
            self.wait(ins, outs, ssem, rsem)

        return pl.pallas_call(body, name=name, in_specs=[_ANY] * n, out_specs=[_ANY] * n, out_shape=self.out_shapes,
                              scratch_shapes=self.scratch)(*self.ins)


def _carried(body, n_in, n_out, grid, carry):
    if carry is None:
        return body, [], [], [], []
    n = carry.n

    def wrapped(*refs):
        ins, cin = refs[:n_in], refs[n_in:n_in + n]
        outs, cout = refs[n_in + n:n_in + n + n_out], refs[n_in + n + n_out:n_in + 2 * n + n_out]
        scratch, (ssem, rsem) = refs[n_in + 2 * n + n_out:-2], refs[-2:]
        step = pl.program_id(0) * grid[1] + pl.program_id(1)

        @pl.when(step == 0)
        def _():
            carry.start(cin, cout, ssem, rsem)

        body(*ins, *outs, *scratch)

        @pl.when(step == grid[0] * grid[1] - 1)
        def _():
            carry.wait(cin, cout, ssem, rsem)

    return wrapped, [_ANY] * n, [_ANY] * n, carry.out_shapes, carry.scratch


def _gather_finish(shards, landed, name):
    n = len(shards)

    def body(*refs):
        outs, (ssem, rsem) = refs[n:2 * n], refs[2 * n:]
        x, y, c, chips = _place()
        started = []
        for o, buf in enumerate(outs):
            rh = buf.shape[1] // 2
            for k, (px, py) in enumerate(chips):
                block = buf.at[2 * px + py, _half(c, rh)]
                cp = _remote(block, block, ssem.at[o, k], rsem.at[o, k], (x, y, 1 - c))
                cp.start()
                started.append(cp)
        for o, buf in enumerate(outs):
            rh = buf.shape[1] // 2
            for k, (px, py) in enumerate(chips):
                block = buf.at[2 * px + py, _half(1 - c, rh)]
                _remote(block, block, ssem.at[o, k], rsem.at[o, k], (x, y, 1 - c)).wait_recv()
        for cp in started:
            cp.wait_send()

    outs = pl.pallas_call(
        body, name=name, in_specs=[_ANY] * n, out_specs=[_ANY] * n,
        out_shape=[jax.ShapeDtypeStruct(a.shape, a.dtype) for a in landed], input_output_aliases={i: i for i in range(n)},
        scratch_shapes=[pltpu.SemaphoreType.DMA((n, 3)), pltpu.SemaphoreType.DMA((n, 3))],
    )(*landed)
    me = _chip_index()
    return [lax.dynamic_update_slice(g, s[None], (me, 0, 0)) for g, s in zip(outs, shards)]


def _gather_chips(shards, name):
    landed = _ChipExchange("gather", shards).standalone(f"{name}_ici")
    return _gather_finish(shards, landed, f"{name}_pass")


def _pair_swap_halves(ps, name):
    n = len(ps)

    def body(*refs):
        ins, outs, (ssem, rsem) = refs[:n], refs[n:2 * n], refs[2 * n:]
        x, y, c, _ = _place()
        cps = [_remote(p.at[:, _half(1 - c, p.shape[1] // 2)], out, ssem.at[o], rsem.at[o], (x, y, 1 - c))
               for o, (p, out) in enumerate(zip(ins, outs))]
        for cp in cps:
            cp.start()
        for cp in cps:
            cp.wait()

    return pl.pallas_call(
        body, name=name, in_specs=[_ANY] * n, out_specs=[_ANY] * n,
        out_shape=[jax.ShapeDtypeStruct((p.shape[0], p.shape[1] // 2, p.shape[2]), p.dtype) for p in ps],
        scratch_shapes=[pltpu.SemaphoreType.DMA((n,)), pltpu.SemaphoreType.DMA((n,))],
    )(*ps)


def _pair_add(p, got, name):
    n, R, W = p.shape
    rh = R // 2
    tr = _row_tile(rh, W, 2 * 1024 * 1024)
    nb = rh // tr

    def body(c_ref, p_ref, g_ref, o_ref):
        o_ref[...] = (p_ref[...].astype(f32) + g_ref[...].astype(f32)).astype(o_ref.dtype)

    return pl.pallas_call(
        body, name=name,
        grid_spec=pltpu.PrefetchScalarGridSpec(
            num_scalar_prefetch=1, grid=(n, nb),
            in_specs=[pl.BlockSpec((1, tr, W), lambda j, i, c_ref: (j, c_ref[0] * nb + i, 0)),
                      pl.BlockSpec((1, tr, W), lambda j, i, c_ref: (j, i, 0))],
            out_specs=pl.BlockSpec((1, tr, W), lambda j, i, c_ref: (j, i, 0))),
        out_shape=jax.ShapeDtypeStruct((n, rh, W), p.dtype), compiler_params=_cp(("parallel", "parallel")),
    )(lax.axis_index("c").astype(jnp.int32).reshape(1), p, got)


def _own_slot_filled(landed, ps):
    me = _chip_index()
    return [lax.dynamic_update_slice(out, lax.dynamic_slice(p, (me, 0, 0), (1,) + p.shape[1:]), (me, 0, 0))
            for out, p in zip(landed, ps)]


def _sum_slots(r4, name):
    n, R, W = r4.shape
    tr = _row_tile(R, W, 1024 * 1024)

    def body(r_ref, o_ref):
        acc = r_ref[0].astype(f32)
        for k in range(1, n):
            acc = acc + r_ref[k].astype(f32)
        o_ref[...] = acc

    return pl.pallas_call(
        body, name=name, grid=(R // tr,), in_specs=[pl.BlockSpec((n, tr, W), lambda i: (0, i, 0))],
        out_specs=pl.BlockSpec((tr, W), lambda i: (i, 0)), out_shape=jax.ShapeDtypeStruct((R, W), f32),
        compiler_params=_cp(("parallel",)),
    )(r4)


def _pair_share(ss, name):
    n = len(ss)

    def body(*refs):
        ins, outs, (ssem, rsem) = refs[:n], refs[n:2 * n], refs[2 * n:]
        x, y, c, _ = _place()
        cps = [_remote(s, out.at[c], ssem.at[o], rsem.at[o], (x, y, 1 - c)) for o, (s, out) in enumerate(zip(ins, outs))]
        for cp in cps:
            cp.start()
        for o, (s, out) in enumerate(zip(ins, outs)):
            _remote(s, out.at[1 - c], ssem.at[o], rsem.at[o], (x, y, 1 - c)).wait_recv()
        for cp in cps:
            cp.wait_send()

    outs = pl.pallas_call(
        body, name=name, in_specs=[_ANY] * n, out_specs=[_ANY] * n,
        out_shape=[jax.ShapeDtypeStruct((2,) + s.shape, s.dtype) for s in ss],
        scratch_shapes=[pltpu.SemaphoreType.DMA((n,)), pltpu.SemaphoreType.DMA((n,))],
    )(*ss)
    c = lax.axis_index("c")
    return [lax.dynamic_update_slice(out, s[None], (c, 0, 0)).reshape(2 * s.shape[0], s.shape[1]) for out, s in zip(outs, ss)]


def _reduce_begin(ps, tag):
    got = _pair_swap_halves(ps, f"rs_pair_swap_{tag}")
    return [_pair_add(p, g, f"rs_pair_add{o}_{tag}") for o, (p, g) in enumerate(zip(ps, got))]


def _reduce_finish(pair_sums, landed, tag):
    r4 = _own_slot_filled(landed, pair_sums)
    return _pair_share([_sum_slots(r, f"rs_sum{o}_{tag}") for o, r in enumerate(r4)], f"rs_share_{tag}")


def _reduce_to_chips(ps, tag):
    pair_sums = _reduce_begin(ps, tag)
    return _reduce_finish(pair_sums, _ChipExchange("scatter", pair_sums).standalone(f"rs_scatter_{tag}"), tag)


def _pad_rows(a, mult):
    r = (-a.shape[-2]) % mult
    return a if r == 0 else jnp.pad(a, [(0, 0)] * (a.ndim - 2) + [(0, r), (0, 0)])


def _hidden_pad(fs):
    return -(-fs // LANES) * LANES


def _w_out_block(w):
    return 2 * w["wgu"].shape[2] // (w["m"].shape[2] // N_CHIPS)


def _pack_wgu(wg, wu):
    fs = wg.shape[-1]
    t = jnp.stack([wg, wu], axis=1).astype(bf16)
    return jnp.pad(t, ((0, 0), (0, 0), (0, 0), (0, _hidden_pad(fs) - fs))).reshape(-1, _hidden_pad(fs))


def _pack_m(wd, w_out, w_branch):
    fs, D = wd.shape[1:]
    wdp = jnp.pad(wd.astype(bf16), ((0, 0), (0, _hidden_pad(fs) - fs), (0, 0))).reshape(-1, D)
    return jnp.concatenate([wdp, w_out.astype(bf16), w_branch.astype(bf16).reshape(-1, D)], axis=0)


def _w_in_cols(cg, lo, hi):
    p = cg.shape[2]
    parts = [cg[k][:, max(lo, k * p) - k * p:min(hi, (k + 1) * p) - k * p] for k in range(N_CHIPS)
             if max(lo, k * p) < min(hi, (k + 1) * p)]
    return parts[0] if len(parts) == 1 else jnp.concatenate(parts, axis=1)


def _layer_weights(ag, mg, cg, small):
    D = mg.shape[2]
    dc = D // N_CHIPS
    fp2 = 2 * ag.shape[2]
    wb = mg[:, fp2 + dc:].reshape(N_CHIPS, 3, BRANCH, dc).transpose(1, 2, 0, 3).reshape(3, BRANCH, D)
    c0, c1, c2 = 5 * BRANCH, 5 * BRANCH + 2 * N_HEADS, 8 * BRANCH + 2 * N_HEADS
    w = dict(small)
    w.update(wgu=ag, m=mg, w_branch=wb, w_main=_w_in_cols(cg, 0, c0),
             w_ab=jnp.pad(_w_in_cols(cg, c0, c1), ((0, 0), (0, LANES - 2 * N_HEADS))), w_sb=_w_in_cols(cg, c1, c2),
             w_gates=_w_in_cols(cg, c2, N_CHIPS * cg.shape[2]))
    return w


def _small_pack(pieces):
    flat, offs, r = [], [], 0
    for a in pieces:
        v = a.reshape(-1)
        pad = (-v.shape[0]) % PACK_W
        flat.append(jnp.pad(v, (0, pad)) if pad else v)
        offs.append(r)
        r += (v.shape[0] + pad) // PACK_W
    pack = jnp.concatenate(flat).reshape(r, PACK_W)
    return _pad_rows(pack, 16), offs


def _small_unpack(pack, offs, shapes):
    out = []
    for o, s in zip(offs, shapes):
        n = math.prod(s)
        rows = -(-n // PACK_W)
        out.append(pack[o:o + rows].reshape(-1)[:n].reshape(s))
    return out


SMALL_SHARDED = ("ffn_norm", "dn_conv")
SMALL_REPLICATED = ("mix_norm", "b_gate", "pool_w", "pool_scale", "dn_A_log", "dn_dt_bias", "dn_out_norm")


def kernel(x, ffn_norm, ffn_w_gate, ffn_w_up, ffn_w_down, mix_norm, w_in, b_gate, pool_w, pool_scale, dn_conv, dn_A_log, dn_dt_bias, dn_out_norm, w_branch, w_out, final_norm, loss_target, m_ffn_norm, m_ffn_w_gate, m_ffn_w_up, m_ffn_w_down, m_mix_norm, m_w_in, m_b_gate, m_pool_w, m_pool_scale, m_dn_conv, m_dn_A_log, m_dn_dt_bias, m_dn_out_norm, m_w_branch, m_w_out, m_final_norm, v_ffn_norm, v_ffn_w_gate, v_ffn_w_up, v_ffn_w_down, v_mix_norm, v_w_in, v_b_gate, v_pool_w, v_pool_scale, v_dn_conv, v_dn_A_log, v_dn_dt_bias, v_dn_out_norm, v_w_branch, v_w_out, v_final_norm):
    names = ("ffn_norm", "ffn_w_gate", "ffn_w_up", "ffn_w_down", "mix_norm", "w_in", "b_gate", "pool_w", "pool_scale", "dn_conv",
             "dn_A_log", "dn_dt_bias", "dn_out_norm", "w_branch", "w_out", "final_norm")
    wts = dict(zip(names, (ffn_norm, ffn_w_gate, ffn_w_up, ffn_w_down, mix_norm, w_in, b_gate, pool_w, pool_scale, dn_conv,
                           dn_A_log, dn_dt_bias, dn_out_norm, w_branch, w_out, final_norm)))
    ms = dict(zip(names, (m_ffn_norm, m_ffn_w_gate, m_ffn_w_up, m_ffn_w_down, m_mix_norm, m_w_in, m_b_gate, m_pool_w, m_pool_scale,
                          m_dn_conv, m_dn_A_log, m_dn_dt_bias, m_dn_out_norm, m_w_branch, m_w_out, m_final_norm)))
    vs = dict(zip(names, (v_ffn_norm, v_ffn_w_gate, v_ffn_w_up, v_ffn_w_down, v_mix_norm, v_w_in, v_b_gate, v_pool_w, v_pool_scale,
                          v_dn_conv, v_dn_A_log, v_dn_dt_bias, v_dn_out_norm, v_w_branch, v_w_out, v_final_norm)))
    B, S, D = x.shape
    T = B * S
    L = ffn_w_gate.shape[0]
    fs, ds_, cs = ffn_w_gate.shape[3], D // N_CHIPS, dn_conv.shape[2]
    fp = _hidden_pad(fs)
    chip = _chip_index()

    shards = [[_pack_wgu(ffn_w_gate[l], ffn_w_up[l]), _pack_m(ffn_w_down[l], w_out[l], w_branch[l]), w_in[l].astype(bf16)]
              for l in range(L)]
    small_shard, soffs = _small_pack([ffn_norm, dn_conv])
    small_g = _gather_chips([small_shard], "gather_small")[0]

    def chip_major(i, a):
        rows = -(-a.size // PACK_W)
        return small_g[:, soffs[i]:soffs[i] + rows].reshape(N_CHIPS, -1)[:, :a.size].reshape((N_CHIPS,) + a.shape)

    fn_full = jnp.moveaxis(chip_major(0, ffn_norm), 0, 2).reshape(L, 2, D)
    conv_full = jnp.moveaxis(chip_major(1, dn_conv), 0, 2).reshape(L, dn_conv.shape[1], N_CHIPS * cs)

    def small_params(l):
        return dict(ffn_norm=fn_full[l], mix_norm=mix_norm[l], b_gate=b_gate[l], pool_w=pool_w[l], pool_scale=pool_scale[l],
                    dn_conv=conv_full[l], dn_A_log=dn_A_log[l], dn_dt_bias=dn_dt_bias[l], dn_out_norm=dn_out_norm[l])

    xt = x.reshape(T, D)
    gathered = _gather_chips(shards[0], "gather_l0")
    layers, saved = [], []
    for l in range(L):
        w = _layer_weights(*gathered, small_params(l))
        more = l + 1 < L
        carry = [_ChipExchange("gather", [s]) if more else None for s in shards[min(l + 1, L - 1)]]
        xt, sv, (landed_dn, landed_sb, landed_ffn) = _layer_fwd(xt, w, B, f"l{l}", (carry[0], carry[1], carry[2]))
        layers.append(w)
        saved.append(sv)
        if more:
            gathered = _gather_finish(shards[l + 1], [landed_dn[0], landed_sb[0], landed_ffn[0]], f"gather_pass_l{l + 1}")
    dx, dfn8, ls8 = _final_loss(xt, loss_target.reshape(T, D), final_norm)
    loss = lax.psum(jnp.sum(ls8), ("x", "y", "c")) * (0.5 / D)

    grads, red, pending = [None] * L, [None] * L, None
    for l in reversed(range(L)):
        carry = (None, None) if pending is None else (_ChipExchange("scatter", pending[:2]), _ChipExchange("scatter", pending[2:]))
        dx, grads[l], (landed_dn, landed_sb) = _layer_bwd(dx, layers[l], saved[l], B, f"l{l}", carry)
        if pending is not None:
            red[l + 1] = _reduce_finish(pending, list(landed_dn) + list(landed_sb), f"l{l + 1}")
        pending = _reduce_begin([grads[l]["A"], grads[l]["M"], grads[l]["C"]], f"l{l}")
    red[0] = _reduce_finish(pending, _ChipExchange("scatter", pending).standalone("rs_scatter_l0"), "l0")

    small_names = SMALL_SHARDED + SMALL_REPLICATED
    pieces = [g[k] for g in grads for k in small_names] + [jnp.sum(dfn8, axis=0)]
    spack, offs = _small_pack(pieces)
    sred = _reduce_to_chips([jnp.broadcast_to(spack[None], (N_CHIPS,) + spack.shape)], "small")[0]
    small_red = _small_unpack(sred, offs, [p.shape for p in pieces])

    gw = {k: [] for k in names if k != "final_norm"}
    for l in range(L):
        ga, gm, gc = red[l]
        gu = ga.reshape(2, 2, D, fp)[..., :fs]
        gw["ffn_w_gate"].append(gu[:, 0])
        gw["ffn_w_up"].append(gu[:, 1])
        gw["ffn_w_down"].append(gm[:2 * fp].reshape(2, fp, D)[:, :fs])
        gw["w_out"].append(gm[2 * fp:2 * fp + ds_])
        gw["w_branch"].append(gm[2 * fp + ds_:].reshape(3, BRANCH, ds_))
        gw["w_in"].append(gc)
        sm = dict(zip(small_names, small_red[l * len(small_names):(l + 1) * len(small_names)]))
        gw["ffn_norm"].append(lax.dynamic_slice_in_dim(sm["ffn_norm"], chip * ds_, ds_, axis=1))
        gw["dn_conv"].append(lax.dynamic_slice_in_dim(sm["dn_conv"], chip * cs, cs, axis=1))
        for k in SMALL_REPLICATED:
            gw[k].append(sm[k])
    gw = {k: jnp.stack(v) for k, v in gw.items()}
    gw["final_norm"] = small_red[-1]

    deltas, new_m, new_v = [], [], []
    for k in names:
        d, mo, vo = _adamw(wts[k], gw[k], ms[k], vs[k], f"adamw_{k}")
        deltas.append(d)
        new_m.append(mo)
        new_v.append(vo)
    return (loss, dx[0].reshape(B, S, D), *[gw[k] for k in names], *deltas, *new_m, *new_v)
```

```python
import functools
import math

import jax
import jax.numpy as jnp
from jax import lax
from jax.experimental import pallas as pl
from jax.experimental.pallas import tpu as pltpu

f32 = jnp.float32
bf16 = jnp.bfloat16
HI = lax.Precision.HIGHEST
MID = lax.Precision.HIGH

EPS = 1e-6
HEAD = 128
N_HEADS = 4
BRANCH = 512
CHUNK = 64
SB_BLOCK = 128
SB_QUERIES = 512
POOL_WINDOWS = (2, 4, 8, 16)
N_CHIPS = 4
LANES = 128
PACK_W = 1024
ADAM_LR, ADAM_B1, ADAM_B2, ADAM_EPS, ADAM_WD, ADAM_STEP = 0.001, 0.9, 0.999, 1e-08, 0.01, 10
VMEM_LIMIT = 56 * 1024 * 1024
MM_VMEM_BUDGET = 36 * 1024 * 1024

NN = (((1,), (0,)), ((), ()))
NT = (((1,), (1,)), ((), ()))
TN = (((0,), (0,)), ((), ()))


def _cp(sem=None):
    return pltpu.CompilerParams(dimension_semantics=sem, vmem_limit_bytes=VMEM_LIMIT)


def _tile(n, prefs):
    for p in prefs:
        if n % p == 0:
            return p
    return n


def _row_tile(rows, width, max_elems):
    for d in range(rows, 0, -1):
        if rows % d == 0 and (d % 16 == 0 or d == rows) and d * width <= max_elems:
            return d
    return rows


def _dot(a, b, dn=NN):
    return lax.dot_general(a, b, dn, preferred_element_type=f32)


def _sigmoid(x):
    return 0.5 * jnp.tanh(0.5 * x) + 0.5


def _softplus(x):
    return jnp.maximum(x, 0.0) + jnp.log1p(jnp.exp(-jnp.abs(x)))


def _mm(pairs, M, N, out_dtype, name, tm=None, tn=None, res=None, scale=1.0, out_chip=False):
    tn = N // N_CHIPS if out_chip else (tn or _tile(N, (512, 256, 128)))

    def vmem_bytes(rows):
        total = rows * tn * (2 * jnp.dtype(out_dtype).itemsize + (8 if res is not None else 0) + 8)
        for form, a, b, o in pairs:
            K = a.shape[-2] if form == "tn" else (o.get("K") or a.shape[-1])
            total += 2 * K * (rows * a.dtype.itemsize + tn * b.dtype.itemsize)
        return total

    tm = tm or next((t for t in (1024, 512, 256, 128) if M % t == 0 and vmem_bytes(t) <= MM_VMEM_BUDGET), M)
    specs, arrs, dns = [], [], []

    def lead(sel, shape, imap):
        if sel is None:
            return pl.BlockSpec(shape, imap)
        return pl.BlockSpec((None,) + shape, lambda i, j, sel=sel, imap=imap: (sel,) + imap(i, j))

    for form, a, b, o in pairs:
        ka, kb, moff, noff = o.get("ka", 0), o.get("kb", 0), o.get("moff", 0), o.get("noff", 0)
        asel, bsel = o.get("asel"), o.get("bsel")
        if form == "nn":
            K = o.get("K") or a.shape[-1]
            sa = lead(asel, (tm, K), lambda i, j, ka=ka, moff=moff: (i + moff, ka))
            sb = lead(bsel, (K, tn), lambda i, j, kb=kb, noff=noff: (kb, j + noff))
            dn = NN
        elif form == "nt":
            K = o.get("K") or a.shape[-1]
            sa = lead(asel, (tm, K), lambda i, j, ka=ka, moff=moff: (i + moff, ka))
            if o.get("b_by_chip"):
                sb = pl.BlockSpec((None, tn, K), lambda i, j, kb=kb, noff=noff: (j, noff, kb))
            else:
                sb = lead(bsel, (tn, K), lambda i, j, kb=kb, noff=noff: (j + noff, kb))
            dn = NT
        else:
            K = a.shape[-2]
            sa = lead(asel, (K, tm), lambda i, j, moff=moff: (0, i + moff))
            sb = lead(bsel, (K, tn), lambda i, j, noff=noff: (0, j + noff))
            dn = TN
        specs += [sa, sb]
        arrs += [a, b]
        dns.append(dn)
    if res is not None:
        specs.append(pl.BlockSpec((tm, tn), lambda i, j: (i, j)))
        arrs.append(res)
    n = len(pairs)

    def body(*refs):
        o_ref = refs[-1]
        acc = None
        for p in range(n):
            d = _dot(refs[2 * p][...].astype(bf16), refs[2 * p + 1][...].astype(bf16), dns[p])
            acc = d if acc is None else acc + d
        if scale != 1.0:
            acc = acc * scale
        if res is not None:
            acc = acc + refs[2 * n][...]
        o_ref[...] = acc.astype(o_ref.dtype)

    if out_chip:
        out_spec = pl.BlockSpec((None, tm, tn), lambda i, j: (j, i, 0))
        out_shape = jax.ShapeDtypeStruct((N_CHIPS, M, tn), out_dtype)
    else:
        out_spec = pl.BlockSpec((tm, tn), lambda i, j: (i, j))
        out_shape = jax.ShapeDtypeStruct((M, N), out_dtype)
    return pl.pallas_call(
        body, name=name, grid=(M // tm, N // tn), in_specs=specs, out_specs=out_spec, out_shape=out_shape,
        compiler_params=_cp(("parallel", "parallel")),
    )(*arrs)


def _rms_fwd(x, g, name):
    T, D = x.shape
    tm = _tile(T, (512, 256, 128))

    def body(x_ref, g_ref, h_ref):
        xv = x_ref[...]
        r = lax.rsqrt(jnp.mean(xv * xv, axis=-1, keepdims=True) + EPS)
        h_ref[...] = (xv * r * g_ref[...]).astype(bf16)

    return pl.pallas_call(
        body, name=name, grid=(T // tm,),
        in_specs=[pl.BlockSpec((tm, D), lambda i: (i, 0)), pl.BlockSpec((1, D), lambda i: (0, 0))],
        out_specs=pl.BlockSpec((tm, D), lambda i: (i, 0)),
        out_shape=jax.ShapeDtypeStruct((T, D), bf16), compiler_params=_cp(("parallel",)),
    )(x, g.reshape(1, D))


def _fold8(v):
    r, d = v.shape
    return jnp.sum(v.reshape(r // 8, 8, d), axis=0)


def _rms_bwd(dh, x, g, dres, name):
    T, D = x.shape
    tm = _tile(T, (512, 256, 128))

    def body(dh_ref, x_ref, g_ref, dres_ref, dx_ref, dxb_ref, dg_ref):
        xv = x_ref[...]
        r = lax.rsqrt(jnp.mean(xv * xv, axis=-1, keepdims=True) + EPS)
        xh = xv * r
        dhv = dh_ref[...]
        dxh = dhv * g_ref[...]
        dx = dres_ref[...] + r * (dxh - xh * jnp.mean(dxh * xh, axis=-1, keepdims=True))
        dx_ref[...] = dx
        dxb_ref[...] = dx.astype(bf16)

        @pl.when(pl.program_id(0) == 0)
        def _():
            dg_ref[...] = jnp.zeros_like(dg_ref)

        dg_ref[...] += _fold8(dhv * xh)

    row = pl.BlockSpec((tm, D), lambda i: (i, 0))
    dx, dxb, dg8 = pl.pallas_call(
        body, name=name, grid=(T // tm,), in_specs=[row, row, pl.BlockSpec((1, D), lambda i: (0, 0)), row],
        out_specs=[row, row, pl.BlockSpec((8, D), lambda i: (0, 0))],
        out_shape=[jax.ShapeDtypeStruct((T, D), f32), jax.ShapeDtypeStruct((T, D), bf16), jax.ShapeDtypeStruct((8, D), f32)],
        compiler_params=_cp(("arbitrary",)),
    )(dh, x, g.reshape(1, D), dres)
    return (dx, dxb), dg8


def _final_loss(x, target, g):
    T, D = x.shape
    tm = _tile(T, (512, 256, 128))

    def body(x_ref, t_ref, g_ref, dx_ref, dxb_ref, dg_ref, ls_ref):
        xv = x_ref[...]
        r = lax.rsqrt(jnp.mean(xv * xv, axis=-1, keepdims=True) + EPS)
        xh = xv * r
        gv = g_ref[...]
        e = xh * gv - t_ref[...]
        dy = e * (1.0 / D)
        dxh = dy * gv
        dx = r * (dxh - xh * jnp.mean(dxh * xh, axis=-1, keepdims=True))
        dx_ref[...] = dx
        dxb_ref[...] = dx.astype(bf16)

        @pl.when(pl.program_id(0) == 0)
        def _():
            dg_ref[...] = jnp.zeros_like(dg_ref)
            ls_ref[...] = jnp.zeros_like(ls_ref)

        dg_ref[...] += _fold8(dy * xh)
        ls_ref[...] += _fold8(e * e)

    row = pl.BlockSpec((tm, D), lambda i: (i, 0))
    acc = pl.BlockSpec((8, D), lambda i: (0, 0))
    dx, dxb, dg8, ls8 = pl.pallas_call(
        body, name="final_loss", grid=(T // tm,), in_specs=[row, row, pl.BlockSpec((1, D), lambda i: (0, 0))],
        out_specs=[row, row, acc, acc],
        out_shape=[jax.ShapeDtypeStruct((T, D), f32), jax.ShapeDtypeStruct((T, D), bf16), jax.ShapeDtypeStruct((8, D), f32),
                   jax.ShapeDtypeStruct((8, D), f32)],
        compiler_params=_cp(("arbitrary",)),
    )(x, target, g.reshape(1, D))
    return (dx, dxb), dg8, ls8


def _ffn_up(h, wgu, f, name, carry=None):
    T, D = h.shape
    FP = wgu.shape[2]
    tm = _tile(T, (1024, 512, 256, 128))

    def body(h_ref, wg_ref, wu_ref, a_ref, b_ref, m_ref):
        hv = h_ref[...]
        a = _dot(hv, wg_ref[...])
        b = _dot(hv, wu_ref[...])
        a_ref[...] = a.astype(bf16)
        b_ref[...] = b.astype(bf16)
        m_ref[...] = (a * _sigmoid(a) * b).astype(bf16)

    o = jax.ShapeDtypeStruct((T, N_CHIPS * FP), bf16)
    ospec = pl.BlockSpec((tm, FP), lambda j, i: (i, j))
    grid = (N_CHIPS, T // tm)
    body, cin, cout, cshapes, cscratch = _carried(body, 3, 3, grid, carry)
    outs = pl.pallas_call(
        body, name=name, grid=grid,
        in_specs=[pl.BlockSpec((tm, D), lambda j, i: (i, 0)), pl.BlockSpec((None, D, FP), lambda j, i: (j, 2 * f, 0)),
                  pl.BlockSpec((None, D, FP), lambda j, i: (j, 2 * f + 1, 0))] + cin,
        out_specs=[ospec, ospec, ospec] + cout, out_shape=[o, o, o] + cshapes, scratch_shapes=cscratch,
        compiler_params=_cp(("arbitrary", "arbitrary") if carry else ("parallel", "parallel")),
    )(h, wgu, wgu, *(carry.ins if carry else []))
    return outs[:3], outs[3:]


def _ffn_bwd_mid(dx, m, a, b, f, name):
    T, D = dx.shape
    FP = a.shape[1] // N_CHIPS
    tm = _tile(T, (1024, 512, 256, 128))

    def body(dx_ref, wd_ref, a_ref, b_ref, da_ref, db_ref, m_ref):
        dm = 0.5 * _dot(dx_ref[...].astype(bf16), wd_ref[...], NT)
        av = a_ref[...].astype(f32)
        bv = b_ref[...].astype(f32)
        s = _sigmoid(av)
        silu = av * s
        da_ref[...] = (dm * bv * (s * (1.0 + av * (1.0 - s)))).astype(bf16)
        db_ref[...] = (dm * silu).astype(bf16)
        m_ref[...] = (silu * bv).astype(bf16)

    o = jax.ShapeDtypeStruct((T, N_CHIPS * FP), bf16)
    ospec = pl.BlockSpec((tm, FP), lambda j, i: (i, j))
    return pl.pallas_call(
        body, name=name, grid=(N_CHIPS, T // tm),
        in_specs=[pl.BlockSpec((tm, D), lambda j, i: (i, 0)), pl.BlockSpec((None, FP, D), lambda j, i: (j, f, 0)), ospec, ospec],
        out_specs=[ospec, ospec, ospec], out_shape=[o, o, o], compiler_params=_cp(("parallel", "parallel")),
    )(dx, m, a, b)


def _shift_down(x, k, row):
    return jnp.where(row < k, 0.0, pltpu.roll(x, k, 0))


def _shift_up(x, k, row):
    s = x.shape[0]
    return jnp.where(row >= s - k, 0.0, pltpu.roll(x, s - k, 0))


def _window_sum(x, g, row, shift):
    s2 = x + shift(x, 1, row)
    s4 = s2 + shift(s2, 2, row)
    s8 = s4 + shift(s4, 4, row)
    s16 = s8 + shift(s8, 8, row)
    return jnp.where(g == 0, s2, jnp.where(g == 1, s4, jnp.where(g == 2, s8, s16)))


def _pool_fwd(proj, pool_w, scale, name):
    B, S = proj.shape[0], proj.shape[1]
    G = len(POOL_WINDOWS)

    def body(u_ref, w_ref, sc_ref, y_ref):
        g = pl.program_id(1)
        u = u_ref[0].astype(f32)
        row = lax.broadcasted_iota(jnp.int32, u.shape, 0)
        win = _window_sum(u, g, row, _shift_down)
        cnt = jnp.minimum(row + 1, jnp.left_shift(2, g)).astype(f32)
        pooled = win / cnt - u
        mixed = _dot(pooled.astype(bf16), w_ref[0].astype(bf16))
        y_ref[0] = (mixed * sc_ref[...]).astype(bf16)

    return pl.pallas_call(
        body, name=name, grid=(B, G),
        in_specs=[pl.BlockSpec((1, S, HEAD), lambda b, g: (b, 0, g)), pl.BlockSpec((1, HEAD, HEAD), lambda b, g: (g, 0, 0)),
                  pl.BlockSpec((1, HEAD), lambda b, g: (0, g))],
        out_specs=pl.BlockSpec((1, S, HEAD), lambda b, g: (b, 0, g)),
        out_shape=jax.ShapeDtypeStruct((B, S, BRANCH), bf16), compiler_params=_cp(("parallel", "parallel")),
    )(proj, pool_w, scale.reshape(1, BRANCH))


def _pool_bwd(proj, dy, pool_w, scale, name):
    B, S = proj.shape[0], proj.shape[1]
    G = len(POOL_WINDOWS)

    def body(u_ref, dy_ref, w_ref, sc_ref, du_ref, dw_ref, dsc_ref):
        g = pl.program_id(0)
        u = u_ref[0].astype(f32)
        row = lax.broadcasted_iota(jnp.int32, u.shape, 0)
        cnt = jnp.minimum(row + 1, jnp.left_shift(2, g)).astype(f32)
        pooled = _window_sum(u, g, row, _shift_down) / cnt - u
        wv = w_ref[0].astype(bf16)
        mixed = _dot(pooled.astype(bf16), wv)
        dyv = dy_ref[0].astype(f32)
        dmix = (dyv * sc_ref[...]).astype(bf16)
        dpool = _dot(dmix, wv, NT)
        du_ref[0] = (_window_sum(dpool / cnt, g, row, _shift_up) - dpool).astype(bf16)

        @pl.when(pl.program_id(1) == 0)
        def _():
            dw_ref[...] = jnp.zeros_like(dw_ref)
            dsc_ref[...] = jnp.zeros_like(dsc_ref)

        dw_ref[0] += _dot(pooled.astype(bf16), dmix, TN)
        dsc_ref[...] += _fold8(dyv * mixed)

    return pl.pallas_call(
        body, name=name, grid=(G, B),
        in_specs=[pl.BlockSpec((1, S, HEAD), lambda g, b: (b, 0, g)), pl.BlockSpec((1, S, HEAD), lambda g, b: (b, 0, g)),
                  pl.BlockSpec((1, HEAD, HEAD), lambda g, b: (g, 0, 0)), pl.BlockSpec((1, HEAD), lambda g, b: (0, g))],
        out_specs=[pl.BlockSpec((1, S, HEAD), lambda g, b: (b, 0, g)), pl.BlockSpec((1, HEAD, HEAD), lambda g, b: (g, 0, 0)),
                   pl.BlockSpec((8, HEAD), lambda g, b: (0, g))],
        out_shape=[jax.ShapeDtypeStruct((B, S, BRANCH), bf16), jax.ShapeDtypeStruct((G, HEAD, HEAD), f32),
                   jax.ShapeDtypeStruct((8, BRANCH), f32)],
        compiler_params=_cp(("parallel", "arbitrary")),
    )(proj, dy, pool_w, scale.reshape(1, BRANCH))


def _split_dot(x, u):
    hi = x.astype(bf16)
    lo = (x - hi.astype(f32)).astype(bf16)
    return _dot(hi, u) + _dot(lo, u)


def _sb_fwd(sbqkv, name, carry=None):
    B, S, _ = sbqkv.shape
    KB = SB_BLOCK
    TQ = _tile(S, (SB_QUERIES, KB))
    ns = TQ // KB
    nq = S // TQ
    scale = HEAD ** -0.5

    def body(q_ref, k_ref, v_ref, o_ref, tot_ref, run_s, acc_s):
        r = lax.broadcasted_iota(jnp.int32, (KB, KB), 0)
        c = lax.broadcasted_iota(jnp.int32, (KB, KB), 1)
        causal = c < r
        after = (r > c).astype(bf16)

        def sub(qa, krows, masked, run, acc):
            z = _dot(qa, k_ref[0, krows, :], NT) * scale
            sp = jnp.maximum(z, 0.0) + jnp.log(1.0 + jnp.exp(-jnp.abs(z)))
            ln = -sp
            if masked:
                ln = jnp.where(causal, ln, 0.0)
            w = jnp.exp(z - sp + _split_dot(ln, after) + run)
            if masked:
                w = jnp.where(causal, w, 0.0)
            acc = acc + _dot(w.astype(bf16), v_ref[0, krows, :])
            return run + jnp.sum(ln, axis=1, keepdims=True), acc

        def qloop(i, carry):
            base = pl.multiple_of(i * TQ, TQ)
            qi = q_ref[0, pl.ds(base, TQ), :]
            for a in range(ns):
                qa = qi[a * KB:(a + 1) * KB]
                run, acc = jnp.zeros((KB, LANES), f32), jnp.zeros((KB, HEAD), f32)
                for s in range(a, -1, -1):
                    run, acc = sub(qa, pl.ds(base + s * KB, KB), s == a, run, acc)
                run_s[a * KB:(a + 1) * KB, :] = run
                acc_s[a * KB:(a + 1) * KB, :] = acc

            def group(t, cr):
                g0 = pl.multiple_of((i - 1 - t) * TQ, TQ)
                for s in range(ns - 1, -1, -1):
                    cr = sub(qi, pl.ds(g0 + s * KB, KB), False, *cr)
                return cr

            run, acc = lax.fori_loop(0, i, group, (run_s[...], acc_s[...]))
            o_ref[0, pl.ds(base, TQ), :] = acc.astype(bf16)
            tot_ref[0, 0, pl.ds(base, TQ), :] = run
            return carry

        lax.fori_loop(0, nq, qloop, 0)

    def spec(off):
        return pl.BlockSpec((1, S, HEAD), lambda b, h, off=off: (b, 0, off + h))

    body, cin, cout, cshapes, cscratch = _carried(body, 3, 2, (B, N_HEADS), carry)
    outs = pl.pallas_call(
        body, name=name, grid=(B, N_HEADS), in_specs=[spec(0), spec(N_HEADS), spec(2 * N_HEADS)] + cin,
        out_specs=[pl.BlockSpec((1, S, HEAD), lambda b, h: (b, 0, h)),
                   pl.BlockSpec((1, 1, S, LANES), lambda b, h: (b, h, 0, 0))] + cout,
        out_shape=[jax.ShapeDtypeStruct((B, S, BRANCH), bf16), jax.ShapeDtypeStruct((B, N_HEADS, S, LANES), f32)] + cshapes,
        scratch_shapes=[pltpu.VMEM((TQ, LANES), f32), pltpu.VMEM((TQ, HEAD), f32)] + cscratch,
        compiler_params=_cp(("arbitrary", "arbitrary") if carry else ("parallel", "parallel")),
    )(sbqkv, sbqkv, sbqkv, *(carry.ins if carry else []))
    return outs[:2], outs[2:]


def _sb_bwd(sbqkv, do, tot, name, carry=None):
    B, S, _ = sbqkv.shape
    KB = SB_BLOCK
    TQ = _tile(S, (SB_QUERIES, KB))
    ns = TQ // KB
    nq = S // TQ
    scale = HEAD ** -0.5

    def body(q_ref, k_ref, v_ref, do_ref, tot_ref, dq_ref, dk_ref, dv_ref, dk_acc, dv_acc):
        r = lax.broadcasted_iota(jnp.int32, (KB, KB), 0)
        c = lax.broadcasted_iota(jnp.int32, (KB, KB), 1)
        wide_r = lax.broadcasted_iota(jnp.int32, (TQ, KB), 0)
        wide_c = lax.broadcasted_iota(jnp.int32, (TQ, KB), 1)
        after = (r > c).astype(bf16)
        before = (r < c).astype(bf16)
        dk_acc[...] = jnp.zeros_like(dk_acc)
        dv_acc[...] = jnp.zeros_like(dv_acc)

        def sub(qi, doi, total, rows, mask, cl, cp, dq):
            kj = k_ref[0, rows, :]
            vj = v_ref[0, rows, :]
            z = _dot(qi, kj, NT) * scale
            sp = jnp.maximum(z, 0.0) + jnp.log(1.0 + jnp.exp(-jnp.abs(z)))
            ln = -sp
            if mask is not None:
                ln = jnp.where(mask, ln, 0.0)
            bs = jnp.sum(ln, axis=1, keepdims=True)
            w = jnp.exp(z - sp + _split_dot(ln, after) + (total - cl - bs))
            if mask is not None:
                w = jnp.where(mask, w, 0.0)
            p = _dot(doi, vj, NT) * w
            qsum = cp + _split_dot(p, before)
            sig = jnp.exp(z - sp)
            dz = (p - (p + qsum) * sig) * scale
            if mask is not None:
                dz = jnp.where(mask, dz, 0.0)
            dzb = dz.astype(bf16)
            dq = dq + _dot(dzb, kj)
            dk_acc[rows, :] += _dot(dzb, qi, TN)
            dv_acc[rows, :] += _dot(w.astype(bf16), doi, TN)
            return cl + bs, cp + jnp.sum(p, axis=1, keepdims=True), dq

        def qloop(i, carry):
            base = pl.multiple_of(i * TQ, TQ)
            rows = pl.ds(base, TQ)
            qi = q_ref[0, rows, :]
            doi = do_ref[0, rows, :]
            total = tot_ref[0, 0, rows, :][:, 0:1]
            zero = jnp.zeros((TQ, 1), f32)

            def group(g, st):
                g0 = pl.multiple_of(g * TQ, TQ)
                for s in range(ns):
                    st = sub(qi, doi, total, pl.ds(g0 + s * KB, KB), None, *st)
                return st

            st = lax.fori_loop(0, i, group, (zero, zero, jnp.zeros((TQ, HEAD), f32)))
            for s in range(ns):
                st = sub(qi, doi, total, pl.ds(base + s * KB, KB), (s * KB + wide_c) < wide_r, *st)
            dq_ref[0, rows, :] = st[2].astype(bf16)
            return carry

        lax.fori_loop(0, nq, qloop, 0)
        dk_ref[0] = dk_acc[...].astype(bf16)
        dv_ref[0] = dv_acc[...].astype(bf16)

    def spec(off):
        return pl.BlockSpec((1, S, HEAD), lambda b, h, off=off: (b, 0, off + h))

    o = jax.ShapeDtypeStruct((B, S, BRANCH), bf16)
    body, cin, cout, cshapes, cscratch = _carried(body, 5, 3, (B, N_HEADS), carry)
    outs = pl.pallas_call(
        body, name=name, grid=(B, N_HEADS),
        in_specs=[spec(0), spec(N_HEADS), spec(2 * N_HEADS), spec(0),
                  pl.BlockSpec((1, 1, S, LANES), lambda b, h: (b, h, 0, 0))] + cin,
        out_specs=[spec(0), spec(0), spec(0)] + cout, out_shape=[o, o, o] + cshapes,
        scratch_shapes=[pltpu.VMEM((S, HEAD), f32), pltpu.VMEM((S, HEAD), f32)] + cscratch,
        compiler_params=_cp(("arbitrary", "arbitrary") if carry else ("parallel", "parallel")),
    )(sbqkv, sbqkv, sbqkv, do, tot, *(carry.ins if carry else []))
    return outs[:3], outs[3:]


def _dn_params(a_log, dt_bias):
    p = jnp.zeros((8, LANES), f32)
    p = p.at[0, :N_HEADS].set(a_log)
    return p.at[1, :N_HEADS].set(dt_bias)


def _dn_prep(ab, par, name):
    B, S, _ = ab.shape
    R = 2 * CHUNK
    nt = S // R

    def body(ab_ref, par_ref, gcb_ref, bb_ref, gcr_ref):
        x = ab_ref[0]
        g = -jnp.exp(par_ref[0:1, :]) * _softplus(x + par_ref[1:2, :])
        r = lax.broadcasted_iota(jnp.int32, (R, R), 0)
        c = lax.broadcasted_iota(jnp.int32, (R, R), 1)
        tri = ((r >= c) & ((r >> 6) == (c >> 6))).astype(f32)
        cs = jnp.dot(tri, g, precision=HI, preferred_element_type=f32)
        beta = _sigmoid(x)
        cst = cs.T
        for h in range(N_HEADS):
            gcb_ref[0, h] = jnp.broadcast_to(cs[:, h:h + 1], (R, LANES))
            bb_ref[0, h] = jnp.broadcast_to(beta[:, N_HEADS + h:N_HEADS + h + 1], (R, LANES))
            gcr_ref[0, h, 0] = jnp.broadcast_to(cst[h:h + 1, 0:CHUNK], (8, CHUNK))
            gcr_ref[0, h, 1] = jnp.broadcast_to(cst[h:h + 1, CHUNK:R], (8, CHUNK))

    return pl.pallas_call(
        body, name=name, grid=(B, nt),
        in_specs=[pl.BlockSpec((1, R, LANES), lambda b, i: (b, i, 0)), pl.BlockSpec((8, LANES), lambda b, i: (0, 0))],
        out_specs=[pl.BlockSpec((1, N_HEADS, R, LANES), lambda b, i: (b, 0, i, 0)),
                   pl.BlockSpec((1, N_HEADS, R, LANES), lambda b, i: (b, 0, i, 0)),
                   pl.BlockSpec((1, N_HEADS, 2, 8, CHUNK), lambda b, i: (b, 0, i, 0, 0))],
        out_shape=[jax.ShapeDtypeStruct((B, N_HEADS, S, LANES), f32), jax.ShapeDtypeStruct((B, N_HEADS, S, LANES), f32),
                   jax.ShapeDtypeStruct((B, N_HEADS, S // CHUNK, 8, CHUNK), f32)],
        compiler_params=_cp(("parallel", "parallel")),
    )(ab, par)


def _bmm(a, b, prec=None):
    return jnp.einsum("nij,njk->nik", a, b, preferred_element_type=f32, precision=prec)


def _bmm_nt(a, b, prec=None):
    return jnp.einsum("nik,njk->nij", a, b, preferred_element_type=f32, precision=prec)


def _bmm_tn(a, b, prec=None):
    return jnp.einsum("nki,nkj->nij", a, b, preferred_element_type=f32, precision=prec)


def _tri_inv(L):
    C = L.shape[-1]
    r = lax.broadcasted_iota(jnp.int32, (C, C), 0)
    c = lax.broadcasted_iota(jnp.int32, (C, C), 1)
    eye = (r == c).astype(f32)
    bd16 = (r >> 4) == (c >> 4)
    bd32 = (r >> 5) == (c >> 5)
    mm = functools.partial(_bmm, prec=MID)
    n1 = -jnp.where(bd16, L, 0.0)
    n2 = mm(n1, n1)
    n4 = mm(n2, n2)
    n8 = mm(n4, n4)
    t = mm(mm(mm(eye + n1, eye + n2), eye + n4), eye + n8)
    t = t - mm(mm(t, jnp.where(bd32 & jnp.logical_not(bd16), L, 0.0)), t)
    t = t - mm(mm(t, jnp.where(bd32, 0.0, L)), t)
    return t


def _conv_silu(x, w, row):
    c = w[3:4] * x + w[2:3] * _shift_down(x, 1, row) + w[1:2] * _shift_down(x, 2, row) + w[0:1] * _shift_down(x, 3, row)
    return c, c * _sigmoid(c)


def _dn_intra(qn, kn, v, gcb, beta, gr):
    C = CHUNK
    r = lax.broadcasted_iota(jnp.int32, (C, C), 0)
    c = lax.broadcasted_iota(jnp.int32, (C, C), 1)
    incl = r >= c
    diff = gcb[:, :, :C] - gr
    dm = jnp.where(incl, jnp.exp(jnp.where(incl, diff, 0.0)), 0.0)
    ds = jnp.where(r > c, dm, 0.0)
    kb = kn * beta
    knb = kn.astype(bf16)
    L = _bmm_nt(kb.astype(bf16), knb) * ds
    eg = jnp.exp(gcb)
    a = _bmm_nt(qn.astype(bf16), knb) * dm
    gl = gcb[:, C - 1:C, :]
    ekd = jnp.exp(gl - gcb)
    return dict(dm=dm, ds=ds, kb=kb, L=L, eg=eg, rhs_u=v * beta, rhs_w=kb * eg, a=a,
                qd=qn * eg, kd=kn * ekd, ekd=ekd, cd=jnp.exp(gl))


def _dn_specs(S):
    def col(off):
        return pl.BlockSpec((1, S, HEAD), lambda b, h, off=off: (b, 0, off + h))

    def cw(off):
        return pl.BlockSpec((4, HEAD), lambda b, h, off=off: (0, off + h))

    per_head = pl.BlockSpec((1, 1, S, LANES), lambda b, h: (b, h, 0, 0))
    rowform = pl.BlockSpec((1, 1, S // CHUNK, 8, CHUNK), lambda b, h: (b, h, 0, 0, 0))
    gain = pl.BlockSpec((1, HEAD), lambda b, h: (0, 0))
    ins = [col(4), col(8), col(12), col(16), cw(0), cw(4), cw(8), per_head, per_head, rowform, gain]
    return ins, per_head


def _dn_chunked_specs(n):
    return [pl.BlockSpec((1, 1, n, CHUNK, CHUNK), lambda b, h: (b, h, 0, 0, 0)),
            pl.BlockSpec((1, 1, n, CHUNK, HEAD), lambda b, h: (b, h, 0, 0, 0)),
            pl.BlockSpec((1, 1, n, CHUNK, HEAD), lambda b, h: (b, h, 0, 0, 0))]


def _dn_act(x_ref, cw_ref, row, normalise, out_scale=1.0):
    _, act = _conv_silu(x_ref[0].astype(f32), cw_ref[...], row)
    if normalise:
        act = act * (lax.rsqrt(jnp.sum(act * act, axis=-1, keepdims=True) + EPS) * out_scale)
    return act


def _dn_group(qn_s, kn_s, v_s, gcb_ref, bb_ref, gcr_ref, g, ng):
    C = CHUNK
    rows = pl.ds(pl.multiple_of(g * (ng * C), ng * C), ng * C)
    ch = pl.ds(g * ng, ng)
    sh = (ng, C, HEAD)
    qn, kn, v = qn_s[rows, :].reshape(sh), kn_s[rows, :].reshape(sh), v_s[rows, :].reshape(sh)
    gcb3, beta = gcb_ref[0, 0, rows, :].reshape(sh), bb_ref[0, 0, rows, :].reshape(sh)
    gr = gcr_ref[0, 0, ch][:, 0:1, :]
    it = _dn_intra(qn, kn, v, gcb3, beta, gr)
    it.update(qn=qn, kn=kn, v=v, gcb=gcb3, beta=beta, gr=gr)
    return rows, ch, it


def _dn_fwd(proj, conv_w, gcb, betab, gcr, gain, name, carry=None):
    B, S, _ = proj.shape
    n, C = S // CHUNK, CHUNK
    ng = min(8, n)
    ins, per_head = _dn_specs(S)

    def body(q_ref, k_ref, v_ref, z_ref, cq_ref, ck_ref, cv_ref, gcb_ref, bb_ref, gcr_ref, gain_ref,
             y_ref, st_ref, vn_ref, t_ref, ub_ref, wb_ref, qn_s, kn_s, v_s, u_s, w_s, qd_s, a_s, cd_s, g_s, h_s):
        row = lax.broadcasted_iota(jnp.int32, (S, HEAD), 0)
        qn_s[...] = _dn_act(q_ref, cq_ref, row, True, HEAD ** -0.5)
        kn_s[...] = _dn_act(k_ref, ck_ref, row, True)
        v_s[...] = _dn_act(v_ref, cv_ref, row, False)

        def group(g, carry):
            _, ch, it = _dn_group(qn_s, kn_s, v_s, gcb_ref, bb_ref, gcr_ref, g, ng)
            t = _tri_inv(it["L"])
            u = _bmm(t, it["rhs_u"], MID)
            wb = _bmm(t, it["rhs_w"], MID).astype(bf16)
            kdb = it["kd"].astype(bf16)
            ub = u.astype(bf16)
            u_s[ch] = u
            w_s[ch] = wb
            t_ref[0, 0, ch] = t
            ub_ref[0, 0, ch] = ub
            wb_ref[0, 0, ch] = wb
            qd_s[ch] = it["qd"].astype(bf16)
            a_s[ch] = it["a"].astype(bf16)
            cd_s[ch] = it["cd"]
            g_s[ch] = _bmm_tn(kdb, wb).astype(bf16)
            h_s[ch] = _bmm_tn(kdb, ub)
            return carry

        lax.fori_loop(0, n // ng, group, 0)

        def step(i, st):
            sb = st.astype(bf16)
            st_ref[0, 0, i] = sb
            return st * cd_s[i] - _dot(g_s[i], sb) + h_s[i]

        lax.fori_loop(0, n, step, jnp.zeros((HEAD, HEAD), f32))

        def group_out(g, carry):
            rows = pl.ds(pl.multiple_of(g * (ng * C), ng * C), ng * C)
            ch = pl.ds(g * ng, ng)
            sn = st_ref[0, 0, ch]
            vn = (u_s[ch] - _bmm(w_s[ch], sn)).astype(bf16)
            vn_ref[0, 0, rows, :] = vn.reshape(ng * C, HEAD)
            o = (_bmm(qd_s[ch], sn) + _bmm(a_s[ch], vn)).reshape(ng * C, HEAD)
            zz = z_ref[0, rows, :].astype(f32)
            rr = lax.rsqrt(jnp.mean(o * o, axis=-1, keepdims=True) + EPS)
            y_ref[0, rows, :] = (o * rr * gain_ref[...] * (zz * _sigmoid(zz))).astype(bf16)
            return carry

        lax.fori_loop(0, n // ng, group_out, 0)

    seq = pltpu.VMEM((S, HEAD), f32)
    body, cin, cout, cshapes, cscratch = _carried(body, len(ins), 6, (B, N_HEADS), carry)
    chunked = _dn_chunked_specs(n)
    outs = pl.pallas_call(
        body, name=name, grid=(B, N_HEADS), in_specs=ins + cin,
        out_specs=[pl.BlockSpec((1, S, HEAD), lambda b, h: (b, 0, h)),
                   pl.BlockSpec((1, 1, n, HEAD, HEAD), lambda b, h: (b, h, 0, 0, 0)), per_head] + chunked + cout,
        out_shape=[jax.ShapeDtypeStruct((B, S, BRANCH), bf16), jax.ShapeDtypeStruct((B, N_HEADS, n, HEAD, HEAD), bf16),
                   jax.ShapeDtypeStruct((B, N_HEADS, S, HEAD), bf16), jax.ShapeDtypeStruct((B, N_HEADS, n, C, C), f32),
                   jax.ShapeDtypeStruct((B, N_HEADS, n, C, HEAD), bf16), jax.ShapeDtypeStruct((B, N_HEADS, n, C, HEAD), bf16)]
        + cshapes,
        scratch_shapes=[seq, seq, seq, pltpu.VMEM((n, C, HEAD), f32), pltpu.VMEM((n, C, HEAD), bf16),
                        pltpu.VMEM((n, C, HEAD), bf16), pltpu.VMEM((n, C, C), bf16), pltpu.VMEM((n, 1, HEAD), f32),
                        pltpu.VMEM((n, HEAD, HEAD), bf16), pltpu.VMEM((n, HEAD, HEAD), f32)] + cscratch,
        compiler_params=_cp(("arbitrary", "arbitrary") if carry else ("parallel", "parallel")),
    )(proj, proj, proj, proj, conv_w, conv_w, conv_w, gcb, betab, gcr, gain.reshape(1, HEAD), *(carry.ins if carry else []))
    return outs[:6], outs[6:]


def _rowsum(x):
    return jnp.sum(x, axis=-1, keepdims=True)


def _dn_bwd(proj, dy, conv_w, gcb, betab, gcr, gain, states, vnew, tinv, ub, wb, name, carry=None):
    B, S, _ = proj.shape
    n, C = S // CHUNK, CHUNK
    ng = min(8, n)
    ins, per_head = _dn_specs(S)
    ins = ins + [pl.BlockSpec((1, S, HEAD), lambda b, h: (b, 0, h)),
                 pl.BlockSpec((1, 1, n, HEAD, HEAD), lambda b, h: (b, h, 0, 0, 0)), per_head] + _dn_chunked_specs(n)

    def body(q_ref, k_ref, v_ref, z_ref, cq_ref, ck_ref, cv_ref, gcb_ref, bb_ref, gcr_ref, gain_ref, dy_ref, st_ref, vn_ref,
             t_ref, ub_ref, wb_ref, dq_ref, dk_ref, dv_ref, dz_ref, dg_ref, dbeta_ref, dconv_ref, dgain_ref,
             qn_s, kn_s, v_s, cd_s, do_s, dsp_s, dvn_s, g_s, q_s):
        row = lax.broadcasted_iota(jnp.int32, (S, HEAD), 0)
        qn_s[...] = _dn_act(q_ref, cq_ref, row, True, HEAD ** -0.5)
        kn_s[...] = _dn_act(k_ref, ck_ref, row, True)
        v_s[...] = _dn_act(v_ref, cv_ref, row, False)
        dgain_ref[...] = jnp.zeros_like(dgain_ref)
        gv = gain_ref[...]

        def group_fwd(g, carry):
            rows, ch, it = _dn_group(qn_s, kn_s, v_s, gcb_ref, bb_ref, gcr_ref, g, ng)
            wb = wb_ref[0, 0, ch]
            ab, qdb = it["a"].astype(bf16), it["qd"].astype(bf16)
            vn = vn_ref[0, 0, rows, :].reshape(ng, C, HEAD)
            o = (_bmm(qdb, st_ref[0, 0, ch]) + _bmm(ab, vn)).reshape(ng * C, HEAD)
            zz = z_ref[0, rows, :].astype(f32)
            dyv = dy_ref[0, rows, :].astype(f32)
            rr = lax.rsqrt(jnp.mean(o * o, axis=-1, keepdims=True) + EPS)
            on = o * rr
            sz = _sigmoid(zz)
            dz_ref[0, rows, :] = (dyv * on * gv * (sz * (1.0 + zz * (1.0 - sz)))).astype(bf16)
            dnrm = dyv * (zz * sz)
            dgain_ref[0, 0] += _fold8(dnrm * on)
            doh = dnrm * gv
            do = rr * (doh - on * jnp.mean(doh * on, axis=-1, keepdims=True))
            dob = do.reshape(ng, C, HEAD).astype(bf16)
            atdo = _bmm_tn(ab, dob)
            cd_s[ch] = it["cd"]
            do_s[ch] = dob
            dvn_s[ch] = atdo
            g_s[ch] = _bmm_tn(it["kd"].astype(bf16), wb).astype(bf16)
            q_s[ch] = _bmm_tn(qdb, dob) - _bmm_tn(wb, atdo.astype(bf16))
            return carry

        lax.fori_loop(0, n // ng, group_fwd, 0)

        def step(t, dsp):
            i = n - 1 - t
            dspb = dsp.astype(bf16)
            dsp_s[i] = dspb
            return dsp * cd_s[i] - _dot(g_s[i], dspb, TN) + q_s[i]

        lax.fori_loop(0, n, step, jnp.zeros((HEAD, HEAD), f32))

        r = lax.broadcasted_iota(jnp.int32, (C, C), 0)
        c = lax.broadcasted_iota(jnp.int32, (C, C), 1)
        upper = r <= c

        def group_bwd(g, carry):
            rows, ch, it = _dn_group(qn_s, kn_s, v_s, gcb_ref, bb_ref, gcr_ref, g, ng)
            sh = (ng, C, HEAD)
            qn, kn, v, beta, gcb3, gr = it["qn"], it["kn"], it["v"], it["beta"], it["gcb"], it["gr"]
            sn = st_ref[0, 0, ch]
            vn = vn_ref[0, 0, rows, :].reshape(sh)
            dsp, dob = dsp_s[ch], do_s[ch]
            dvn = dvn_s[ch] + _bmm(it["kd"].astype(bf16), dsp)
            t, ub, wb = t_ref[0, 0, ch], ub_ref[0, 0, ch], wb_ref[0, 0, ch]
            dvnb = dvn.astype(bf16)
            da = _bmm_nt(dob, vn)
            dat = _bmm_nt(vn, dob)
            dqd = _bmm_nt(dob, sn)
            dkd = _bmm_nt(vn, dsp)
            dcd = jnp.sum(jnp.sum(dsp.astype(f32) * sn.astype(f32), axis=2, keepdims=True), axis=1, keepdims=True)
            dw = -_bmm_nt(dvnb, sn)
            ru = _bmm_tn(t, dvn, MID)
            rw = _bmm_tn(t, dw, MID)
            rub, rwb = ru.astype(bf16), rw.astype(bf16)
            dL = -(_bmm_nt(rub, ub) + _bmm_nt(rwb, wb))
            dLt = -(_bmm_nt(ub, rub) + _bmm_nt(wb, rwb))
            knb, qnb, kbb = kn.astype(bf16), qn.astype(bf16), it["kb"].astype(bf16)
            dmt = jnp.where(upper, jnp.exp(jnp.where(upper, gr - gcb3[:, :, :C], 0.0)), 0.0)
            Lt = _bmm_nt(knb, kbb) * jnp.where(r < c, dmt, 0.0)
            At = _bmm_nt(knb, qnb) * dmt
            dgc = _rowsum(dL * it["L"] + da * it["a"]) - _rowsum(dLt * Lt + dat * At)
            dkk = (dL * it["ds"]).astype(bf16)
            dqk = (da * it["dm"]).astype(bf16)
            dkb = _bmm(dkk, knb) + rw * it["eg"]
            dkn = _bmm_tn(dkk, kbb) + _bmm_tn(dqk, qnb) + dkd * it["ekd"] + dkb * beta
            dqn = _bmm(dqk, knb) + dqd * it["eg"]
            tkd = _rowsum(dkd * it["kd"])
            dgl = jnp.sum(tkd, axis=1, keepdims=True) + dcd * it["cd"][:, :, 0:1]
            dgc = dgc + _rowsum(dqd * it["qd"]) - tkd + _rowsum(rw * it["rhs_w"])
            dbeta = _rowsum(ru * v) + _rowsum(dkb * kn)
            rowc = lax.broadcasted_iota(jnp.int32, (ng, C, 1), 1)
            dgc = dgc + jnp.where(rowc == C - 1, dgl, 0.0)
            rev = jnp.broadcast_to(upper.astype(f32), (ng, C, C))
            dg_ref[0, 0, rows, :] = _bmm(rev, jnp.broadcast_to(dgc, sh), HI).reshape(ng * C, LANES).astype(bf16)
            dbeta_ref[0, 0, rows, :] = jnp.broadcast_to(dbeta, sh).reshape(ng * C, LANES).astype(bf16)
            qn_s[rows, :] = dqn.reshape(ng * C, HEAD)
            kn_s[rows, :] = dkn.reshape(ng * C, HEAD)
            v_s[rows, :] = (ru * beta).reshape(ng * C, HEAD)
            return carry

        lax.fori_loop(0, n // ng, group_bwd, 0)

        def conv_back(x_ref, cw_ref, grad_s, out_ref, slot, normalise, out_scale):
            x = x_ref[0].astype(f32)
            w = cw_ref[...]
            pre, act = _conv_silu(x, w, row)
            dact = grad_s[...]
            if normalise:
                rn = lax.rsqrt(jnp.sum(act * act, axis=-1, keepdims=True) + EPS)
                unit = act * rn
                dact = (out_scale * rn) * (dact - unit * _rowsum(dact * unit))
            s = _sigmoid(pre)
            dc = dact * (s * (1.0 + pre * (1.0 - s)))
            out_ref[0] = (w[3:4] * dc + w[2:3] * _shift_up(dc, 1, row) + w[1:2] * _shift_up(dc, 2, row)
                          + w[0:1] * _shift_up(dc, 3, row)).astype(bf16)
            for tap in range(4):
                xs = x if tap == 3 else _shift_down(x, 3 - tap, row)
                dconv_ref[0, slot, tap:tap + 1, :] = jnp.sum(dc * xs, axis=0, keepdims=True)

        conv_back(q_ref, cq_ref, qn_s, dq_ref, 0, True, HEAD ** -0.5)
        conv_back(k_ref, ck_ref, kn_s, dk_ref, 1, True, 1.0)
        conv_back(v_ref, cv_ref, v_s, dv_ref, 2, False, 1.0)

    o512 = jax.ShapeDtypeStruct((B, S, BRANCH), bf16)
    s512 = pl.BlockSpec((1, S, HEAD), lambda b, h: (b, 0, h))
    ph = jax.ShapeDtypeStruct((B, N_HEADS, S, LANES), bf16)
    seq = pltpu.VMEM((S, HEAD), f32)
    cb = pltpu.VMEM((n, C, HEAD), bf16)
    body, cin, cout, cshapes, cscratch = _carried(body, len(ins), 8, (B, N_HEADS), carry)
    outs = pl.pallas_call(
        body, name=name, grid=(B, N_HEADS), in_specs=ins + cin,
        out_specs=[s512, s512, s512, s512, per_head, per_head, pl.BlockSpec((1, 3, 4, HEAD), lambda b, h: (b, 0, 0, h)),
                   pl.BlockSpec((1, 1, 8, HEAD), lambda b, h: (b, h, 0, 0))] + cout,
        out_shape=[o512, o512, o512, o512, ph, ph, jax.ShapeDtypeStruct((B, 3, 4, BRANCH), f32),
                   jax.ShapeDtypeStruct((B, N_HEADS, 8, HEAD), f32)] + cshapes,
        scratch_shapes=[seq, seq, seq, pltpu.VMEM((n, 1, HEAD), f32), cb, pltpu.VMEM((n, HEAD, HEAD), bf16),
                        pltpu.VMEM((n, C, HEAD), f32), pltpu.VMEM((n, HEAD, HEAD), bf16), pltpu.VMEM((n, HEAD, HEAD), f32)]
        + cscratch,
        compiler_params=_cp(("arbitrary", "arbitrary") if carry else ("parallel", "parallel")),
    )(proj, proj, proj, proj, conv_w, conv_w, conv_w, gcb, betab, gcr, gain.reshape(1, HEAD), dy, states, vnew, tinv, ub, wb,
      *(carry.ins if carry else []))
    return outs[:8], outs[8:]


def _dn_post(ab, par, dg, dbeta, name):
    B, S, _ = ab.shape
    ts = _tile(S, (512, 256, 128))

    def body(ab_ref, par_ref, dg_ref, db_ref, dab_ref, acc_ref):
        x = ab_ref[0]
        lane = lax.broadcasted_iota(jnp.int32, x.shape, 1)
        dgs = jnp.zeros_like(x)
        dbs = jnp.zeros_like(x)
        for h in range(N_HEADS):
            dgs = jnp.where(lane == h, dg_ref[0, h], dgs)
            dbs = jnp.where(lane == N_HEADS + h, db_ref[0, h], dbs)
        nega = -jnp.exp(par_ref[0:1, :])
        pre = x + par_ref[1:2, :]
        da = dgs * nega * _sigmoid(pre)
        beta = _sigmoid(x)
        dab_ref[0] = (da + dbs * beta * (1.0 - beta)).astype(bf16)

        @pl.when((pl.program_id(0) == 0) & (pl.program_id(1) == 0))
        def _():
            acc_ref[...] = jnp.zeros_like(acc_ref)

        acc_ref[0] += _fold8(dgs * nega * _softplus(pre))
        acc_ref[1] += _fold8(da)

    return pl.pallas_call(
        body, name=name, grid=(B, S // ts),
        in_specs=[pl.BlockSpec((1, ts, LANES), lambda b, i: (b, i, 0)), pl.BlockSpec((8, LANES), lambda b, i: (0, 0)),
                  pl.BlockSpec((1, N_HEADS, ts, LANES), lambda b, i: (b, 0, i, 0)),
                  pl.BlockSpec((1, N_HEADS, ts, LANES), lambda b, i: (b, 0, i, 0))],
        out_specs=[pl.BlockSpec((1, ts, LANES), lambda b, i: (b, i, 0)), pl.BlockSpec((2, 8, LANES), lambda b, i: (0, 0, 0))],
        out_shape=[jax.ShapeDtypeStruct((B, S, LANES), bf16), jax.ShapeDtypeStruct((2, 8, LANES), f32)],
        compiler_params=_cp(("arbitrary", "arbitrary")),
    )(ab, par, dg, dbeta)


def _merge_specs(T, D, tm, tn, order):
    nj = D // tn

    def ij(f):
        return (lambda i, j: f(i, j)) if order == "ij" else (lambda j, i: f(i, j))

    ys = [pl.BlockSpec((tm, BRANCH), ij(lambda i, j: (i, 0))) for _ in range(3)]
    wb = pl.BlockSpec((3, BRANCH, tn), ij(lambda i, j: (0, 0, j)))
    gl = [pl.BlockSpec((tm, tn), ij(lambda i, j, k=k: (i, k * nj + j))) for k in range(3)]
    bg = [pl.BlockSpec((1, tn), ij(lambda i, j, k=k: (0, k * nj + j))) for k in range(3)]
    return ys, wb, gl, bg


def _merge_fwd(ys, wb, gl, b_gate, name):
    T, D = ys[0].shape[0], wb.shape[2]
    tm, tn = _tile(T, (512, 256, 128)), _tile(D, (512, 256, 128))
    sy, swb, sgl, sbg = _merge_specs(T, D, tm, tn, "ij")

    def body(y0, y1, y2, wb_ref, g0, g1, g2, b0, b1, b2, o_ref):
        acc = None
        for k, (y, g, b) in enumerate(((y0, g0, b0), (y1, g1, b1), (y2, g2, b2))):
            term = _sigmoid(g[...].astype(f32) + b[...]) * _dot(y[...], wb_ref[k])
            acc = term if acc is None else acc + term
        o_ref[...] = acc.astype(bf16)

    bg = b_gate.reshape(1, 3 * D)
    return pl.pallas_call(
        body, name=name, grid=(T // tm, D // tn), in_specs=sy + [swb] + sgl + sbg,
        out_specs=pl.BlockSpec((tm, tn), lambda i, j: (i, j)), out_shape=jax.ShapeDtypeStruct((T, D), bf16),
        compiler_params=_cp(("parallel", "parallel")),
    )(*ys, wb, gl, gl, gl, bg, bg, bg)


def _merge_bwd(dm, ys, wb, gl, b_gate, name):
    T, D = dm.shape
    tm, tn = _tile(T, (512, 256, 128)), _tile(D, (512, 256, 128))
    sy, swb, sgl, sbg = _merge_specs(T, D, tm, tn, "ji")

    def body(dm_ref, y0, y1, y2, wb_ref, g0, g1, g2, b0, b1, b2, dgl_ref, dbd_ref, dbg_ref):
        dmv = dm_ref[...].astype(f32)

        @pl.when(pl.program_id(1) == 0)
        def _():
            dbg_ref[...] = jnp.zeros_like(dbg_ref)

        for k, (y, g, b) in enumerate(((y0, g0, b0), (y1, g1, b1), (y2, g2, b2))):
            s = _sigmoid(g[...].astype(f32) + b[...])
            dg = dmv * _dot(y[...], wb_ref[k]) * s * (1.0 - s)
            dgl_ref[k] = dg.astype(bf16)
            dbd_ref[k] = (dmv * s).astype(bf16)
            dbg_ref[k] += _fold8(dg)

    bg = b_gate.reshape(1, 3 * D)
    o3 = jax.ShapeDtypeStruct((3, T, D), bf16)
    s3 = pl.BlockSpec((3, tm, tn), lambda j, i: (0, i, j))
    return pl.pallas_call(
        body, name=name, grid=(D // tn, T // tm),
        in_specs=[pl.BlockSpec((tm, tn), lambda j, i: (i, j))] + sy + [swb] + sgl + sbg,
        out_specs=[s3, s3, pl.BlockSpec((3, 8, tn), lambda j, i: (0, 0, j))],
        out_shape=[o3, o3, jax.ShapeDtypeStruct((3, 8, D), f32)],
        compiler_params=_cp(("parallel", "arbitrary")),
    )(dm, *ys, wb, gl, gl, gl, bg, bg, bg)


def _ffn_fwd(x, g, w, f, tag, carry=None):
    T, D = x.shape
    FP = w["wgu"].shape[2]
    h = _rms_fwd(x, g, f"rms_{tag}")
    (a, b, hm), landed = _ffn_up(h, w["wgu"], f, f"ffn_up_{tag}", carry)
    down = [("nn", hm, w["m"], {"K": FP, "ka": k, "bsel": k, "kb": f}) for k in range(N_CHIPS)]
    y = _mm(down, T, D, f32, f"ffn_down_{tag}", res=x, scale=0.5)
    return y, (x, h, a, b), landed


def _ffn_bwd(dy2, saved, g, w, f, tag):
    x, h, a, b = saved
    dy, dyb = dy2
    T, D = x.shape
    FP = w["wgu"].shape[2]
    F4 = N_CHIPS * FP
    da, db, hm = _ffn_bwd_mid(dyb, w["m"], a, b, f, f"ffn_mid_bwd_{tag}")
    dwd = _mm([("tn", hm, dyb, {})], F4, D, bf16, f"ffn_dwd_{tag}", scale=0.5)
    dwg = _mm([("tn", h, da, {})], D, F4, bf16, f"ffn_dwg_{tag}", out_chip=True)
    dwu = _mm([("tn", h, db, {})], D, F4, bf16, f"ffn_dwu_{tag}", out_chip=True)
    tn = _tile(D, (512, 256, 128))
    pairs = [("nt", t, w["wgu"], {"K": FP, "ka": k, "bsel": k, "noff": (2 * f + u) * (D // tn)})
             for u, t in enumerate((da, db)) for k in range(N_CHIPS)]
    dh = _mm(pairs, T, D, f32, f"ffn_dh_{tag}", tn=tn)
    dx, dg8 = _rms_bwd(dh, x, g, dy, f"rms_bwd_{tag}")
    return dx, dict(norm=jnp.sum(dg8, axis=0), wgu=[dwg, dwu], wd=dwd.reshape(N_CHIPS, FP, D))


def _layer_fwd(x, w, B, tag, carry=(None, None, None)):
    T, D = x.shape
    S = T // B
    x1, sv0, landed_ffn = _ffn_fwd(x, w["ffn_norm"][0], w, 0, f"pre_{tag}", carry[2])
    h = _rms_fwd(x1, w["mix_norm"], f"rms_mix_{tag}")
    pm = _mm([("nn", h, w["w_main"], {})], T, 5 * BRANCH, bf16, f"proj_main_{tag}")
    ab = _mm([("nn", h, w["w_ab"], {})], T, LANES, f32, f"proj_ab_{tag}", tn=LANES)
    sb = _mm([("nn", h, w["w_sb"], {})], T, 3 * BRANCH, bf16, f"proj_sb_{tag}")
    gl = _mm([("nn", h, w["w_gates"], {})], T, 3 * D, bf16, f"proj_gates_{tag}")
    pm3, ab3, sb3 = pm.reshape(B, S, -1), ab.reshape(B, S, LANES), sb.reshape(B, S, -1)
    y_pool = _pool_fwd(pm3, w["pool_w"], w["pool_scale"], f"pool_{tag}")
    par = _dn_params(w["dn_A_log"], w["dn_dt_bias"])
    gcb, betab, gcr = _dn_prep(ab3, par, f"dn_prep_{tag}")
    (y_dn, states, vnew, tinv, ub, wb), landed_dn = _dn_fwd(pm3, w["dn_conv"], gcb, betab, gcr, w["dn_out_norm"], f"dn_fwd_{tag}", carry[0])
    (y_sb, tot), landed_sb = _sb_fwd(sb3, f"sb_fwd_{tag}", carry[1])
    ys = [y_pool.reshape(T, BRANCH), y_dn.reshape(T, BRANCH), y_sb.reshape(T, BRANCH)]
    merged = _merge_fwd(ys, w["w_branch"], gl, w["b_gate"], f"merge_{tag}")
    dc = D // N_CHIPS
    out_pairs = [("nn", merged, w["m"], {"K": dc, "ka": k, "bsel": k, "kb": _w_out_block(w)}) for k in range(N_CHIPS)]
    x2 = _mm(out_pairs, T, D, f32, f"mix_out_{tag}", res=x1)
    x3, sv1, _ = _ffn_fwd(x2, w["ffn_norm"][1], w, 1, f"post_{tag}")
    saved = dict(sv0=sv0, sv1=sv1, x1=x1, h=h, pm3=pm3, ab3=ab3, sb3=sb3, gl=gl, par=par, gcb=gcb, betab=betab, gcr=gcr,
                 states=states, vnew=vnew, tinv=tinv, ub=ub, wb=wb, tot=tot, ys=ys, merged=merged)
    return x3, saved, (landed_dn, landed_sb, landed_ffn)


def _layer_bwd(dx3, w, sv, B, tag, carry=(None, None)):
    T, D = dx3[0].shape
    S = T // B
    (dx2, dx2b), g1 = _ffn_bwd(dx3, sv["sv1"], w["ffn_norm"][1], w, 1, f"post_{tag}")
    dc = D // N_CHIPS
    dmerged = _mm([("nt", dx2b, w["m"], {"b_by_chip": True, "noff": _w_out_block(w)})], T, D, bf16, f"mix_dmerged_{tag}", tn=dc)
    dw_out = _mm([("tn", sv["merged"], dx2b, {})], D, D, bf16, f"mix_dwout_{tag}")
    ys = sv["ys"]
    dgl, dbd, dbg8 = _merge_bwd(dmerged, ys, w["w_branch"], sv["gl"], w["b_gate"], f"merge_bwd_{tag}")
    dys, dwb = [], []
    for k in range(3):
        dys.append(_mm([("nt", dbd, w["w_branch"], {"asel": k, "bsel": k})], T, BRANCH, bf16, f"branch_dy{k}_{tag}"))
        dwb.append(_mm([("tn", ys[k], dbd, {"bsel": k})], BRANCH, D, bf16, f"branch_dw{k}_{tag}"))
    pm3, ab3, sb3 = sv["pm3"], sv["ab3"], sv["sb3"]
    du, dpool_w, dsc8 = _pool_bwd(pm3, dys[0].reshape(B, S, BRANCH), w["pool_w"], w["pool_scale"], f"pool_bwd_{tag}")
    (dq, dk, dv, dz, dg, dbeta, dconv, dgain), landed = _dn_bwd(
        pm3, dys[1].reshape(B, S, BRANCH), w["dn_conv"], sv["gcb"], sv["betab"], sv["gcr"], w["dn_out_norm"],
        sv["states"], sv["vnew"], sv["tinv"], sv["ub"], sv["wb"], f"dn_bwd_{tag}", carry[0])
    dab, dn_acc = _dn_post(ab3, sv["par"], dg, dbeta, f"dn_post_{tag}")
    (dsq, dsk, dsv), landed_sb = _sb_bwd(sb3, dys[2].reshape(B, S, BRANCH), sv["tot"], f"sb_bwd_{tag}", carry[1])
    main_parts = [t.reshape(T, BRANCH) for t in (du, dq, dk, dv, dz)]
    sb_parts = [t.reshape(T, BRANCH) for t in (dsq, dsk, dsv)]
    dab2 = dab.reshape(T, LANES)
    pairs = [("nt", t, w["w_main"], {"K": BRANCH, "kb": k}) for k, t in enumerate(main_parts)]
    pairs.append(("nt", dab2, w["w_ab"], {}))
    pairs += [("nt", t, w["w_sb"], {"K": BRANCH, "kb": k}) for k, t in enumerate(sb_parts)]
    pairs += [("nt", dgl, w["w_gates"], {"K": D, "kb": k, "asel": k}) for k in range(3)]
    dh = _mm(pairs, T, D, f32, f"mix_dh_{tag}")
    h = sv["h"]
    dw_cols = [_mm([("tn", h, t, {})], D, BRANCH, bf16, f"dwin_main{k}_{tag}") for k, t in enumerate(main_parts)]
    dw_cols.append(_mm([("tn", h, dab2, {})], D, LANES, bf16, f"dwin_ab_{tag}", tn=LANES)[:, :2 * N_HEADS])
    dw_cols += [_mm([("tn", h, t, {})], D, BRANCH, bf16, f"dwin_sb{k}_{tag}") for k, t in enumerate(sb_parts)]
    dw_cols += [_mm([("tn", h, dgl, {"bsel": k})], D, D, bf16, f"dwin_gate{k}_{tag}") for k in range(3)]
    dx1, dmix8 = _rms_bwd(dh, sv["x1"], w["mix_norm"], dx2, f"rms_mix_bwd_{tag}")
    dx0, g0 = _ffn_bwd(dx1, sv["sv0"], w["ffn_norm"][0], w, 0, f"pre_{tag}")
    dwb = jnp.stack(dwb).reshape(3, BRANCH, N_CHIPS, dc).transpose(2, 0, 1, 3).reshape(N_CHIPS, -1, D)
    dw_in = jnp.concatenate(dw_cols, axis=1)
    pc = dw_in.shape[1] // N_CHIPS
    grads = dict(
        ffn_norm=jnp.stack([g0["norm"], g1["norm"]]),
        A=jnp.concatenate(g0["wgu"] + g1["wgu"], axis=1),
        M=jnp.concatenate([g0["wd"], g1["wd"], dw_out.reshape(N_CHIPS, dc, D), dwb], axis=1),
        C=jnp.stack([dw_in[:, k * pc:(k + 1) * pc] for k in range(N_CHIPS)]),
        mix_norm=jnp.sum(dmix8, axis=0), b_gate=jnp.sum(dbg8, axis=1).reshape(3 * D),
        pool_w=dpool_w, pool_scale=jnp.sum(dsc8, axis=0), dn_conv=jnp.sum(dconv, axis=0).transpose(1, 0, 2).reshape(4, 3 * BRANCH),
        dn_A_log=jnp.sum(dn_acc[0], axis=0)[:N_HEADS], dn_dt_bias=jnp.sum(dn_acc[1], axis=0)[:N_HEADS],
        dn_out_norm=jnp.sum(dgain, axis=(0, 1, 2)))
    return dx0, grads, (landed, landed_sb)


def _local_step(x, target, layers, final_norm, B):
    saved = []
    for l, w in enumerate(layers):
        x, sv, _ = _layer_fwd(x, w, B, f"l{l}")
        saved.append(sv)
    dx, dfn8, ls8 = _final_loss(x, target, final_norm)
    grads = [None] * len(layers)
    for l in reversed(range(len(layers))):
        dx, grads[l], _ = _layer_bwd(dx, layers[l], saved[l], B, f"l{l}")
    return jnp.sum(ls8), dx[0], grads, jnp.sum(dfn8, axis=0)


def _adamw(w, g, m, v, name):
    shape = w.shape
    cols = shape[-1]
    rows = math.prod(shape[:-1]) if len(shape) > 1 else 1
    w2, g2, m2, v2 = (t.reshape(rows, cols) for t in (w, g, m, v))
    block_elems = 256 * 1024
    tr = rows if rows * cols <= block_elems else _tile(rows, [t for t in (512, 256, 128, 64, 32, 16, 8) if t * cols <= block_elems])

    def body(w_ref, g_ref, m_ref, v_ref, d_ref, mo_ref, vo_ref):
        gv = g_ref[...]
        mn = ADAM_B1 * m_ref[...] + (1.0 - ADAM_B1) * gv
        vn = ADAM_B2 * v_ref[...] + (1.0 - ADAM_B2) * (gv * gv)
        m_hat = mn / (1.0 - ADAM_B1 ** ADAM_STEP)
        v_hat = vn / (1.0 - ADAM_B2 ** ADAM_STEP)
        d_ref[...] = -ADAM_LR * (m_hat / (jnp.sqrt(v_hat) + ADAM_EPS) + ADAM_WD * w_ref[...])
        mo_ref[...] = mn
        vo_ref[...] = vn

    spec = pl.BlockSpec((tr, cols), lambda i: (i, 0))
    o = jax.ShapeDtypeStruct((rows, cols), f32)
    d, mo, vo = pl.pallas_call(
        body, name=name, grid=(rows // tr,), in_specs=[spec] * 4, out_specs=[spec] * 3, out_shape=[o, o, o],
        compiler_params=_cp(("parallel",)),
    )(w2, g2, m2, v2)
    return d.reshape(shape), mo.reshape(shape), vo.reshape(shape)


MESH = pl.DeviceIdType.MESH
_ANY = pl.BlockSpec(memory_space=pl.ANY)


def _place():
    x, y, c = lax.axis_index("x"), lax.axis_index("y"), lax.axis_index("c")
    return x, y, c, [(1 - x, y), (x, 1 - y), (1 - x, 1 - y)]


def _chip_index():
    return 2 * lax.axis_index("x") + lax.axis_index("y")


def _half(c, rh):
    return pl.ds(c * rh, rh)


def _remote(src, dst, ssem, rsem, to):
    return pltpu.make_async_remote_copy(src_ref=src, dst_ref=dst, send_sem=ssem, recv_sem=rsem, device_id=to,
                                        device_id_type=MESH)


class _ChipExchange:
    def __init__(self, kind, ins):
        self.kind, self.ins = kind, list(ins)
        self.n = len(self.ins)
        self.out_shapes = [jax.ShapeDtypeStruct((N_CHIPS,) + a.shape[-2:], a.dtype) for a in self.ins]
        self.scratch = [pltpu.SemaphoreType.DMA((self.n, 3)), pltpu.SemaphoreType.DMA((self.n, 3))]

    def _copies(self, in_refs, out_refs, ssem, rsem):
        x, y, c, chips = _place()
        me = 2 * x + y
        pairs = []
        for o, (src, dst) in enumerate(zip(in_refs, out_refs)):
            for k, (px, py) in enumerate(chips):
                peer = 2 * px + py
                if self.kind == "gather":
                    rows = _half(c, src.shape[0] // 2)
                    out, land = (src.at[rows], dst.at[me, rows]), dst.at[peer, rows]
                else:
                    out, land = (src.at[peer], dst.at[me]), dst.at[peer]
                pairs.append((_remote(out[0], out[1], ssem.at[o, k], rsem.at[o, k], (px, py, c)),
                              _remote(land, land, ssem.at[o, k], rsem.at[o, k], (px, py, c))))
        return pairs

    def start(self, in_refs, out_refs, ssem, rsem):
        for mine, _ in self._copies(in_refs, out_refs, ssem, rsem):
            mine.start()

    def wait(self, in_refs, out_refs, ssem, rsem):
        for mine, landing in self._copies(in_refs, out_refs, ssem, rsem):
            landing.wait_recv()
            mine.wait_send()

    def standalone(self, name):
        n = self.n

        def body(*refs):
            ins, outs, (ssem, rsem) = refs[:n], refs[n:2 * n], refs[2 * n:]
            self.start(ins, outs, ssem, rsem)
            self.wait(ins, outs, ssem, rsem)

        return pl.pallas_call(body, name=name, in_specs=[_ANY] * n, out_specs=[_ANY] * n, out_shape=self.out_shapes,
                              scratch_shapes=self.scratch)(*self.ins)


def _carried(body, n_in, n_out, grid, carry):
    if carry is None:
        return body, [], [], [], []
    n = carry.n

    def wrapped(*refs):
        ins, cin = refs[:n_in], refs[n_in:n_in + n]
        outs, cout = refs[n_in + n:n_in + n + n_out], refs[n_in + n + n_out:n_in + 2 * n + n_out]
        scratch, (ssem, rsem) = refs[n_in + 2 * n + n_out:-2], refs[-2:]
        step = pl.program_id(0) * grid[1] + pl.program_id(1)

        @pl.when(step == 0)
        def _():
            carry.start(cin, cout, ssem, rsem)

        body(*ins, *outs, *scratch)

        @pl.when(step == grid[0] * grid[1] - 1)
        def _():
            carry.wait(cin, cout, ssem, rsem)

    return wrapped, [_ANY] * n, [_ANY] * n, carry.out_shapes, carry.scratch


def _gather_finish(shards, landed, name):
    n = len(shards)

    def body(*refs):
        outs, (ssem, rsem) = refs[n:2 * n], refs[2 * n:]
        x, y, c, chips = _place()
        started = []
        for o, buf in enumerate(outs):
            rh = buf.shape[1] // 2
            for k, (px, py) in enumerate(chips):
                block = buf.at[2 * px + py, _half(c, rh)]
                cp = _remote(block, block, ssem.at[o, k], rsem.at[o, k], (x, y, 1 - c))
                cp.start()
                started.append(cp)
        for o, buf in enumerate(outs):
            rh = buf.shape[1] // 2
            for k, (px, py) in enumerate(chips):
                block = buf.at[2 * px + py, _half(1 - c, rh)]
                _remote(block, block, ssem.at[o, k], rsem.at[o, k], (x, y, 1 - c)).wait_recv()
        for cp in started:
            cp.wait_send()

    outs = pl.pallas_call(
        body, name=name, in_specs=[_ANY] * n, out_specs=[_ANY] * n,
        out_shape=[jax.ShapeDtypeStruct(a.shape, a.dtype) for a in landed], input_output_aliases={i: i for i in range(n)},
        scratch_shapes=[pltpu.SemaphoreType.DMA((n, 3)), pltpu.SemaphoreType.DMA((n, 3))],
    )(*landed)
    me = _chip_index()
    return [lax.dynamic_update_slice(g, s[None], (me, 0, 0)) for g, s in zip(outs, shards)]


def _gather_chips(shards, name):
    landed = _ChipExchange("gather", shards).standalone(f"{name}_ici")
    return _gather_finish(shards, landed, f"{name}_pass")


def _pair_swap_halves(ps, name):
    n = len(ps)

    def body(*refs):
        ins, outs, (ssem, rsem) = refs[:n], refs[n:2 * n], refs[2 * n:]
        x, y, c, _ = _place()
        cps = [_remote(p.at[:, _half(1 - c, p.shape[1] // 2)], out, ssem.at[o], rsem.at[o], (x, y, 1 - c))
               for o, (p, out) in enumerate(zip(ins, outs))]
        for cp in cps:
            cp.start()
        for cp in cps:
            cp.wait()

    return pl.pallas_call(
        body, name=name, in_specs=[_ANY] * n, out_specs=[_ANY] * n,
        out_shape=[jax.ShapeDtypeStruct((p.shape[0], p.shape[1] // 2, p.shape[2]), p.dtype) for p in ps],
        scratch_shapes=[pltpu.SemaphoreType.DMA((n,)), pltpu.SemaphoreType.DMA((n,))],
    )(*ps)


def _pair_add(p, got, name):
    n, R, W = p.shape
    rh = R // 2
    tr = _row_tile(rh, W, 2 * 1024 * 1024)
    nb = rh // tr

    def body(c_ref, p_ref, g_ref, o_ref):
        o_ref[...] = (p_ref[...].astype(f32) + g_ref[...].astype(f32)).astype(o_ref.dtype)

    return pl.pallas_call(
        body, name=name,
        grid_spec=pltpu.PrefetchScalarGridSpec(
            num_scalar_prefetch=1, grid=(n, nb),
            in_specs=[pl.BlockSpec((1, tr, W), lambda j, i, c_ref: (j, c_ref[0] * nb + i, 0)),
                      pl.BlockSpec((1, tr, W), lambda j, i, c_ref: (j, i, 0))],
            out_specs=pl.BlockSpec((1, tr, W), lambda j, i, c_ref: (j, i, 0))),
        out_shape=jax.ShapeDtypeStruct((n, rh, W), p.dtype), compiler_params=_cp(("parallel", "parallel")),
    )(lax.axis_index("c").astype(jnp.int32).reshape(1), p, got)


def _own_slot_filled(landed, ps):
    me = _chip_index()
    return [lax.dynamic_update_slice(out, lax.dynamic_slice(p, (me, 0, 0), (1,) + p.shape[1:]), (me, 0, 0))
            for out, p in zip(landed, ps)]


def _sum_slots(r4, name):
    n, R, W = r4.shape
    tr = _row_tile(R, W, 1024 * 1024)

    def body(r_ref, o_ref):
        acc = r_ref[0].astype(f32)
        for k in range(1, n):
            acc = acc + r_ref[k].astype(f32)
        o_ref[...] = acc

    return pl.pallas_call(
        body, name=name, grid=(R // tr,), in_specs=[pl.BlockSpec((n, tr, W), lambda i: (0, i, 0))],
        out_specs=pl.BlockSpec((tr, W), lambda i: (i, 0)), out_shape=jax.ShapeDtypeStruct((R, W), f32),
        compiler_params=_cp(("parallel",)),
    )(r4)


def _pair_share(ss, name):
    n = len(ss)

    def body(*refs):
        ins, outs, (ssem, rsem) = refs[:n], refs[n:2 * n], refs[2 * n:]
        x, y, c, _ = _place()
        cps = [_remote(s, out.at[c], ssem.at[o], rsem.at[o], (x, y, 1 - c)) for o, (s, out) in enumerate(zip(ins, outs))]
        for cp in cps:
            cp.start()
        for o, (s, out) in enumerate(zip(ins, outs)):
            _remote(s, out.at[1 - c], ssem.at[o], rsem.at[o], (x, y, 1 - c)).wait_recv()
        for cp in cps:
            cp.wait_send()

    outs = pl.pallas_call(
        body, name=name, in_specs=[_ANY] * n, out_specs=[_ANY] * n,
        out_shape=[jax.ShapeDtypeStruct((2,) + s.shape, s.dtype) for s in ss],
        scratch_shapes=[pltpu.SemaphoreType.DMA((n,)), pltpu.SemaphoreType.DMA((n,))],
    )(*ss)
    c = lax.axis_index("c")
    return [lax.dynamic_update_slice(out, s[None], (c, 0, 0)).reshape(2 * s.shape[0], s.shape[1]) for out, s in zip(outs, ss)]


def _reduce_begin(ps, tag):
    got = _pair_swap_halves(ps, f"rs_pair_swap_{tag}")
    return [_pair_add(p, g, f"rs_pair_add{o}_{tag}") for o, (p, g) in enumerate(zip(ps, got))]


def _reduce_finish(pair_sums, landed, tag):
    r4 = _own_slot_filled(landed, pair_sums)
    return _pair_share([_sum_slots(r, f"rs_sum{o}_{tag}") for o, r in enumerate(r4)], f"rs_share_{tag}")


def _reduce_to_chips(ps, tag):
    pair_sums = _reduce_begin(ps, tag)
    return _reduce_finish(pair_sums, _ChipExchange("scatter", pair_sums).standalone(f"rs_scatter_{tag}"), tag)


def _pad_rows(a, mult):
    r = (-a.shape[-2]) % mult
    return a if r == 0 else jnp.pad(a, [(0, 0)] * (a.ndim - 2) + [(0, r), (0, 0)])


def _hidden_pad(fs):
    return -(-fs // LANES) * LANES


def _w_out_block(w):
    return 2 * w["wgu"].shape[2] // (w["m"].shape[2] // N_CHIPS)


def _pack_wgu(wg, wu):
    fs = wg.shape[-1]
    t = jnp.stack([wg, wu], axis=1).astype(bf16)
    return jnp.pad(t, ((0, 0), (0, 0), (0, 0), (0, _hidden_pad(fs) - fs))).reshape(-1, _hidden_pad(fs))


def _pack_m(wd, w_out, w_branch):
    fs, D = wd.shape[1:]
    wdp = jnp.pad(wd.astype(bf16), ((0, 0), (0, _hidden_pad(fs) - fs), (0, 0))).reshape(-1, D)
    return jnp.concatenate([wdp, w_out.astype(bf16), w_branch.astype(bf16).reshape(-1, D)], axis=0)


def _w_in_cols(cg, lo, hi):
    p = cg.shape[2]
    parts = [cg[k][:, max(lo, k * p) - k * p:min(hi, (k + 1) * p) - k * p] for k in range(N_CHIPS)
             if max(lo, k * p) < min(hi, (k + 1) * p)]
    return parts[0] if len(parts) == 1 else jnp.concatenate(parts, axis=1)


def _layer_weights(ag, mg, cg, small):
    D = mg.shape[2]
    dc = D // N_CHIPS
    fp2 = 2 * ag.shape[2]
    wb = mg[:, fp2 + dc:].reshape(N_CHIPS, 3, BRANCH, dc).transpose(1, 2, 0, 3).reshape(3, BRANCH, D)
    c0, c1, c2 = 5 * BRANCH, 5 * BRANCH + 2 * N_HEADS, 8 * BRANCH + 2 * N_HEADS
    w = dict(small)
    w.update(wgu=ag, m=mg, w_branch=wb, w_main=_w_in_cols(cg, 0, c0),
             w_ab=jnp.pad(_w_in_cols(cg, c0, c1), ((0, 0), (0, LANES - 2 * N_HEADS))), w_sb=_w_in_cols(cg, c1, c2),
             w_gates=_w_in_cols(cg, c2, N_CHIPS * cg.shape[2]))
    return w


def _small_pack(pieces):
    flat, offs, r = [], [], 0
    for a in pieces:
        v = a.reshape(-1)
        pad = (-v.shape[0]) % PACK_W
        flat.append(jnp.pad(v, (0, pad)) if pad else v)
        offs.append(r)
        r += (v.shape[0] + pad) // PACK_W
    pack = jnp.concatenate(flat).reshape(r, PACK_W)
    return _pad_rows(pack, 16), offs


def _small_unpack(pack, offs, shapes):
    out = []
    for o, s in zip(offs, shapes):
        n = math.prod(s)
        rows = -(-n // PACK_W)
        out.append(pack[o:o + rows].reshape(-1)[:n].reshape(s))
    return out


SMALL_SHARDED = ("ffn_norm", "dn_conv")
SMALL_REPLICATED = ("mix_norm", "b_gate", "pool_w", "pool_scale", "dn_A_log", "dn_dt_bias", "dn_out_norm")


def kernel(x, ffn_norm, ffn_w_gate, ffn_w_up, ffn_w_down, mix_norm, w_in, b_gate, pool_w, pool_scale, dn_conv, dn_A_log, dn_dt_bias, dn_out_norm, w_branch, w_out, final_norm, loss_target, m_ffn_norm, m_ffn_w_gate, m_ffn_w_up, m_ffn_w_down, m_mix_norm, m_w_in, m_b_gate, m_pool_w, m_pool_scale, m_dn_conv, m_dn_A_log, m_dn_dt_bias, m_dn_out_norm, m_w_branch, m_w_out, m_final_norm, v_ffn_norm, v_ffn_w_gate, v_ffn_w_up, v_ffn_w_down, v_mix_norm, v_w_in, v_b_gate, v_pool_w, v_pool_scale, v_dn_conv, v_dn_A_log, v_dn_dt_bias, v_dn_out_norm, v_w_branch, v_w_out, v_final_norm):
    names = ("ffn_norm", "ffn_w_gate", "ffn_w_up", "ffn_w_down", "mix_norm", "w_in", "b_gate", "pool_w", "pool_scale", "dn_conv",
             "dn_A_log", "dn_dt_bias", "dn_out_norm", "w_branch", "w_out", "final_norm")
    wts = dict(zip(names, (ffn_norm, ffn_w_gate, ffn_w_up, ffn_w_down, mix_norm, w_in, b_gate, pool_w, pool_scale, dn_conv,
                           dn_A_log, dn_dt_bias, dn_out_norm, w_branch, w_out, final_norm)))
    ms = dict(zip(names, (m_ffn_norm, m_ffn_w_gate, m_ffn_w_up, m_ffn_w_down, m_mix_norm, m_w_in, m_b_gate, m_pool_w, m_pool_scale,
                          m_dn_conv, m_dn_A_log, m_dn_dt_bias, m_dn_out_norm, m_w_branch, m_w_out, m_final_norm)))
    vs = dict(zip(names, (v_ffn_norm, v_ffn_w_gate, v_ffn_w_up, v_ffn_w_down, v_mix_norm, v_w_in, v_b_gate, v_pool_w, v_pool_scale,
                          v_dn_conv, v_dn_A_log, v_dn_dt_bias, v_dn_out_norm, v_w_branch, v_w_out, v_final_norm)))
    B, S, D = x.shape
    T = B * S
    L = ffn_w_gate.shape[0]
    fs, ds_, cs = ffn_w_gate.shape[3], D // N_CHIPS, dn_conv.shape[2]
    fp = _hidden_pad(fs)
    chip = _chip_index()

    shards = [[_pack_wgu(ffn_w_gate[l], ffn_w_up[l]), _pack_m(ffn_w_down[l], w_out[l], w_branch[l]), w_in[l].astype(bf16)]
              for l in range(L)]
    small_shard, soffs = _small_pack([ffn_norm, dn_conv])
    small_g = _gather_chips([small_shard], "gather_small")[0]

    def chip_major(i, a):
        rows = -(-a.size // PACK_W)
        return small_g[:, soffs[i]:soffs[i] + rows].reshape(N_CHIPS, -1)[:, :a.size].reshape((N_CHIPS,) + a.shape)

    fn_full = jnp.moveaxis(chip_major(0, ffn_norm), 0, 2).reshape(L, 2, D)
    conv_full = jnp.moveaxis(chip_major(1, dn_conv), 0, 2).reshape(L, dn_conv.shape[1], N_CHIPS * cs)

    def small_params(l):
        return dict(ffn_norm=fn_full[l], mix_norm=mix_norm[l], b_gate=b_gate[l], pool_w=pool_w[l], pool_scale=pool_scale[l],
                    dn_conv=conv_full[l], dn_A_log=dn_A_log[l], dn_dt_bias=dn_dt_bias[l], dn_out_norm=dn_out_norm[l])

    xt = x.reshape(T, D)
    gathered = _gather_chips(shards[0], "gather_l0")
    layers, saved = [], []
    for l in range(L):
        w = _layer_weights(*gathered, small_params(l))
        more = l + 1 < L
        carry = [_ChipExchange("gather", [s]) if more else None for s in shards[min(l + 1, L - 1)]]
        xt, sv, (landed_dn, landed_sb, landed_ffn) = _layer_fwd(xt, w, B, f"l{l}", (carry[0], carry[1], carry[2]))
        layers.append(w)
        saved.append(sv)
        if more:
            gathered = _gather_finish(shards[l + 1], [landed_dn[0], landed_sb[0], landed_ffn[0]], f"gather_pass_l{l + 1}")
    dx, dfn8, ls8 = _final_loss(xt, loss_target.reshape(T, D), final_norm)
    loss = lax.psum(jnp.sum(ls8), ("x", "y", "c")) * (0.5 / D)

    grads, red, pending = [None] * L, [None] * L, None
    for l in reversed(range(L)):
        carry = (None, None) if pending is None else (_ChipExchange("scatter", pending[:2]), _ChipExchange("scatter", pending[2:]))
        dx, grads[l], (landed_dn, landed_sb) = _layer_bwd(dx, layers[l], saved[l], B, f"l{l}", carry)
        if pending is not None:
            red[l + 1] = _reduce_finish(pending, list(landed_dn) + list(landed_sb), f"l{l + 1}")
        pending = _reduce_begin([grads[l]["A"], grads[l]["M"], grads[l]["C"]], f"l{l}")
    red[0] = _reduce_finish(pending, _ChipExchange("scatter", pending).standalone("rs_scatter_l0"), "l0")

    small_names = SMALL_SHARDED + SMALL_REPLICATED
    pieces = [g[k] for g in grads for k in small_names] + [jnp.sum(dfn8, axis=0)]
    spack, offs = _small_pack(pieces)
    sred = _reduce_to_chips([jnp.broadcast_to(spack[None], (N_CHIPS,) + spack.shape)], "small")[0]
    small_red = _small_unpack(sred, offs, [p.shape for p in pieces])

    gw = {k: [] for k in names if k != "final_norm"}
    for l in range(L):
        ga, gm, gc = red[l]
        gu = ga.reshape(2, 2, D, fp)[..., :fs]
        gw["ffn_w_gate"].append(gu[:, 0])
        gw["ffn_w_up"].append(gu[:, 1])
        gw["ffn_w_down"].append(gm[:2 * fp].reshape(2, fp, D)[:, :fs])
        gw["w_out"].append(gm[2 * fp:2 * fp + ds_])
        gw["w_branch"].append(gm[2 * fp + ds_:].reshape(3, BRANCH, ds_))
        gw["w_in"].append(gc)
        sm = dict(zip(small_names, small_red[l * len(small_names):(l + 1) * len(small_names)]))
        gw["ffn_norm"].append(lax.dynamic_slice_in_dim(sm["ffn_norm"], chip * ds_, ds_, axis=1))
        gw["dn_conv"].append(lax.dynamic_slice_in_dim(sm["dn_conv"], chip * cs, cs, axis=1))
        for k in SMALL_REPLICATED:
            gw[k].append(sm[k])
    gw = {k: jnp.stack(v) for k, v in gw.items()}
    gw["final_norm"] = small_red[-1]

    deltas, new_m, new_v = [], [], []
    for k in names:
        d, mo, vo = _adamw(wts[k], gw[k], ms[k], vs[k], f"adamw_{k}")
        deltas.append(d)
        new_m.append(mo)
        new_v.append(vo)
    return (loss, dx[0].reshape(B, S, D), *[gw[k] for k in names], *deltas, *new_m, *new_v)
```

```python
import functools
import math

import jax
import jax.numpy as jnp
from jax import lax
from jax.experimental import pallas as pl
from jax.experimental.pallas import tpu as pltpu

f32 = jnp.float32
bf16 = jnp.bfloat16
HI = lax.Precision.HIGHEST
MID = lax.Precision.HIGH

EPS = 1e-6
HEAD = 128
N_HEADS = 4
BRANCH = 512
CHUNK = 64
SB_BLOCK = 128
SB_QUERIES = 1024
POOL_WINDOWS = (2, 4, 8, 16)
N_CHIPS = 4
LANES = 128
PACK_W = 1024
ADAM_LR, ADAM_B1, ADAM_B2, ADAM_EPS, ADAM_WD, ADAM_STEP = 0.001, 0.9, 0.999, 1e-08, 0.01, 10
VMEM_LIMIT = 56 * 1024 * 1024
MM_VMEM_BUDGET = 36 * 1024 * 1024

NN = (((1,), (0,)), ((), ()))
NT = (((1,), (1,)), ((), ()))
TN = (((0,), (0,)), ((), ()))


def _cp(sem=None):
    return pltpu.CompilerParams(dimension_semantics=sem, vmem_limit_bytes=VMEM_LIMIT)


def _tile(n, prefs):
    for p in prefs:
        if n % p == 0:
            return p
    return n


def _row_tile(rows, width, max_elems):
    for d in range(rows, 0, -1):
        if rows % d == 0 and (d % 16 == 0 or d == rows) and d * width <= max_elems:
            return d
    return rows


def _dot(a, b, dn=NN):
    return lax.dot_general(a, b, dn, preferred_element_type=f32)


def _sigmoid(x):
    return 0.5 * jnp.tanh(0.5 * x) + 0.5


def _softplus(x):
    return jnp.maximum(x, 0.0) + jnp.log1p(jnp.exp(-jnp.abs(x)))


def _mm(pairs, M, N, out_dtype, name, tm=None, tn=None, res=None, scale=1.0, out_chip=False):
    tn = N // N_CHIPS if out_chip else (tn or _tile(N, (512, 256, 128)))

    def vmem_bytes(rows):
        total = rows * tn * (2 * jnp.dtype(out_dtype).itemsize + (8 if res is not None else 0) + 8)
        for form, a, b, o in pairs:
            K = a.shape[-2] if form == "tn" else (o.get("K") or a.shape[-1])
            total += 2 * K * (rows * a.dtype.itemsize + tn * b.dtype.itemsize)
        return total

    tm = tm or next((t for t in (1024, 512, 256, 128) if M % t == 0 and vmem_bytes(t) <= MM_VMEM_BUDGET), M)
    specs, arrs, dns = [], [], []

    def lead(sel, shape, imap):
        if sel is None:
            return pl.BlockSpec(shape, imap)
        return pl.BlockSpec((None,) + shape, lambda i, j, sel=sel, imap=imap: (sel,) + imap(i, j))

    for form, a, b, o in pairs:
        ka, kb, moff, noff = o.get("ka", 0), o.get("kb", 0), o.get("moff", 0), o.get("noff", 0)
        asel, bsel = o.get("asel"), o.get("bsel")
        if form == "nn":
            K = o.get("K") or a.shape[-1]
            sa = lead(asel, (tm, K), lambda i, j, ka=ka, moff=moff: (i + moff, ka))
            sb = lead(bsel, (K, tn), lambda i, j, kb=kb, noff=noff: (kb, j + noff))
            dn = NN
        elif form == "nt":
            K = o.get("K") or a.shape[-1]
            sa = lead(asel, (tm, K), lambda i, j, ka=ka, moff=moff: (i + moff, ka))
            if o.get("b_by_chip"):
                sb = pl.BlockSpec((None, tn, K), lambda i, j, kb=kb, noff=noff: (j, noff, kb))
            else:
                sb = lead(bsel, (tn, K), lambda i, j, kb=kb, noff=noff: (j + noff, kb))
            dn = NT
        else:
            K = a.shape[-2]
            sa = lead(asel, (K, tm), lambda i, j, moff=moff: (0, i + moff))
            sb = lead(bsel, (K, tn), lambda i, j, noff=noff: (0, j + noff))
            dn = TN
        specs += [sa, sb]
        arrs += [a, b]
        dns.append(dn)
    if res is not None:
        specs.append(pl.BlockSpec((tm, tn), lambda i, j: (i, j)))
        arrs.append(res)
    n = len(pairs)

    def body(*refs):
        o_ref = refs[-1]
        acc = None
        for p in range(n):
            d = _dot(refs[2 * p][...].astype(bf16), refs[2 * p + 1][...].astype(bf16), dns[p])
            acc = d if acc is None else acc + d
        if scale != 1.0:
            acc = acc * scale
        if res is not None:
            acc = acc + refs[2 * n][...]
        o_ref[...] = acc.astype(o_ref.dtype)

    if out_chip:
        out_spec = pl.BlockSpec((None, tm, tn), lambda i, j: (j, i, 0))
        out_shape = jax.ShapeDtypeStruct((N_CHIPS, M, tn), out_dtype)
    else:
        out_spec = pl.BlockSpec((tm, tn), lambda i, j: (i, j))
        out_shape = jax.ShapeDtypeStruct((M, N), out_dtype)
    return pl.pallas_call(
        body, name=name, grid=(M // tm, N // tn), in_specs=specs, out_specs=out_spec, out_shape=out_shape,
        compiler_params=_cp(("parallel", "parallel")),
    )(*arrs)


def _rms_fwd(x, g, name):
    T, D = x.shape
    tm = _tile(T, (512, 256, 128))

    def body(x_ref, g_ref, h_ref):
        xv = x_ref[...]
        r = lax.rsqrt(jnp.mean(xv * xv, axis=-1, keepdims=True) + EPS)
        h_ref[...] = (xv * r * g_ref[...]).astype(bf16)

    return pl.pallas_call(
        body, name=name, grid=(T // tm,),
        in_specs=[pl.BlockSpec((tm, D), lambda i: (i, 0)), pl.BlockSpec((1, D), lambda i: (0, 0))],
        out_specs=pl.BlockSpec((tm, D), lambda i: (i, 0)),
        out_shape=jax.ShapeDtypeStruct((T, D), bf16), compiler_params=_cp(("parallel",)),
    )(x, g.reshape(1, D))


def _fold8(v):
    r, d = v.shape
    return jnp.sum(v.reshape(r // 8, 8, d), axis=0)


def _rms_bwd(dh, x, g, dres, name):
    T, D = x.shape
    tm = _tile(T, (512, 256, 128))

    def body(dh_ref, x_ref, g_ref, dres_ref, dx_ref, dxb_ref, dg_ref):
        xv = x_ref[...]
        r = lax.rsqrt(jnp.mean(xv * xv, axis=-1, keepdims=True) + EPS)
        xh = xv * r
        dhv = dh_ref[...]
        dxh = dhv * g_ref[...]
        dx = dres_ref[...] + r * (dxh - xh * jnp.mean(dxh * xh, axis=-1, keepdims=True))
        dx_ref[...] = dx
        dxb_ref[...] = dx.astype(bf16)

        @pl.when(pl.program_id(0) == 0)
        def _():
            dg_ref[...] = jnp.zeros_like(dg_ref)

        dg_ref[...] += _fold8(dhv * xh)

    row = pl.BlockSpec((tm, D), lambda i: (i, 0))
    dx, dxb, dg8 = pl.pallas_call(
        body, name=name, grid=(T // tm,), in_specs=[row, row, pl.BlockSpec((1, D), lambda i: (0, 0)), row],
        out_specs=[row, row, pl.BlockSpec((8, D), lambda i: (0, 0))],
        out_shape=[jax.ShapeDtypeStruct((T, D), f32), jax.ShapeDtypeStruct((T, D), bf16), jax.ShapeDtypeStruct((8, D), f32)],
        compiler_params=_cp(("arbitrary",)),
    )(dh, x, g.reshape(1, D), dres)
    return (dx, dxb), dg8


def _final_loss(x, target, g):
    T, D = x.shape
    tm = _tile(T, (512, 256, 128))

    def body(x_ref, t_ref, g_ref, dx_ref, dxb_ref, dg_ref, ls_ref):
        xv = x_ref[...]
        r = lax.rsqrt(jnp.mean(xv * xv, axis=-1, keepdims=True) + EPS)
        xh = xv * r
        gv = g_ref[...]
        e = xh * gv - t_ref[...]
        dy = e * (1.0 / D)
        dxh = dy * gv
        dx = r * (dxh - xh * jnp.mean(dxh * xh, axis=-1, keepdims=True))
        dx_ref[...] = dx
        dxb_ref[...] = dx.astype(bf16)

        @pl.when(pl.program_id(0) == 0)
        def _():
            dg_ref[...] = jnp.zeros_like(dg_ref)
            ls_ref[...] = jnp.zeros_like(ls_ref)

        dg_ref[...] += _fold8(dy * xh)
        ls_ref[...] += _fold8(e * e)

    row = pl.BlockSpec((tm, D), lambda i: (i, 0))
    acc = pl.BlockSpec((8, D), lambda i: (0, 0))
    dx, dxb, dg8, ls8 = pl.pallas_call(
        body, name="final_loss", grid=(T // tm,), in_specs=[row, row, pl.BlockSpec((1, D), lambda i: (0, 0))],
        out_specs=[row, row, acc, acc],
        out_shape=[jax.ShapeDtypeStruct((T, D), f32), jax.ShapeDtypeStruct((T, D), bf16), jax.ShapeDtypeStruct((8, D), f32),
                   jax.ShapeDtypeStruct((8, D), f32)],
        compiler_params=_cp(("arbitrary",)),
    )(x, target, g.reshape(1, D))
    return (dx, dxb), dg8, ls8


def _ffn_up(h, wgu, f, name, carry=None):
    T, D = h.shape
    FP = wgu.shape[2]
    tm = _tile(T, (1024, 512, 256, 128))

    def body(h_ref, wg_ref, wu_ref, a_ref, b_ref, m_ref):
        hv = h_ref[...]
        a = _dot(hv, wg_ref[...])
        b = _dot(hv, wu_ref[...])
        a_ref[...] = a.astype(bf16)
        b_ref[...] = b.astype(bf16)
        m_ref[...] = (a * _sigmoid(a) * b).astype(bf16)

    o = jax.ShapeDtypeStruct((T, N_CHIPS * FP), bf16)
    ospec = pl.BlockSpec((tm, FP), lambda j, i: (i, j))
    grid = (N_CHIPS, T // tm)
    body, cin, cout, cshapes, cscratch = _carried(body, 3, 3, grid, carry)
    outs = pl.pallas_call(
        body, name=name, grid=grid,
        in_specs=[pl.BlockSpec((tm, D), lambda j, i: (i, 0)), pl.BlockSpec((None, D, FP), lambda j, i: (j, 2 * f, 0)),
                  pl.BlockSpec((None, D, FP), lambda j, i: (j, 2 * f + 1, 0))] + cin,
        out_specs=[ospec, ospec, ospec] + cout, out_shape=[o, o, o] + cshapes, scratch_shapes=cscratch,
        compiler_params=_cp(("arbitrary", "arbitrary") if carry else ("parallel", "parallel")),
    )(h, wgu, wgu, *(carry.ins if carry else []))
    return outs[:3], outs[3:]


def _ffn_bwd_mid(dx, m, a, b, f, name):
    T, D = dx.shape
    FP = a.shape[1] // N_CHIPS
    tm = _tile(T, (1024, 512, 256, 128))

    def body(dx_ref, wd_ref, a_ref, b_ref, da_ref, db_ref, m_ref):
        dm = 0.5 * _dot(dx_ref[...].astype(bf16), wd_ref[...], NT)
        av = a_ref[...].astype(f32)
        bv = b_ref[...].astype(f32)
        s = _sigmoid(av)
        silu = av * s
        da_ref[...] = (dm * bv * (s * (1.0 + av * (1.0 - s)))).astype(bf16)
        db_ref[...] = (dm * silu).astype(bf16)
        m_ref[...] = (silu * bv).astype(bf16)

    o = jax.ShapeDtypeStruct((T, N_CHIPS * FP), bf16)
    ospec = pl.BlockSpec((tm, FP), lambda j, i: (i, j))
    return pl.pallas_call(
        body, name=name, grid=(N_CHIPS, T // tm),
        in_specs=[pl.BlockSpec((tm, D), lambda j, i: (i, 0)), pl.BlockSpec((None, FP, D), lambda j, i: (j, f, 0)), ospec, ospec],
        out_specs=[ospec, ospec, ospec], out_shape=[o, o, o], compiler_params=_cp(("parallel", "parallel")),
    )(dx, m, a, b)


def _shift_down(x, k, row):
    return jnp.where(row < k, 0.0, pltpu.roll(x, k, 0))


def _shift_up(x, k, row):
    s = x.shape[0]
    return jnp.where(row >= s - k, 0.0, pltpu.roll(x, s - k, 0))


def _window_sum(x, g, row, shift):
    s2 = x + shift(x, 1, row)
    s4 = s2 + shift(s2, 2, row)
    s8 = s4 + shift(s4, 4, row)
    s16 = s8 + shift(s8, 8, row)
    return jnp.where(g == 0, s2, jnp.where(g == 1, s4, jnp.where(g == 2, s8, s16)))


def _pool_fwd(proj, pool_w, scale, name):
    B, S = proj.shape[0], proj.shape[1]
    G = len(POOL_WINDOWS)

    def body(u_ref, w_ref, sc_ref, y_ref):
        g = pl.program_id(1)
        u = u_ref[0].astype(f32)
        row = lax.broadcasted_iota(jnp.int32, u.shape, 0)
        win = _window_sum(u, g, row, _shift_down)
        cnt = jnp.minimum(row + 1, jnp.left_shift(2, g)).astype(f32)
        pooled = win / cnt - u
        mixed = _dot(pooled.astype(bf16), w_ref[0].astype(bf16))
        y_ref[0] = (mixed * sc_ref[...]).astype(bf16)

    return pl.pallas_call(
        body, name=name, grid=(B, G),
        in_specs=[pl.BlockSpec((1, S, HEAD), lambda b, g: (b, 0, g)), pl.BlockSpec((1, HEAD, HEAD), lambda b, g: (g, 0, 0)),
                  pl.BlockSpec((1, HEAD), lambda b, g: (0, g))],
        out_specs=pl.BlockSpec((1, S, HEAD), lambda b, g: (b, 0, g)),
        out_shape=jax.ShapeDtypeStruct((B, S, BRANCH), bf16), compiler_params=_cp(("parallel", "parallel")),
    )(proj, pool_w, scale.reshape(1, BRANCH))


def _pool_bwd(proj, dy, pool_w, scale, name):
    B, S = proj.shape[0], proj.shape[1]
    G = len(POOL_WINDOWS)

    def body(u_ref, dy_ref, w_ref, sc_ref, du_ref, dw_ref, dsc_ref):
        g = pl.program_id(0)
        u = u_ref[0].astype(f32)
        row = lax.broadcasted_iota(jnp.int32, u.shape, 0)
        cnt = jnp.minimum(row + 1, jnp.left_shift(2, g)).astype(f32)
        pooled = _window_sum(u, g, row, _shift_down) / cnt - u
        wv = w_ref[0].astype(bf16)
        mixed = _dot(pooled.astype(bf16), wv)
        dyv = dy_ref[0].astype(f32)
        dmix = (dyv * sc_ref[...]).astype(bf16)
        dpool = _dot(dmix, wv, NT)
        du_ref[0] = (_window_sum(dpool / cnt, g, row, _shift_up) - dpool).astype(bf16)

        @pl.when(pl.program_id(1) == 0)
        def _():
            dw_ref[...] = jnp.zeros_like(dw_ref)
            dsc_ref[...] = jnp.zeros_like(dsc_ref)

        dw_ref[0] += _dot(pooled.astype(bf16), dmix, TN)
        dsc_ref[...] += _fold8(dyv * mixed)

    return pl.pallas_call(
        body, name=name, grid=(G, B),
        in_specs=[pl.BlockSpec((1, S, HEAD), lambda g, b: (b, 0, g)), pl.BlockSpec((1, S, HEAD), lambda g, b: (b, 0, g)),
                  pl.BlockSpec((1, HEAD, HEAD), lambda g, b: (g, 0, 0)), pl.BlockSpec((1, HEAD), lambda g, b: (0, g))],
        out_specs=[pl.BlockSpec((1, S, HEAD), lambda g, b: (b, 0, g)), pl.BlockSpec((1, HEAD, HEAD), lambda g, b: (g, 0, 0)),
                   pl.BlockSpec((8, HEAD), lambda g, b: (0, g))],
        out_shape=[jax.ShapeDtypeStruct((B, S, BRANCH), bf16), jax.ShapeDtypeStruct((G, HEAD, HEAD), f32),
                   jax.ShapeDtypeStruct((8, BRANCH), f32)],
        compiler_params=_cp(("parallel", "arbitrary")),
    )(proj, dy, pool_w, scale.reshape(1, BRANCH))


def _split_dot(x, u):
    hi = x.astype(bf16)
    lo = (x - hi.astype(f32)).astype(bf16)
    return _dot(hi, u) + _dot(lo, u)


def _sb_fwd(sbqkv, name, carry=None):
    B, S, _ = sbqkv.shape
    KB = SB_BLOCK
    TQ = _tile(S, (SB_QUERIES, SB_QUERIES // 2, SB_QUERIES // 4, KB))
    ns = TQ // KB
    nq = S // TQ
    scale = HEAD ** -0.5

    def body(q_ref, k_ref, v_ref, o_ref, tot_ref):
        r = lax.broadcasted_iota(jnp.int32, (KB, KB), 0)
        c = lax.broadcasted_iota(jnp.int32, (KB, KB), 1)
        wide_r = lax.broadcasted_iota(jnp.int32, (TQ, KB), 0)
        wide_c = lax.broadcasted_iota(jnp.int32, (TQ, KB), 1)
        after = (r > c).astype(bf16)

        def sub(qa, krows, mask, run, acc):
            z = _dot(qa, k_ref[0, krows, :], NT) * scale
            sp = jnp.maximum(z, 0.0) + jnp.log(1.0 + jnp.exp(-jnp.abs(z)))
            ln = -sp
            if mask is not None:
                ln = jnp.where(mask, ln, 0.0)
            w = jnp.exp(z - sp + _split_dot(ln, after) + run)
            if mask is not None:
                w = jnp.where(mask, w, 0.0)
            acc = acc + _dot(w.astype(bf16), v_ref[0, krows, :])
            return run + jnp.sum(ln, axis=1, keepdims=True), acc

        def qloop(i, carry):
            base = pl.multiple_of(i * TQ, TQ)
            qi = q_ref[0, pl.ds(base, TQ), :]
            cr = (jnp.zeros((TQ, LANES), f32), jnp.zeros((TQ, HEAD), f32))
            for s in range(ns - 1, -1, -1):
                cr = sub(qi, pl.ds(base + s * KB, KB), (s * KB + wide_c) < wide_r, *cr)

            def group(t, cr):
                g0 = pl.multiple_of((i - 1 - t) * TQ, TQ)
                for s in range(ns - 1, -1, -1):
                    cr = sub(qi, pl.ds(g0 + s * KB, KB), None, *cr)
                return cr

            run, acc = lax.fori_loop(0, i, group, cr)
            o_ref[0, pl.ds(base, TQ), :] = acc.astype(bf16)
            tot_ref[0, 0, pl.ds(base, TQ), :] = run
            return carry

        lax.fori_loop(0, nq, qloop, 0)

    def spec(off):
        return pl.BlockSpec((1, S, HEAD), lambda b, h, off=off: (b, 0, off + h))

    body, cin, cout, cshapes, cscratch = _carried(body, 3, 2, (B, N_HEADS), carry)
    outs = pl.pallas_call(
        body, name=name, grid=(B, N_HEADS), in_specs=[spec(0), spec(N_HEADS), spec(2 * N_HEADS)] + cin,
        out_specs=[pl.BlockSpec((1, S, HEAD), lambda b, h: (b, 0, h)),
                   pl.BlockSpec((1, 1, S, LANES), lambda b, h: (b, h, 0, 0))] + cout,
        out_shape=[jax.ShapeDtypeStruct((B, S, BRANCH), bf16), jax.ShapeDtypeStruct((B, N_HEADS, S, LANES), f32)] + cshapes,
        scratch_shapes=cscratch,
        compiler_params=_cp(("arbitrary", "arbitrary") if carry else ("parallel", "parallel")),
    )(sbqkv, sbqkv, sbqkv, *(carry.ins if carry else []))
    return outs[:2], outs[2:]


def _sb_bwd(sbqkv, do, tot, name, carry=None):
    B, S, _ = sbqkv.shape
    KB = SB_BLOCK
    TQ = _tile(S, (SB_QUERIES, SB_QUERIES // 2, SB_QUERIES // 4, KB))
    ns = TQ // KB
    nq = S // TQ
    scale = HEAD ** -0.5

    def body(q_ref, k_ref, v_ref, do_ref, tot_ref, dq_ref, dk_ref, dv_ref, dk_acc, dv_acc):
        r = lax.broadcasted_iota(jnp.int32, (KB, KB), 0)
        c = lax.broadcasted_iota(jnp.int32, (KB, KB), 1)
        wide_r = lax.broadcasted_iota(jnp.int32, (TQ, KB), 0)
        wide_c = lax.broadcasted_iota(jnp.int32, (TQ, KB), 1)
        after = (r > c).astype(bf16)
        before = (r < c).astype(bf16)
        dk_acc[...] = jnp.zeros_like(dk_acc)
        dv_acc[...] = jnp.zeros_like(dv_acc)

        def sub(qi, doi, total, rows, mask, cl, cp, dq):
            kj = k_ref[0, rows, :]
            vj = v_ref[0, rows, :]
            z = _dot(qi, kj, NT) * scale
            sp = jnp.maximum(z, 0.0) + jnp.log(1.0 + jnp.exp(-jnp.abs(z)))
            ln = -sp
            if mask is not None:
                ln = jnp.where(mask, ln, 0.0)
            bs = jnp.sum(ln, axis=1, keepdims=True)
            w = jnp.exp(z - sp + _split_dot(ln, after) + (total - cl - bs))
            if mask is not None:
                w = jnp.where(mask, w, 0.0)
            p = _dot(doi, vj, NT) * w
            qsum = cp + _split_dot(p, before)
            sig = jnp.exp(z - sp)
            dz = (p - (p + qsum) * sig) * scale
            if mask is not None:
                dz = jnp.where(mask, dz, 0.0)
            dzb = dz.astype(bf16)
            dq = dq + _dot(dzb, kj)
            dk_acc[rows, :] += _dot(dzb, qi, TN)
            dv_acc[rows, :] += _dot(w.astype(bf16), doi, TN)
            return cl + bs, cp + jnp.sum(p, axis=1, keepdims=True), dq

        def qloop(i, carry):
            base = pl.multiple_of(i * TQ, TQ)
            rows = pl.ds(base, TQ)
            qi = q_ref[0, rows, :]
            doi = do_ref[0, rows, :]
            total = tot_ref[0, 0, rows, :][:, 0:1]
            zero = jnp.zeros((TQ, 1), f32)

            def group(g, st):
                g0 = pl.multiple_of(g * TQ, TQ)
                for s in range(ns):
                    st = sub(qi, doi, total, pl.ds(g0 + s * KB, KB), None, *st)
                return st

            st = lax.fori_loop(0, i, group, (zero, zero, jnp.zeros((TQ, HEAD), f32)))
            for s in range(ns):
                st = sub(qi, doi, total, pl.ds(base + s * KB, KB), (s * KB + wide_c) < wide_r, *st)
            dq_ref[0, rows, :] = st[2].astype(bf16)
            return carry

        lax.fori_loop(0, nq, qloop, 0)
        dk_ref[0] = dk_acc[...].astype(bf16)
        dv_ref[0] = dv_acc[...].astype(bf16)

    def spec(off):
        return pl.BlockSpec((1, S, HEAD), lambda b, h, off=off: (b, 0, off + h))

    o = jax.ShapeDtypeStruct((B, S, BRANCH), bf16)
    body, cin, cout, cshapes, cscratch = _carried(body, 5, 3, (B, N_HEADS), carry)
    outs = pl.pallas_call(
        body, name=name, grid=(B, N_HEADS),
        in_specs=[spec(0), spec(N_HEADS), spec(2 * N_HEADS), spec(0),
                  pl.BlockSpec((1, 1, S, LANES), lambda b, h: (b, h, 0, 0))] + cin,
        out_specs=[spec(0), spec(0), spec(0)] + cout, out_shape=[o, o, o] + cshapes,
        scratch_shapes=[pltpu.VMEM((S, HEAD), f32), pltpu.VMEM((S, HEAD), f32)] + cscratch,
        compiler_params=_cp(("arbitrary", "arbitrary") if carry else ("parallel", "parallel")),
    )(sbqkv, sbqkv, sbqkv, do, tot, *(carry.ins if carry else []))
    return outs[:3], outs[3:]


def _dn_params(a_log, dt_bias):
    p = jnp.zeros((8, LANES), f32)
    p = p.at[0, :N_HEADS].set(a_log)
    return p.at[1, :N_HEADS].set(dt_bias)


def _dn_prep(ab, par, name):
    B, S, _ = ab.shape
    R = 2 * CHUNK
    nt = S // R

    def body(ab_ref, par_ref, gcb_ref, bb_ref, gcr_ref):
        x = ab_ref[0]
        g = -jnp.exp(par_ref[0:1, :]) * _softplus(x + par_ref[1:2, :])
        r = lax.broadcasted_iota(jnp.int32, (R, R), 0)
        c = lax.broadcasted_iota(jnp.int32, (R, R), 1)
        tri = ((r >= c) & ((r >> 6) == (c >> 6))).astype(f32)
        cs = jnp.dot(tri, g, precision=HI, preferred_element_type=f32)
        beta = _sigmoid(x)
        cst = cs.T
        for h in range(N_HEADS):
            gcb_ref[0, h] = jnp.broadcast_to(cs[:, h:h + 1], (R, LANES))
            bb_ref[0, h] = jnp.broadcast_to(beta[:, N_HEADS + h:N_HEADS + h + 1], (R, LANES))
            gcr_ref[0, h, 0] = jnp.broadcast_to(cst[h:h + 1, 0:CHUNK], (8, CHUNK))
            gcr_ref[0, h, 1] = jnp.broadcast_to(cst[h:h + 1, CHUNK:R], (8, CHUNK))

    return pl.pallas_call(
        body, name=name, grid=(B, nt),
        in_specs=[pl.BlockSpec((1, R, LANES), lambda b, i: (b, i, 0)), pl.BlockSpec((8, LANES), lambda b, i: (0, 0))],
        out_specs=[pl.BlockSpec((1, N_HEADS, R, LANES), lambda b, i: (b, 0, i, 0)),
                   pl.BlockSpec((1, N_HEADS, R, LANES), lambda b, i: (b, 0, i, 0)),
                   pl.BlockSpec((1, N_HEADS, 2, 8, CHUNK), lambda b, i: (b, 0, i, 0, 0))],
        out_shape=[jax.ShapeDtypeStruct((B, N_HEADS, S, LANES), f32), jax.ShapeDtypeStruct((B, N_HEADS, S, LANES), f32),
                   jax.ShapeDtypeStruct((B, N_HEADS, S // CHUNK, 8, CHUNK), f32)],
        compiler_params=_cp(("parallel", "parallel")),
    )(ab, par)


def _bmm(a, b, prec=None):
    return jnp.einsum("nij,njk->nik", a, b, preferred_element_type=f32, precision=prec)


def _bmm_nt(a, b, prec=None):
    return jnp.einsum("nik,njk->nij", a, b, preferred_element_type=f32, precision=prec)


def _bmm_tn(a, b, prec=None):
    return jnp.einsum("nki,nkj->nij", a, b, preferred_element_type=f32, precision=prec)


def _tri_inv(L):
    C = L.shape[-1]
    r = lax.broadcasted_iota(jnp.int32, (C, C), 0)
    c = lax.broadcasted_iota(jnp.int32, (C, C), 1)
    eye = (r == c).astype(f32)
    bd16 = (r >> 4) == (c >> 4)
    bd32 = (r >> 5) == (c >> 5)
    mm = functools.partial(_bmm, prec=MID)
    n1 = -jnp.where(bd16, L, 0.0)
    n2 = mm(n1, n1)
    n4 = mm(n2, n2)
    n8 = mm(n4, n4)
    t = mm(mm(mm(eye + n1, eye + n2), eye + n4), eye + n8)
    t = t - mm(mm(t, jnp.where(bd32 & jnp.logical_not(bd16), L, 0.0)), t)
    t = t - mm(mm(t, jnp.where(bd32, 0.0, L)), t)
    return t


def _conv_silu(x, w, row):
    c = w[3:4] * x + w[2:3] * _shift_down(x, 1, row) + w[1:2] * _shift_down(x, 2, row) + w[0:1] * _shift_down(x, 3, row)
    return c, c * _sigmoid(c)


def _dn_intra(qn, kn, v, gcb, beta, gr):
    C = CHUNK
    r = lax.broadcasted_iota(jnp.int32, (C, C), 0)
    c = lax.broadcasted_iota(jnp.int32, (C, C), 1)
    incl = r >= c
    diff = gcb[:, :, :C] - gr
    dm = jnp.where(incl, jnp.exp(jnp.where(incl, diff, 0.0)), 0.0)
    ds = jnp.where(r > c, dm, 0.0)
    kb = kn * beta
    knb = kn.astype(bf16)
    L = _bmm_nt(kb.astype(bf16), knb) * ds
    eg = jnp.exp(gcb)
    a = _bmm_nt(qn.astype(bf16), knb) * dm
    gl = gcb[:, C - 1:C, :]
    ekd = jnp.exp(gl - gcb)
    return dict(dm=dm, ds=ds, kb=kb, L=L, eg=eg, rhs_u=v * beta, rhs_w=kb * eg, a=a,
                qd=qn * eg, kd=kn * ekd, ekd=ekd, cd=jnp.exp(gl))


def _dn_specs(S):
    def col(off):
        return pl.BlockSpec((1, S, HEAD), lambda b, h, off=off: (b, 0, off + h))

    def cw(off):
        return pl.BlockSpec((4, HEAD), lambda b, h, off=off: (0, off + h))

    per_head = pl.BlockSpec((1, 1, S, LANES), lambda b, h: (b, h, 0, 0))
    rowform = pl.BlockSpec((1, 1, S // CHUNK, 8, CHUNK), lambda b, h: (b, h, 0, 0, 0))
    gain = pl.BlockSpec((1, HEAD), lambda b, h: (0, 0))
    ins = [col(4), col(8), col(12), col(16), cw(0), cw(4), cw(8), per_head, per_head, rowform, gain]
    return ins, per_head


def _dn_chunked_specs(n):
    return [pl.BlockSpec((1, 1, n, CHUNK, CHUNK), lambda b, h: (b, h, 0, 0, 0)),
            pl.BlockSpec((1, 1, n, CHUNK, HEAD), lambda b, h: (b, h, 0, 0, 0)),
            pl.BlockSpec((1, 1, n, CHUNK, HEAD), lambda b, h: (b, h, 0, 0, 0))]


def _dn_act(x_ref, cw_ref, row, normalise, out_scale=1.0):
    _, act = _conv_silu(x_ref[0].astype(f32), cw_ref[...], row)
    if normalise:
        act = act * (lax.rsqrt(jnp.sum(act * act, axis=-1, keepdims=True) + EPS) * out_scale)
    return act


def _dn_group(qn_s, kn_s, v_s, gcb_ref, bb_ref, gcr_ref, g, ng):
    C = CHUNK
    rows = pl.ds(pl.multiple_of(g * (ng * C), ng * C), ng * C)
    ch = pl.ds(g * ng, ng)
    sh = (ng, C, HEAD)
    qn, kn, v = qn_s[rows, :].reshape(sh), kn_s[rows, :].reshape(sh), v_s[rows, :].reshape(sh)
    gcb3, beta = gcb_ref[0, 0, rows, :].reshape(sh), bb_ref[0, 0, rows, :].reshape(sh)
    gr = gcr_ref[0, 0, ch][:, 0:1, :]
    it = _dn_intra(qn, kn, v, gcb3, beta, gr)
    it.update(qn=qn, kn=kn, v=v, gcb=gcb3, beta=beta, gr=gr)
    return rows, ch, it


def _dn_fwd(proj, conv_w, gcb, betab, gcr, gain, name, carry=None):
    B, S, _ = proj.shape
    n, C = S // CHUNK, CHUNK
    ng = min(8, n)
    ins, per_head = _dn_specs(S)

    def body(q_ref, k_ref, v_ref, z_ref, cq_ref, ck_ref, cv_ref, gcb_ref, bb_ref, gcr_ref, gain_ref,
             y_ref, st_ref, vn_ref, t_ref, ub_ref, wb_ref, qn_s, kn_s, v_s, u_s, w_s, qd_s, a_s, cd_s, g_s, h_s):
        row = lax.broadcasted_iota(jnp.int32, (S, HEAD), 0)
        qn_s[...] = _dn_act(q_ref, cq_ref, row, True, HEAD ** -0.5)
        kn_s[...] = _dn_act(k_ref, ck_ref, row, True)
        v_s[...] = _dn_act(v_ref, cv_ref, row, False)

        def group(g, carry):
            _, ch, it = _dn_group(qn_s, kn_s, v_s, gcb_ref, bb_ref, gcr_ref, g, ng)
            t = _tri_inv(it["L"])
            u = _bmm(t, it["rhs_u"], MID)
            wb = _bmm(t, it["rhs_w"], MID).astype(bf16)
            kdb = it["kd"].astype(bf16)
            ub = u.astype(bf16)
            u_s[ch] = u
            w_s[ch] = wb
            t_ref[0, 0, ch] = t
            ub_ref[0, 0, ch] = ub
            wb_ref[0, 0, ch] = wb
            qd_s[ch] = it["qd"].astype(bf16)
            a_s[ch] = it["a"].astype(bf16)
            cd_s[ch] = it["cd"]
            g_s[ch] = _bmm_tn(kdb, wb).astype(bf16)
            h_s[ch] = _bmm_tn(kdb, ub)
            return carry

        lax.fori_loop(0, n // ng, group, 0)

        def step(i, st):
            sb = st.astype(bf16)
            st_ref[0, 0, i] = sb
            return st * cd_s[i] - _dot(g_s[i], sb) + h_s[i]

        lax.fori_loop(0, n, step, jnp.zeros((HEAD, HEAD), f32))

        def group_out(g, carry):
            rows = pl.ds(pl.multiple_of(g * (ng * C), ng * C), ng * C)
            ch = pl.ds(g * ng, ng)
            sn = st_ref[0, 0, ch]
            vn = (u_s[ch] - _bmm(w_s[ch], sn)).astype(bf16)
            vn_ref[0, 0, rows, :] = vn.reshape(ng * C, HEAD)
            o = (_bmm(qd_s[ch], sn) + _bmm(a_s[ch], vn)).reshape(ng * C, HEAD)
            zz = z_ref[0, rows, :].astype(f32)
            rr = lax.rsqrt(jnp.mean(o * o, axis=-1, keepdims=True) + EPS)
            y_ref[0, rows, :] = (o * rr * gain_ref[...] * (zz * _sigmoid(zz))).astype(bf16)
            return carry

        lax.fori_loop(0, n // ng, group_out, 0)

    seq = pltpu.VMEM((S, HEAD), f32)
    body, cin, cout, cshapes, cscratch = _carried(body, len(ins), 6, (B, N_HEADS), carry)
    chunked = _dn_chunked_specs(n)
    outs = pl.pallas_call(
        body, name=name, grid=(B, N_HEADS), in_specs=ins + cin,
        out_specs=[pl.BlockSpec((1, S, HEAD), lambda b, h: (b, 0, h)),
                   pl.BlockSpec((1, 1, n, HEAD, HEAD), lambda b, h: (b, h, 0, 0, 0)), per_head] + chunked + cout,
        out_shape=[jax.ShapeDtypeStruct((B, S, BRANCH), bf16), jax.ShapeDtypeStruct((B, N_HEADS, n, HEAD, HEAD), bf16),
                   jax.ShapeDtypeStruct((B, N_HEADS, S, HEAD), bf16), jax.ShapeDtypeStruct((B, N_HEADS, n, C, C), f32),
                   jax.ShapeDtypeStruct((B, N_HEADS, n, C, HEAD), bf16), jax.ShapeDtypeStruct((B, N_HEADS, n, C, HEAD), bf16)]
        + cshapes,
        scratch_shapes=[seq, seq, seq, pltpu.VMEM((n, C, HEAD), f32), pltpu.VMEM((n, C, HEAD), bf16),
                        pltpu.VMEM((n, C, HEAD), bf16), pltpu.VMEM((n, C, C), bf16), pltpu.VMEM((n, 1, HEAD), f32),
                        pltpu.VMEM((n, HEAD, HEAD), bf16), pltpu.VMEM((n, HEAD, HEAD), f32)] + cscratch,
        compiler_params=_cp(("arbitrary", "arbitrary") if carry else ("parallel", "parallel")),
    )(proj, proj, proj, proj, conv_w, conv_w, conv_w, gcb, betab, gcr, gain.reshape(1, HEAD), *(carry.ins if carry else []))
    return outs[:6], outs[6:]


def _rowsum(x):
    return jnp.sum(x, axis=-1, keepdims=True)


def _dn_bwd(proj, dy, conv_w, gcb, betab, gcr, gain, states, vnew, tinv, ub, wb, name, carry=None):
    B, S, _ = proj.shape
    n, C = S // CHUNK, CHUNK
    ng = min(8, n)
    ins, per_head = _dn_specs(S)
    ins = ins + [pl.BlockSpec((1, S, HEAD), lambda b, h: (b, 0, h)),
                 pl.BlockSpec((1, 1, n, HEAD, HEAD), lambda b, h: (b, h, 0, 0, 0)), per_head] + _dn_chunked_specs(n)

    def body(q_ref, k_ref, v_ref, z_ref, cq_ref, ck_ref, cv_ref, gcb_ref, bb_ref, gcr_ref, gain_ref, dy_ref, st_ref, vn_ref,
             t_ref, ub_ref, wb_ref, dq_ref, dk_ref, dv_ref, dz_ref, dg_ref, dbeta_ref, dconv_ref, dgain_ref,
             qn_s, kn_s, v_s, cd_s, do_s, dsp_s, dvn_s, g_s, q_s):
        row = lax.broadcasted_iota(jnp.int32, (S, HEAD), 0)
        qn_s[...] = _dn_act(q_ref, cq_ref, row, True, HEAD ** -0.5)
        kn_s[...] = _dn_act(k_ref, ck_ref, row, True)
        v_s[...] = _dn_act(v_ref, cv_ref, row, False)
        dgain_ref[...] = jnp.zeros_like(dgain_ref)
        gv = gain_ref[...]

        def group_fwd(g, carry):
            rows, ch, it = _dn_group(qn_s, kn_s, v_s, gcb_ref, bb_ref, gcr_ref, g, ng)
            wb = wb_ref[0, 0, ch]
            ab, qdb = it["a"].astype(bf16), it["qd"].astype(bf16)
            vn = vn_ref[0, 0, rows, :].reshape(ng, C, HEAD)
            o = (_bmm(qdb, st_ref[0, 0, ch]) + _bmm(ab, vn)).reshape(ng * C, HEAD)
            zz = z_ref[0, rows, :].astype(f32)
            dyv = dy_ref[0, rows, :].astype(f32)
            rr = lax.rsqrt(jnp.mean(o * o, axis=-1, keepdims=True) + EPS)
            on = o * rr
            sz = _sigmoid(zz)
            dz_ref[0, rows, :] = (dyv * on * gv * (sz * (1.0 + zz * (1.0 - sz)))).astype(bf16)
            dnrm = dyv * (zz * sz)
            dgain_ref[0, 0] += _fold8(dnrm * on)
            doh = dnrm * gv
            do = rr * (doh - on * jnp.mean(doh * on, axis=-1, keepdims=True))
            dob = do.reshape(ng, C, HEAD).astype(bf16)
            atdo = _bmm_tn(ab, dob)
            cd_s[ch] = it["cd"]
            do_s[ch] = dob
            dvn_s[ch] = atdo
            g_s[ch] = _bmm_tn(it["kd"].astype(bf16), wb).astype(bf16)
            q_s[ch] = _bmm_tn(qdb, dob) - _bmm_tn(wb, atdo.astype(bf16))
            return carry

        lax.fori_loop(0, n // ng, group_fwd, 0)

        def step(t, dsp):
            i = n - 1 - t
            dspb = dsp.astype(bf16)
            dsp_s[i] = dspb
            return dsp * cd_s[i] - _dot(g_s[i], dspb, TN) + q_s[i]

        lax.fori_loop(0, n, step, jnp.zeros((HEAD, HEAD), f32))

        r = lax.broadcasted_iota(jnp.int32, (C, C), 0)
        c = lax.broadcasted_iota(jnp.int32, (C, C), 1)
        upper = r <= c

        def group_bwd(g, carry):
            rows, ch, it = _dn_group(qn_s, kn_s, v_s, gcb_ref, bb_ref, gcr_ref, g, ng)
            sh = (ng, C, HEAD)
            qn, kn, v, beta, gcb3, gr = it["qn"], it["kn"], it["v"], it["beta"], it["gcb"], it["gr"]
            sn = st_ref[0, 0, ch]
            vn = vn_ref[0, 0, rows, :].reshape(sh)
            dsp, dob = dsp_s[ch], do_s[ch]
            dvn = dvn_s[ch] + _bmm(it["kd"].astype(bf16), dsp)
            t, ub, wb = t_ref[0, 0, ch], ub_ref[0, 0, ch], wb_ref[0, 0, ch]
            dvnb = dvn.astype(bf16)
            da = _bmm_nt(dob, vn)
            dat = _bmm_nt(vn, dob)
            dqd = _bmm_nt(dob, sn)
            dkd = _bmm_nt(vn, dsp)
            dcd = jnp.sum(jnp.sum(dsp.astype(f32) * sn.astype(f32), axis=2, keepdims=True), axis=1, keepdims=True)
            dw = -_bmm_nt(dvnb, sn)
            ru = _bmm_tn(t, dvn, MID)
            rw = _bmm_tn(t, dw, MID)
            rub, rwb = ru.astype(bf16), rw.astype(bf16)
            dL = -(_bmm_nt(rub, ub) + _bmm_nt(rwb, wb))
            dLt = -(_bmm_nt(ub, rub) + _bmm_nt(wb, rwb))
            knb, qnb, kbb = kn.astype(bf16), qn.astype(bf16), it["kb"].astype(bf16)
            dmt = jnp.where(upper, jnp.exp(jnp.where(upper, gr - gcb3[:, :, :C], 0.0)), 0.0)
            Lt = _bmm_nt(knb, kbb) * jnp.where(r < c, dmt, 0.0)
            At = _bmm_nt(knb, qnb) * dmt
            dgc = _rowsum(dL * it["L"] + da * it["a"]) - _rowsum(dLt * Lt + dat * At)
            dkk = (dL * it["ds"]).astype(bf16)
            dqk = (da * it["dm"]).astype(bf16)
            dkb = _bmm(dkk, knb) + rw * it["eg"]
            dkn = _bmm_tn(dkk, kbb) + _bmm_tn(dqk, qnb) + dkd * it["ekd"] + dkb * beta
            dqn = _bmm(dqk, knb) + dqd * it["eg"]
            tkd = _rowsum(dkd * it["kd"])
            dgl = jnp.sum(tkd, axis=1, keepdims=True) + dcd * it["cd"][:, :, 0:1]
            dgc = dgc + _rowsum(dqd * it["qd"]) - tkd + _rowsum(rw * it["rhs_w"])
            dbeta = _rowsum(ru * v) + _rowsum(dkb * kn)
            rowc = lax.broadcasted_iota(jnp.int32, (ng, C, 1), 1)
            dgc = dgc + jnp.where(rowc == C - 1, dgl, 0.0)
            rev = jnp.broadcast_to(upper.astype(f32), (ng, C, C))
            dg_ref[0, 0, rows, :] = _bmm(rev, jnp.broadcast_to(dgc, sh), HI).reshape(ng * C, LANES).astype(bf16)
            dbeta_ref[0, 0, rows, :] = jnp.broadcast_to(dbeta, sh).reshape(ng * C, LANES).astype(bf16)
            qn_s[rows, :] = dqn.reshape(ng * C, HEAD)
            kn_s[rows, :] = dkn.reshape(ng * C, HEAD)
            v_s[rows, :] = (ru * beta).reshape(ng * C, HEAD)
            return carry

        lax.fori_loop(0, n // ng, group_bwd, 0)

        def conv_back(x_ref, cw_ref, grad_s, out_ref, slot, normalise, out_scale):
            x = x_ref[0].astype(f32)
            w = cw_ref[...]
            pre, act = _conv_silu(x, w, row)
            dact = grad_s[...]
            if normalise:
                rn = lax.rsqrt(jnp.sum(act * act, axis=-1, keepdims=True) + EPS)
                unit = act * rn
                dact = (out_scale * rn) * (dact - unit * _rowsum(dact * unit))
            s = _sigmoid(pre)
            dc = dact * (s * (1.0 + pre * (1.0 - s)))
            out_ref[0] = (w[3:4] * dc + w[2:3] * _shift_up(dc, 1, row) + w[1:2] * _shift_up(dc, 2, row)
                          + w[0:1] * _shift_up(dc, 3, row)).astype(bf16)
            for tap in range(4):
                xs = x if tap == 3 else _shift_down(x, 3 - tap, row)
                dconv_ref[0, slot, tap:tap + 1, :] = jnp.sum(dc * xs, axis=0, keepdims=True)

        conv_back(q_ref, cq_ref, qn_s, dq_ref, 0, True, HEAD ** -0.5)
        conv_back(k_ref, ck_ref, kn_s, dk_ref, 1, True, 1.0)
        conv_back(v_ref, cv_ref, v_s, dv_ref, 2, False, 1.0)

    o512 = jax.ShapeDtypeStruct((B, S, BRANCH), bf16)
    s512 = pl.BlockSpec((1, S, HEAD), lambda b, h: (b, 0, h))
    ph = jax.ShapeDtypeStruct((B, N_HEADS, S, LANES), bf16)
    seq = pltpu.VMEM((S, HEAD), f32)
    cb = pltpu.VMEM((n, C, HEAD), bf16)
    body, cin, cout, cshapes, cscratch = _carried(body, len(ins), 8, (B, N_HEADS), carry)
    outs = pl.pallas_call(
        body, name=name, grid=(B, N_HEADS), in_specs=ins + cin,
        out_specs=[s512, s512, s512, s512, per_head, per_head, pl.BlockSpec((1, 3, 4, HEAD), lambda b, h: (b, 0, 0, h)),
                   pl.BlockSpec((1, 1, 8, HEAD), lambda b, h: (b, h, 0, 0))] + cout,
        out_shape=[o512, o512, o512, o512, ph, ph, jax.ShapeDtypeStruct((B, 3, 4, BRANCH), f32),
                   jax.ShapeDtypeStruct((B, N_HEADS, 8, HEAD), f32)] + cshapes,
        scratch_shapes=[seq, seq, seq, pltpu.VMEM((n, 1, HEAD), f32), cb, pltpu.VMEM((n, HEAD, HEAD), bf16),
                        pltpu.VMEM((n, C, HEAD), f32), pltpu.VMEM((n, HEAD, HEAD), bf16), pltpu.VMEM((n, HEAD, HEAD), f32)]
        + cscratch,
        compiler_params=_cp(("arbitrary", "arbitrary") if carry else ("parallel", "parallel")),
    )(proj, proj, proj, proj, conv_w, conv_w, conv_w, gcb, betab, gcr, gain.reshape(1, HEAD), dy, states, vnew, tinv, ub, wb,
      *(carry.ins if carry else []))
    return outs[:8], outs[8:]


def _dn_post(ab, par, dg, dbeta, name):
    B, S, _ = ab.shape
    ts = _tile(S, (512, 256, 128))

    def body(ab_ref, par_ref, dg_ref, db_ref, dab_ref, acc_ref):
        x = ab_ref[0]
        lane = lax.broadcasted_iota(jnp.int32, x.shape, 1)
        dgs = jnp.zeros_like(x)
        dbs = jnp.zeros_like(x)
        for h in range(N_HEADS):
            dgs = jnp.where(lane == h, dg_ref[0, h], dgs)
            dbs = jnp.where(lane == N_HEADS + h, db_ref[0, h], dbs)
        nega = -jnp.exp(par_ref[0:1, :])
        pre = x + par_ref[1:2, :]
        da = dgs * nega * _sigmoid(pre)
        beta = _sigmoid(x)
        dab_ref[0] = (da + dbs * beta * (1.0 - beta)).astype(bf16)

        @pl.when((pl.program_id(0) == 0) & (pl.program_id(1) == 0))
        def _():
            acc_ref[...] = jnp.zeros_like(acc_ref)

        acc_ref[0] += _fold8(dgs * nega * _softplus(pre))
        acc_ref[1] += _fold8(da)

    return pl.pallas_call(
        body, name=name, grid=(B, S // ts),
        in_specs=[pl.BlockSpec((1, ts, LANES), lambda b, i: (b, i, 0)), pl.BlockSpec((8, LANES), lambda b, i: (0, 0)),
                  pl.BlockSpec((1, N_HEADS, ts, LANES), lambda b, i: (b, 0, i, 0)),
                  pl.BlockSpec((1, N_HEADS, ts, LANES), lambda b, i: (b, 0, i, 0))],
        out_specs=[pl.BlockSpec((1, ts, LANES), lambda b, i: (b, i, 0)), pl.BlockSpec((2, 8, LANES), lambda b, i: (0, 0, 0))],
        out_shape=[jax.ShapeDtypeStruct((B, S, LANES), bf16), jax.ShapeDtypeStruct((2, 8, LANES), f32)],
        compiler_params=_cp(("arbitrary", "arbitrary")),
    )(ab, par, dg, dbeta)


def _merge_specs(T, D, tm, tn, order):
    nj = D // tn

    def ij(f):
        return (lambda i, j: f(i, j)) if order == "ij" else (lambda j, i: f(i, j))

    ys = [pl.BlockSpec((tm, BRANCH), ij(lambda i, j: (i, 0))) for _ in range(3)]
    wb = pl.BlockSpec((3, BRANCH, tn), ij(lambda i, j: (0, 0, j)))
    gl = [pl.BlockSpec((tm, tn), ij(lambda i, j, k=k: (i, k * nj + j))) for k in range(3)]
    bg = [pl.BlockSpec((1, tn), ij(lambda i, j, k=k: (0, k * nj + j))) for k in range(3)]
    return ys, wb, gl, bg


def _merge_fwd(ys, wb, gl, b_gate, name):
    T, D = ys[0].shape[0], wb.shape[2]
    tm, tn = _tile(T, (512, 256, 128)), _tile(D, (512, 256, 128))
    sy, swb, sgl, sbg = _merge_specs(T, D, tm, tn, "ij")

    def body(y0, y1, y2, wb_ref, g0, g1, g2, b0, b1, b2, o_ref):
        acc = None
        for k, (y, g, b) in enumerate(((y0, g0, b0), (y1, g1, b1), (y2, g2, b2))):
            term = _sigmoid(g[...].astype(f32) + b[...]) * _dot(y[...], wb_ref[k])
            acc = term if acc is None else acc + term
        o_ref[...] = acc.astype(bf16)

    bg = b_gate.reshape(1, 3 * D)
    return pl.pallas_call(
        body, name=name, grid=(T // tm, D // tn), in_specs=sy + [swb] + sgl + sbg,
        out_specs=pl.BlockSpec((tm, tn), lambda i, j: (i, j)), out_shape=jax.ShapeDtypeStruct((T, D), bf16),
        compiler_params=_cp(("parallel", "parallel")),
    )(*ys, wb, gl, gl, gl, bg, bg, bg)


def _merge_bwd(dm, ys, wb, gl, b_gate, name):
    T, D = dm.shape
    tm, tn = _tile(T, (512, 256, 128)), _tile(D, (512, 256, 128))
    sy, swb, sgl, sbg = _merge_specs(T, D, tm, tn, "ji")

    def body(dm_ref, y0, y1, y2, wb_ref, g0, g1, g2, b0, b1, b2, dgl_ref, dbd_ref, dbg_ref):
        dmv = dm_ref[...].astype(f32)

        @pl.when(pl.program_id(1) == 0)
        def _():
            dbg_ref[...] = jnp.zeros_like(dbg_ref)

        for k, (y, g, b) in enumerate(((y0, g0, b0), (y1, g1, b1), (y2, g2, b2))):
            s = _sigmoid(g[...].astype(f32) + b[...])
            dg = dmv * _dot(y[...], wb_ref[k]) * s * (1.0 - s)
            dgl_ref[k] = dg.astype(bf16)
            dbd_ref[k] = (dmv * s).astype(bf16)
            dbg_ref[k] += _fold8(dg)

    bg = b_gate.reshape(1, 3 * D)
    o3 = jax.ShapeDtypeStruct((3, T, D), bf16)
    s3 = pl.BlockSpec((3, tm, tn), lambda j, i: (0, i, j))
    return pl.pallas_call(
        body, name=name, grid=(D // tn, T // tm),
        in_specs=[pl.BlockSpec((tm, tn), lambda j, i: (i, j))] + sy + [swb] + sgl + sbg,
        out_specs=[s3, s3, pl.BlockSpec((3, 8, tn), lambda j, i: (0, 0, j))],
        out_shape=[o3, o3, jax.ShapeDtypeStruct((3, 8, D), f32)],
        compiler_params=_cp(("parallel", "arbitrary")),
    )(dm, *ys, wb, gl, gl, gl, bg, bg, bg)


def _ffn_fwd(x, g, w, f, tag, carry=None):
    T, D = x.shape
    FP = w["wgu"].shape[2]
    h = _rms_fwd(x, g, f"rms_{tag}")
    (a, b, hm), landed = _ffn_up(h, w["wgu"], f, f"ffn_up_{tag}", carry)
    down = [("nn", hm, w["m"], {"K": FP, "ka": k, "bsel": k, "kb": f}) for k in range(N_CHIPS)]
    y = _mm(down, T, D, f32, f"ffn_down_{tag}", res=x, scale=0.5)
    return y, (x, h, a, b), landed


def _ffn_bwd(dy2, saved, g, w, f, tag):
    x, h, a, b = saved
    dy, dyb = dy2
    T, D = x.shape
    FP = w["wgu"].shape[2]
    F4 = N_CHIPS * FP
    da, db, hm = _ffn_bwd_mid(dyb, w["m"], a, b, f, f"ffn_mid_bwd_{tag}")
    dwd = _mm([("tn", hm, dyb, {})], F4, D, bf16, f"ffn_dwd_{tag}", scale=0.5)
    dwg = _mm([("tn", h, da, {})], D, F4, bf16, f"ffn_dwg_{tag}", out_chip=True)
    dwu = _mm([("tn", h, db, {})], D, F4, bf16, f"ffn_dwu_{tag}", out_chip=True)
    tn = _tile(D, (512, 256, 128))
    pairs = [("nt", t, w["wgu"], {"K": FP, "ka": k, "bsel": k, "noff": (2 * f + u) * (D // tn)})
             for u, t in enumerate((da, db)) for k in range(N_CHIPS)]
    dh = _mm(pairs, T, D, f32, f"ffn_dh_{tag}", tn=tn)
    dx, dg8 = _rms_bwd(dh, x, g, dy, f"rms_bwd_{tag}")
    return dx, dict(norm=jnp.sum(dg8, axis=0), wgu=[dwg, dwu], wd=dwd.reshape(N_CHIPS, FP, D))


def _layer_fwd(x, w, B, tag, carry=(None, None, None)):
    T, D = x.shape
    S = T // B
    x1, sv0, landed_ffn = _ffn_fwd(x, w["ffn_norm"][0], w, 0, f"pre_{tag}", carry[2])
    h = _rms_fwd(x1, w["mix_norm"], f"rms_mix_{tag}")
    pm = _mm([("nn", h, w["w_main"], {})], T, 5 * BRANCH, bf16, f"proj_main_{tag}")
    ab = _mm([("nn", h, w["w_ab"], {})], T, LANES, f32, f"proj_ab_{tag}", tn=LANES)
    sb = _mm([("nn", h, w["w_sb"], {})], T, 3 * BRANCH, bf16, f"proj_sb_{tag}")
    gl = _mm([("nn", h, w["w_gates"], {})], T, 3 * D, bf16, f"proj_gates_{tag}")
    pm3, ab3, sb3 = pm.reshape(B, S, -1), ab.reshape(B, S, LANES), sb.reshape(B, S, -1)
    y_pool = _pool_fwd(pm3, w["pool_w"], w["pool_scale"], f"pool_{tag}")
    par = _dn_params(w["dn_A_log"], w["dn_dt_bias"])
    gcb, betab, gcr = _dn_prep(ab3, par, f"dn_prep_{tag}")
    (y_dn, states, vnew, tinv, ub, wb), landed_dn = _dn_fwd(pm3, w["dn_conv"], gcb, betab, gcr, w["dn_out_norm"], f"dn_fwd_{tag}", carry[0])
    (y_sb, tot), landed_sb = _sb_fwd(sb3, f"sb_fwd_{tag}", carry[1])
    ys = [y_pool.reshape(T, BRANCH), y_dn.reshape(T, BRANCH), y_sb.reshape(T, BRANCH)]
    merged = _merge_fwd(ys, w["w_branch"], gl, w["b_gate"], f"merge_{tag}")
    dc = D // N_CHIPS
    out_pairs = [("nn", merged, w["m"], {"K": dc, "ka": k, "bsel": k, "kb": _w_out_block(w)}) for k in range(N_CHIPS)]
    x2 = _mm(out_pairs, T, D, f32, f"mix_out_{tag}", res=x1)
    x3, sv1, _ = _ffn_fwd(x2, w["ffn_norm"][1], w, 1, f"post_{tag}")
    saved = dict(sv0=sv0, sv1=sv1, x1=x1, h=h, pm3=pm3, ab3=ab3, sb3=sb3, gl=gl, par=par, gcb=gcb, betab=betab, gcr=gcr,
                 states=states, vnew=vnew, tinv=tinv, ub=ub, wb=wb, tot=tot, ys=ys, merged=merged)
    return x3, saved, (landed_dn, landed_sb, landed_ffn)


def _layer_bwd(dx3, w, sv, B, tag, carry=(None, None)):
    T, D = dx3[0].shape
    S = T // B
    (dx2, dx2b), g1 = _ffn_bwd(dx3, sv["sv1"], w["ffn_norm"][1], w, 1, f"post_{tag}")
    dc = D // N_CHIPS
    dmerged = _mm([("nt", dx2b, w["m"], {"b_by_chip": True, "noff": _w_out_block(w)})], T, D, bf16, f"mix_dmerged_{tag}", tn=dc)
    dw_out = _mm([("tn", sv["merged"], dx2b, {})], D, D, bf16, f"mix_dwout_{tag}")
    ys = sv["ys"]
    dgl, dbd, dbg8 = _merge_bwd(dmerged, ys, w["w_branch"], sv["gl"], w["b_gate"], f"merge_bwd_{tag}")
    dys, dwb = [], []
    for k in range(3):
        dys.append(_mm([("nt", dbd, w["w_branch"], {"asel": k, "bsel": k})], T, BRANCH, bf16, f"branch_dy{k}_{tag}"))
        dwb.append(_mm([("tn", ys[k], dbd, {"bsel": k})], BRANCH, D, bf16, f"branch_dw{k}_{tag}"))
    pm3, ab3, sb3 = sv["pm3"], sv["ab3"], sv["sb3"]
    du, dpool_w, dsc8 = _pool_bwd(pm3, dys[0].reshape(B, S, BRANCH), w["pool_w"], w["pool_scale"], f"pool_bwd_{tag}")
    (dq, dk, dv, dz, dg, dbeta, dconv, dgain), landed = _dn_bwd(
        pm3, dys[1].reshape(B, S, BRANCH), w["dn_conv"], sv["gcb"], sv["betab"], sv["gcr"], w["dn_out_norm"],
        sv["states"], sv["vnew"], sv["tinv"], sv["ub"], sv["wb"], f"dn_bwd_{tag}", carry[0])
    dab, dn_acc = _dn_post(ab3, sv["par"], dg, dbeta, f"dn_post_{tag}")
    (dsq, dsk, dsv), landed_sb = _sb_bwd(sb3, dys[2].reshape(B, S, BRANCH), sv["tot"], f"sb_bwd_{tag}", carry[1])
    main_parts = [t.reshape(T, BRANCH) for t in (du, dq, dk, dv, dz)]
    sb_parts = [t.reshape(T, BRANCH) for t in (dsq, dsk, dsv)]
    dab2 = dab.reshape(T, LANES)
    pairs = [("nt", t, w["w_main"], {"K": BRANCH, "kb": k}) for k, t in enumerate(main_parts)]
    pairs.append(("nt", dab2, w["w_ab"], {}))
    pairs += [("nt", t, w["w_sb"], {"K": BRANCH, "kb": k}) for k, t in enumerate(sb_parts)]
    pairs += [("nt", dgl, w["w_gates"], {"K": D, "kb": k, "asel": k}) for k in range(3)]
    dh = _mm(pairs, T, D, f32, f"mix_dh_{tag}")
    h = sv["h"]
    dw_cols = [_mm([("tn", h, t, {})], D, BRANCH, bf16, f"dwin_main{k}_{tag}") for k, t in enumerate(main_parts)]
    dw_cols.append(_mm([("tn", h, dab2, {})], D, LANES, bf16, f"dwin_ab_{tag}", tn=LANES)[:, :2 * N_HEADS])
    dw_cols += [_mm([("tn", h, t, {})], D, BRANCH, bf16, f"dwin_sb{k}_{tag}") for k, t in enumerate(sb_parts)]
    dw_cols += [_mm([("tn", h, dgl, {"bsel": k})], D, D, bf16, f"dwin_gate{k}_{tag}") for k in range(3)]
    dx1, dmix8 = _rms_bwd(dh, sv["x1"], w["mix_norm"], dx2, f"rms_mix_bwd_{tag}")
    dx0, g0 = _ffn_bwd(dx1, sv["sv0"], w["ffn_norm"][0], w, 0, f"pre_{tag}")
    dwb = jnp.stack(dwb).reshape(3, BRANCH, N_CHIPS, dc).transpose(2, 0, 1, 3).reshape(N_CHIPS, -1, D)
    dw_in = jnp.concatenate(dw_cols, axis=1)
    pc = dw_in.shape[1] // N_CHIPS
    grads = dict(
        ffn_norm=jnp.stack([g0["norm"], g1["norm"]]),
        A=jnp.concatenate(g0["wgu"] + g1["wgu"], axis=1),
        M=jnp.concatenate([g0["wd"], g1["wd"], dw_out.reshape(N_CHIPS, dc, D), dwb], axis=1),
        C=jnp.stack([dw_in[:, k * pc:(k + 1) * pc] for k in range(N_CHIPS)]),
        mix_norm=jnp.sum(dmix8, axis=0), b_gate=jnp.sum(dbg8, axis=1).reshape(3 * D),
        pool_w=dpool_w, pool_scale=jnp.sum(dsc8, axis=0), dn_conv=jnp.sum(dconv, axis=0).transpose(1, 0, 2).reshape(4, 3 * BRANCH),
        dn_A_log=jnp.sum(dn_acc[0], axis=0)[:N_HEADS], dn_dt_bias=jnp.sum(dn_acc[1], axis=0)[:N_HEADS],
        dn_out_norm=jnp.sum(dgain, axis=(0, 1, 2)))
    return dx0, grads, (landed, landed_sb)


def _local_step(x, target, layers, final_norm, B):
    saved = []
    for l, w in enumerate(layers):
        x, sv, _ = _layer_fwd(x, w, B, f"l{l}")
        saved.append(sv)
    dx, dfn8, ls8 = _final_loss(x, target, final_norm)
    grads = [None] * len(layers)
    for l in reversed(range(len(layers))):
        dx, grads[l], _ = _layer_bwd(dx, layers[l], saved[l], B, f"l{l}")
    return jnp.sum(ls8), dx[0], grads, jnp.sum(dfn8, axis=0)


def _adamw(w, g, m, v, name):
    shape = w.shape
    cols = shape[-1]
    rows = math.prod(shape[:-1]) if len(shape) > 1 else 1
    w2, g2, m2, v2 = (t.reshape(rows, cols) for t in (w, g, m, v))
    block_elems = 256 * 1024
    tr = rows if rows * cols <= block_elems else _tile(rows, [t for t in (512, 256, 128, 64, 32, 16, 8) if t * cols <= block_elems])

    def body(w_ref, g_ref, m_ref, v_ref, d_ref, mo_ref, vo_ref):
        gv = g_ref[...]
        mn = ADAM_B1 * m_ref[...] + (1.0 - ADAM_B1) * gv
        vn = ADAM_B2 * v_ref[...] + (1.0 - ADAM_B2) * (gv * gv)
        m_hat = mn / (1.0 - ADAM_B1 ** ADAM_STEP)
        v_hat = vn / (1.0 - ADAM_B2 ** ADAM_STEP)
        d_ref[...] = -ADAM_LR * (m_hat / (jnp.sqrt(v_hat) + ADAM_EPS) + ADAM_WD * w_ref[...])
        mo_ref[...] = mn
        vo_ref[...] = vn

    spec = pl.BlockSpec((tr, cols), lambda i: (i, 0))
    o = jax.ShapeDtypeStruct((rows, cols), f32)
    d, mo, vo = pl.pallas_call(
        body, name=name, grid=(rows // tr,), in_specs=[spec] * 4, out_specs=[spec] * 3, out_shape=[o, o, o],
        compiler_params=_cp(("parallel",)),
    )(w2, g2, m2, v2)
    return d.reshape(shape), mo.reshape(shape), vo.reshape(shape)


MESH = pl.DeviceIdType.MESH
_ANY = pl.BlockSpec(memory_space=pl.ANY)


def _place():
    x, y, c = lax.axis_index("x"), lax.axis_index("y"), lax.axis_index("c")
    return x, y, c, [(1 - x, y), (x, 1 - y), (1 - x, 1 - y)]


def _chip_index():
    return 2 * lax.axis_index("x") + lax.axis_index("y")


def _half(c, rh):
    return pl.ds(c * rh, rh)


def _remote(src, dst, ssem, rsem, to):
    return pltpu.make_async_remote_copy(src_ref=src, dst_ref=dst, send_sem=ssem, recv_sem=rsem, device_id=to,
                                        device_id_type=MESH)


class _ChipExchange:
    def __init__(self, kind, ins):
        self.kind, self.ins = kind, list(ins)
        self.n = len(self.ins)
        self.out_shapes = [jax.ShapeDtypeStruct((N_CHIPS,) + a.shape[-2:], a.dtype) for a in self.ins]
        self.scratch = [pltpu.SemaphoreType.DMA((self.n, 3)), pltpu.SemaphoreType.DMA((self.n, 3))]

    def _copies(self, in_refs, out_refs, ssem, rsem):
        x, y, c, chips = _place()
        me = 2 * x + y
        pairs = []
        for o, (src, dst) in enumerate(zip(in_refs, out_refs)):
            for k, (px, py) in enumerate(chips):
                peer = 2 * px + py
                if self.kind == "gather":
                    rows = _half(c, src.shape[0] // 2)
                    out, land = (src.at[rows], dst.at[me, rows]), dst.at[peer, rows]
                else:
                    out, land = (src.at[peer], dst.at[me]), dst.at[peer]
                pairs.append((_remote(out[0], out[1], ssem.at[o, k], rsem.at[o, k], (px, py, c)),
                              _remote(land, land, ssem.at[o, k], rsem.at[o, k], (px, py, c))))
        return pairs

    def start(self, in_refs, out_refs, ssem, rsem):
        for mine, _ in self._copies(in_refs, out_refs, ssem, rsem):
            mine.start()

    def wait(self, in_refs, out_refs, ssem, rsem):
        for mine, landing in self._copies(in_refs, out_refs, ssem, rsem):
            landing.wait_recv()
            mine.wait_send()

    def standalone(self, name):
        n = self.n

        def body(*refs):
            ins, outs, (ssem, rsem) = refs[:n], refs[n:2 * n], refs[2 * n:]
            self.start(ins, outs, ssem, rsem)
            self.wait(ins, outs, ssem, rsem)

        return pl.pallas_call(body, name=name, in_specs=[_ANY] * n, out_specs=[_ANY] * n, out_shape=self.out_shapes,
                              scratch_shapes=self.scratch)(*self.ins)


def _carried(body, n_in, n_out, grid, carry):
    if carry is None:
        return body, [], [], [], []
    n = carry.n

    def wrapped(*refs):
        ins, cin = refs[:n_in], refs[n_in:n_in + n]
        outs, cout = refs[n_in + n:n_in + n + n_out], refs[n_in + n + n_out:n_in + 2 * n + n_out]
        scratch, (ssem, rsem) = refs[n_in + 2 * n + n_out:-2], refs[-2:]
        step = pl.program_id(0) * grid[1] + pl.program_id(1)

        @pl.when(step == 0)
        def _():
            carry.start(cin, cout, ssem, rsem)

        body(*ins, *outs, *scratch)

        @pl.when(step == grid[0] * grid[1] - 1)
        def _():
            carry.wait(cin, cout, ssem, rsem)

    return wrapped, [_ANY] * n, [_ANY] * n, carry.out_shapes, carry.scratch


def _gather_finish(shards, landed, name):
    n = len(shards)

    def body(*refs):
        outs, (ssem, rsem) = refs[n:2 * n], refs[2 * n:]
        x, y, c, chips = _place()
        started = []
        for o, buf in enumerate(outs):
            rh = buf.shape[1] // 2
            for k, (px, py) in enumerate(chips):
                block = buf.at[2 * px + py, _half(c, rh)]
                cp = _remote(block, block, ssem.at[o, k], rsem.at[o, k], (x, y, 1 - c))
                cp.start()
                started.append(cp)
        for o, buf in enumerate(outs):
            rh = buf.shape[1] // 2
            for k, (px, py) in enumerate(chips):
                block = buf.at[2 * px + py, _half(1 - c, rh)]
                _remote(block, block, ssem.at[o, k], rsem.at[o, k], (x, y, 1 - c)).wait_recv()
        for cp in started:
            cp.wait_send()

    outs = pl.pallas_call(
        body, name=name, in_specs=[_ANY] * n, out_specs=[_ANY] * n,
        out_shape=[jax.ShapeDtypeStruct(a.shape, a.dtype) for a in landed], input_output_aliases={i: i for i in range(n)},
        scratch_shapes=[pltpu.SemaphoreType.DMA((n, 3)), pltpu.SemaphoreType.DMA((n, 3))],
    )(*landed)
    me = _chip_index()
    return [lax.dynamic_update_slice(g, s[None], (me, 0, 0)) for g, s in zip(outs, shards)]


def _gather_chips(shards, name):
    landed = _ChipExchange("gather", shards).standalone(f"{name}_ici")
    return _gather_finish(shards, landed, f"{name}_pass")


def _pair_swap_halves(ps, name):
    n = len(ps)

    def body(*refs):
        ins, outs, (ssem, rsem) = refs[:n], refs[n:2 * n], refs[2 * n:]
        x, y, c, _ = _place()
        cps = [_remote(p.at[:, _half(1 - c, p.shape[1] // 2)], out, ssem.at[o], rsem.at[o], (x, y, 1 - c))
               for o, (p, out) in enumerate(zip(ins, outs))]
        for cp in cps:
            cp.start()
        for cp in cps:
            cp.wait()

    return pl.pallas_call(
        body, name=name, in_specs=[_ANY] * n, out_specs=[_ANY] * n,
        out_shape=[jax.ShapeDtypeStruct((p.shape[0], p.shape[1] // 2, p.shape[2]), p.dtype) for p in ps],
        scratch_shapes=[pltpu.SemaphoreType.DMA((n,)), pltpu.SemaphoreType.DMA((n,))],
    )(*ps)


def _pair_add(p, got, name):
    n, R, W = p.shape
    rh = R // 2
    tr = _row_tile(rh, W, 2 * 1024 * 1024)
    nb = rh // tr

    def body(c_ref, p_ref, g_ref, o_ref):
        o_ref[...] = (p_ref[...].astype(f32) + g_ref[...].astype(f32)).astype(o_ref.dtype)

    return pl.pallas_call(
        body, name=name,
        grid_spec=pltpu.PrefetchScalarGridSpec(
            num_scalar_prefetch=1, grid=(n, nb),
            in_specs=[pl.BlockSpec((1, tr, W), lambda j, i, c_ref: (j, c_ref[0] * nb + i, 0)),
                      pl.BlockSpec((1, tr, W), lambda j, i, c_ref: (j, i, 0))],
            out_specs=pl.BlockSpec((1, tr, W), lambda j, i, c_ref: (j, i, 0))),
        out_shape=jax.ShapeDtypeStruct((n, rh, W), p.dtype), compiler_params=_cp(("parallel", "parallel")),
    )(lax.axis_index("c").astype(jnp.int32).reshape(1), p, got)


def _own_slot_filled(landed, ps):
    me = _chip_index()
    return [lax.dynamic_update_slice(out, lax.dynamic_slice(p, (me, 0, 0), (1,) + p.shape[1:]), (me, 0, 0))
            for out, p in zip(landed, ps)]


def _sum_slots(r4, name):
    n, R, W = r4.shape
    tr = _row_tile(R, W, 1024 * 1024)

    def body(r_ref, o_ref):
        acc = r_ref[0].astype(f32)
        for k in range(1, n):
            acc = acc + r_ref[k].astype(f32)
        o_ref[...] = acc

    return pl.pallas_call(
        body, name=name, grid=(R // tr,), in_specs=[pl.BlockSpec((n, tr, W), lambda i: (0, i, 0))],
        out_specs=pl.BlockSpec((tr, W), lambda i: (i, 0)), out_shape=jax.ShapeDtypeStruct((R, W), f32),
        compiler_params=_cp(("parallel",)),
    )(r4)


def _pair_share(ss, name):
    n = len(ss)

    def body(*refs):
        ins, outs, (ssem, rsem) = refs[:n], refs[n:2 * n], refs[2 * n:]
        x, y, c, _ = _place()
        cps = [_remote(s, out.at[c], ssem.at[o], rsem.at[o], (x, y, 1 - c)) for o, (s, out) in enumerate(zip(ins, outs))]
        for cp in cps:
            cp.start()
        for o, (s, out) in enumerate(zip(ins, outs)):
            _remote(s, out.at[1 - c], ssem.at[o], rsem.at[o], (x, y, 1 - c)).wait_recv()
        for cp in cps:
            cp.wait_send()

    outs = pl.pallas_call(
        body, name=name, in_specs=[_ANY] * n, out_specs=[_ANY] * n,
        out_shape=[jax.ShapeDtypeStruct((2,) + s.shape, s.dtype) for s in ss],
        scratch_shapes=[pltpu.SemaphoreType.DMA((n,)), pltpu.SemaphoreType.DMA((n,))],
    )(*ss)
    c = lax.axis_index("c")
    return [lax.dynamic_update_slice(out, s[None], (c, 0, 0)).reshape(2 * s.shape[0], s.shape[1]) for out, s in zip(outs, ss)]


def _reduce_begin(ps, tag):
    got = _pair_swap_halves(ps, f"rs_pair_swap_{tag}")
    return [_pair_add(p, g, f"rs_pair_add{o}_{tag}") for o, (p, g) in enumerate(zip(ps, got))]


def _reduce_finish(pair_sums, landed, tag):
    r4 = _own_slot_filled(landed, pair_sums)
    return _pair_share([_sum_slots(r, f"rs_sum{o}_{tag}") for o, r in enumerate(r4)], f"rs_share_{tag}")


def _reduce_to_chips(ps, tag):
    pair_sums = _reduce_begin(ps, tag)
    return _reduce_finish(pair_sums, _ChipExchange("scatter", pair_sums).standalone(f"rs_scatter_{tag}"), tag)


def _pad_rows(a, mult):
    r = (-a.shape[-2]) % mult
    return a if r == 0 else jnp.pad(a, [(0, 0)] * (a.ndim - 2) + [(0, r), (0, 0)])


def _hidden_pad(fs):
    return -(-fs // LANES) * LANES


def _w_out_block(w):
    return 2 * w["wgu"].shape[2] // (w["m"].shape[2] // N_CHIPS)


def _pack_wgu(wg, wu):
    fs = wg.shape[-1]
    t = jnp.stack([wg, wu], axis=1).astype(bf16)
    return jnp.pad(t, ((0, 0), (0, 0), (0, 0), (0, _hidden_pad(fs) - fs))).reshape(-1, _hidden_pad(fs))


def _pack_m(wd, w_out, w_branch):
    fs, D = wd.shape[1:]
    wdp = jnp.pad(wd.astype(bf16), ((0, 0), (0, _hidden_pad(fs) - fs), (0, 0))).reshape(-1, D)
    return jnp.concatenate([wdp, w_out.astype(bf16), w_branch.astype(bf16).reshape(-1, D)], axis=0)


def _w_in_cols(cg, lo, hi):
    p = cg.shape[2]
    parts = [cg[k][:, max(lo, k * p) - k * p:min(hi, (k + 1) * p) - k * p] for k in range(N_CHIPS)
             if max(lo, k * p) < min(hi, (k + 1) * p)]
    return parts[0] if len(parts) == 1 else jnp.concatenate(parts, axis=1)


def _layer_weights(ag, mg, cg, small):
    D = mg.shape[2]
    dc = D // N_CHIPS
    fp2 = 2 * ag.shape[2]
    wb = mg[:, fp2 + dc:].reshape(N_CHIPS, 3, BRANCH, dc).transpose(1, 2, 0, 3).reshape(3, BRANCH, D)
    c0, c1, c2 = 5 * BRANCH, 5 * BRANCH + 2 * N_HEADS, 8 * BRANCH + 2 * N_HEADS
    w = dict(small)
    w.update(wgu=ag, m=mg, w_branch=wb, w_main=_w_in_cols(cg, 0, c0),
             w_ab=jnp.pad(_w_in_cols(cg, c0, c1), ((0, 0), (0, LANES - 2 * N_HEADS))), w_sb=_w_in_cols(cg, c1, c2),
             w_gates=_w_in_cols(cg, c2, N_CHIPS * cg.shape[2]))
    return w


def _small_pack(pieces):
    flat, offs, r = [], [], 0
    for a in pieces:
        v = a.reshape(-1)
        pad = (-v.shape[0]) % PACK_W
        flat.append(jnp.pad(v, (0, pad)) if pad else v)
        offs.append(r)
        r += (v.shape[0] + pad) // PACK_W
    pack = jnp.concatenate(flat).reshape(r, PACK_W)
    return _pad_rows(pack, 16), offs


def _small_unpack(pack, offs, shapes):
    out = []
    for o, s in zip(offs, shapes):
        n = math.prod(s)
        rows = -(-n // PACK_W)
        out.append(pack[o:o + rows].reshape(-1)[:n].reshape(s))
    return out


SMALL_SHARDED = ("ffn_norm", "dn_conv")
SMALL_REPLICATED = ("mix_norm", "b_gate", "pool_w", "pool_scale", "dn_A_log", "dn_dt_bias", "dn_out_norm")


def kernel(x, ffn_norm, ffn_w_gate, ffn_w_up, ffn_w_down, mix_norm, w_in, b_gate, pool_w, pool_scale, dn_conv, dn_A_log, dn_dt_bias, dn_out_norm, w_branch, w_out, final_norm, loss_target, m_ffn_norm, m_ffn_w_gate, m_ffn_w_up, m_ffn_w_down, m_mix_norm, m_w_in, m_b_gate, m_pool_w, m_pool_scale, m_dn_conv, m_dn_A_log, m_dn_dt_bias, m_dn_out_norm, m_w_branch, m_w_out, m_final_norm, v_ffn_norm, v_ffn_w_gate, v_ffn_w_up, v_ffn_w_down, v_mix_norm, v_w_in, v_b_gate, v_pool_w, v_pool_scale, v_dn_conv, v_dn_A_log, v_dn_dt_bias, v_dn_out_norm, v_w_branch, v_w_out, v_final_norm):
    names = ("ffn_norm", "ffn_w_gate", "ffn_w_up", "ffn_w_down", "mix_norm", "w_in", "b_gate", "pool_w", "pool_scale", "dn_conv",
             "dn_A_log", "dn_dt_bias", "dn_out_norm", "w_branch", "w_out", "final_norm")
    wts = dict(zip(names, (ffn_norm, ffn_w_gate, ffn_w_up, ffn_w_down, mix_norm, w_in, b_gate, pool_w, pool_scale, dn_conv,
                           dn_A_log, dn_dt_bias, dn_out_norm, w_branch, w_out, final_norm)))
    ms = dict(zip(names, (m_ffn_norm, m_ffn_w_gate, m_ffn_w_up, m_ffn_w_down, m_mix_norm, m_w_in, m_b_gate, m_pool_w, m_pool_scale,
                          m_dn_conv, m_dn_A_log, m_dn_dt_bias, m_dn_out_norm, m_w_branch, m_w_out, m_final_norm)))
    vs = dict(zip(names, (v_ffn_norm, v_ffn_w_gate, v_ffn_w_up, v_ffn_w_down, v_mix_norm, v_w_in, v_b_gate, v_pool_w, v_pool_scale,
                          v_dn_conv, v_dn_A_log, v_dn_dt_bias, v_dn_out_norm, v_w_branch, v_w_out, v_final_norm)))
    B, S, D = x.shape
    T = B * S
    L = ffn_w_gate.shape[0]
    fs, ds_, cs = ffn_w_gate.shape[3], D // N_CHIPS, dn_conv.shape[2]
    fp = _hidden_pad(fs)
    chip = _chip_index()

    shards = [[_pack_wgu(ffn_w_gate[l], ffn_w_up[l]), _pack_m(ffn_w_down[l], w_out[l], w_branch[l]), w_in[l].astype(bf16)]
              for l in range(L)]
    small_shard, soffs = _small_pack([ffn_norm, dn_conv])
    small_g = _gather_chips([small_shard], "gather_small")[0]

    def chip_major(i, a):
        rows = -(-a.size // PACK_W)
        return small_g[:, soffs[i]:soffs[i] + rows].reshape(N_CHIPS, -1)[:, :a.size].reshape((N_CHIPS,) + a.shape)

    fn_full = jnp.moveaxis(chip_major(0, ffn_norm), 0, 2).reshape(L, 2, D)
    conv_full = jnp.moveaxis(chip_major(1, dn_conv), 0, 2).reshape(L, dn_conv.shape[1], N_CHIPS * cs)

    def small_params(l):
        return dict(ffn_norm=fn_full[l], mix_norm=mix_norm[l], b_gate=b_gate[l], pool_w=pool_w[l], pool_scale=pool_scale[l],
                    dn_conv=conv_full[l], dn_A_log=dn_A_log[l], dn_dt_bias=dn_dt_bias[l], dn_out_norm=dn_out_norm[l])

    xt = x.reshape(T, D)
    gathered = _gather_chips(shards[0], "gather_l0")
    layers, saved = [], []
    for l in range(L):
        w = _layer_weights(*gathered, small_params(l))
        more = l + 1 < L
        carry = [_ChipExchange("gather", [s]) if more else None for s in shards[min(l + 1, L - 1)]]
        xt, sv, (landed_dn, landed_sb, landed_ffn) = _layer_fwd(xt, w, B, f"l{l}", (carry[0], carry[1], carry[2]))
        layers.append(w)
        saved.append(sv)
        if more:
            gathered = _gather_finish(shards[l + 1], [landed_dn[0], landed_sb[0], landed_ffn[0]], f"gather_pass_l{l + 1}")
    dx, dfn8, ls8 = _final_loss(xt, loss_target.reshape(T, D), final_norm)
    loss = lax.psum(jnp.sum(ls8), ("x", "y", "c")) * (0.5 / D)

    grads, red, pending = [None] * L, [None] * L, None
    for l in reversed(range(L)):
        carry = (None, None) if pending is None else (_ChipExchange("scatter", pending[:2]), _ChipExchange("scatter", pending[2:]))
        dx, grads[l], (landed_dn, landed_sb) = _layer_bwd(dx, layers[l], saved[l], B, f"l{l}", carry)
        if pending is not None:
            red[l + 1] = _reduce_finish(pending, list(landed_dn) + list(landed_sb), f"l{l + 1}")
        pending = _reduce_begin([grads[l]["A"], grads[l]["M"], grads[l]["C"]], f"l{l}")
    red[0] = _reduce_finish(pending, _ChipExchange("scatter", pending).standalone("rs_scatter_l0"), "l0")

    small_names = SMALL_SHARDED + SMALL_REPLICATED
    pieces = [g[k] for g in grads for k in small_names] + [jnp.sum(dfn8, axis=0)]
    spack, offs = _small_pack(pieces)
    sred = _reduce_to_chips([jnp.broadcast_to(spack[None], (N_CHIPS,) + spack.shape)], "small")[0]
    small_red = _small_unpack(sred, offs, [p.shape for p in pieces])

    gw = {k: [] for k in names if k != "final_norm"}
    for l in range(L):
        ga, gm, gc = red[l]
        gu = ga.reshape(2, 2, D, fp)[..., :fs]
        gw["ffn_w_gate"].append(gu[:, 0])
        gw["ffn_w_up"].append(gu[:, 1])
        gw["ffn_w_down"].append(gm[:2 * fp].reshape(2, fp, D)[:, :fs])
        gw["w_out"].append(gm[2 * fp:2 * fp + ds_])
        gw["w_branch"].append(gm[2 * fp + ds_:].reshape(3, BRANCH, ds_))
        gw["w_in"].append(gc)
        sm = dict(zip(small_names, small_red[l * len(small_names):(l + 1) * len(small_names)]))
        gw["ffn_norm"].append(lax.dynamic_slice_in_dim(sm["ffn_norm"], chip * ds_, ds_, axis=1))
        gw["dn_conv"].append(lax.dynamic_slice_in_dim(sm["dn_conv"], chip * cs, cs, axis=1))
        for k in SMALL_REPLICATED:
            gw[k].append(sm[k])
    gw = {k: jnp.stack(v) for k, v in gw.items()}
    gw["final_norm"] = small_red[-1]

    deltas, new_m, new_v = [], [], []
    for k in names:
        d, mo, vo = _adamw(wts[k], gw[k], ms[k], vs[k], f"adamw_{k}")
        deltas.append(d)
        new_m.append(mo)
        new_v.append(vo)
    return (loss, dx[0].reshape(B, S, D), *[gw[k] for k in names], *deltas, *new_m, *new_v)
```

```python
import functools
import math

import jax
import jax.numpy as jnp
from jax import lax
from jax.experimental import pallas as pl
from jax.experimental.pallas import tpu as pltpu

f32 = jnp.float32
bf16 = jnp.bfloat16
HI = lax.Precision.HIGHEST
MID = lax.Precision.HIGH

EPS = 1e-6
HEAD = 128
N_HEADS = 4
BRANCH = 512
CHUNK = 64
SB_BLOCK = 128
SB_QUERIES = 1024
POOL_WINDOWS = (2, 4, 8, 16)
N_CHIPS = 4
LANES = 128
PACK_W = 1024
ADAM_LR, ADAM_B1, ADAM_B2, ADAM_EPS, ADAM_WD, ADAM_STEP = 0.001, 0.9, 0.999, 1e-08, 0.01, 10
VMEM_LIMIT = 56 * 1024 * 1024
MM_VMEM_BUDGET = 36 * 1024 * 1024

NN = (((1,), (0,)), ((), ()))
NT = (((1,), (1,)), ((), ()))
TN = (((0,), (0,)), ((), ()))


def _cp(sem=None):
    return pltpu.CompilerParams(dimension_semantics=sem, vmem_limit_bytes=VMEM_LIMIT)


def _tile(n, prefs):
    for p in prefs:
        if n % p == 0:
            return p
    return n


def _row_tile(rows, width, max_elems):
    for d in range(rows, 0, -1):
        if rows % d == 0 and (d % 16 == 0 or d == rows) and d * width <= max_elems:
            return d
    return rows


def _dot(a, b, dn=NN):
    return lax.dot_general(a, b, dn, preferred_element_type=f32)


def _sigmoid(x):
    return 0.5 * jnp.tanh(0.5 * x) + 0.5


def _softplus(x):
    return jnp.maximum(x, 0.0) + jnp.log1p(jnp.exp(-jnp.abs(x)))


def _mm(pairs, M, N, out_dtype, name, tm=None, tn=None, res=None, scale=1.0, out_chip=False):
    tn = N // N_CHIPS if out_chip else (tn or _tile(N, (512, 256, 128)))

    def vmem_bytes(rows):
        total = rows * tn * (2 * jnp.dtype(out_dtype).itemsize + (8 if res is not None else 0) + 8)
        for form, a, b, o in pairs:
            K = a.shape[-2] if form == "tn" else (o.get("K") or a.shape[-1])
            total += 2 * K * (rows * a.dtype.itemsize + tn * b.dtype.itemsize)
        return total

    tm = tm or next((t for t in (1024, 512, 256, 128) if M % t == 0 and vmem_bytes(t) <= MM_VMEM_BUDGET), M)
    specs, arrs, dns = [], [], []

    def lead(sel, shape, imap):
        if sel is None:
            return pl.BlockSpec(shape, imap)
        return pl.BlockSpec((None,) + shape, lambda i, j, sel=sel, imap=imap: (sel,) + imap(i, j))

    for form, a, b, o in pairs:
        ka, kb, moff, noff = o.get("ka", 0), o.get("kb", 0), o.get("moff", 0), o.get("noff", 0)
        asel, bsel = o.get("asel"), o.get("bsel")
        if form == "nn":
            K = o.get("K") or a.shape[-1]
            sa = lead(asel, (tm, K), lambda i, j, ka=ka, moff=moff: (i + moff, ka))
            sb = lead(bsel, (K, tn), lambda i, j, kb=kb, noff=noff: (kb, j + noff))
            dn = NN
        elif form == "nt":
            K = o.get("K") or a.shape[-1]
            sa = lead(asel, (tm, K), lambda i, j, ka=ka, moff=moff: (i + moff, ka))
            if o.get("b_by_chip"):
                sb = pl.BlockSpec((None, tn, K), lambda i, j, kb=kb, noff=noff: (j, noff, kb))
            else:
                sb = lead(bsel, (tn, K), lambda i, j, kb=kb, noff=noff: (j + noff, kb))
            dn = NT
        else:
            K = a.shape[-2]
            sa = lead(asel, (K, tm), lambda i, j, moff=moff: (0, i + moff))
            sb = lead(bsel, (K, tn), lambda i, j, noff=noff: (0, j + noff))
            dn = TN
        specs += [sa, sb]
        arrs += [a, b]
        dns.append(dn)
    if res is not None:
        specs.append(pl.BlockSpec((tm, tn), lambda i, j: (i, j)))
        arrs.append(res)
    n = len(pairs)

    def body(*refs):
        o_ref = refs[-1]
        acc = None
        for p in range(n):
            d = _dot(refs[2 * p][...].astype(bf16), refs[2 * p + 1][...].astype(bf16), dns[p])
            acc = d if acc is None else acc + d
        if scale != 1.0:
            acc = acc * scale
        if res is not None:
            acc = acc + refs[2 * n][...]
        o_ref[...] = acc.astype(o_ref.dtype)

    if out_chip:
        out_spec = pl.BlockSpec((None, tm, tn), lambda i, j: (j, i, 0))
        out_shape = jax.ShapeDtypeStruct((N_CHIPS, M, tn), out_dtype)
    else:
        out_spec = pl.BlockSpec((tm, tn), lambda i, j: (i, j))
        out_shape = jax.ShapeDtypeStruct((M, N), out_dtype)
    return pl.pallas_call(
        body, name=name, grid=(M // tm, N // tn), in_specs=specs, out_specs=out_spec, out_shape=out_shape,
        compiler_params=_cp(("parallel", "parallel")),
    )(*arrs)


def _rms_fwd(x, g, name):
    T, D = x.shape
    tm = _tile(T, (512, 256, 128))

    def body(x_ref, g_ref, h_ref):
        xv = x_ref[...]
        r = lax.rsqrt(jnp.mean(xv * xv, axis=-1, keepdims=True) + EPS)
        h_ref[...] = (xv * r * g_ref[...]).astype(bf16)

    return pl.pallas_call(
        body, name=name, grid=(T // tm,),
        in_specs=[pl.BlockSpec((tm, D), lambda i: (i, 0)), pl.BlockSpec((1, D), lambda i: (0, 0))],
        out_specs=pl.BlockSpec((tm, D), lambda i: (i, 0)),
        out_shape=jax.ShapeDtypeStruct((T, D), bf16), compiler_params=_cp(("parallel",)),
    )(x, g.reshape(1, D))


def _fold8(v):
    r, d = v.shape
    return jnp.sum(v.reshape(r // 8, 8, d), axis=0)


def _rms_bwd(dh, x, g, dres, name):
    T, D = x.shape
    tm = _tile(T, (512, 256, 128))

    def body(dh_ref, x_ref, g_ref, dres_ref, dx_ref, dxb_ref, dg_ref):
        xv = x_ref[...]
        r = lax.rsqrt(jnp.mean(xv * xv, axis=-1, keepdims=True) + EPS)
        xh = xv * r
        dhv = dh_ref[...]
        dxh = dhv * g_ref[...]
        dx = dres_ref[...] + r * (dxh - xh * jnp.mean(dxh * xh, axis=-1, keepdims=True))
        dx_ref[...] = dx
        dxb_ref[...] = dx.astype(bf16)

        @pl.when(pl.program_id(0) == 0)
        def _():
            dg_ref[...] = jnp.zeros_like(dg_ref)

        dg_ref[...] += _fold8(dhv * xh)

    row = pl.BlockSpec((tm, D), lambda i: (i, 0))
    dx, dxb, dg8 = pl.pallas_call(
        body, name=name, grid=(T // tm,), in_specs=[row, row, pl.BlockSpec((1, D), lambda i: (0, 0)), row],
        out_specs=[row, row, pl.BlockSpec((8, D), lambda i: (0, 0))],
        out_shape=[jax.ShapeDtypeStruct((T, D), f32), jax.ShapeDtypeStruct((T, D), bf16), jax.ShapeDtypeStruct((8, D), f32)],
        compiler_params=_cp(("arbitrary",)),
    )(dh, x, g.reshape(1, D), dres)
    return (dx, dxb), dg8


def _final_loss(x, target, g):
    T, D = x.shape
    tm = _tile(T, (512, 256, 128))

    def body(x_ref, t_ref, g_ref, dx_ref, dxb_ref, dg_ref, ls_ref):
        xv = x_ref[...]
        r = lax.rsqrt(jnp.mean(xv * xv, axis=-1, keepdims=True) + EPS)
        xh = xv * r
        gv = g_ref[...]
        e = xh * gv - t_ref[...]
        dy = e * (1.0 / D)
        dxh = dy * gv
        dx = r * (dxh - xh * jnp.mean(dxh * xh, axis=-1, keepdims=True))
        dx_ref[...] = dx
        dxb_ref[...] = dx.astype(bf16)

        @pl.when(pl.program_id(0) == 0)
        def _():
            dg_ref[...] = jnp.zeros_like(dg_ref)
            ls_ref[...] = jnp.zeros_like(ls_ref)

        dg_ref[...] += _fold8(dy * xh)
        ls_ref[...] += _fold8(e * e)

    row = pl.BlockSpec((tm, D), lambda i: (i, 0))
    acc = pl.BlockSpec((8, D), lambda i: (0, 0))
    dx, dxb, dg8, ls8 = pl.pallas_call(
        body, name="final_loss", grid=(T // tm,), in_specs=[row, row, pl.BlockSpec((1, D), lambda i: (0, 0))],
        out_specs=[row, row, acc, acc],
        out_shape=[jax.ShapeDtypeStruct((T, D), f32), jax.ShapeDtypeStruct((T, D), bf16), jax.ShapeDtypeStruct((8, D), f32),
                   jax.ShapeDtypeStruct((8, D), f32)],
        compiler_params=_cp(("arbitrary",)),
    )(x, target, g.reshape(1, D))
    return (dx, dxb), dg8, ls8


def _ffn_up(h, wgu, f, name, carry=None):
    T, D = h.shape
    FP = wgu.shape[2]
    tm = _tile(T, (1024, 512, 256, 128))

    def body(h_ref, wg_ref, wu_ref, a_ref, b_ref, m_ref):
        hv = h_ref[...]
        a = _dot(hv, wg_ref[...])
        b = _dot(hv, wu_ref[...])
        a_ref[...] = a.astype(bf16)
        b_ref[...] = b.astype(bf16)
        m_ref[...] = (a * _sigmoid(a) * b).astype(bf16)

    o = jax.ShapeDtypeStruct((T, N_CHIPS * FP), bf16)
    ospec = pl.BlockSpec((tm, FP), lambda j, i: (i, j))
    grid = (N_CHIPS, T // tm)
    body, cin, cout, cshapes, cscratch = _carried(body, 3, 3, grid, carry)
    outs = pl.pallas_call(
        body, name=name, grid=grid,
        in_specs=[pl.BlockSpec((tm, D), lambda j, i: (i, 0)), pl.BlockSpec((None, D, FP), lambda j, i: (j, 2 * f, 0)),
                  pl.BlockSpec((None, D, FP), lambda j, i: (j, 2 * f + 1, 0))] + cin,
        out_specs=[ospec, ospec, ospec] + cout, out_shape=[o, o, o] + cshapes, scratch_shapes=cscratch,
        compiler_params=_cp(("arbitrary", "arbitrary") if carry else ("parallel", "parallel")),
    )(h, wgu, wgu, *(carry.ins if carry else []))
    return outs[:3], outs[3:]


def _ffn_bwd_mid(dx, m, a, b, f, name):
    T, D = dx.shape
    FP = a.shape[1] // N_CHIPS
    tm = _tile(T, (1024, 512, 256, 128))

    def body(dx_ref, wd_ref, a_ref, b_ref, da_ref, db_ref, m_ref):
        dm = 0.5 * _dot(dx_ref[...].astype(bf16), wd_ref[...], NT)
        av = a_ref[...].astype(f32)
        bv = b_ref[...].astype(f32)
        s = _sigmoid(av)
        silu = av * s
        da_ref[...] = (dm * bv * (s * (1.0 + av * (1.0 - s)))).astype(bf16)
        db_ref[...] = (dm * silu).astype(bf16)
        m_ref[...] = (silu * bv).astype(bf16)

    o = jax.ShapeDtypeStruct((T, N_CHIPS * FP), bf16)
    ospec = pl.BlockSpec((tm, FP), lambda j, i: (i, j))
    return pl.pallas_call(
        body, name=name, grid=(N_CHIPS, T // tm),
        in_specs=[pl.BlockSpec((tm, D), lambda j, i: (i, 0)), pl.BlockSpec((None, FP, D), lambda j, i: (j, f, 0)), ospec, ospec],
        out_specs=[ospec, ospec, ospec], out_shape=[o, o, o], compiler_params=_cp(("parallel", "parallel")),
    )(dx, m, a, b)


def _shift_down(x, k, row):
    return jnp.where(row < k, 0.0, pltpu.roll(x, k, 0))


def _shift_up(x, k, row):
    s = x.shape[0]
    return jnp.where(row >= s - k, 0.0, pltpu.roll(x, s - k, 0))


def _window_sum(x, g, row, shift):
    s2 = x + shift(x, 1, row)
    s4 = s2 + shift(s2, 2, row)
    s8 = s4 + shift(s4, 4, row)
    s16 = s8 + shift(s8, 8, row)
    return jnp.where(g == 0, s2, jnp.where(g == 1, s4, jnp.where(g == 2, s8, s16)))


def _pool_fwd(proj, pool_w, scale, name):
    B, S = proj.shape[0], proj.shape[1]
    G = len(POOL_WINDOWS)

    def body(u_ref, w_ref, sc_ref, y_ref):
        g = pl.program_id(1)
        u = u_ref[0].astype(f32)
        row = lax.broadcasted_iota(jnp.int32, u.shape, 0)
        win = _window_sum(u, g, row, _shift_down)
        cnt = jnp.minimum(row + 1, jnp.left_shift(2, g)).astype(f32)
        pooled = win / cnt - u
        mixed = _dot(pooled.astype(bf16), w_ref[0].astype(bf16))
        y_ref[0] = (mixed * sc_ref[...]).astype(bf16)

    return pl.pallas_call(
        body, name=name, grid=(B, G),
        in_specs=[pl.BlockSpec((1, S, HEAD), lambda b, g: (b, 0, g)), pl.BlockSpec((1, HEAD, HEAD), lambda b, g: (g, 0, 0)),
                  pl.BlockSpec((1, HEAD), lambda b, g: (0, g))],
        out_specs=pl.BlockSpec((1, S, HEAD), lambda b, g: (b, 0, g)),
        out_shape=jax.ShapeDtypeStruct((B, S, BRANCH), bf16), compiler_params=_cp(("parallel", "parallel")),
    )(proj, pool_w, scale.reshape(1, BRANCH))


def _pool_bwd(proj, dy, pool_w, scale, name):
    B, S = proj.shape[0], proj.shape[1]
    G = len(POOL_WINDOWS)

    def body(u_ref, dy_ref, w_ref, sc_ref, du_ref, dw_ref, dsc_ref):
        g = pl.program_id(0)
        u = u_ref[0].astype(f32)
        row = lax.broadcasted_iota(jnp.int32, u.shape, 0)
        cnt = jnp.minimum(row + 1, jnp.left_shift(2, g)).astype(f32)
        pooled = _window_sum(u, g, row, _shift_down) / cnt - u
        wv = w_ref[0].astype(bf16)
        mixed = _dot(pooled.astype(bf16), wv)
        dyv = dy_ref[0].astype(f32)
        dmix = (dyv * sc_ref[...]).astype(bf16)
        dpool = _dot(dmix, wv, NT)
        du_ref[0] = (_window_sum(dpool / cnt, g, row, _shift_up) - dpool).astype(bf16)

        @pl.when(pl.program_id(1) == 0)
        def _():
            dw_ref[...] = jnp.zeros_like(dw_ref)
            dsc_ref[...] = jnp.zeros_like(dsc_ref)

        dw_ref[0] += _dot(pooled.astype(bf16), dmix, TN)
        dsc_ref[...] += _fold8(dyv * mixed)

    return pl.pallas_call(
        body, name=name, grid=(G, B),
        in_specs=[pl.BlockSpec((1, S, HEAD), lambda g, b: (b, 0, g)), pl.BlockSpec((1, S, HEAD), lambda g, b: (b, 0, g)),
                  pl.BlockSpec((1, HEAD, HEAD), lambda g, b: (g, 0, 0)), pl.BlockSpec((1, HEAD), lambda g, b: (0, g))],
        out_specs=[pl.BlockSpec((1, S, HEAD), lambda g, b: (b, 0, g)), pl.BlockSpec((1, HEAD, HEAD), lambda g, b: (g, 0, 0)),
                   pl.BlockSpec((8, HEAD), lambda g, b: (0, g))],
        out_shape=[jax.ShapeDtypeStruct((B, S, BRANCH), bf16), jax.ShapeDtypeStruct((G, HEAD, HEAD), f32),
                   jax.ShapeDtypeStruct((8, BRANCH), f32)],
        compiler_params=_cp(("parallel", "arbitrary")),
    )(proj, dy, pool_w, scale.reshape(1, BRANCH))


def _split_dot(x, u):
    hi = x.astype(bf16)
    lo = (x - hi.astype(f32)).astype(bf16)
    return _dot(hi, u) + _dot(lo, u)


def _sb_fwd(sbqkv, name, carry=None):
    B, S, _ = sbqkv.shape
    KB = SB_BLOCK
    TQ = _tile(S, (SB_QUERIES, SB_QUERIES // 2, SB_QUERIES // 4, KB))
    ns = TQ // KB
    nq = S // TQ
    scale = HEAD ** -0.5

    def body(q_ref, k_ref, v_ref, o_ref, tot_ref):
        r = lax.broadcasted_iota(jnp.int32, (KB, KB), 0)
        c = lax.broadcasted_iota(jnp.int32, (KB, KB), 1)
        wide_r = lax.broadcasted_iota(jnp.int32, (TQ, KB), 0)
        wide_c = lax.broadcasted_iota(jnp.int32, (TQ, KB), 1)
        after = (r > c).astype(bf16)

        def sub(qa, krows, mask, run, acc):
            z = _dot(qa, k_ref[0, krows, :], NT) * scale
            sp = jnp.maximum(z, 0.0) + jnp.log(1.0 + jnp.exp(-jnp.abs(z)))
            ln = -sp
            if mask is not None:
                ln = jnp.where(mask, ln, 0.0)
            w = jnp.exp(z - sp + _split_dot(ln, after) + run)
            if mask is not None:
                w = jnp.where(mask, w, 0.0)
            acc = acc + _dot(w.astype(bf16), v_ref[0, krows, :])
            return run + jnp.sum(ln, axis=1, keepdims=True), acc

        def qloop(i, carry):
            base = pl.multiple_of(i * TQ, TQ)
            qi = q_ref[0, pl.ds(base, TQ), :]
            cr = (jnp.zeros((TQ, LANES), f32), jnp.zeros((TQ, HEAD), f32))
            for s in range(ns - 1, -1, -1):
                cr = sub(qi, pl.ds(base + s * KB, KB), (s * KB + wide_c) < wide_r, *cr)

            def group(t, cr):
                g0 = pl.multiple_of((i - 1 - t) * TQ, TQ)
                for s in range(ns - 1, -1, -1):
                    cr = sub(qi, pl.ds(g0 + s * KB, KB), None, *cr)
                return cr

            run, acc = lax.fori_loop(0, i, group, cr)
            o_ref[0, pl.ds(base, TQ), :] = acc.astype(bf16)
            tot_ref[0, 0, pl.ds(base, TQ), :] = run
            return carry

        lax.fori_loop(0, nq, qloop, 0)

    def spec(off):
        return pl.BlockSpec((1, S, HEAD), lambda b, h, off=off: (b, 0, off + h))

    body, cin, cout, cshapes, cscratch = _carried(body, 3, 2, (B, N_HEADS), carry)
    outs = pl.pallas_call(
        body, name=name, grid=(B, N_HEADS), in_specs=[spec(0), spec(N_HEADS), spec(2 * N_HEADS)] + cin,
        out_specs=[pl.BlockSpec((1, S, HEAD), lambda b, h: (b, 0, h)),
                   pl.BlockSpec((1, 1, S, LANES), lambda b, h: (b, h, 0, 0))] + cout,
        out_shape=[jax.ShapeDtypeStruct((B, S, BRANCH), bf16), jax.ShapeDtypeStruct((B, N_HEADS, S, LANES), f32)] + cshapes,
        scratch_shapes=cscratch,
        compiler_params=_cp(("arbitrary", "arbitrary") if carry else ("parallel", "parallel")),
    )(sbqkv, sbqkv, sbqkv, *(carry.ins if carry else []))
    return outs[:2], outs[2:]


def _sb_bwd(sbqkv, do, tot, name, carry=None):
    B, S, _ = sbqkv.shape
    KB = SB_BLOCK
    TQ = _tile(S, (SB_QUERIES, SB_QUERIES // 2, SB_QUERIES // 4, KB))
    ns = TQ // KB
    nq = S // TQ
    scale = HEAD ** -0.5

    def body(q_ref, k_ref, v_ref, do_ref, tot_ref, dq_ref, dk_ref, dv_ref, dk_acc, dv_acc):
        r = lax.broadcasted_iota(jnp.int32, (KB, KB), 0)
        c = lax.broadcasted_iota(jnp.int32, (KB, KB), 1)
        wide_r = lax.broadcasted_iota(jnp.int32, (TQ, KB), 0)
        wide_c = lax.broadcasted_iota(jnp.int32, (TQ, KB), 1)
        after = (r > c).astype(bf16)
        before = (r < c).astype(bf16)
        dk_acc[...] = jnp.zeros_like(dk_acc)
        dv_acc[...] = jnp.zeros_like(dv_acc)

        def sub(qi, doi, total, rows, mask, cl, cp, dq):
            kj = k_ref[0, rows, :]
            vj = v_ref[0, rows, :]
            z = _dot(qi, kj, NT) * scale
            sp = jnp.maximum(z, 0.0) + jnp.log(1.0 + jnp.exp(-jnp.abs(z)))
            ln = -sp
            if mask is not None:
                ln = jnp.where(mask, ln, 0.0)
            bs = jnp.sum(ln, axis=1, keepdims=True)
            w = jnp.exp(z - sp + _split_dot(ln, after) + (total - cl - bs))
            if mask is not None:
                w = jnp.where(mask, w, 0.0)
            p = _dot(doi, vj, NT) * w
            qsum = cp + _split_dot(p, before)
            sig = jnp.exp(z - sp)
            dz = (p - (p + qsum) * sig) * scale
            if mask is not None:
                dz = jnp.where(mask, dz, 0.0)
            dzb = dz.astype(bf16)
            dq = dq + _dot(dzb, kj)
            dk_acc[rows, :] += _dot(dzb, qi, TN)
            dv_acc[rows, :] += _dot(w.astype(bf16), doi, TN)
            return cl + bs, cp + jnp.sum(p, axis=1, keepdims=True), dq

        def qloop(i, carry):
            base = pl.multiple_of(i * TQ, TQ)
            rows = pl.ds(base, TQ)
            qi = q_ref[0, rows, :]
            doi = do_ref[0, rows, :]
            total = tot_ref[0, 0, rows, :][:, 0:1]
            zero = jnp.zeros((TQ, 1), f32)

            def group(g, st):
                g0 = pl.multiple_of(g * TQ, TQ)
                for s in range(ns):
                    st = sub(qi, doi, total, pl.ds(g0 + s * KB, KB), None, *st)
                return st

            st = lax.fori_loop(0, i, group, (zero, zero, jnp.zeros((TQ, HEAD), f32)))
            for s in range(ns):
                st = sub(qi, doi, total, pl.ds(base + s * KB, KB), (s * KB + wide_c) < wide_r, *st)
            dq_ref[0, rows, :] = st[2].astype(bf16)
            return carry

        lax.fori_loop(0, nq, qloop, 0)
        dk_ref[0] = dk_acc[...].astype(bf16)
        dv_ref[0] = dv_acc[...].astype(bf16)

    def spec(off):
        return pl.BlockSpec((1, S, HEAD), lambda b, h, off=off: (b, 0, off + h))

    o = jax.ShapeDtypeStruct((B, S, BRANCH), bf16)
    body, cin, cout, cshapes, cscratch = _carried(body, 5, 3, (B, N_HEADS), carry)
    outs = pl.pallas_call(
        body, name=name, grid=(B, N_HEADS),
        in_specs=[spec(0), spec(N_HEADS), spec(2 * N_HEADS), spec(0),
                  pl.BlockSpec((1, 1, S, LANES), lambda b, h: (b, h, 0, 0))] + cin,
        out_specs=[spec(0), spec(0), spec(0)] + cout, out_shape=[o, o, o] + cshapes,
        scratch_shapes=[pltpu.VMEM((S, HEAD), f32), pltpu.VMEM((S, HEAD), f32)] + cscratch,
        compiler_params=_cp(("arbitrary", "arbitrary") if carry else ("parallel", "parallel")),
    )(sbqkv, sbqkv, sbqkv, do, tot, *(carry.ins if carry else []))
    return outs[:3], outs[3:]


def _dn_params(a_log, dt_bias):
    p = jnp.zeros((8, LANES), f32)
    p = p.at[0, :N_HEADS].set(a_log)
    return p.at[1, :N_HEADS].set(dt_bias)


def _dn_prep(ab, par, name):
    B, S, _ = ab.shape
    R = 2 * CHUNK
    nt = S // R

    def body(ab_ref, par_ref, gcb_ref, bb_ref, gcr_ref):
        x = ab_ref[0]
        g = -jnp.exp(par_ref[0:1, :]) * _softplus(x + par_ref[1:2, :])
        r = lax.broadcasted_iota(jnp.int32, (R, R), 0)
        c = lax.broadcasted_iota(jnp.int32, (R, R), 1)
        tri = ((r >= c) & ((r >> 6) == (c >> 6))).astype(f32)
        cs = jnp.dot(tri, g, precision=HI, preferred_element_type=f32)
        beta = _sigmoid(x)
        cst = cs.T
        for h in range(N_HEADS):
            gcb_ref[0, h] = jnp.broadcast_to(cs[:, h:h + 1], (R, LANES))
            bb_ref[0, h] = jnp.broadcast_to(beta[:, N_HEADS + h:N_HEADS + h + 1], (R, LANES))
            gcr_ref[0, h, 0] = jnp.broadcast_to(cst[h:h + 1, 0:CHUNK], (8, CHUNK))
            gcr_ref[0, h, 1] = jnp.broadcast_to(cst[h:h + 1, CHUNK:R], (8, CHUNK))

    return pl.pallas_call(
        body, name=name, grid=(B, nt),
        in_specs=[pl.BlockSpec((1, R, LANES), lambda b, i: (b, i, 0)), pl.BlockSpec((8, LANES), lambda b, i: (0, 0))],
        out_specs=[pl.BlockSpec((1, N_HEADS, R, LANES), lambda b, i: (b, 0, i, 0)),
                   pl.BlockSpec((1, N_HEADS, R, LANES), lambda b, i: (b, 0, i, 0)),
                   pl.BlockSpec((1, N_HEADS, 2, 8, CHUNK), lambda b, i: (b, 0, i, 0, 0))],
        out_shape=[jax.ShapeDtypeStruct((B, N_HEADS, S, LANES), f32), jax.ShapeDtypeStruct((B, N_HEADS, S, LANES), f32),
                   jax.ShapeDtypeStruct((B, N_HEADS, S // CHUNK, 8, CHUNK), f32)],
        compiler_params=_cp(("parallel", "parallel")),
    )(ab, par)


def _bmm(a, b, prec=None):
    return jnp.einsum("nij,njk->nik", a, b, preferred_element_type=f32, precision=prec)


def _bmm_nt(a, b, prec=None):
    return jnp.einsum("nik,njk->nij", a, b, preferred_element_type=f32, precision=prec)


def _bmm_tn(a, b, prec=None):
    return jnp.einsum("nki,nkj->nij", a, b, preferred_element_type=f32, precision=prec)


def _tri_inv(L):
    C = L.shape[-1]
    r = lax.broadcasted_iota(jnp.int32, (C, C), 0)
    c = lax.broadcasted_iota(jnp.int32, (C, C), 1)
    eye = (r == c).astype(f32)
    bd16 = (r >> 4) == (c >> 4)
    bd32 = (r >> 5) == (c >> 5)
    mm = functools.partial(_bmm, prec=MID)
    n1 = -jnp.where(bd16, L, 0.0)
    n2 = mm(n1, n1)
    n4 = mm(n2, n2)
    n8 = mm(n4, n4)
    t = mm(mm(mm(eye + n1, eye + n2), eye + n4), eye + n8)
    t = t - mm(mm(t, jnp.where(bd32 & jnp.logical_not(bd16), L, 0.0)), t)
    t = t - mm(mm(t, jnp.where(bd32, 0.0, L)), t)
    return t


def _conv_silu(x, w, row):
    c = w[3:4] * x + w[2:3] * _shift_down(x, 1, row) + w[1:2] * _shift_down(x, 2, row) + w[0:1] * _shift_down(x, 3, row)
    return c, c * _sigmoid(c)


def _dn_intra(qn, kn, v, gcb, beta, gr):
    C = CHUNK
    r = lax.broadcasted_iota(jnp.int32, (C, C), 0)
    c = lax.broadcasted_iota(jnp.int32, (C, C), 1)
    incl = r >= c
    diff = gcb[:, :, :C] - gr
    dm = jnp.where(incl, jnp.exp(jnp.where(incl, diff, 0.0)), 0.0)
    ds = jnp.where(r > c, dm, 0.0)
    kb = kn * beta
    knb = kn.astype(bf16)
    L = _bmm_nt(kb.astype(bf16), knb) * ds
    eg = jnp.exp(gcb)
    a = _bmm_nt(qn.astype(bf16), knb) * dm
    gl = gcb[:, C - 1:C, :]
    ekd = jnp.exp(gl - gcb)
    return dict(dm=dm, ds=ds, kb=kb, L=L, eg=eg, rhs_u=v * beta, rhs_w=kb * eg, a=a,
                qd=qn * eg, kd=kn * ekd, ekd=ekd, cd=jnp.exp(gl))


def _dn_specs(S):
    def col(off):
        return pl.BlockSpec((1, S, HEAD), lambda b, h, off=off: (b, 0, off + h))

    def cw(off):
        return pl.BlockSpec((4, HEAD), lambda b, h, off=off: (0, off + h))

    per_head = pl.BlockSpec((1, 1, S, LANES), lambda b, h: (b, h, 0, 0))
    rowform = pl.BlockSpec((1, 1, S // CHUNK, 8, CHUNK), lambda b, h: (b, h, 0, 0, 0))
    gain = pl.BlockSpec((1, HEAD), lambda b, h: (0, 0))
    ins = [col(4), col(8), col(12), col(16), cw(0), cw(4), cw(8), per_head, per_head, rowform, gain]
    return ins, per_head


def _dn_chunked_specs(n):
    return [pl.BlockSpec((1, 1, n, CHUNK, CHUNK), lambda b, h: (b, h, 0, 0, 0)),
            pl.BlockSpec((1, 1, n, CHUNK, HEAD), lambda b, h: (b, h, 0, 0, 0)),
            pl.BlockSpec((1, 1, n, CHUNK, HEAD), lambda b, h: (b, h, 0, 0, 0))]


def _dn_act(x_ref, cw_ref, row, normalise, out_scale=1.0):
    _, act = _conv_silu(x_ref[0].astype(f32), cw_ref[...], row)
    if normalise:
        act = act * (lax.rsqrt(jnp.sum(act * act, axis=-1, keepdims=True) + EPS) * out_scale)
    return act


def _dn_group(qn_s, kn_s, v_s, gcb_ref, bb_ref, gcr_ref, g, ng):
    C = CHUNK
    rows = pl.ds(pl.multiple_of(g * (ng * C), ng * C), ng * C)
    ch = pl.ds(g * ng, ng)
    sh = (ng, C, HEAD)
    qn, kn, v = qn_s[rows, :].reshape(sh), kn_s[rows, :].reshape(sh), v_s[rows, :].reshape(sh)
    gcb3, beta = gcb_ref[0, 0, rows, :].reshape(sh), bb_ref[0, 0, rows, :].reshape(sh)
    gr = gcr_ref[0, 0, ch][:, 0:1, :]
    it = _dn_intra(qn, kn, v, gcb3, beta, gr)
    it.update(qn=qn, kn=kn, v=v, gcb=gcb3, beta=beta, gr=gr)
    return rows, ch, it


def _dn_fwd(proj, conv_w, gcb, betab, gcr, gain, name, carry=None):
    B, S, _ = proj.shape
    n, C = S // CHUNK, CHUNK
    ng = min(8, n)
    ins, per_head = _dn_specs(S)

    def body(q_ref, k_ref, v_ref, z_ref, cq_ref, ck_ref, cv_ref, gcb_ref, bb_ref, gcr_ref, gain_ref,
             y_ref, st_ref, vn_ref, t_ref, ub_ref, wb_ref, qn_s, kn_s, v_s, u_s, w_s, qd_s, a_s, cd_s, g_s, h_s):
        row = lax.broadcasted_iota(jnp.int32, (S, HEAD), 0)
        qn_s[...] = _dn_act(q_ref, cq_ref, row, True, HEAD ** -0.5)
        kn_s[...] = _dn_act(k_ref, ck_ref, row, True)
        v_s[...] = _dn_act(v_ref, cv_ref, row, False)

        def group(g, carry):
            _, ch, it = _dn_group(qn_s, kn_s, v_s, gcb_ref, bb_ref, gcr_ref, g, ng)
            t = _tri_inv(it["L"])
            u = _bmm(t, it["rhs_u"], MID)
            wb = _bmm(t, it["rhs_w"], MID).astype(bf16)
            kdb = it["kd"].astype(bf16)
            ub = u.astype(bf16)
            u_s[ch] = u
            w_s[ch] = wb
            t_ref[0, 0, ch] = t
            ub_ref[0, 0, ch] = ub
            wb_ref[0, 0, ch] = wb
            qd_s[ch] = it["qd"].astype(bf16)
            a_s[ch] = it["a"].astype(bf16)
            cd_s[ch] = it["cd"]
            g_s[ch] = _bmm_tn(kdb, wb).astype(bf16)
            h_s[ch] = _bmm_tn(kdb, ub)
            return carry

        lax.fori_loop(0, n // ng, group, 0)

        def step(i, st):
            sb = st.astype(bf16)
            st_ref[0, 0, i] = sb
            return st * cd_s[i] - _dot(g_s[i], sb) + h_s[i]

        lax.fori_loop(0, n, step, jnp.zeros((HEAD, HEAD), f32))

        def group_out(g, carry):
            rows = pl.ds(pl.multiple_of(g * (ng * C), ng * C), ng * C)
            ch = pl.ds(g * ng, ng)
            sn = st_ref[0, 0, ch]
            vn = (u_s[ch] - _bmm(w_s[ch], sn)).astype(bf16)
            vn_ref[0, 0, rows, :] = vn.reshape(ng * C, HEAD)
            o = (_bmm(qd_s[ch], sn) + _bmm(a_s[ch], vn)).reshape(ng * C, HEAD)
            zz = z_ref[0, rows, :].astype(f32)
            rr = lax.rsqrt(jnp.mean(o * o, axis=-1, keepdims=True) + EPS)
            y_ref[0, rows, :] = (o * rr * gain_ref[...] * (zz * _sigmoid(zz))).astype(bf16)
            return carry

        lax.fori_loop(0, n // ng, group_out, 0)

    seq = pltpu.VMEM((S, HEAD), f32)
    body, cin, cout, cshapes, cscratch = _carried(body, len(ins), 6, (B, N_HEADS), carry)
    chunked = _dn_chunked_specs(n)
    outs = pl.pallas_call(
        body, name=name, grid=(B, N_HEADS), in_specs=ins + cin,
        out_specs=[pl.BlockSpec((1, S, HEAD), lambda b, h: (b, 0, h)),
                   pl.BlockSpec((1, 1, n, HEAD, HEAD), lambda b, h: (b, h, 0, 0, 0)), per_head] + chunked + cout,
        out_shape=[jax.ShapeDtypeStruct((B, S, BRANCH), bf16), jax.ShapeDtypeStruct((B, N_HEADS, n, HEAD, HEAD), bf16),
                   jax.ShapeDtypeStruct((B, N_HEADS, S, HEAD), bf16), jax.ShapeDtypeStruct((B, N_HEADS, n, C, C), f32),
                   jax.ShapeDtypeStruct((B, N_HEADS, n, C, HEAD), bf16), jax.ShapeDtypeStruct((B, N_HEADS, n, C, HEAD), bf16)]
        + cshapes,
        scratch_shapes=[seq, seq, seq, pltpu.VMEM((n, C, HEAD), f32), pltpu.VMEM((n, C, HEAD), bf16),
                        pltpu.VMEM((n, C, HEAD), bf16), pltpu.VMEM((n, C, C), bf16), pltpu.VMEM((n, 1, HEAD), f32),
                        pltpu.VMEM((n, HEAD, HEAD), bf16), pltpu.VMEM((n, HEAD, HEAD), f32)] + cscratch,
        compiler_params=_cp(("arbitrary", "arbitrary") if carry else ("parallel", "parallel")),
    )(proj, proj, proj, proj, conv_w, conv_w, conv_w, gcb, betab, gcr, gain.reshape(1, HEAD), *(carry.ins if carry else []))
    return outs[:6], outs[6:]


def _rowsum(x):
    return jnp.sum(x, axis=-1, keepdims=True)


def _dn_bwd(proj, dy, conv_w, gcb, betab, gcr, gain, states, vnew, tinv, ub, wb, name, carry=None):
    B, S, _ = proj.shape
    n, C = S // CHUNK, CHUNK
    ng = min(16, n)
    ins, per_head = _dn_specs(S)
    ins = ins + [pl.BlockSpec((1, S, HEAD), lambda b, h: (b, 0, h)),
                 pl.BlockSpec((1, 1, n, HEAD, HEAD), lambda b, h: (b, h, 0, 0, 0)), per_head] + _dn_chunked_specs(n)

    def body(q_ref, k_ref, v_ref, z_ref, cq_ref, ck_ref, cv_ref, gcb_ref, bb_ref, gcr_ref, gain_ref, dy_ref, st_ref, vn_ref,
             t_ref, ub_ref, wb_ref, dq_ref, dk_ref, dv_ref, dz_ref, dg_ref, dbeta_ref, dconv_ref, dgain_ref,
             qn_s, kn_s, v_s, cd_s, do_s, dsp_s, dvn_s, g_s, q_s):
        row = lax.broadcasted_iota(jnp.int32, (S, HEAD), 0)
        qn_s[...] = _dn_act(q_ref, cq_ref, row, True, HEAD ** -0.5)
        kn_s[...] = _dn_act(k_ref, ck_ref, row, True)
        v_s[...] = _dn_act(v_ref, cv_ref, row, False)
        dgain_ref[...] = jnp.zeros_like(dgain_ref)
        gv = gain_ref[...]

        def group_fwd(g, carry):
            rows, ch, it = _dn_group(qn_s, kn_s, v_s, gcb_ref, bb_ref, gcr_ref, g, ng)
            wb = wb_ref[0, 0, ch]
            ab, qdb = it["a"].astype(bf16), it["qd"].astype(bf16)
            vn = vn_ref[0, 0, rows, :].reshape(ng, C, HEAD)
            o = (_bmm(qdb, st_ref[0, 0, ch]) + _bmm(ab, vn)).reshape(ng * C, HEAD)
            zz = z_ref[0, rows, :].astype(f32)
            dyv = dy_ref[0, rows, :].astype(f32)
            rr = lax.rsqrt(jnp.mean(o * o, axis=-1, keepdims=True) + EPS)
            on = o * rr
            sz = _sigmoid(zz)
            dz_ref[0, rows, :] = (dyv * on * gv * (sz * (1.0 + zz * (1.0 - sz)))).astype(bf16)
            dnrm = dyv * (zz * sz)
            dgain_ref[0, 0] += _fold8(dnrm * on)
            doh = dnrm * gv
            do = rr * (doh - on * jnp.mean(doh * on, axis=-1, keepdims=True))
            dob = do.reshape(ng, C, HEAD).astype(bf16)
            atdo = _bmm_tn(ab, dob)
            cd_s[ch] = it["cd"]
            do_s[ch] = dob
            dvn_s[ch] = atdo
            g_s[ch] = _bmm_tn(it["kd"].astype(bf16), wb).astype(bf16)
            q_s[ch] = _bmm_tn(qdb, dob) - _bmm_tn(wb, atdo.astype(bf16))
            return carry

        lax.fori_loop(0, n // ng, group_fwd, 0)

        def step(t, dsp):
            i = n - 1 - t
            dspb = dsp.astype(bf16)
            dsp_s[i] = dspb
            return dsp * cd_s[i] - _dot(g_s[i], dspb, TN) + q_s[i]

        lax.fori_loop(0, n, step, jnp.zeros((HEAD, HEAD), f32))

        r = lax.broadcasted_iota(jnp.int32, (C, C), 0)
        c = lax.broadcasted_iota(jnp.int32, (C, C), 1)
        upper = r <= c

        def group_bwd(g, carry):
            rows, ch, it = _dn_group(qn_s, kn_s, v_s, gcb_ref, bb_ref, gcr_ref, g, ng)
            sh = (ng, C, HEAD)
            qn, kn, v, beta, gcb3, gr = it["qn"], it["kn"], it["v"], it["beta"], it["gcb"], it["gr"]
            sn = st_ref[0, 0, ch]
            vn = vn_ref[0, 0, rows, :].reshape(sh)
            dsp, dob = dsp_s[ch], do_s[ch]
            dvn = dvn_s[ch] + _bmm(it["kd"].astype(bf16), dsp)
            t, ub, wb = t_ref[0, 0, ch], ub_ref[0, 0, ch], wb_ref[0, 0, ch]
            dvnb = dvn.astype(bf16)
            da = _bmm_nt(dob, vn)
            dat = _bmm_nt(vn, dob)
            dqd = _bmm_nt(dob, sn)
            dkd = _bmm_nt(vn, dsp)
            dcd = jnp.sum(jnp.sum(dsp.astype(f32) * sn.astype(f32), axis=2, keepdims=True), axis=1, keepdims=True)
            dw = -_bmm_nt(dvnb, sn)
            ru = _bmm_tn(t, dvn, MID)
            rw = _bmm_tn(t, dw, MID)
            rub, rwb = ru.astype(bf16), rw.astype(bf16)
            dL = -(_bmm_nt(rub, ub) + _bmm_nt(rwb, wb))
            dLt = -(_bmm_nt(ub, rub) + _bmm_nt(wb, rwb))
            knb, qnb, kbb = kn.astype(bf16), qn.astype(bf16), it["kb"].astype(bf16)
            dmt = jnp.where(upper, jnp.exp(jnp.where(upper, gr - gcb3[:, :, :C], 0.0)), 0.0)
            Lt = _bmm_nt(knb, kbb) * jnp.where(r < c, dmt, 0.0)
            At = _bmm_nt(knb, qnb) * dmt
            dgc = _rowsum(dL * it["L"] + da * it["a"]) - _rowsum(dLt * Lt + dat * At)
            dkk = (dL * it["ds"]).astype(bf16)
            dqk = (da * it["dm"]).astype(bf16)
            dkb = _bmm(dkk, knb) + rw * it["eg"]
            dkn = _bmm_tn(dkk, kbb) + _bmm_tn(dqk, qnb) + dkd * it["ekd"] + dkb * beta
            dqn = _bmm(dqk, knb) + dqd * it["eg"]
            tkd = _rowsum(dkd * it["kd"])
            dgl = jnp.sum(tkd, axis=1, keepdims=True) + dcd * it["cd"][:, :, 0:1]
            dgc = dgc + _rowsum(dqd * it["qd"]) - tkd + _rowsum(rw * it["rhs_w"])
            dbeta = _rowsum(ru * v) + _rowsum(dkb * kn)
            rowc = lax.broadcasted_iota(jnp.int32, (ng, C, 1), 1)
            dgc = dgc + jnp.where(rowc == C - 1, dgl, 0.0)
            rev = jnp.broadcast_to(upper.astype(f32), (ng, C, C))
            dg_ref[0, 0, rows, :] = _bmm(rev, jnp.broadcast_to(dgc, sh), HI).reshape(ng * C, LANES).astype(bf16)
            dbeta_ref[0, 0, rows, :] = jnp.broadcast_to(dbeta, sh).reshape(ng * C, LANES).astype(bf16)
            qn_s[rows, :] = dqn.reshape(ng * C, HEAD)
            kn_s[rows, :] = dkn.reshape(ng * C, HEAD)
            v_s[rows, :] = (ru * beta).reshape(ng * C, HEAD)
            return carry

        lax.fori_loop(0, n // ng, group_bwd, 0)

        def conv_back(x_ref, cw_ref, grad_s, out_ref, slot, normalise, out_scale):
            x = x_ref[0].astype(f32)
            w = cw_ref[...]
            pre, act = _conv_silu(x, w, row)
            dact = grad_s[...]
            if normalise:
                rn = lax.rsqrt(jnp.sum(act * act, axis=-1, keepdims=True) + EPS)
                unit = act * rn
                dact = (out_scale * rn) * (dact - unit * _rowsum(dact * unit))
            s = _sigmoid(pre)
            dc = dact * (s * (1.0 + pre * (1.0 - s)))
            out_ref[0] = (w[3:4] * dc + w[2:3] * _shift_up(dc, 1, row) + w[1:2] * _shift_up(dc, 2, row)
                          + w[0:1] * _shift_up(dc, 3, row)).astype(bf16)
            for tap in range(4):
                xs = x if tap == 3 else _shift_down(x, 3 - tap, row)
                dconv_ref[0, slot, tap:tap + 1, :] = jnp.sum(dc * xs, axis=0, keepdims=True)

        conv_back(q_ref, cq_ref, qn_s, dq_ref, 0, True, HEAD ** -0.5)
        conv_back(k_ref, ck_ref, kn_s, dk_ref, 1, True, 1.0)
        conv_back(v_ref, cv_ref, v_s, dv_ref, 2, False, 1.0)

    o512 = jax.ShapeDtypeStruct((B, S, BRANCH), bf16)
    s512 = pl.BlockSpec((1, S, HEAD), lambda b, h: (b, 0, h))
    ph = jax.ShapeDtypeStruct((B, N_HEADS, S, LANES), bf16)
    seq = pltpu.VMEM((S, HEAD), f32)
    cb = pltpu.VMEM((n, C, HEAD), bf16)
    body, cin, cout, cshapes, cscratch = _carried(body, len(ins), 8, (B, N_HEADS), carry)
    outs = pl.pallas_call(
        body, name=name, grid=(B, N_HEADS), in_specs=ins + cin,
        out_specs=[s512, s512, s512, s512, per_head, per_head, pl.BlockSpec((1, 3, 4, HEAD), lambda b, h: (b, 0, 0, h)),
                   pl.BlockSpec((1, 1, 8, HEAD), lambda b, h: (b, h, 0, 0))] + cout,
        out_shape=[o512, o512, o512, o512, ph, ph, jax.ShapeDtypeStruct((B, 3, 4, BRANCH), f32),
                   jax.ShapeDtypeStruct((B, N_HEADS, 8, HEAD), f32)] + cshapes,
        scratch_shapes=[seq, seq, seq, pltpu.VMEM((n, 1, HEAD), f32), cb, pltpu.VMEM((n, HEAD, HEAD), bf16),
                        pltpu.VMEM((n, C, HEAD), f32), pltpu.VMEM((n, HEAD, HEAD), bf16), pltpu.VMEM((n, HEAD, HEAD), f32)]
        + cscratch,
        compiler_params=_cp(("arbitrary", "arbitrary") if carry else ("parallel", "parallel")),
    )(proj, proj, proj, proj, conv_w, conv_w, conv_w, gcb, betab, gcr, gain.reshape(1, HEAD), dy, states, vnew, tinv, ub, wb,
      *(carry.ins if carry else []))
    return outs[:8], outs[8:]


def _dn_post(ab, par, dg, dbeta, name):
    B, S, _ = ab.shape
    ts = _tile(S, (512, 256, 128))

    def body(ab_ref, par_ref, dg_ref, db_ref, dab_ref, acc_ref):
        x = ab_ref[0]
        lane = lax.broadcasted_iota(jnp.int32, x.shape, 1)
        dgs = jnp.zeros_like(x)
        dbs = jnp.zeros_like(x)
        for h in range(N_HEADS):
            dgs = jnp.where(lane == h, dg_ref[0, h], dgs)
            dbs = jnp.where(lane == N_HEADS + h, db_ref[0, h], dbs)
        nega = -jnp.exp(par_ref[0:1, :])
        pre = x + par_ref[1:2, :]
        da = dgs * nega * _sigmoid(pre)
        beta = _sigmoid(x)
        dab_ref[0] = (da + dbs * beta * (1.0 - beta)).astype(bf16)

        @pl.when((pl.program_id(0) == 0) & (pl.program_id(1) == 0))
        def _():
            acc_ref[...] = jnp.zeros_like(acc_ref)

        acc_ref[0] += _fold8(dgs * nega * _softplus(pre))
        acc_ref[1] += _fold8(da)

    return pl.pallas_call(
        body, name=name, grid=(B, S // ts),
        in_specs=[pl.BlockSpec((1, ts, LANES), lambda b, i: (b, i, 0)), pl.BlockSpec((8, LANES), lambda b, i: (0, 0)),
                  pl.BlockSpec((1, N_HEADS, ts, LANES), lambda b, i: (b, 0, i, 0)),
                  pl.BlockSpec((1, N_HEADS, ts, LANES), lambda b, i: (b, 0, i, 0))],
        out_specs=[pl.BlockSpec((1, ts, LANES), lambda b, i: (b, i, 0)), pl.BlockSpec((2, 8, LANES), lambda b, i: (0, 0, 0))],
        out_shape=[jax.ShapeDtypeStruct((B, S, LANES), bf16), jax.ShapeDtypeStruct((2, 8, LANES), f32)],
        compiler_params=_cp(("arbitrary", "arbitrary")),
    )(ab, par, dg, dbeta)


def _merge_specs(T, D, tm, tn, order):
    nj = D // tn

    def ij(f):
        return (lambda i, j: f(i, j)) if order == "ij" else (lambda j, i: f(i, j))

    ys = [pl.BlockSpec((tm, BRANCH), ij(lambda i, j: (i, 0))) for _ in range(3)]
    wb = pl.BlockSpec((3, BRANCH, tn), ij(lambda i, j: (0, 0, j)))
    gl = [pl.BlockSpec((tm, tn), ij(lambda i, j, k=k: (i, k * nj + j))) for k in range(3)]
    bg = [pl.BlockSpec((1, tn), ij(lambda i, j, k=k: (0, k * nj + j))) for k in range(3)]
    return ys, wb, gl, bg


def _merge_fwd(ys, wb, gl, b_gate, name):
    T, D = ys[0].shape[0], wb.shape[2]
    tm, tn = _tile(T, (512, 256, 128)), _tile(D, (512, 256, 128))
    sy, swb, sgl, sbg = _merge_specs(T, D, tm, tn, "ij")

    def body(y0, y1, y2, wb_ref, g0, g1, g2, b0, b1, b2, o_ref):
        acc = None
        for k, (y, g, b) in enumerate(((y0, g0, b0), (y1, g1, b1), (y2, g2, b2))):
            term = _sigmoid(g[...].astype(f32) + b[...]) * _dot(y[...], wb_ref[k])
            acc = term if acc is None else acc + term
        o_ref[...] = acc.astype(bf16)

    bg = b_gate.reshape(1, 3 * D)
    return pl.pallas_call(
        body, name=name, grid=(T // tm, D // tn), in_specs=sy + [swb] + sgl + sbg,
        out_specs=pl.BlockSpec((tm, tn), lambda i, j: (i, j)), out_shape=jax.ShapeDtypeStruct((T, D), bf16),
        compiler_params=_cp(("parallel", "parallel")),
    )(*ys, wb, gl, gl, gl, bg, bg, bg)


def _merge_bwd(dm, ys, wb, gl, b_gate, name):
    T, D = dm.shape
    tm, tn = _tile(T, (512, 256, 128)), _tile(D, (512, 256, 128))
    sy, swb, sgl, sbg = _merge_specs(T, D, tm, tn, "ji")

    def body(dm_ref, y0, y1, y2, wb_ref, g0, g1, g2, b0, b1, b2, dgl_ref, dbd_ref, dbg_ref):
        dmv = dm_ref[...].astype(f32)

        @pl.when(pl.program_id(1) == 0)
        def _():
            dbg_ref[...] = jnp.zeros_like(dbg_ref)

        for k, (y, g, b) in enumerate(((y0, g0, b0), (y1, g1, b1), (y2, g2, b2))):
            s = _sigmoid(g[...].astype(f32) + b[...])
            dg = dmv * _dot(y[...], wb_ref[k]) * s * (1.0 - s)
            dgl_ref[k] = dg.astype(bf16)
            dbd_ref[k] = (dmv * s).astype(bf16)
            dbg_ref[k] += _fold8(dg)

    bg = b_gate.reshape(1, 3 * D)
    o3 = jax.ShapeDtypeStruct((3, T, D), bf16)
    s3 = pl.BlockSpec((3, tm, tn), lambda j, i: (0, i, j))
    return pl.pallas_call(
        body, name=name, grid=(D // tn, T // tm),
        in_specs=[pl.BlockSpec((tm, tn), lambda j, i: (i, j))] + sy + [swb] + sgl + sbg,
        out_specs=[s3, s3, pl.BlockSpec((3, 8, tn), lambda j, i: (0, 0, j))],
        out_shape=[o3, o3, jax.ShapeDtypeStruct((3, 8, D), f32)],
        compiler_params=_cp(("parallel", "arbitrary")),
    )(dm, *ys, wb, gl, gl, gl, bg, bg, bg)


def _ffn_fwd(x, g, w, f, tag, carry=None):
    T, D = x.shape
    FP = w["wgu"].shape[2]
    h = _rms_fwd(x, g, f"rms_{tag}")
    (a, b, hm), landed = _ffn_up(h, w["wgu"], f, f"ffn_up_{tag}", carry)
    down = [("nn", hm, w["m"], {"K": FP, "ka": k, "bsel": k, "kb": f}) for k in range(N_CHIPS)]
    y = _mm(down, T, D, f32, f"ffn_down_{tag}", res=x, scale=0.5)
    return y, (x, h, a, b), landed


def _ffn_bwd(dy2, saved, g, w, f, tag):
    x, h, a, b = saved
    dy, dyb = dy2
    T, D = x.shape
    FP = w["wgu"].shape[2]
    F4 = N_CHIPS * FP
    da, db, hm = _ffn_bwd_mid(dyb, w["m"], a, b, f, f"ffn_mid_bwd_{tag}")
    dwd = _mm([("tn", hm, dyb, {})], F4, D, bf16, f"ffn_dwd_{tag}", scale=0.5)
    dwg = _mm([("tn", h, da, {})], D, F4, bf16, f"ffn_dwg_{tag}", out_chip=True)
    dwu = _mm([("tn", h, db, {})], D, F4, bf16, f"ffn_dwu_{tag}", out_chip=True)
    tn = _tile(D, (512, 256, 128))
    pairs = [("nt", t, w["wgu"], {"K": FP, "ka": k, "bsel": k, "noff": (2 * f + u) * (D // tn)})
             for u, t in enumerate((da, db)) for k in range(N_CHIPS)]
    dh = _mm(pairs, T, D, f32, f"ffn_dh_{tag}", tn=tn)
    dx, dg8 = _rms_bwd(dh, x, g, dy, f"rms_bwd_{tag}")
    return dx, dict(norm=jnp.sum(dg8, axis=0), wgu=[dwg, dwu], wd=dwd.reshape(N_CHIPS, FP, D))


def _layer_fwd(x, w, B, tag, carry=(None, None, None)):
    T, D = x.shape
    S = T // B
    x1, sv0, landed_ffn = _ffn_fwd(x, w["ffn_norm"][0], w, 0, f"pre_{tag}", carry[2])
    h = _rms_fwd(x1, w["mix_norm"], f"rms_mix_{tag}")
    pm = _mm([("nn", h, w["w_main"], {})], T, 5 * BRANCH, bf16, f"proj_main_{tag}")
    ab = _mm([("nn", h, w["w_ab"], {})], T, LANES, f32, f"proj_ab_{tag}", tn=LANES)
    sb = _mm([("nn", h, w["w_sb"], {})], T, 3 * BRANCH, bf16, f"proj_sb_{tag}")
    gl = _mm([("nn", h, w["w_gates"], {})], T, 3 * D, bf16, f"proj_gates_{tag}")
    pm3, ab3, sb3 = pm.reshape(B, S, -1), ab.reshape(B, S, LANES), sb.reshape(B, S, -1)
    y_pool = _pool_fwd(pm3, w["pool_w"], w["pool_scale"], f"pool_{tag}")
    par = _dn_params(w["dn_A_log"], w["dn_dt_bias"])
    gcb, betab, gcr = _dn_prep(ab3, par, f"dn_prep_{tag}")
    (y_dn, states, vnew, tinv, ub, wb), landed_dn = _dn_fwd(pm3, w["dn_conv"], gcb, betab, gcr, w["dn_out_norm"], f"dn_fwd_{tag}", carry[0])
    (y_sb, tot), landed_sb = _sb_fwd(sb3, f"sb_fwd_{tag}", carry[1])
    ys = [y_pool.reshape(T, BRANCH), y_dn.reshape(T, BRANCH), y_sb.reshape(T, BRANCH)]
    merged = _merge_fwd(ys, w["w_branch"], gl, w["b_gate"], f"merge_{tag}")
    dc = D // N_CHIPS
    out_pairs = [("nn", merged, w["m"], {"K": dc, "ka": k, "bsel": k, "kb": _w_out_block(w)}) for k in range(N_CHIPS)]
    x2 = _mm(out_pairs, T, D, f32, f"mix_out_{tag}", res=x1)
    x3, sv1, _ = _ffn_fwd(x2, w["ffn_norm"][1], w, 1, f"post_{tag}")
    saved = dict(sv0=sv0, sv1=sv1, x1=x1, h=h, pm3=pm3, ab3=ab3, sb3=sb3, gl=gl, par=par, gcb=gcb, betab=betab, gcr=gcr,
                 states=states, vnew=vnew, tinv=tinv, ub=ub, wb=wb, tot=tot, ys=ys, merged=merged)
    return x3, saved, (landed_dn, landed_sb, landed_ffn)


def _layer_bwd(dx3, w, sv, B, tag, carry=(None, None)):
    T, D = dx3[0].shape
    S = T // B
    (dx2, dx2b), g1 = _ffn_bwd(dx3, sv["sv1"], w["ffn_norm"][1], w, 1, f"post_{tag}")
    dc = D // N_CHIPS
    dmerged = _mm([("nt", dx2b, w["m"], {"b_by_chip": True, "noff": _w_out_block(w)})], T, D, bf16, f"mix_dmerged_{tag}", tn=dc)
    dw_out = _mm([("tn", sv["merged"], dx2b, {})], D, D, bf16, f"mix_dwout_{tag}")
    ys = sv["ys"]
    dgl, dbd, dbg8 = _merge_bwd(dmerged, ys, w["w_branch"], sv["gl"], w["b_gate"], f"merge_bwd_{tag}")
    dys, dwb = [], []
    for k in range(3):
        dys.append(_mm([("nt", dbd, w["w_branch"], {"asel": k, "bsel": k})], T, BRANCH, bf16, f"branch_dy{k}_{tag}"))
        dwb.append(_mm([("tn", ys[k], dbd, {"bsel": k})], BRANCH, D, bf16, f"branch_dw{k}_{tag}"))
    pm3, ab3, sb3 = sv["pm3"], sv["ab3"], sv["sb3"]
    du, dpool_w, dsc8 = _pool_bwd(pm3, dys[0].reshape(B, S, BRANCH), w["pool_w"], w["pool_scale"], f"pool_bwd_{tag}")
    (dq, dk, dv, dz, dg, dbeta, dconv, dgain), landed = _dn_bwd(
        pm3, dys[1].reshape(B, S, BRANCH), w["dn_conv"], sv["gcb"], sv["betab"], sv["gcr"], w["dn_out_norm"],
        sv["states"], sv["vnew"], sv["tinv"], sv["ub"], sv["wb"], f"dn_bwd_{tag}", carry[0])
    dab, dn_acc = _dn_post(ab3, sv["par"], dg, dbeta, f"dn_post_{tag}")
    (dsq, dsk, dsv), landed_sb = _sb_bwd(sb3, dys[2].reshape(B, S, BRANCH), sv["tot"], f"sb_bwd_{tag}", carry[1])
    main_parts = [t.reshape(T, BRANCH) for t in (du, dq, dk, dv, dz)]
    sb_parts = [t.reshape(T, BRANCH) for t in (dsq, dsk, dsv)]
    dab2 = dab.reshape(T, LANES)
    pairs = [("nt", t, w["w_main"], {"K": BRANCH, "kb": k}) for k, t in enumerate(main_parts)]
    pairs.append(("nt", dab2, w["w_ab"], {}))
    pairs += [("nt", t, w["w_sb"], {"K": BRANCH, "kb": k}) for k, t in enumerate(sb_parts)]
    pairs += [("nt", dgl, w["w_gates"], {"K": D, "kb": k, "asel": k}) for k in range(3)]
    dh = _mm(pairs, T, D, f32, f"mix_dh_{tag}")
    h = sv["h"]
    dw_cols = [_mm([("tn", h, t, {})], D, BRANCH, bf16, f"dwin_main{k}_{tag}") for k, t in enumerate(main_parts)]
    dw_cols.append(_mm([("tn", h, dab2, {})], D, LANES, bf16, f"dwin_ab_{tag}", tn=LANES)[:, :2 * N_HEADS])
    dw_cols += [_mm([("tn", h, t, {})], D, BRANCH, bf16, f"dwin_sb{k}_{tag}") for k, t in enumerate(sb_parts)]
    dw_cols += [_mm([("tn", h, dgl, {"bsel": k})], D, D, bf16, f"dwin_gate{k}_{tag}") for k in range(3)]
    dx1, dmix8 = _rms_bwd(dh, sv["x1"], w["mix_norm"], dx2, f"rms_mix_bwd_{tag}")
    dx0, g0 = _ffn_bwd(dx1, sv["sv0"], w["ffn_norm"][0], w, 0, f"pre_{tag}")
    dwb = jnp.stack(dwb).reshape(3, BRANCH, N_CHIPS, dc).transpose(2, 0, 1, 3).reshape(N_CHIPS, -1, D)
    dw_in = jnp.concatenate(dw_cols, axis=1)
    pc = dw_in.shape[1] // N_CHIPS
    grads = dict(
        ffn_norm=jnp.stack([g0["norm"], g1["norm"]]),
        A=jnp.concatenate(g0["wgu"] + g1["wgu"], axis=1),
        M=jnp.concatenate([g0["wd"], g1["wd"], dw_out.reshape(N_CHIPS, dc, D), dwb], axis=1),
        C=jnp.stack([dw_in[:, k * pc:(k + 1) * pc] for k in range(N_CHIPS)]),
        mix_norm=jnp.sum(dmix8, axis=0), b_gate=jnp.sum(dbg8, axis=1).reshape(3 * D),
        pool_w=dpool_w, pool_scale=jnp.sum(dsc8, axis=0), dn_conv=jnp.sum(dconv, axis=0).transpose(1, 0, 2).reshape(4, 3 * BRANCH),
        dn_A_log=jnp.sum(dn_acc[0], axis=0)[:N_HEADS], dn_dt_bias=jnp.sum(dn_acc[1], axis=0)[:N_HEADS],
        dn_out_norm=jnp.sum(dgain, axis=(0, 1, 2)))
    return dx0, grads, (landed, landed_sb)


def _local_step(x, target, layers, final_norm, B):
    saved = []
    for l, w in enumerate(layers):
        x, sv, _ = _layer_fwd(x, w, B, f"l{l}")
        saved.append(sv)
    dx, dfn8, ls8 = _final_loss(x, target, final_norm)
    grads = [None] * len(layers)
    for l in reversed(range(len(layers))):
        dx, grads[l], _ = _layer_bwd(dx, layers[l], saved[l], B, f"l{l}")
    return jnp.sum(ls8), dx[0], grads, jnp.sum(dfn8, axis=0)


def _adamw(w, g, m, v, name):
    shape = w.shape
    cols = shape[-1]
    rows = math.prod(shape[:-1]) if len(shape) > 1 else 1
    w2, g2, m2, v2 = (t.reshape(rows, cols) for t in (w, g, m, v))
    block_elems = 256 * 1024
    tr = rows if rows * cols <= block_elems else _tile(rows, [t for t in (512, 256, 128, 64, 32, 16, 8) if t * cols <= block_elems])

    def body(w_ref, g_ref, m_ref, v_ref, d_ref, mo_ref, vo_ref):
        gv = g_ref[...]
        mn = ADAM_B1 * m_ref[...] + (1.0 - ADAM_B1) * gv
        vn = ADAM_B2 * v_ref[...] + (1.0 - ADAM_B2) * (gv * gv)
        m_hat = mn / (1.0 - ADAM_B1 ** ADAM_STEP)
        v_hat = vn / (1.0 - ADAM_B2 ** ADAM_STEP)
        d_ref[...] = -ADAM_LR * (m_hat / (jnp.sqrt(v_hat) + ADAM_EPS) + ADAM_WD * w_ref[...])
        mo_ref[...] = mn
        vo_ref[...] = vn

    spec = pl.BlockSpec((tr, cols), lambda i: (i, 0))
    o = jax.ShapeDtypeStruct((rows, cols), f32)
    d, mo, vo = pl.pallas_call(
        body, name=name, grid=(rows // tr,), in_specs=[spec] * 4, out_specs=[spec] * 3, out_shape=[o, o, o],
        compiler_params=_cp(("parallel",)),
    )(w2, g2, m2, v2)
    return d.reshape(shape), mo.reshape(shape), vo.reshape(shape)


MESH = pl.DeviceIdType.MESH
_ANY = pl.BlockSpec(memory_space=pl.ANY)


def _place():
    x, y, c = lax.axis_index("x"), lax.axis_index("y"), lax.axis_index("c")
    return x, y, c, [(1 - x, y), (x, 1 - y), (1 - x, 1 - y)]


def _chip_index():
    return 2 * lax.axis_index("x") + lax.axis_index("y")


def _half(c, rh):
    return pl.ds(c * rh, rh)


def _remote(src, dst, ssem, rsem, to):
    return pltpu.make_async_remote_copy(src_ref=src, dst_ref=dst, send_sem=ssem, recv_sem=rsem, device_id=to,
                                        device_id_type=MESH)


class _ChipExchange:
    def __init__(self, kind, ins):
        self.kind, self.ins = kind, list(ins)
        self.n = len(self.ins)
        self.out_shapes = [jax.ShapeDtypeStruct((N_CHIPS,) + a.shape[-2:], a.dtype) for a in self.ins]
        self.scratch = [pltpu.SemaphoreType.DMA((self.n, 3)), pltpu.SemaphoreType.DMA((self.n, 3))]

    def _copies(self, in_refs, out_refs, ssem, rsem):
        x, y, c, chips = _place()
        me = 2 * x + y
        pairs = []
        for o, (src, dst) in enumerate(zip(in_refs, out_refs)):
            for k, (px, py) in enumerate(chips):
                peer = 2 * px + py
                if self.kind == "gather":
                    rows = _half(c, src.shape[0] // 2)
                    out, land = (src.at[rows], dst.at[me, rows]), dst.at[peer, rows]
                else:
                    out, land = (src.at[peer], dst.at[me]), dst.at[peer]
                pairs.append((_remote(out[0], out[1], ssem.at[o, k], rsem.at[o, k], (px, py, c)),
                              _remote(land, land, ssem.at[o, k], rsem.at[o, k], (px, py, c))))
        return pairs

    def start(self, in_refs, out_refs, ssem, rsem):
        for mine, _ in self._copies(in_refs, out_refs, ssem, rsem):
            mine.start()

    def wait(self, in_refs, out_refs, ssem, rsem):
        for mine, landing in self._copies(in_refs, out_refs, ssem, rsem):
            landing.wait_recv()
            mine.wait_send()

    def standalone(self, name):
        n = self.n

        def body(*refs):
            ins, outs, (ssem, rsem) = refs[:n], refs[n:2 * n], refs[2 * n:]
            self.start(ins, outs, ssem, rsem)
            self.wait(ins, outs, ssem, rsem)

        return pl.pallas_call(body, name=name, in_specs=[_ANY] * n, out_specs=[_ANY] * n, out_shape=self.out_shapes,
                              scratch_shapes=self.scratch)(*self.ins)


def _carried(body, n_in, n_out, grid, carry):
    if carry is None:
        return body, [], [], [], []
    n = carry.n

    def wrapped(*refs):
        ins, cin = refs[:n_in], refs[n_in:n_in + n]
        outs, cout = refs[n_in + n:n_in + n + n_out], refs[n_in + n + n_out:n_in + 2 * n + n_out]
        scratch, (ssem, rsem) = refs[n_in + 2 * n + n_out:-2], refs[-2:]
        step = pl.program_id(0) * grid[1] + pl.program_id(1)

        @pl.when(step == 0)
        def _():
            carry.start(cin, cout, ssem, rsem)

        body(*ins, *outs, *scratch)

        @pl.when(step == grid[0] * grid[1] - 1)
        def _():
            carry.wait(cin, cout, ssem, rsem)

    return wrapped, [_ANY] * n, [_ANY] * n, carry.out_shapes, carry.scratch


def _gather_finish(shards, landed, name):
    n = len(shards)

    def body(*refs):
        outs, (ssem, rsem) = refs[n:2 * n], refs[2 * n:]
        x, y, c, chips = _place()
        started = []
        for o, buf in enumerate(outs):
            rh = buf.shape[1] // 2
            for k, (px, py) in enumerate(chips):
                block = buf.at[2 * px + py, _half(c, rh)]
                cp = _remote(block, block, ssem.at[o, k], rsem.at[o, k], (x, y, 1 - c))
                cp.start()
                started.append(cp)
        for o, buf in enumerate(outs):
            rh = buf.shape[1] // 2
            for k, (px, py) in enumerate(chips):
                block = buf.at[2 * px + py, _half(1 - c, rh)]
                _remote(block, block, ssem.at[o, k], rsem.at[o, k], (x, y, 1 - c)).wait_recv()
        for cp in started:
            cp.wait_send()

    outs = pl.pallas_call(
        body, name=name, in_specs=[_ANY] * n, out_specs=[_ANY] * n,
        out_shape=[jax.ShapeDtypeStruct(a.shape, a.dtype) for a in landed], input_output_aliases={i: i for i in range(n)},
        scratch_shapes=[pltpu.SemaphoreType.DMA((n, 3)), pltpu.SemaphoreType.DMA((n, 3))],
    )(*landed)
    me = _chip_index()
    return [lax.dynamic_update_slice(g, s[None], (me, 0, 0)) for g, s in zip(outs, shards)]


def _gather_chips(shards, name):
    landed = _ChipExchange("gather", shards).standalone(f"{name}_ici")
    return _gather_finish(shards, landed, f"{name}_pass")


def _pair_swap_halves(ps, name):
    n = len(ps)

    def body(*refs):
        ins, outs, (ssem, rsem) = refs[:n], refs[n:2 * n], refs[2 * n:]
        x, y, c, _ = _place()
        cps = [_remote(p.at[:, _half(1 - c, p.shape[1] // 2)], out, ssem.at[o], rsem.at[o], (x, y, 1 - c))
               for o, (p, out) in enumerate(zip(ins, outs))]
        for cp in cps:
            cp.start()
        for cp in cps:
            cp.wait()

    return pl.pallas_call(
        body, name=name, in_specs=[_ANY] * n, out_specs=[_ANY] * n,
        out_shape=[jax.ShapeDtypeStruct((p.shape[0], p.shape[1] // 2, p.shape[2]), p.dtype) for p in ps],
        scratch_shapes=[pltpu.SemaphoreType.DMA((n,)), pltpu.SemaphoreType.DMA((n,))],
    )(*ps)


def _pair_add(p, got, name):
    n, R, W = p.shape
    rh = R // 2
    tr = _row_tile(rh, W, 2 * 1024 * 1024)
    nb = rh // tr

    def body(c_ref, p_ref, g_ref, o_ref):
        o_ref[...] = (p_ref[...].astype(f32) + g_ref[...].astype(f32)).astype(o_ref.dtype)

    return pl.pallas_call(
        body, name=name,
        grid_spec=pltpu.PrefetchScalarGridSpec(
            num_scalar_prefetch=1, grid=(n, nb),
            in_specs=[pl.BlockSpec((1, tr, W), lambda j, i, c_ref: (j, c_ref[0] * nb + i, 0)),
                      pl.BlockSpec((1, tr, W), lambda j, i, c_ref: (j, i, 0))],
            out_specs=pl.BlockSpec((1, tr, W), lambda j, i, c_ref: (j, i, 0))),
        out_shape=jax.ShapeDtypeStruct((n, rh, W), p.dtype), compiler_params=_cp(("parallel", "parallel")),
    )(lax.axis_index("c").astype(jnp.int32).reshape(1), p, got)


def _own_slot_filled(landed, ps):
    me = _chip_index()
    return [lax.dynamic_update_slice(out, lax.dynamic_slice(p, (me, 0, 0), (1,) + p.shape[1:]), (me, 0, 0))
            for out, p in zip(landed, ps)]


def _sum_slots(r4, name):
    n, R, W = r4.shape
    tr = _row_tile(R, W, 1024 * 1024)

    def body(r_ref, o_ref):
        acc = r_ref[0].astype(f32)
        for k in range(1, n):
            acc = acc + r_ref[k].astype(f32)
        o_ref[...] = acc

    return pl.pallas_call(
        body, name=name, grid=(R // tr,), in_specs=[pl.BlockSpec((n, tr, W), lambda i: (0, i, 0))],
        out_specs=pl.BlockSpec((tr, W), lambda i: (i, 0)), out_shape=jax.ShapeDtypeStruct((R, W), f32),
        compiler_params=_cp(("parallel",)),
    )(r4)


def _pair_share(ss, name):
    n = len(ss)

    def body(*refs):
        ins, outs, (ssem, rsem) = refs[:n], refs[n:2 * n], refs[2 * n:]
        x, y, c, _ = _place()
        cps = [_remote(s, out.at[c], ssem.at[o], rsem.at[o], (x, y, 1 - c)) for o, (s, out) in enumerate(zip(ins, outs))]
        for cp in cps:
            cp.start()
        for o, (s, out) in enumerate(zip(ins, outs)):
            _remote(s, out.at[1 - c], ssem.at[o], rsem.at[o], (x, y, 1 - c)).wait_recv()
        for cp in cps:
            cp.wait_send()

    outs = pl.pallas_call(
        body, name=name, in_specs=[_ANY] * n, out_specs=[_ANY] * n,
        out_shape=[jax.ShapeDtypeStruct((2,) + s.shape, s.dtype) for s in ss],
        scratch_shapes=[pltpu.SemaphoreType.DMA((n,)), pltpu.SemaphoreType.DMA((n,))],
    )(*ss)
    c = lax.axis_index("c")
    return [lax.dynamic_update_slice(out, s[None], (c, 0, 0)).reshape(2 * s.shape[0], s.shape[1]) for out, s in zip(outs, ss)]


def _reduce_begin(ps, tag):
    got = _pair_swap_halves(ps, f"rs_pair_swap_{tag}")
    return [_pair_add(p, g, f"rs_pair_add{o}_{tag}") for o, (p, g) in enumerate(zip(ps, got))]


def _reduce_finish(pair_sums, landed, tag):
    r4 = _own_slot_filled(landed, pair_sums)
    return _pair_share([_sum_slots(r, f"rs_sum{o}_{tag}") for o, r in enumerate(r4)], f"rs_share_{tag}")


def _reduce_to_chips(ps, tag):
    pair_sums = _reduce_begin(ps, tag)
    return _reduce_finish(pair_sums, _ChipExchange("scatter", pair_sums).standalone(f"rs_scatter_{tag}"), tag)


def _pad_rows(a, mult):
    r = (-a.shape[-2]) % mult
    return a if r == 0 else jnp.pad(a, [(0, 0)] * (a.ndim - 2) + [(0, r), (0, 0)])


def _hidden_pad(fs):
    return -(-fs // LANES) * LANES


def _w_out_block(w):
    return 2 * w["wgu"].shape[2] // (w["m"].shape[2] // N_CHIPS)


def _pack_wgu(wg, wu):
    fs = wg.shape[-1]
    t = jnp.stack([wg, wu], axis=1).astype(bf16)
    return jnp.pad(t, ((0, 0), (0, 0), (0, 0), (0, _hidden_pad(fs) - fs))).reshape(-1, _hidden_pad(fs))


def _pack_m(wd, w_out, w_branch):
    fs, D = wd.shape[1:]
    wdp = jnp.pad(wd.astype(bf16), ((0, 0), (0, _hidden_pad(fs) - fs), (0, 0))).reshape(-1, D)
    return jnp.concatenate([wdp, w_out.astype(bf16), w_branch.astype(bf16).reshape(-1, D)], axis=0)


def _w_in_cols(cg, lo, hi):
    p = cg.shape[2]
    parts = [cg[k][:, max(lo, k * p) - k * p:min(hi, (k + 1) * p) - k * p] for k in range(N_CHIPS)
             if max(lo, k * p) < min(hi, (k + 1) * p)]
    return parts[0] if len(parts) == 1 else jnp.concatenate(parts, axis=1)


def _layer_weights(ag, mg, cg, small):
    D = mg.shape[2]
    dc = D // N_CHIPS
    fp2 = 2 * ag.shape[2]
    wb = mg[:, fp2 + dc:].reshape(N_CHIPS, 3, BRANCH, dc).transpose(1, 2, 0, 3).reshape(3, BRANCH, D)
    c0, c1, c2 = 5 * BRANCH, 5 * BRANCH + 2 * N_HEADS, 8 * BRANCH + 2 * N_HEADS
    w = dict(small)
    w.update(wgu=ag, m=mg, w_branch=wb, w_main=_w_in_cols(cg, 0, c0),
             w_ab=jnp.pad(_w_in_cols(cg, c0, c1), ((0, 0), (0, LANES - 2 * N_HEADS))), w_sb=_w_in_cols(cg, c1, c2),
             w_gates=_w_in_cols(cg, c2, N_CHIPS * cg.shape[2]))
    return w


def _small_pack(pieces):
    flat, offs, r = [], [], 0
    for a in pieces:
        v = a.reshape(-1)
        pad = (-v.shape[0]) % PACK_W
        flat.append(jnp.pad(v, (0, pad)) if pad else v)
        offs.append(r)
        r += (v.shape[0] + pad) // PACK_W
    pack = jnp.concatenate(flat).reshape(r, PACK_W)
    return _pad_rows(pack, 16), offs


def _small_unpack(pack, offs, shapes):
    out = []
    for o, s in zip(offs, shapes):
        n = math.prod(s)
        rows = -(-n // PACK_W)
        out.append(pack[o:o + rows].reshape(-1)[:n].reshape(s))
    return out


SMALL_SHARDED = ("ffn_norm", "dn_conv")
SMALL_REPLICATED = ("mix_norm", "b_gate", "pool_w", "pool_scale", "dn_A_log", "dn_dt_bias", "dn_out_norm")


def kernel(x, ffn_norm, ffn_w_gate, ffn_w_up, ffn_w_down, mix_norm, w_in, b_gate, pool_w, pool_scale, dn_conv, dn_A_log, dn_dt_bias, dn_out_norm, w_branch, w_out, final_norm, loss_target, m_ffn_norm, m_ffn_w_gate, m_ffn_w_up, m_ffn_w_down, m_mix_norm, m_w_in, m_b_gate, m_pool_w, m_pool_scale, m_dn_conv, m_dn_A_log, m_dn_dt_bias, m_dn_out_norm, m_w_branch, m_w_out, m_final_norm, v_ffn_norm, v_ffn_w_gate, v_ffn_w_up, v_ffn_w_down, v_mix_norm, v_w_in, v_b_gate, v_pool_w, v_pool_scale, v_dn_conv, v_dn_A_log, v_dn_dt_bias, v_dn_out_norm, v_w_branch, v_w_out, v_final_norm):
    names = ("ffn_norm", "ffn_w_gate", "ffn_w_up", "ffn_w_down", "mix_norm", "w_in", "b_gate", "pool_w", "pool_scale", "dn_conv",
             "dn_A_log", "dn_dt_bias", "dn_out_norm", "w_branch", "w_out", "final_norm")
    wts = dict(zip(names, (ffn_norm, ffn_w_gate, ffn_w_up, ffn_w_down, mix_norm, w_in, b_gate, pool_w, pool_scale, dn_conv,
                           dn_A_log, dn_dt_bias, dn_out_norm, w_branch, w_out, final_norm)))
    ms = dict(zip(names, (m_ffn_norm, m_ffn_w_gate, m_ffn_w_up, m_ffn_w_down, m_mix_norm, m_w_in, m_b_gate, m_pool_w, m_pool_scale,
                          m_dn_conv, m_dn_A_log, m_dn_dt_bias, m_dn_out_norm, m_w_branch, m_w_out, m_final_norm)))
    vs = dict(zip(names, (v_ffn_norm, v_ffn_w_gate, v_ffn_w_up, v_ffn_w_down, v_mix_norm, v_w_in, v_b_gate, v_pool_w, v_pool_scale,
                          v_dn_conv, v_dn_A_log, v_dn_dt_bias, v_dn_out_norm, v_w_branch, v_w_out, v_final_norm)))
    B, S, D = x.shape
    T = B * S
    L = ffn_w_gate.shape[0]
    fs, ds_, cs = ffn_w_gate.shape[3], D // N_CHIPS, dn_conv.shape[2]
    fp = _hidden_pad(fs)
    chip = _chip_index()

    shards = [[_pack_wgu(ffn_w_gate[l], ffn_w_up[l]), _pack_m(ffn_w_down[l], w_out[l], w_branch[l]), w_in[l].astype(bf16)]
              for l in range(L)]
    small_shard, soffs = _small_pack([ffn_norm, dn_conv])
    small_g = _gather_chips([small_shard], "gather_small")[0]

    def chip_major(i, a):
        rows = -(-a.size // PACK_W)
        return small_g[:, soffs[i]:soffs[i] + rows].reshape(N_CHIPS, -1)[:, :a.size].reshape((N_CHIPS,) + a.shape)

    fn_full = jnp.moveaxis(chip_major(0, ffn_norm), 0, 2).reshape(L, 2, D)
    conv_full = jnp.moveaxis(chip_major(1, dn_conv), 0, 2).reshape(L, dn_conv.shape[1], N_CHIPS * cs)

    def small_params(l):
        return dict(ffn_norm=fn_full[l], mix_norm=mix_norm[l], b_gate=b_gate[l], pool_w=pool_w[l], pool_scale=pool_scale[l],
                    dn_conv=conv_full[l], dn_A_log=dn_A_log[l], dn_dt_bias=dn_dt_bias[l], dn_out_norm=dn_out_norm[l])

    xt = x.reshape(T, D)
    gathered = _gather_chips(shards[0], "gather_l0")
    layers, saved = [], []
    for l in range(L):
        w = _layer_weights(*gathered, small_params(l))
        more = l + 1 < L
        carry = [_ChipExchange("gather", [s]) if more else None for s in shards[min(l + 1, L - 1)]]
        xt, sv, (landed_dn, landed_sb, landed_ffn) = _layer_fwd(xt, w, B, f"l{l}", (carry[0], carry[1], carry[2]))
        layers.append(w)
        saved.append(sv)
        if more:
            gathered = _gather_finish(shards[l + 1], [landed_dn[0], landed_sb[0], landed_ffn[0]], f"gather_pass_l{l + 1}")
    dx, dfn8, ls8 = _final_loss(xt, loss_target.reshape(T, D), final_norm)
    loss = lax.psum(jnp.sum(ls8), ("x", "y", "c")) * (0.5 / D)

    grads, red, pending = [None] * L, [None] * L, None
    for l in reversed(range(L)):
        carry = (None, None) if pending is None else (_ChipExchange("scatter", pending[:2]), _ChipExchange("scatter", pending[2:]))
        dx, grads[l], (landed_dn, landed_sb) = _layer_bwd(dx, layers[l], saved[l], B, f"l{l}", carry)
        if pending is not None:
            red[l + 1] = _reduce_finish(pending, list(landed_dn) + list(landed_sb), f"l{l + 1}")
        pending = _reduce_begin([grads[l]["A"], grads[l]["M"], grads[l]["C"]], f"l{l}")
    red[0] = _reduce_finish(pending, _ChipExchange("scatter", pending).standalone("rs_scatter_l0"), "l0")

    small_names = SMALL_SHARDED + SMALL_REPLICATED
    pieces = [g[k] for g in grads for k in small_names] + [jnp.sum(dfn8, axis=0)]
    spack, offs = _small_pack(pieces)
    sred = _reduce_to_chips([jnp.broadcast_to(spack[None], (N_CHIPS,) + spack.shape)], "small")[0]
    small_red = _small_unpack(sred, offs, [p.shape for p in pieces])

    gw = {k: [] for k in names if k != "final_norm"}
    for l in range(L):
        ga, gm, gc = red[l]
        gu = ga.reshape(2, 2, D, fp)[..., :fs]
        gw["ffn_w_gate"].append(gu[:, 0])
        gw["ffn_w_up"].append(gu[:, 1])
        gw["ffn_w_down"].append(gm[:2 * fp].reshape(2, fp, D)[:, :fs])
        gw["w_out"].append(gm[2 * fp:2 * fp + ds_])
        gw["w_branch"].append(gm[2 * fp + ds_:].reshape(3, BRANCH, ds_))
        gw["w_in"].append(gc)
        sm = dict(zip(small_names, small_red[l * len(small_names):(l + 1) * len(small_names)]))
        gw["ffn_norm"].append(lax.dynamic_slice_in_dim(sm["ffn_norm"], chip * ds_, ds_, axis=1))
        gw["dn_conv"].append(lax.dynamic_slice_in_dim(sm["dn_conv"], chip * cs, cs, axis=1))
        for k in SMALL_REPLICATED:
            gw[k].append(sm[k])
    gw = {k: jnp.stack(v) for k, v in gw.items()}
    gw["final_norm"] = small_red[-1]

    deltas, new_m, new_v = [], [], []
    for k in names:
        d, mo, vo = _adamw(wts[k], gw[k], ms[k], vs[k], f"adamw_{k}")
        deltas.append(d)
        new_m.append(mo)
        new_v.append(vo)
    return (loss, dx[0].reshape(B, S, D), *[gw[k] for k in names], *deltas, *new_m, *new_v)
```
